```python
import math
import jax, jax.numpy as jnp
from jax import lax
import numpy as np

D_MODEL = 1024
BATCH = 8
SEQ = 4096
DEPTH = 1

EPS = 1e-6
GLA_HEADS = 4
GLA_DK = D_MODEL // (2 * GLA_HEADS)
GLA_DV = D_MODEL // GLA_HEADS
GLA_KEY = GLA_HEADS * GLA_DK
GLA_VAL = GLA_HEADS * GLA_DV
GLA_RANK = 16
GLA_GATE_TEMP = 16.0
GLA_CHUNK = 64
HY_WIDTH = D_MODEL
HY_BANDS = 16
HY_EMB = 1 + 2 * HY_BANDS
HY_FFN = 64
HY_TARGET = 1e-2
HY_FAST_PCT = 0.3
HY_SLOW_PCT = 1.5
N_EXPERTS = 32
TOP_K = 4
D_FF = D_MODEL
SWIGLU_LIMIT = 7.0
SWIGLU_ALPHA = 1.702
IN_SIZES = (GLA_KEY, GLA_KEY, GLA_VAL, GLA_VAL, 2 * GLA_RANK, 3 * HY_WIDTH, 2 * D_MODEL)
IN_TOTAL = sum(IN_SIZES)
IN_SPLITS = tuple(int(v) for v in np.cumsum(IN_SIZES)[:-1])

kernel_name = "hybrid_gla_hyena_moe_encoder"


def rms_norm(x, w):
    xf = x.astype(jnp.float32)
    y = xf * lax.rsqrt(jnp.mean(xf * xf, axis=-1, keepdims=True) + EPS)
    return (y * w.astype(jnp.float32)).astype(x.dtype)


def gla_one_direction(q, k, v, g, strict):
    B, L, _ = q.shape
    N = L // GLA_CHUNK

    def chunks(t, d):
        return t.reshape(B, N, GLA_CHUNK, GLA_HEADS, d).transpose(0, 3, 1, 2, 4)

    qc, kc, gc, vc = chunks(q, GLA_DK), chunks(k, GLA_DK), chunks(g, GLA_DK), chunks(v, GLA_DV)
    b = jnp.cumsum(gc, axis=-2)
    b_last = b[..., -1:, :]
    q_t = qc * jnp.exp(b)
    k_t = kc * jnp.exp(-b)
    idx = jnp.arange(GLA_CHUNK)
    mask = (idx[:, None] > idx[None, :]) if strict else (idx[:, None] >= idx[None, :])
    scores = jnp.einsum('bhncd,bhnsd->bhncs', q_t, k_t)
    scores = jnp.where(mask, scores, 0.0)
    o_intra = jnp.einsum('bhncs,bhnse->bhnce', scores, vc)
    k_end = kc * jnp.exp(b_last - b)
    upd = jnp.einsum('bhncd,bhnce->bhnde', k_end, vc)
    dec = jnp.exp(b_last[..., 0, :])

    def step(state, inp):
        d_n, u_n = inp
        return d_n[..., None] * state + u_n, state

    init = jnp.zeros((B, GLA_HEADS, GLA_DK, GLA_DV), jnp.float32)
    _, s_prev = lax.scan(step, init, (jnp.moveaxis(dec, 2, 0), jnp.moveaxis(upd, 2, 0)))
    s_prev = jnp.moveaxis(s_prev, 0, 2)
    o_inter = jnp.einsum('bhncd,bhnde->bhnce', q_t, s_prev)
    o = o_intra + o_inter
    return o.transpose(0, 2, 3, 1, 4).reshape(B, L, GLA_HEADS, GLA_DV)


def gla_branch(q, k, v, r, a_low, gla_gate_w2, gla_gate_b, gla_norm_w):
    dt = q.dtype
    B, L, _ = q.shape
    qf = q.astype(jnp.float32) * (GLA_DK ** -0.5)
    kf = k.astype(jnp.float32)
    vf = v.astype(jnp.float32)
    a2 = a_low.astype(jnp.float32).reshape(B, L, 2, GLA_RANK)
    logits = jnp.einsum('bldr,drk->bldk', a2, gla_gate_w2.astype(jnp.float32)) + gla_gate_b.astype(jnp.float32)
    log_a = jax.nn.log_sigmoid(logits) / GLA_GATE_TEMP
    o_f = gla_one_direction(qf, kf, vf, log_a[:, :, 0], strict=False)
    flip = lambda t: jnp.flip(t, axis=1)
    o_b = flip(gla_one_direction(flip(qf), flip(kf), flip(vf), flip(log_a[:, :, 1]), strict=True))
    o = o_f + o_b
    o = o * lax.rsqrt(jnp.mean(o * o, axis=-1, keepdims=True) + EPS) * gla_norm_w.astype(jnp.float32)
    o = o.reshape(B, L, GLA_VAL).astype(dt)
    return o * jax.nn.silu(r)


def short_conv3(u, w, b):
    up = jnp.pad(u, ((0, 0), (1, 1), (0, 0)))
    return w[0] * up[:, :-2] + w[1] * up[:, 1:-1] + w[2] * up[:, 2:] + b


def hyena_filters(L, hy_freq, w1, b1, w2, b2, w3, hy_decay):
    f32 = jnp.float32
    t = jnp.linspace(0.0, 1.0, L, dtype=f32)[:, None]
    omega = 2.0 * math.pi * jnp.arange(L, dtype=f32) / L
    f = jnp.linspace(1e-4, HY_BANDS - 1, HY_BANDS, dtype=f32)
    ang = omega[:, None] * f[None, :]
    z = jnp.concatenate([t, jnp.cos(ang), -jnp.sin(ang)], axis=-1)
    fr = hy_freq.astype(f32)
    h = jnp.sin(fr * (z @ w1.astype(f32) + b1.astype(f32)))
    h = jnp.sin(fr * (h @ w2.astype(f32) + b2.astype(f32)))
    h = (h @ w3.astype(f32)).reshape(L, 2, HY_WIDTH)
    h = h * jnp.exp(-t[:, :, None] * jnp.abs(hy_decay.astype(f32))[None])
    kern = jnp.concatenate([h[:, 0], jnp.zeros((1, HY_WIDTH), f32), h[:0:-1, 1]], axis=0)
    return kern * lax.rsqrt(jnp.sum(kern * kern, axis=0, keepdims=True) + EPS)


def hyena_branch(u_proj, conv_w, conv_b, kern, hy_bias):
    dt = u_proj.dtype
    u = short_conv3(u_proj, conv_w, conv_b)
    x0, x1, v = jnp.split(u, 3, axis=-1)
    z = (v * x1).astype(jnp.float32)
    L = z.shape[1]
    Z = jnp.fft.rfft(z, n=2 * L, axis=1)
    K = jnp.fft.rfft(kern, n=2 * L, axis=0)
    y = jnp.fft.irfft(Z * K[None], n=2 * L, axis=1)[:, :L] + z * hy_bias.astype(jnp.float32)
    return y.astype(dt) * x0


def moe_ffn(h, router_w, router_b, w_gate_up, b_gate_up, w_down, b_down):
    B, L, D = h.shape
    ht = h.reshape(B * L, D)
    logits = ht.astype(jnp.float32) @ router_w.astype(jnp.float32) + router_b.astype(jnp.float32)
    top_vals, top_idx = lax.top_k(logits, TOP_K)
    weights = jax.nn.softmax(top_vals, axis=-1)
    combine = jnp.sum(jax.nn.one_hot(top_idx, N_EXPERTS, dtype=jnp.float32) * weights[..., None], axis=1)
    combine = combine.astype(ht.dtype)
    out = jnp.zeros_like(ht)
    for e in range(N_EXPERTS):
        gu = ht @ w_gate_up[e] + b_gate_up[e]
        gate, up = gu[:, :D_FF], gu[:, D_FF:]
        gate = jnp.minimum(gate, SWIGLU_LIMIT)
        up = jnp.clip(up, -SWIGLU_LIMIT, SWIGLU_LIMIT)
        act = gate * jax.nn.sigmoid(SWIGLU_ALPHA * gate) * (up + 1.0)
        out = out + combine[:, e:e + 1] * (act @ w_down[e] + b_down[e])
    return out.reshape(B, L, D)


def setup_inputs(seed: int = 0) -> dict:
    key = jax.random.key(seed)
    ks = jax.random.split(key, 32)
    f32 = jnp.float32
    nrm = lambda k, shape, s: jax.random.normal(k, shape, f32) * s
    min_decay = abs(math.log(HY_TARGET) / HY_SLOW_PCT)
    max_decay = abs(math.log(HY_TARGET) / HY_FAST_PCT)
    base_decay = jnp.linspace(min_decay, max_decay, HY_WIDTH, dtype=f32)
    return {
        "x": nrm(ks[0], (BATCH, SEQ, D_MODEL), 1.0),
        "norm1_w": 1.0 + nrm(ks[1], (D_MODEL,), 0.02),
        "w_in": nrm(ks[2], (D_MODEL, IN_TOTAL), D_MODEL ** -0.5),
        "gla_gate_w2": nrm(ks[3], (2, GLA_RANK, GLA_KEY), GLA_RANK ** -0.5),
        "gla_gate_b": nrm(ks[4], (2, GLA_KEY), 0.5),
        "gla_norm_w": 1.0 + nrm(ks[5], (GLA_DV,), 0.02),
        "hy_conv_w": nrm(ks[6], (3, 3 * HY_WIDTH), 0.5),
        "hy_conv_b": nrm(ks[7], (3 * HY_WIDTH,), 0.02),
        "hy_freq": 1.0 + nrm(ks[8], (HY_FFN,), 0.05),
        "hy_ffn_w1": nrm(ks[9], (HY_EMB, HY_FFN), HY_EMB ** -0.5),
        "hy_ffn_b1": nrm(ks[10], (HY_FFN,), 0.1),
        "hy_ffn_w2": nrm(ks[11], (HY_FFN, HY_FFN), HY_FFN ** -0.5),
        "hy_ffn_b2": nrm(ks[12], (HY_FFN,), 0.1),
        "hy_ffn_w3": nrm(ks[13], (HY_FFN, 2 * HY_WIDTH), HY_FFN ** -0.5),
        "hy_decay": base_decay[None, :] * (1.0 + nrm(ks[14], (2, HY_WIDTH), 0.05)),
        "hy_bias": nrm(ks[15], (HY_WIDTH,), 0.1),
        "w_branch_gla": nrm(ks[16], (GLA_VAL, D_MODEL), GLA_VAL ** -0.5),
        "w_branch_hy": nrm(ks[17], (HY_WIDTH, D_MODEL), HY_WIDTH ** -0.5),
        "w_out": nrm(ks[18], (D_MODEL, D_MODEL), D_MODEL ** -0.5),
        "norm2_w": 1.0 + nrm(ks[19], (D_MODEL,), 0.02),
        "router_w": nrm(ks[20], (D_MODEL, N_EXPERTS), D_MODEL ** -0.5),
        "router_b": nrm(ks[21], (N_EXPERTS,), 0.01),
        "w_gate_up": nrm(ks[22], (N_EXPERTS, D_MODEL, 2 * D_FF), D_MODEL ** -0.5),
        "b_gate_up": nrm(ks[23], (N_EXPERTS, 2 * D_FF), 0.01),
        "w_down": nrm(ks[24], (N_EXPERTS, D_FF, D_MODEL), D_FF ** -0.5),
        "b_down": nrm(ks[25], (N_EXPERTS, D_MODEL), 0.01),
        "norm_f_w": 1.0 + nrm(ks[26], (D_MODEL,), 0.02),
    }


def reference(x, norm1_w, w_in, gla_gate_w2, gla_gate_b, gla_norm_w, hy_conv_w, hy_conv_b,
              hy_freq, hy_ffn_w1, hy_ffn_b1, hy_ffn_w2, hy_ffn_b2, hy_ffn_w3, hy_decay, hy_bias,
              w_branch_gla, w_branch_hy, w_out, norm2_w, router_w, router_b,
              w_gate_up, b_gate_up, w_down, b_down, norm_f_w):
    h = x
    L = x.shape[1]
    for _ in range(DEPTH):
        xn = rms_norm(h, norm1_w)
        proj = xn @ w_in
        q, k, v, r, a_low, hy_in, gates = jnp.split(proj, IN_SPLITS, axis=-1)
        o_gla = gla_branch(q, k, v, r, a_low, gla_gate_w2, gla_gate_b, gla_norm_w)
        kern = hyena_filters(L, hy_freq, hy_ffn_w1, hy_ffn_b1, hy_ffn_w2, hy_ffn_b2, hy_ffn_w3, hy_decay)
        o_hy = hyena_branch(hy_in, hy_conv_w, hy_conv_b, kern, hy_bias)
        g_gla, g_hy = jnp.split(gates, 2, axis=-1)
        merged = jax.nn.sigmoid(g_gla) * (o_gla @ w_branch_gla) + jax.nn.sigmoid(g_hy) * (o_hy @ w_branch_hy)
        h = h + merged @ w_out
        hn = rms_norm(h, norm2_w)
        h = h + moe_ffn(hn, router_w, router_b, w_gate_up, b_gate_up, w_down, b_down)
    return rms_norm(h, norm_f_w)
```

```python
import functools
import math

import numpy as np
import jax
import jax.numpy as jnp
from jax import lax
from jax.experimental import pallas as pl
from jax.experimental.pallas import tpu as pltpu

D_MODEL = 1024
EPS = 1e-6
GLA_HEADS = 4
GLA_DK = 128
GLA_DV = 256
GLA_KEY = GLA_HEADS * GLA_DK
GLA_VAL = GLA_HEADS * GLA_DV
GLA_RANK = 16
GLA_GATE_TEMP = 16.0
GLA_CHUNK = 64
HY_WIDTH = D_MODEL
HY_BANDS = 16
HY_EMB = 1 + 2 * HY_BANDS
HY_FFN = 64
N_EXPERTS = 32
TOP_K = 4
D_FF = D_MODEL
SWIGLU_LIMIT = 7.0
SWIGLU_ALPHA = 1.702

LANES = 128
ALOW_PAD = LANES
ROUTE_PAD = LANES
VMEM_LIMIT = 56 * 1024 * 1024

F32 = jnp.float32
BF16 = jnp.bfloat16
HIGHEST = lax.Precision.HIGHEST


def _cparams(*sem):
    return pltpu.CompilerParams(dimension_semantics=sem, vmem_limit_bytes=VMEM_LIMIT)


def _const_spec(shape):
    nd = len(shape)
    return pl.BlockSpec(shape, lambda *_: (0,) * nd, pipeline_mode=pl.Buffered(1))


N_QKVR = 2 * GLA_KEY + 2 * GLA_VAL
N_HY = 3 * HY_WIDTH
N_GATES = 2 * D_MODEL
IN_COLS = N_QKVR + N_HY + N_GATES + ALOW_PAD


def _inproj_kernel(x_ref, nw_ref, w_ref, qkvr_ref, hy_ref, gates_ref, alow_ref):
    x = x_ref[...]
    ms = jnp.mean(x * x, axis=-1, keepdims=True)
    xn = (x * lax.rsqrt(ms + EPS) * nw_ref[...]).astype(BF16)
    c0, c1, c2 = N_QKVR, N_QKVR + N_HY, N_QKVR + N_HY + N_GATES
    qkvr_ref[...] = jnp.dot(xn, w_ref[:, :c0], preferred_element_type=F32).astype(BF16)
    hy_ref[...] = jnp.dot(xn, w_ref[:, c0:c1], preferred_element_type=F32).astype(BF16)
    gates_ref[...] = jnp.dot(xn, w_ref[:, c1:c2], preferred_element_type=F32).astype(BF16)
    alow_ref[...] = jnp.dot(xn, w_ref[:, c2:], preferred_element_type=F32)


def _in_proj(x2, norm1_w, w_cat):
    T = x2.shape[0]
    tm = min(512, T)
    return pl.pallas_call(
        _inproj_kernel,
        grid=(T // tm,),
        in_specs=[
            pl.BlockSpec((tm, D_MODEL), lambda i: (i, 0)),
            _const_spec((1, D_MODEL)),
            _const_spec((D_MODEL, IN_COLS)),
        ],
        out_specs=[
            pl.BlockSpec((tm, N_QKVR), lambda i: (i, 0)),
            pl.BlockSpec((tm, N_HY), lambda i: (i, 0)),
            pl.BlockSpec((tm, N_GATES), lambda i: (i, 0)),
            pl.BlockSpec((tm, ALOW_PAD), lambda i: (i, 0)),
        ],
        out_shape=[
            jax.ShapeDtypeStruct((T, N_QKVR), BF16),
            jax.ShapeDtypeStruct((T, N_HY), BF16),
            jax.ShapeDtypeStruct((T, N_GATES), BF16),
            jax.ShapeDtypeStruct((T, ALOW_PAD), F32),
        ],
        compiler_params=_cparams("parallel"),
        name="in_proj",
    )(x2, norm1_w.reshape(1, D_MODEL), w_cat)


GLA_PREP_ROWS = 256


def _log_sigmoid(x):
    return jnp.minimum(x, 0.0) - jnp.log1p(jnp.exp(-jnp.abs(x)))


def _gla_kernel(q_ref, k_ref, v_ref, r_ref, alow_ref, w2_ref, gb_ref, nw_ref, o_ref,
                qtf, ktf, kef, qtb, ktb, keb, decf, decb, of_s, ob_s):
    L = q_ref.shape[0]
    C = GLA_CHUNK
    n_chunks = L // C
    R = min(GLA_PREP_ROWS, L)
    cpr = R // C
    scale = GLA_DK ** -0.5

    ri = lax.broadcasted_iota(jnp.int32, (R, R), 0)
    ci = lax.broadcasted_iota(jnp.int32, (R, R), 1)
    tri = jnp.where((ri // C == ci // C) & (ri >= ci), 1.0, 0.0).astype(F32)
    last = jnp.where(ci == (ri // C) * C + (C - 1), 1.0, 0.0).astype(F32)

    def prep(g, carry):
        rows = pl.ds(pl.multiple_of(g * R, R), R)
        al = alow_ref[rows, :]
        q = q_ref[rows, :].astype(F32) * scale
        k = k_ref[rows, :].astype(F32)
        for d, (qt, kt, ke, dec) in enumerate(((qtf, ktf, kef, decf), (qtb, ktb, keb, decb))):
            logits = jnp.dot(al, w2_ref[d], preferred_element_type=F32, precision=HIGHEST) + gb_ref[d]
            lg = _log_sigmoid(logits) * (1.0 / GLA_GATE_TEMP)
            b = jnp.dot(tri, lg, preferred_element_type=F32, precision=HIGHEST)
            tot = jnp.dot(last, b, preferred_element_type=F32, precision=HIGHEST)
            if d == 0:
                e_q, e_k, e_end = b, -b, tot - b
            else:
                bs = tot - b + lg
                e_q, e_k, e_end = bs, -bs, b - lg
            qt[rows, :] = (q * jnp.exp(e_q)).astype(BF16)
            kt[rows, :] = (k * jnp.exp(e_k)).astype(BF16)
            ke[rows, :] = (k * jnp.exp(e_end)).astype(BF16)
            dtot = jnp.exp(tot)
            for c in range(cpr):
                dec[pl.ds(g * cpr + c, 1), :] = dtot[c * C:c * C + 1, :]
        return carry

    lax.fori_loop(0, L // R, prep, 0)

    cr = lax.broadcasted_iota(jnp.int32, (C, C), 0)
    cc = lax.broadcasted_iota(jnp.int32, (C, C), 1)
    mask_f = cr >= cc
    mask_b = cr < cc
    nt = (((1,), (1,)), ((), ()))
    tn = (((0,), (0,)), ((), ()))

    def chunk(n, st, qt, kt, ke, dec, mask, out):
        rows = pl.ds(pl.multiple_of(n * C, C), C)
        qn, kn, en, vn = qt[rows, :], kt[rows, :], ke[rows, :], v_ref[rows, :]
        s = lax.dot_general(qn, kn, nt, preferred_element_type=F32)
        s = jnp.where(mask, s, 0.0).astype(BF16)
        o = jnp.dot(s, vn, preferred_element_type=F32)
        o = o + lax.dot_general(qn, st.astype(BF16), nt, preferred_element_type=F32)
        out[rows, :] = o
        upd = lax.dot_general(vn, en, tn, preferred_element_type=F32)
        return dec[pl.ds(n, 1), :] * st + upd

    def step(i, carry):
        sf, sb = carry
        sf = chunk(i, sf, qtf, ktf, kef, decf, mask_f, of_s)
        sb = chunk(n_chunks - 1 - i, sb, qtb, ktb, keb, decb, mask_b, ob_s)
        return sf, sb

    z = jnp.zeros((GLA_DV, GLA_DK), F32)
    lax.fori_loop(0, n_chunks, step, (z, z))

    o = of_s[...] + ob_s[...]
    o = o * lax.rsqrt(jnp.mean(o * o, axis=-1, keepdims=True) + EPS) * nw_ref[...]
    r = r_ref[...].astype(F32)
    o_ref[...] = (o * (r * jax.nn.sigmoid(r))).astype(BF16)


def _gla(qkvr, alow, w2, gb, norm_w, B, L):
    T = B * L
    H = GLA_HEADS
    kq = GLA_KEY // GLA_DK
    kv = 2 * GLA_KEY // GLA_DV
    kr = (2 * GLA_KEY + GLA_VAL) // GLA_DV
    return pl.pallas_call(
        _gla_kernel,
        grid=(B, H),
        in_specs=[
            pl.BlockSpec((L, GLA_DK), lambda b, h: (b, h)),
            pl.BlockSpec((L, GLA_DK), lambda b, h: (b, kq + h)),
            pl.BlockSpec((L, GLA_DV), lambda b, h: (b, kv + h)),
            pl.BlockSpec((L, GLA_DV), lambda b, h: (b, kr + h)),
            pl.BlockSpec((L, ALOW_PAD), lambda b, h: (b, 0)),
            pl.BlockSpec((2, ALOW_PAD, GLA_DK), lambda b, h: (0, 0, h)),
            pl.BlockSpec((2, 1, GLA_DK), lambda b, h: (0, 0, h)),
            pl.BlockSpec((1, GLA_DV), lambda b, h: (0, 0)),
        ],
        out_specs=pl.BlockSpec((L, GLA_DV), lambda b, h: (b, h)),
        out_shape=jax.ShapeDtypeStruct((T, GLA_VAL), BF16),
        scratch_shapes=[pltpu.VMEM((L, GLA_DK), BF16)] * 6
        + [pltpu.VMEM((L // GLA_CHUNK, GLA_DK), F32)] * 2
        + [pltpu.VMEM((L, GLA_DV), F32)] * 2,
        compiler_params=_cparams("parallel", "parallel"),
        name="gla",
    )(qkvr, qkvr, qkvr, qkvr, alow, w2, gb, norm_w.reshape(1, GLA_DV))


HY_WT = 256


def _hy_filter_kernel(z_ref, t_ref, fr_ref, w1_ref, b1_ref, w2_ref, b2_ref, w3f_ref, w3b_ref, dec_ref, o_ref):
    fr = fr_ref[...]
    h = jnp.sin(fr * (jnp.dot(z_ref[...], w1_ref[...], preferred_element_type=F32, precision=HIGHEST) + b1_ref[...]))
    h = jnp.sin(fr * (jnp.dot(h, w2_ref[...], preferred_element_type=F32, precision=HIGHEST) + b2_ref[...]))
    t = t_ref[...]
    row = lax.broadcasted_iota(jnp.int32, (z_ref.shape[0], 1), 0)
    hf = jnp.dot(h, w3f_ref[...], preferred_element_type=F32, precision=HIGHEST)
    hf = hf * jnp.exp(-t * jnp.abs(dec_ref[0:1, :]))
    hb = jnp.dot(h, w3b_ref[...], preferred_element_type=F32, precision=HIGHEST)
    hb = hb * jnp.exp(-t * jnp.abs(dec_ref[1:2, :]))
    hb = jnp.where(row == 0, 0.0, hb)
    ss = jnp.sum(hf * hf, axis=0, keepdims=True) + jnp.sum(hb * hb, axis=0, keepdims=True)
    inv = lax.rsqrt(ss + EPS)
    o_ref[0] = hf * inv
    o_ref[1] = hb * inv


def _hy_filter(L, hy_freq, w1, b1, w2, b2, w3, hy_decay):
    W = HY_WIDTH
    t = np.linspace(0.0, 1.0, L, dtype=np.float32)[:, None]
    omega = (np.float32(2.0 * math.pi) * np.arange(L, dtype=np.float32) / np.float32(L)).astype(np.float32)
    f = np.linspace(1e-4, HY_BANDS - 1, HY_BANDS, dtype=np.float32)
    ang = (omega[:, None] * f[None, :]).astype(np.float32)
    z = np.concatenate([t, np.cos(ang), -np.sin(ang)], axis=-1).astype(np.float32)
    nw = W // HY_WT
    return pl.pallas_call(
        _hy_filter_kernel,
        grid=(nw,),
        in_specs=[
            _const_spec((L, HY_EMB)),
            _const_spec((L, 1)),
            _const_spec((1, HY_FFN)),
            _const_spec((HY_EMB, HY_FFN)),
            _const_spec((1, HY_FFN)),
            _const_spec((HY_FFN, HY_FFN)),
            _const_spec((1, HY_FFN)),
            pl.BlockSpec((HY_FFN, HY_WT), lambda j: (0, j)),
            pl.BlockSpec((HY_FFN, HY_WT), lambda j: (0, nw + j)),
            pl.BlockSpec((2, HY_WT), lambda j: (0, j)),
        ],
        out_specs=pl.BlockSpec((2, L, HY_WT), lambda j: (0, 0, j)),
        out_shape=jax.ShapeDtypeStruct((2, L, W), F32),
        compiler_params=_cparams("parallel"),
        name="hy_filter",
    )(jnp.asarray(z), jnp.asarray(t), hy_freq.reshape(1, HY_FFN), w1, b1.reshape(1, HY_FFN), w2,
      b2.reshape(1, HY_FFN), w3, w3, hy_decay)


def _hy_pre_kernel(x0_ref, x1_ref, v_ref, w0_ref, w1_ref, wv_ref, b0_ref, b1_ref, bv_ref, z_ref, x0o_ref):
    L = x0_ref.shape[0]
    row = lax.broadcasted_iota(jnp.int32, (L, 1), 0)

    def conv3(u_ref, w_ref, b_ref):
        u = u_ref[...].astype(F32)
        prev = jnp.where(row == 0, 0.0, pltpu.roll(u, 1, 0))
        nxt = jnp.where(row == L - 1, 0.0, pltpu.roll(u, L - 1, 0))
        return w_ref[0:1, :] * prev + w_ref[1:2, :] * u + w_ref[2:3, :] * nxt + b_ref[...]

    x0o_ref[...] = conv3(x0_ref, w0_ref, b0_ref).astype(BF16)
    z_ref[...] = (conv3(v_ref, wv_ref, bv_ref) * conv3(x1_ref, w1_ref, b1_ref)).astype(BF16)


def _hy_pre(hy, conv_w, conv_b, B, L):
    W = HY_WIDTH
    nw = W // HY_WT
    cb = conv_b.reshape(1, 3 * W)
    blk = lambda off: pl.BlockSpec((L, HY_WT), lambda b, j: (b, off + j))
    wblk = lambda off: pl.BlockSpec((3, HY_WT), lambda b, j: (0, off + j))
    bblk = lambda off: pl.BlockSpec((1, HY_WT), lambda b, j: (0, off + j))
    return pl.pallas_call(
        _hy_pre_kernel,
        grid=(B, nw),
        in_specs=[blk(0), blk(nw), blk(2 * nw), wblk(0), wblk(nw), wblk(2 * nw), bblk(0), bblk(nw), bblk(2 * nw)],
        out_specs=[pl.BlockSpec((L, HY_WT), lambda b, j: (b, j))] * 2,
        out_shape=[jax.ShapeDtypeStruct((B * L, W), BF16)] * 2,
        compiler_params=_cparams("parallel", "parallel"),
        name="hy_pre",
    )(hy, hy, hy, conv_w, conv_w, conv_w, cb, cb, cb)


DFT_TK = 256
DFT_TT = 512
DFT_WT = 512


def _dft_tables(L):
    N = 2 * L
    nf = L + 1
    kfp = -(-nf // DFT_TK) * DFT_TK
    ang = lambda a, b: (2.0 * np.pi / N) * ((np.outer(a, b)) % N)
    i = np.arange(DFT_TK)
    n = np.arange(L)
    k = np.arange(kfp)
    k0 = np.arange(0, kfp, DFT_TK)
    t0 = np.arange(0, L, DFT_TK)
    alpha = np.where((k == 0) | (k == L), 1.0, 2.0) / N
    alpha = np.where(k <= L, alpha, 0.0)
    f32 = lambda a: jnp.asarray(a.astype(np.float32))
    return dict(
        kfp=kfp,
        fwd_e=(f32(np.cos(ang(i, n))), f32(np.sin(ang(i, n)))),
        fwd_r=(f32(np.cos(ang(k0, n))), f32(np.sin(ang(k0, n)))),
        inv_e=(f32(np.cos(ang(i, k))), f32(np.sin(ang(i, k)))),
        inv_r=(f32(np.cos(ang(t0, k))), f32(np.sin(ang(t0, k)))),
        alpha=f32(alpha[None, :]),
    )


def _dft_fwd_kernel(ec_ref, es_ref, rc_ref, rs_ref, z_ref, kr_ref, ki_ref, yr_ref, yi_ref, fc_s, fs_s, *, use_filter):
    @pl.when(pl.program_id(1) == 0)
    def _():
        ec, es, rc, rs = ec_ref[...], es_ref[...], rc_ref[0], rs_ref[0]
        fc_s[...] = (ec * rc - es * rs).astype(BF16)
        fs_s[...] = (es * rc + ec * rs).astype(BF16)

    z = z_ref[...]
    xr = jnp.dot(fc_s[...], z, preferred_element_type=F32)
    xi = -jnp.dot(fs_s[...], z, preferred_element_type=F32)
    if use_filter:
        kr, ki = kr_ref[...], ki_ref[...]
        yr_ref[...] = (xr * kr - xi * ki).astype(yr_ref.dtype)
        yi_ref[...] = (xr * ki + xi * kr).astype(yi_ref.dtype)
    else:
        yr_ref[...] = xr.astype(yr_ref.dtype)
        yi_ref[...] = xi.astype(yi_ref.dtype)


def _dft_fwd(tabs, z3, kr, ki, use_filter, out_dtype):
    Bz, L, W = z3.shape
    kfp = tabs["kfp"]
    nk = kfp // DFT_TK
    (ec, es), (rc, rs) = tabs["fwd_e"], tabs["fwd_r"]
    return pl.pallas_call(
        functools.partial(_dft_fwd_kernel, use_filter=use_filter),
        grid=(nk, Bz),
        in_specs=[
            _const_spec((DFT_TK, L)),
            _const_spec((DFT_TK, L)),
            pl.BlockSpec((1, 1, L), lambda k, b: (k, 0, 0)),
            pl.BlockSpec((1, 1, L), lambda k, b: (k, 0, 0)),
            pl.BlockSpec((None, L, W), lambda k, b: (b, 0, 0)),
            pl.BlockSpec((DFT_TK, W), lambda k, b: (k, 0)),
            pl.BlockSpec((DFT_TK, W), lambda k, b: (k, 0)),
        ],
        out_specs=[pl.BlockSpec((None, DFT_TK, W), lambda k, b: (b, k, 0))] * 2,
        out_shape=[jax.ShapeDtypeStruct((Bz, kfp, W), out_dtype)] * 2,
        scratch_shapes=[pltpu.VMEM((DFT_TK, L), BF16)] * 2,
        compiler_params=_cparams("arbitrary", "arbitrary"),
        name="dft_fwd_filter" if use_filter else "dft_fwd",
    )(ec, es, rc.reshape(nk, 1, L), rs.reshape(nk, 1, L), z3, kr, ki)


def _dft_inv_kernel(ec_ref, es_ref, rc_ref, rs_ref, al_ref, yr_ref, yi_ref, z_ref, x0_ref, bias_ref, o_ref, gc_s, gs_s):
    @pl.when((pl.program_id(1) == 0) & (pl.program_id(2) == 0))
    def _():
        ec, es, al = ec_ref[...], es_ref[...], al_ref[...]
        for s in range(DFT_TT // DFT_TK):
            rc, rs = rc_ref[s], rs_ref[s]
            rows = slice(s * DFT_TK, (s + 1) * DFT_TK)
            gc_s[rows, :] = (al * (ec * rc - es * rs)).astype(BF16)
            gs_s[rows, :] = (-al * (es * rc + ec * rs)).astype(BF16)

    y = jnp.dot(gc_s[...], yr_ref[...], preferred_element_type=F32)
    y = y + jnp.dot(gs_s[...], yi_ref[...], preferred_element_type=F32)
    z = z_ref[...].astype(F32)
    o_ref[...] = ((y + z * bias_ref[...]) * x0_ref[...].astype(F32)).astype(BF16)


def _dft_inv(tabs, yr, yi, z3, x03, hy_bias):
    B, L, W = z3.shape
    kfp = tabs["kfp"]
    (ec, es), (rc, rs) = tabs["inv_e"], tabs["inv_r"]
    tt = min(DFT_TT, L)
    sub = tt // DFT_TK
    nt = L // tt
    wt = DFT_WT
    return pl.pallas_call(
        _dft_inv_kernel,
        grid=(nt, B, W // wt),
        in_specs=[
            _const_spec((DFT_TK, kfp)),
            _const_spec((DFT_TK, kfp)),
            pl.BlockSpec((sub, 1, kfp), lambda t, b, w: (t, 0, 0)),
            pl.BlockSpec((sub, 1, kfp), lambda t, b, w: (t, 0, 0)),
            _const_spec((1, kfp)),
            pl.BlockSpec((None, kfp, wt), lambda t, b, w: (b, 0, w)),
            pl.BlockSpec((None, kfp, wt), lambda t, b, w: (b, 0, w)),
            pl.BlockSpec((None, tt, wt), lambda t, b, w: (b, t, w)),
            pl.BlockSpec((None, tt, wt), lambda t, b, w: (b, t, w)),
            pl.BlockSpec((1, wt), lambda t, b, w: (0, w)),
        ],
        out_specs=pl.BlockSpec((None, tt, wt), lambda t, b, w: (b, t, w)),
        out_shape=jax.ShapeDtypeStruct((B, L, W), BF16),
        scratch_shapes=[pltpu.VMEM((tt, kfp), BF16)] * 2,
        compiler_params=_cparams("arbitrary", "arbitrary", "arbitrary"),
        name="dft_inv",
    )(ec, es, rc.reshape(L // DFT_TK, 1, kfp), rs.reshape(L // DFT_TK, 1, kfp), tabs["alpha"], yr, yi, z3, x03,
      hy_bias.reshape(1, W))


def _merge_kernel(og_ref, oh_ref, gates_ref, x_ref, wbg_ref, wbh_ref, wo_ref, n2_ref, rw_ref, rb_ref,
                  h1_ref, hn_ref, idx_ref, wts_ref):
    mg = jnp.dot(og_ref[...], wbg_ref[...], preferred_element_type=F32)
    mh = jnp.dot(oh_ref[...], wbh_ref[...], preferred_element_type=F32)
    gg = jax.nn.sigmoid(gates_ref[:, :D_MODEL].astype(F32))
    gh = jax.nn.sigmoid(gates_ref[:, D_MODEL:].astype(F32))
    merged = (gg * mg + gh * mh).astype(BF16)
    h1 = x_ref[...] + jnp.dot(merged, wo_ref[...], preferred_element_type=F32)
    h1_ref[...] = h1
    hn = h1 * lax.rsqrt(jnp.mean(h1 * h1, axis=-1, keepdims=True) + EPS) * n2_ref[...]
    hn_ref[...] = hn
    logits = jnp.dot(hn, rw_ref[...], preferred_element_type=F32, precision=HIGHEST) + rb_ref[...]
    lane = lax.broadcasted_iota(jnp.int32, logits.shape, 1)
    logits = jnp.where(lane < N_EXPERTS, logits, -jnp.inf)
    idx_out = jnp.zeros(logits.shape, jnp.int32)
    val_out = jnp.zeros(logits.shape, F32)
    v0 = None
    for j in range(TOP_K):
        m = jnp.max(logits, axis=-1, keepdims=True)
        sel = jnp.min(jnp.where(logits == m, lane, ROUTE_PAD), axis=-1, keepdims=True)
        if j == 0:
            v0 = m
        idx_out = jnp.where(lane == j, sel, idx_out)
        val_out = jnp.where(lane == j, jnp.exp(m - v0), val_out)
        logits = jnp.where(lane == sel, -jnp.inf, logits)
    idx_ref[...] = idx_out
    wts_ref[...] = val_out / jnp.sum(val_out, axis=-1, keepdims=True)


def _merge(o_gla, o_hy, gates, x2, wbg, wbh, wo, norm2_w, rw_pad, rb_pad):
    T = x2.shape[0]
    tm = min(512, T)
    row = lambda n: pl.BlockSpec((tm, n), lambda i: (i, 0))
    return pl.pallas_call(
        _merge_kernel,
        grid=(T // tm,),
        in_specs=[row(GLA_VAL), row(HY_WIDTH), row(N_GATES), row(D_MODEL),
                  _const_spec((GLA_VAL, D_MODEL)), _const_spec((HY_WIDTH, D_MODEL)), _const_spec((D_MODEL, D_MODEL)),
                  _const_spec((1, D_MODEL)), _const_spec((D_MODEL, ROUTE_PAD)), _const_spec((1, ROUTE_PAD))],
        out_specs=[row(D_MODEL), row(D_MODEL), row(ROUTE_PAD), row(ROUTE_PAD)],
        out_shape=[jax.ShapeDtypeStruct((T, D_MODEL), F32), jax.ShapeDtypeStruct((T, D_MODEL), F32),
                   jax.ShapeDtypeStruct((T, ROUTE_PAD), jnp.int32), jax.ShapeDtypeStruct((T, ROUTE_PAD), F32)],
        compiler_params=_cparams("parallel"),
        name="merge_route",
    )(o_gla, o_hy, gates, x2, wbg, wbh, wo, norm2_w.reshape(1, D_MODEL), rw_pad, rb_pad)


MOE_TM = 512
COMBINE_TM = 256


def _gather_kernel(nused_ref, idx_ref, src_ref, o_ref, sem):
    i = pl.program_id(0)
    R = o_ref.shape[0]

    @pl.when(i < nused_ref[0])
    def _():
        def issue(r, c):
            pltpu.make_async_copy(src_ref.at[pl.ds(idx_ref[0, 0, r], 1)], o_ref.at[pl.ds(r, 1)], sem).start()
            return c
        lax.fori_loop(0, R, issue, 0)
        pltpu.make_async_copy(src_ref.at[pl.ds(0, R)], o_ref, sem).wait()

    @pl.when(i >= nused_ref[0])
    def _():
        o_ref[...] = jnp.zeros_like(o_ref)


def _gather_rows(src, row_idx, n_used, P):
    R = MOE_TM
    nt = P // R
    return pl.pallas_call(
        _gather_kernel,
        grid_spec=pltpu.PrefetchScalarGridSpec(
            num_scalar_prefetch=1,
            grid=(nt,),
            in_specs=[pl.BlockSpec((1, 1, R), lambda i, nu: (i, 0, 0), memory_space=pltpu.SMEM),
                      pl.BlockSpec(memory_space=pl.ANY)],
            out_specs=pl.BlockSpec((R, D_MODEL), lambda i, nu: (i, 0)),
            scratch_shapes=[pltpu.SemaphoreType.DMA(())],
        ),
        out_shape=jax.ShapeDtypeStruct((P, D_MODEL), F32),
        compiler_params=_cparams("arbitrary"),
        name="moe_gather",
    )(n_used, row_idx.reshape(nt, 1, R), src)


def _expert_kernel(te_ref, nused_ref, xs_ref, wgu_ref, bgu_ref, wd_ref, bd_ref, o_ref, wgu_s, wd_s):
    i = pl.program_id(0)
    prev = te_ref[jnp.maximum(i - 1, 0)]

    @pl.when((i == 0) | (te_ref[i] != prev))
    def _():
        wgu_s[...] = wgu_ref[0].astype(BF16)
        wd_s[...] = wd_ref[0].astype(BF16)

    @pl.when(i < nused_ref[0])
    def _():
        x = xs_ref[...].astype(BF16)
        gu = jnp.dot(x, wgu_s[...], preferred_element_type=F32) + bgu_ref[0]
        gate = jnp.minimum(gu[:, :D_FF], SWIGLU_LIMIT)
        up = jnp.clip(gu[:, D_FF:], -SWIGLU_LIMIT, SWIGLU_LIMIT)
        act = gate * jax.nn.sigmoid(SWIGLU_ALPHA * gate) * (up + 1.0)
        o_ref[...] = jnp.dot(act.astype(BF16), wd_s[...], preferred_element_type=F32) + bd_ref[0]

    @pl.when(i >= nused_ref[0])
    def _():
        o_ref[...] = jnp.zeros_like(o_ref)


def _experts(xs, tile_expert, n_used, w_gate_up, b_gate_up, w_down, b_down):
    P = xs.shape[0]
    tm = MOE_TM
    nt = P // tm
    live = lambda i, te, nu: jnp.minimum(i, nu[0] - 1)
    return pl.pallas_call(
        _expert_kernel,
        grid_spec=pltpu.PrefetchScalarGridSpec(
            num_scalar_prefetch=2,
            grid=(nt,),
            in_specs=[
                pl.BlockSpec((tm, D_MODEL), lambda i, te, nu: (live(i, te, nu), 0)),
                pl.BlockSpec((1, D_MODEL, 2 * D_FF), lambda i, te, nu: (te[i], 0, 0)),
                pl.BlockSpec((1, 1, 2 * D_FF), lambda i, te, nu: (te[i], 0, 0)),
                pl.BlockSpec((1, D_FF, D_MODEL), lambda i, te, nu: (te[i], 0, 0)),
                pl.BlockSpec((1, 1, D_MODEL), lambda i, te, nu: (te[i], 0, 0)),
            ],
            out_specs=pl.BlockSpec((tm, D_MODEL), lambda i, te, nu: (i, 0)),
            scratch_shapes=[pltpu.VMEM((D_MODEL, 2 * D_FF), BF16), pltpu.VMEM((D_FF, D_MODEL), BF16)],
        ),
        out_shape=jax.ShapeDtypeStruct((P, D_MODEL), F32),
        compiler_params=_cparams("arbitrary"),
        name="moe_experts",
    )(tile_expert, n_used, xs, w_gate_up, b_gate_up.reshape(N_EXPERTS, 1, 2 * D_FF), w_down,
      b_down.reshape(N_EXPERTS, 1, D_MODEL))


def _combine_kernel(pos_ref, wts_ref, h1_ref, nf_ref, ys_ref, o_ref, buf, sem):
    tc = o_ref.shape[0]

    def issue(r, c):
        for j in range(TOP_K):
            pltpu.make_async_copy(ys_ref.at[pl.ds(pos_ref[0, 0, r * TOP_K + j], 1)], buf.at[j, pl.ds(r, 1)], sem).start()
        return c
    lax.fori_loop(0, tc, issue, 0)
    for j in range(TOP_K):
        pltpu.make_async_copy(ys_ref.at[pl.ds(0, tc)], buf.at[j], sem).wait()

    w = wts_ref[...]
    h = h1_ref[...]
    for j in range(TOP_K):
        h = h + w[:, j:j + 1] * buf[j]
    o_ref[...] = h * lax.rsqrt(jnp.mean(h * h, axis=-1, keepdims=True) + EPS) * nf_ref[...]


def _combine(pos, wts, h1, norm_f_w, ys):
    T = h1.shape[0]
    tc = min(COMBINE_TM, T)
    nt = T // tc
    return pl.pallas_call(
        _combine_kernel,
        grid=(nt,),
        in_specs=[
            pl.BlockSpec((1, 1, tc * TOP_K), lambda i: (i, 0, 0), memory_space=pltpu.SMEM),
            pl.BlockSpec((tc, ROUTE_PAD), lambda i: (i, 0)),
            pl.BlockSpec((tc, D_MODEL), lambda i: (i, 0)),
            _const_spec((1, D_MODEL)),
            pl.BlockSpec(memory_space=pl.ANY),
        ],
        out_specs=pl.BlockSpec((tc, D_MODEL), lambda i: (i, 0)),
        out_shape=jax.ShapeDtypeStruct((T, D_MODEL), F32),
        scratch_shapes=[pltpu.VMEM((TOP_K, tc, D_MODEL), F32), pltpu.SemaphoreType.DMA(())],
        compiler_params=_cparams("arbitrary"),
        name="moe_combine",
    )(pos.reshape(nt, 1, tc * TOP_K), wts, h1, norm_f_w.reshape(1, D_MODEL), ys)


def _route_plan(top_idx, T):
    tm = MOE_TM
    nt = (T * TOP_K) // tm + N_EXPERTS
    e = top_idx.reshape(-1)
    oh = (e[:, None] == jnp.arange(N_EXPERTS, dtype=jnp.int32)[None, :]).astype(jnp.int32)
    csum = jnp.cumsum(oh, axis=0)
    rank = jnp.take_along_axis(csum, e[:, None], axis=1)[:, 0] - 1
    counts = csum[-1]
    tiles = (counts + tm - 1) // tm
    tile_end = jnp.cumsum(tiles)
    offs = (tile_end - tiles) * tm
    pos = offs[e] + rank
    tok = jnp.arange(T * TOP_K, dtype=jnp.int32) // TOP_K
    row_tok = jnp.zeros((nt * tm,), jnp.int32).at[pos].set(tok)
    tile_expert = jnp.minimum(jnp.searchsorted(tile_end, jnp.arange(nt, dtype=jnp.int32), side="right"),
                              N_EXPERTS - 1).astype(jnp.int32)
    n_used = tile_end[-1:].astype(jnp.int32)
    return pos.astype(jnp.int32), row_tok, tile_expert, n_used, nt * tm


def kernel(x, norm1_w, w_in, gla_gate_w2, gla_gate_b, gla_norm_w, hy_conv_w, hy_conv_b, hy_freq, hy_ffn_w1, hy_ffn_b1,
           hy_ffn_w2, hy_ffn_b2, hy_ffn_w3, hy_decay, hy_bias, w_branch_gla, w_branch_hy, w_out, norm2_w, router_w,
           router_b, w_gate_up, b_gate_up, w_down, b_down, norm_f_w):
    B, L, D = x.shape
    T = B * L
    x2 = x.reshape(T, D)

    s_a = N_QKVR
    s_h = s_a + 2 * GLA_RANK
    w_cat = jnp.concatenate(
        [w_in[:, :s_a], w_in[:, s_h:], w_in[:, s_a:s_h], jnp.zeros((D, ALOW_PAD - 2 * GLA_RANK), w_in.dtype)],
        axis=1).astype(BF16)
    qkvr, hy, gates, alow = _in_proj(x2, norm1_w, w_cat)

    w2 = jnp.zeros((2, ALOW_PAD, GLA_KEY), F32)
    w2 = w2.at[0, :GLA_RANK].set(gla_gate_w2[0]).at[1, GLA_RANK:2 * GLA_RANK].set(gla_gate_w2[1])
    o_gla = _gla(qkvr, alow, w2, gla_gate_b.reshape(2, 1, GLA_KEY), gla_norm_w, B, L)

    taps = _hy_filter(L, hy_freq, hy_ffn_w1, hy_ffn_b1, hy_ffn_w2, hy_ffn_b2, hy_ffn_w3, hy_decay)
    tabs = _dft_tables(L)
    sd = jnp.stack([taps[0] + taps[1], taps[0] - taps[1]]).astype(BF16)
    dummy = jnp.zeros((tabs["kfp"], HY_WIDTH), F32)
    fr, fi = _dft_fwd(tabs, sd, dummy, dummy, False, F32)
    z, x0c = _hy_pre(hy, hy_conv_w, hy_conv_b, B, L)
    z3 = z.reshape(B, L, HY_WIDTH)
    yr, yi = _dft_fwd(tabs, z3, fr[0], fi[1], True, BF16)
    o_hy = _dft_inv(tabs, yr, yi, z3, x0c.reshape(B, L, HY_WIDTH), hy_bias).reshape(T, HY_WIDTH)

    rw_pad = jnp.zeros((D, ROUTE_PAD), F32).at[:, :N_EXPERTS].set(router_w)
    rb_pad = jnp.zeros((1, ROUTE_PAD), F32).at[0, :N_EXPERTS].set(router_b)
    h1, hn, idx_pad, wts_pad = _merge(o_gla, o_hy, gates, x2, w_branch_gla.astype(BF16), w_branch_hy.astype(BF16),
                                      w_out.astype(BF16), norm2_w, rw_pad, rb_pad)

    pos, row_tok, tile_expert, n_used, P = _route_plan(idx_pad[:, :TOP_K], T)
    xs = _gather_rows(hn, row_tok, n_used, P)
    ys = _experts(xs, tile_expert, n_used, w_gate_up, b_gate_up, w_down, b_down)
    out = _combine(pos, wts_pad, h1, norm_f_w, ys)
    return out.reshape(B, L, D)
```

```python
import functools
import math

import numpy as np
import jax
import jax.numpy as jnp
from jax import lax
from jax.experimental import pallas as pl
from jax.experimental.pallas import tpu as pltpu

D_MODEL = 1024
EPS = 1e-6
GLA_HEADS = 4
GLA_DK = 128
GLA_DV = 256
GLA_KEY = GLA_HEADS * GLA_DK
GLA_VAL = GLA_HEADS * GLA_DV
GLA_RANK = 16
GLA_GATE_TEMP = 16.0
GLA_CHUNK = 64
HY_WIDTH = D_MODEL
HY_BANDS = 16
HY_EMB = 1 + 2 * HY_BANDS
HY_FFN = 64
N_EXPERTS = 32
TOP_K = 4
D_FF = D_MODEL
SWIGLU_LIMIT = 7.0
SWIGLU_ALPHA = 1.702

LANES = 128
ALOW_PAD = LANES
ROUTE_PAD = LANES
VMEM_LIMIT = 56 * 1024 * 1024

F32 = jnp.float32
BF16 = jnp.bfloat16
HIGHEST = lax.Precision.HIGHEST


def _cparams(*sem):
    return pltpu.CompilerParams(dimension_semantics=sem, vmem_limit_bytes=VMEM_LIMIT)


def _const_spec(shape):
    nd = len(shape)
    return pl.BlockSpec(shape, lambda *_: (0,) * nd, pipeline_mode=pl.Buffered(1))


TOK_ROWS = D_MODEL // LANES


def _store_token_rows(ref, val):
    n = val.shape[0]
    for s_ in range(TOK_ROWS):
        ref[pl.ds(s_, n, stride=TOK_ROWS), :] = val[:, s_ * LANES:(s_ + 1) * LANES]


def _load_token_rows(ref, first, n):
    return jnp.concatenate(
        [ref[pl.ds(first * TOK_ROWS + s_, n, stride=TOK_ROWS), :] for s_ in range(TOK_ROWS)], axis=1)


N_QKVR = 2 * GLA_KEY + 2 * GLA_VAL
N_HY = 3 * HY_WIDTH
N_GATES = 2 * D_MODEL
IN_COLS = N_QKVR + N_HY + N_GATES + ALOW_PAD


def _inproj_kernel(x_ref, nw_ref, w_ref, qkvr_ref, hy_ref, gates_ref, alow_ref):
    x = x_ref[...]
    ms = jnp.mean(x * x, axis=-1, keepdims=True)
    xn = (x * lax.rsqrt(ms + EPS) * nw_ref[...]).astype(BF16)
    c0, c1, c2 = N_QKVR, N_QKVR + N_HY, N_QKVR + N_HY + N_GATES
    qkvr_ref[...] = jnp.dot(xn, w_ref[:, :c0], preferred_element_type=F32).astype(BF16)
    hy_ref[...] = jnp.dot(xn, w_ref[:, c0:c1], preferred_element_type=F32).astype(BF16)
    gates_ref[...] = jnp.dot(xn, w_ref[:, c1:c2], preferred_element_type=F32).astype(BF16)
    alow_ref[...] = jnp.dot(xn, w_ref[:, c2:], preferred_element_type=F32)


def _in_proj(x2, norm1_w, w_cat):
    T = x2.shape[0]
    tm = min(512, T)
    return pl.pallas_call(
        _inproj_kernel,
        grid=(T // tm,),
        in_specs=[
            pl.BlockSpec((tm, D_MODEL), lambda i: (i, 0)),
            _const_spec((1, D_MODEL)),
            _const_spec((D_MODEL, IN_COLS)),
        ],
        out_specs=[
            pl.BlockSpec((tm, N_QKVR), lambda i: (i, 0)),
            pl.BlockSpec((tm, N_HY), lambda i: (i, 0)),
            pl.BlockSpec((tm, N_GATES), lambda i: (i, 0)),
            pl.BlockSpec((tm, ALOW_PAD), lambda i: (i, 0)),
        ],
        out_shape=[
            jax.ShapeDtypeStruct((T, N_QKVR), BF16),
            jax.ShapeDtypeStruct((T, N_HY), BF16),
            jax.ShapeDtypeStruct((T, N_GATES), BF16),
            jax.ShapeDtypeStruct((T, ALOW_PAD), F32),
        ],
        compiler_params=_cparams("parallel"),
        name="in_proj",
    )(x2, norm1_w.reshape(1, D_MODEL), w_cat)


GLA_PREP_ROWS = 256


def _log_sigmoid(x):
    return jnp.minimum(x, 0.0) - jnp.log1p(jnp.exp(-jnp.abs(x)))


def _gla_kernel(q_ref, k_ref, v_ref, r_ref, alow_ref, w2_ref, gb_ref, nw_ref, o_ref,
                qtf, kef, qtb, keb, decf, decb, of_s, ob_s):
    L = q_ref.shape[0]
    C = GLA_CHUNK
    DK = GLA_DK
    n_chunks = L // C
    R = min(GLA_PREP_ROWS, L)
    cpr = R // C
    scale = DK ** -0.5
    nt = (((1,), (1,)), ((), ()))
    tn = (((0,), (0,)), ((), ()))

    ri = lax.broadcasted_iota(jnp.int32, (R, R), 0)
    ci = lax.broadcasted_iota(jnp.int32, (R, R), 1)
    same = (ri // C) == (ci // C)
    mask_f = same & (ri >= ci)
    mask_b = same & (ri < ci)
    tri = jnp.where(mask_f, 1.0, 0.0).astype(BF16)
    w2 = jnp.concatenate([w2_ref[0], w2_ref[1]], axis=1).astype(BF16)
    gb = jnp.concatenate([gb_ref[0], gb_ref[1]], axis=1)

    def prep(g, carry):
        rows = pl.ds(pl.multiple_of(g * R, R), R)
        logits = jnp.dot(alow_ref[rows, :].astype(BF16), w2, preferred_element_type=F32) + gb
        lg = _log_sigmoid(logits) * (1.0 / GLA_GATE_TEMP)
        hi = lg.astype(BF16)
        lo = (lg - hi.astype(F32)).astype(BF16)
        b = jnp.dot(tri, hi, preferred_element_type=F32) + jnp.dot(tri, lo, preferred_element_type=F32)
        tot = jnp.concatenate(
            [jnp.broadcast_to(b[c * C + C - 1:c * C + C, :], (C, 2 * DK)) for c in range(cpr)], axis=0)
        q = q_ref[rows, :].astype(F32) * scale
        k = k_ref[rows, :].astype(F32)
        vb = v_ref[rows, :]
        dtot = jnp.exp(tot)
        for d, (qt, ke, dec, mask, out) in enumerate(((qtf, kef, decf, mask_f, of_s), (qtb, keb, decb, mask_b, ob_s))):
            sl = slice(d * DK, (d + 1) * DK)
            bd, td, ld = b[:, sl], tot[:, sl], lg[:, sl]
            if d == 0:
                e_q, e_end = bd, td - bd
            else:
                e_q, e_end = td - bd + ld, bd - ld
            qtv = (q * jnp.exp(e_q)).astype(BF16)
            ktv = (k * jnp.exp(-e_q)).astype(BF16)
            qt[rows, :] = qtv
            ke[rows, :] = (k * jnp.exp(e_end)).astype(BF16)
            s = lax.dot_general(qtv, ktv, nt, preferred_element_type=F32)
            s = jnp.where(mask, s, 0.0).astype(BF16)
            out[rows, :] = jnp.dot(s, vb, preferred_element_type=F32)
            for c in range(cpr):
                dec[pl.ds(g * cpr + c, 1), :] = dtot[c * C:c * C + 1, sl]
        return carry

    lax.fori_loop(0, L // R, prep, 0)

    def chunk(n, st, qt, ke, dec, out):
        rows = pl.ds(pl.multiple_of(n * C, C), C)
        out[rows, :] += lax.dot_general(qt[rows, :], st.astype(BF16), nt, preferred_element_type=F32)
        upd = lax.dot_general(v_ref[rows, :], ke[rows, :], tn, preferred_element_type=F32)
        return dec[pl.ds(n, 1), :] * st + upd

    def step(i, carry):
        sf, sb = carry
        sf = chunk(i, sf, qtf, kef, decf, of_s)
        sb = chunk(n_chunks - 1 - i, sb, qtb, keb, decb, ob_s)
        return sf, sb

    z = jnp.zeros((GLA_DV, DK), F32)
    lax.fori_loop(0, n_chunks, step, (z, z), unroll=2)

    o = of_s[...] + ob_s[...]
    o = o * lax.rsqrt(jnp.mean(o * o, axis=-1, keepdims=True) + EPS) * nw_ref[...]
    r = r_ref[...].astype(F32)
    o_ref[...] = (o * (r * jax.nn.sigmoid(r))).astype(BF16)


def _gla(qkvr, alow, w2, gb, norm_w, B, L):
    T = B * L
    H = GLA_HEADS
    kq = GLA_KEY // GLA_DK
    kv = 2 * GLA_KEY // GLA_DV
    kr = (2 * GLA_KEY + GLA_VAL) // GLA_DV
    return pl.pallas_call(
        _gla_kernel,
        grid=(B, H),
        in_specs=[
            pl.BlockSpec((L, GLA_DK), lambda b, h: (b, h)),
            pl.BlockSpec((L, GLA_DK), lambda b, h: (b, kq + h)),
            pl.BlockSpec((L, GLA_DV), lambda b, h: (b, kv + h)),
            pl.BlockSpec((L, GLA_DV), lambda b, h: (b, kr + h)),
            pl.BlockSpec((L, ALOW_PAD), lambda b, h: (b, 0)),
            pl.BlockSpec((2, ALOW_PAD, GLA_DK), lambda b, h: (0, 0, h)),
            pl.BlockSpec((2, 1, GLA_DK), lambda b, h: (0, 0, h)),
            pl.BlockSpec((1, GLA_DV), lambda b, h: (0, 0)),
        ],
        out_specs=pl.BlockSpec((L, GLA_DV), lambda b, h: (b, h)),
        out_shape=jax.ShapeDtypeStruct((T, GLA_VAL), BF16),
        scratch_shapes=[pltpu.VMEM((L, GLA_DK), BF16)] * 4
        + [pltpu.VMEM((L // GLA_CHUNK, GLA_DK), F32)] * 2
        + [pltpu.VMEM((L, GLA_DV), F32)] * 2,
        compiler_params=_cparams("parallel", "parallel"),
        name="gla",
    )(qkvr, qkvr, qkvr, qkvr, alow, w2, gb, norm_w.reshape(1, GLA_DV))


HY_WT = 256


def _hy_filter_kernel(z_ref, t_ref, fr_ref, w1_ref, b1_ref, w2_ref, b2_ref, w3f_ref, w3b_ref, dec_ref, o_ref):
    fr = fr_ref[...]
    h = jnp.sin(fr * (jnp.dot(z_ref[...], w1_ref[...], preferred_element_type=F32, precision=HIGHEST) + b1_ref[...]))
    h = jnp.sin(fr * (jnp.dot(h, w2_ref[...], preferred_element_type=F32, precision=HIGHEST) + b2_ref[...]))
    t = t_ref[...]
    row = lax.broadcasted_iota(jnp.int32, (z_ref.shape[0], 1), 0)
    hf = jnp.dot(h, w3f_ref[...], preferred_element_type=F32, precision=HIGHEST)
    hf = hf * jnp.exp(-t * jnp.abs(dec_ref[0:1, :]))
    hb = jnp.dot(h, w3b_ref[...], preferred_element_type=F32, precision=HIGHEST)
    hb = hb * jnp.exp(-t * jnp.abs(dec_ref[1:2, :]))
    hb = jnp.where(row == 0, 0.0, hb)
    ss = jnp.sum(hf * hf, axis=0, keepdims=True) + jnp.sum(hb * hb, axis=0, keepdims=True)
    inv = lax.rsqrt(ss + EPS)
    o_ref[0] = hf * inv
    o_ref[1] = hb * inv


def _hy_filter(L, hy_freq, w1, b1, w2, b2, w3, hy_decay):
    W = HY_WIDTH
    t = np.linspace(0.0, 1.0, L, dtype=np.float32)[:, None]
    omega = (np.float32(2.0 * math.pi) * np.arange(L, dtype=np.float32) / np.float32(L)).astype(np.float32)
    f = np.linspace(1e-4, HY_BANDS - 1, HY_BANDS, dtype=np.float32)
    ang = (omega[:, None] * f[None, :]).astype(np.float32)
    z = np.concatenate([t, np.cos(ang), -np.sin(ang)], axis=-1).astype(np.float32)
    nw = W // HY_WT
    return pl.pallas_call(
        _hy_filter_kernel,
        grid=(nw,),
        in_specs=[
            _const_spec((L, HY_EMB)),
            _const_spec((L, 1)),
            _const_spec((1, HY_FFN)),
            _const_spec((HY_EMB, HY_FFN)),
            _const_spec((1, HY_FFN)),
            _const_spec((HY_FFN, HY_FFN)),
            _const_spec((1, HY_FFN)),
            pl.BlockSpec((HY_FFN, HY_WT), lambda j: (0, j)),
            pl.BlockSpec((HY_FFN, HY_WT), lambda j: (0, nw + j)),
            pl.BlockSpec((2, HY_WT), lambda j: (0, j)),
        ],
        out_specs=pl.BlockSpec((2, L, HY_WT), lambda j: (0, 0, j)),
        out_shape=jax.ShapeDtypeStruct((2, L, W), F32),
        compiler_params=_cparams("parallel"),
        name="hy_filter",
    )(jnp.asarray(z), jnp.asarray(t), hy_freq.reshape(1, HY_FFN), w1, b1.reshape(1, HY_FFN), w2,
      b2.reshape(1, HY_FFN), w3, w3, hy_decay)


def _hy_pre_kernel(x0_ref, x1_ref, v_ref, w0_ref, w1_ref, wv_ref, b0_ref, b1_ref, bv_ref, z_ref, x0o_ref):
    L = x0_ref.shape[0]
    row = lax.broadcasted_iota(jnp.int32, (L, 1), 0)

    def conv3(u_ref, w_ref, b_ref):
        u = u_ref[...].astype(F32)
        prev = jnp.where(row == 0, 0.0, pltpu.roll(u, 1, 0))
        nxt = jnp.where(row == L - 1, 0.0, pltpu.roll(u, L - 1, 0))
        return w_ref[0:1, :] * prev + w_ref[1:2, :] * u + w_ref[2:3, :] * nxt + b_ref[...]

    x0o_ref[...] = conv3(x0_ref, w0_ref, b0_ref).astype(BF16)
    z_ref[...] = (conv3(v_ref, wv_ref, bv_ref) * conv3(x1_ref, w1_ref, b1_ref)).astype(BF16)


def _hy_pre(hy, conv_w, conv_b, B, L):
    W = HY_WIDTH
    nw = W // HY_WT
    cb = conv_b.reshape(1, 3 * W)
    blk = lambda off: pl.BlockSpec((L, HY_WT), lambda b, j: (b, off + j))
    wblk = lambda off: pl.BlockSpec((3, HY_WT), lambda b, j: (0, off + j))
    bblk = lambda off: pl.BlockSpec((1, HY_WT), lambda b, j: (0, off + j))
    return pl.pallas_call(
        _hy_pre_kernel,
        grid=(B, nw),
        in_specs=[blk(0), blk(nw), blk(2 * nw), wblk(0), wblk(nw), wblk(2 * nw), bblk(0), bblk(nw), bblk(2 * nw)],
        out_specs=[pl.BlockSpec((L, HY_WT), lambda b, j: (b, j))] * 2,
        out_shape=[jax.ShapeDtypeStruct((B * L, W), BF16)] * 2,
        compiler_params=_cparams("parallel", "parallel"),
        name="hy_pre",
    )(hy, hy, hy, conv_w, conv_w, conv_w, cb, cb, cb)


DFT_TK = 256
DFT_TT = 512
DFT_WT = 512


def _dft_tables(L):
    N = 2 * L
    nf = L + 1
    kfp = -(-nf // DFT_TK) * DFT_TK
    ang = lambda a, b: (2.0 * np.pi / N) * ((np.outer(a, b)) % N)
    i = np.arange(DFT_TK)
    n = np.arange(L)
    k = np.arange(kfp)
    k0 = np.arange(0, kfp, DFT_TK)
    t0 = np.arange(0, L, DFT_TK)
    alpha = np.where((k == 0) | (k == L), 1.0, 2.0) / N
    alpha = np.where(k <= L, alpha, 0.0)
    f32 = lambda a: jnp.asarray(a.astype(np.float32))
    return dict(
        kfp=kfp,
        fwd_e=(f32(np.cos(ang(i, n))), f32(np.sin(ang(i, n)))),
        fwd_r=(f32(np.cos(ang(k0, n))), f32(np.sin(ang(k0, n)))),
        inv_e=(f32(np.cos(ang(i, k))), f32(np.sin(ang(i, k)))),
        inv_r=(f32(np.cos(ang(t0, k))), f32(np.sin(ang(t0, k)))),
        alpha=f32(alpha[None, :]),
    )


def _dft_fwd_kernel(ec_ref, es_ref, rc_ref, rs_ref, z_ref, kr_ref, ki_ref, yr_ref, yi_ref, fc_s, fs_s, *, use_filter):
    @pl.when(pl.program_id(1) == 0)
    def _():
        ec, es, rc, rs = ec_ref[...], es_ref[...], rc_ref[0], rs_ref[0]
        fc_s[...] = (ec * rc - es * rs).astype(BF16)
        fs_s[...] = (es * rc + ec * rs).astype(BF16)

    z = z_ref[...]
    xr = jnp.dot(fc_s[...], z, preferred_element_type=F32)
    xi = -jnp.dot(fs_s[...], z, preferred_element_type=F32)
    if use_filter:
        kr, ki = kr_ref[...], ki_ref[...]
        yr_ref[...] = (xr * kr - xi * ki).astype(yr_ref.dtype)
        yi_ref[...] = (xr * ki + xi * kr).astype(yi_ref.dtype)
    else:
        yr_ref[...] = xr.astype(yr_ref.dtype)
        yi_ref[...] = xi.astype(yi_ref.dtype)


def _dft_fwd(tabs, z3, kr, ki, use_filter, out_dtype):
    Bz, L, W = z3.shape
    kfp = tabs["kfp"]
    nk = kfp // DFT_TK
    (ec, es), (rc, rs) = tabs["fwd_e"], tabs["fwd_r"]
    return pl.pallas_call(
        functools.partial(_dft_fwd_kernel, use_filter=use_filter),
        grid=(nk, Bz),
        in_specs=[
            _const_spec((DFT_TK, L)),
            _const_spec((DFT_TK, L)),
            pl.BlockSpec((1, 1, L), lambda k, b: (k, 0, 0)),
            pl.BlockSpec((1, 1, L), lambda k, b: (k, 0, 0)),
            pl.BlockSpec((None, L, W), lambda k, b: (b, 0, 0)),
            pl.BlockSpec((DFT_TK, W), lambda k, b: (k, 0)),
            pl.BlockSpec((DFT_TK, W), lambda k, b: (k, 0)),
        ],
        out_specs=[pl.BlockSpec((None, DFT_TK, W), lambda k, b: (b, k, 0))] * 2,
        out_shape=[jax.ShapeDtypeStruct((Bz, kfp, W), out_dtype)] * 2,
        scratch_shapes=[pltpu.VMEM((DFT_TK, L), BF16)] * 2,
        compiler_params=_cparams("arbitrary", "arbitrary"),
        name="dft_fwd_filter" if use_filter else "dft_fwd",
    )(ec, es, rc.reshape(nk, 1, L), rs.reshape(nk, 1, L), z3, kr, ki)


def _dft_inv_kernel(ec_ref, es_ref, rc_ref, rs_ref, al_ref, yr_ref, yi_ref, z_ref, x0_ref, bias_ref, o_ref, gc_s, gs_s):
    @pl.when((pl.program_id(1) == 0) & (pl.program_id(2) == 0))
    def _():
        ec, es, al = ec_ref[...], es_ref[...], al_ref[...]
        for s in range(DFT_TT // DFT_TK):
            rc, rs = rc_ref[s], rs_ref[s]
            rows = slice(s * DFT_TK, (s + 1) * DFT_TK)
            gc_s[rows, :] = (al * (ec * rc - es * rs)).astype(BF16)
            gs_s[rows, :] = (-al * (es * rc + ec * rs)).astype(BF16)

    y = jnp.dot(gc_s[...], yr_ref[...], preferred_element_type=F32)
    y = y + jnp.dot(gs_s[...], yi_ref[...], preferred_element_type=F32)
    z = z_ref[...].astype(F32)
    o_ref[...] = ((y + z * bias_ref[...]) * x0_ref[...].astype(F32)).astype(BF16)


def _dft_inv(tabs, yr, yi, z3, x03, hy_bias):
    B, L, W = z3.shape
    kfp = tabs["kfp"]
    (ec, es), (rc, rs) = tabs["inv_e"], tabs["inv_r"]
    tt = min(DFT_TT, L)
    sub = tt // DFT_TK
    nt = L // tt
    wt = DFT_WT
    return pl.pallas_call(
        _dft_inv_kernel,
        grid=(nt, B, W // wt),
        in_specs=[
            _const_spec((DFT_TK, kfp)),
            _const_spec((DFT_TK, kfp)),
            pl.BlockSpec((sub, 1, kfp), lambda t, b, w: (t, 0, 0)),
            pl.BlockSpec((sub, 1, kfp), lambda t, b, w: (t, 0, 0)),
            _const_spec((1, kfp)),
            pl.BlockSpec((None, kfp, wt), lambda t, b, w: (b, 0, w)),
            pl.BlockSpec((None, kfp, wt), lambda t, b, w: (b, 0, w)),
            pl.BlockSpec((None, tt, wt), lambda t, b, w: (b, t, w)),
            pl.BlockSpec((None, tt, wt), lambda t, b, w: (b, t, w)),
            pl.BlockSpec((1, wt), lambda t, b, w: (0, w)),
        ],
        out_specs=pl.BlockSpec((None, tt, wt), lambda t, b, w: (b, t, w)),
        out_shape=jax.ShapeDtypeStruct((B, L, W), BF16),
        scratch_shapes=[pltpu.VMEM((tt, kfp), BF16)] * 2,
        compiler_params=_cparams("arbitrary", "arbitrary", "arbitrary"),
        name="dft_inv",
    )(ec, es, rc.reshape(L // DFT_TK, 1, kfp), rs.reshape(L // DFT_TK, 1, kfp), tabs["alpha"], yr, yi, z3, x03,
      hy_bias.reshape(1, W))


def _merge_kernel(og_ref, oh_ref, gates_ref, x_ref, wbg_ref, wbh_ref, wo_ref, n2_ref, rw_ref, rb_ref,
                  h1_ref, hn3_ref, idx_ref, wts_ref):
    mg = jnp.dot(og_ref[...], wbg_ref[...], preferred_element_type=F32)
    mh = jnp.dot(oh_ref[...], wbh_ref[...], preferred_element_type=F32)
    gg = jax.nn.sigmoid(gates_ref[:, :D_MODEL].astype(F32))
    gh = jax.nn.sigmoid(gates_ref[:, D_MODEL:].astype(F32))
    merged = (gg * mg + gh * mh).astype(BF16)
    h1 = x_ref[...] + jnp.dot(merged, wo_ref[...], preferred_element_type=F32)
    h1_ref[...] = h1
    hn = h1 * lax.rsqrt(jnp.mean(h1 * h1, axis=-1, keepdims=True) + EPS) * n2_ref[...]
    _store_token_rows(hn3_ref, hn)
    logits = jnp.dot(hn, rw_ref[...], preferred_element_type=F32, precision=HIGHEST) + rb_ref[...]
    lane = lax.broadcasted_iota(jnp.int32, logits.shape, 1)
    logits = jnp.where(lane < N_EXPERTS, logits, -jnp.inf)
    idx_out = jnp.zeros(logits.shape, jnp.int32)
    val_out = jnp.zeros(logits.shape, F32)
    v0 = None
    for j in range(TOP_K):
        m = jnp.max(logits, axis=-1, keepdims=True)
        sel = jnp.min(jnp.where(logits == m, lane, ROUTE_PAD), axis=-1, keepdims=True)
        if j == 0:
            v0 = m
        idx_out = jnp.where(lane == j, sel, idx_out)
        val_out = jnp.where(lane == j, jnp.exp(m - v0), val_out)
        logits = jnp.where(lane == sel, -jnp.inf, logits)
    idx_ref[...] = idx_out
    wts_ref[...] = val_out / jnp.sum(val_out, axis=-1, keepdims=True)


def _merge(o_gla, o_hy, gates, x2, wbg, wbh, wo, norm2_w, rw_pad, rb_pad):
    T = x2.shape[0]
    tm = min(512, T)
    row = lambda n: pl.BlockSpec((tm, n), lambda i: (i, 0))
    return pl.pallas_call(
        _merge_kernel,
        grid=(T // tm,),
        in_specs=[row(GLA_VAL), row(HY_WIDTH), row(N_GATES), row(D_MODEL),
                  _const_spec((GLA_VAL, D_MODEL)), _const_spec((HY_WIDTH, D_MODEL)), _const_spec((D_MODEL, D_MODEL)),
                  _const_spec((1, D_MODEL)), _const_spec((D_MODEL, ROUTE_PAD)), _const_spec((1, ROUTE_PAD))],
        out_specs=[row(D_MODEL), pl.BlockSpec((tm * TOK_ROWS, LANES), lambda i: (i, 0)), row(ROUTE_PAD), row(ROUTE_PAD)],
        out_shape=[jax.ShapeDtypeStruct((T, D_MODEL), F32), jax.ShapeDtypeStruct((T * TOK_ROWS, LANES), F32),
                   jax.ShapeDtypeStruct((T, ROUTE_PAD), jnp.int32), jax.ShapeDtypeStruct((T, ROUTE_PAD), F32)],
        compiler_params=_cparams("parallel"),
        name="merge_route",
    )(o_gla, o_hy, gates, x2, wbg, wbh, wo, norm2_w.reshape(1, D_MODEL), rw_pad, rb_pad)


MOE_TM = 512
MOE_TC = 256
DMA_UNROLL = 8


def _dispatch_kernel(last_ref, pos_ref, hn_ref, xs_ref, zero_s, sem):
    tc = hn_ref.shape[0] // TOK_ROWS
    zrows = zero_s.shape[0]

    @pl.when(pl.program_id(0) == 0)
    def _():
        zero_s[...] = jnp.zeros_like(zero_s)
        n_fill = last_ref.shape[0]

        def zfill(e, c):
            @pl.when(last_ref[e] >= 0)
            def _():
                dst = pl.multiple_of(last_ref[e] * zrows, zrows)
                pltpu.make_async_copy(zero_s, xs_ref.at[pl.ds(dst, zrows)], sem).start()
            return c
        lax.fori_loop(0, n_fill, zfill, 0)

        def zwait(e, c):
            @pl.when(last_ref[e] >= 0)
            def _():
                pltpu.make_async_copy(zero_s, xs_ref.at[pl.ds(0, zrows)], sem).wait()
            return c
        lax.fori_loop(0, n_fill, zwait, 0)

    def issue(r, c):
        src = hn_ref.at[pl.ds(pl.multiple_of(r * TOK_ROWS, TOK_ROWS), TOK_ROWS)]
        for j in range(TOP_K):
            dst = pl.multiple_of(pos_ref[0, 0, r * TOP_K + j] * TOK_ROWS, TOK_ROWS)
            pltpu.make_async_copy(src, xs_ref.at[pl.ds(dst, TOK_ROWS)], sem).start(priority=j % 2)
        return c
    lax.fori_loop(0, tc, issue, 0, unroll=DMA_UNROLL // TOP_K)
    for j in range(TOP_K):
        pltpu.make_async_copy(hn_ref, xs_ref.at[pl.ds(0, tc * TOK_ROWS)], sem).wait()


def _dispatch(hn3, pos, last_tile, P):
    T = hn3.shape[0] // TOK_ROWS
    tc = min(MOE_TC, T)
    nt = T // tc
    return pl.pallas_call(
        _dispatch_kernel,
        grid_spec=pltpu.PrefetchScalarGridSpec(
            num_scalar_prefetch=1,
            grid=(nt,),
            in_specs=[pl.BlockSpec((1, 1, tc * TOP_K), lambda i, lt: (i, 0, 0), memory_space=pltpu.SMEM),
                      pl.BlockSpec((tc * TOK_ROWS, LANES), lambda i, lt: (i, 0))],
            out_specs=pl.BlockSpec(memory_space=pl.ANY),
            scratch_shapes=[pltpu.VMEM((MOE_TM * TOK_ROWS, LANES), F32), pltpu.SemaphoreType.DMA(())],
        ),
        out_shape=jax.ShapeDtypeStruct((P * TOK_ROWS, LANES), F32),
        compiler_params=_cparams("arbitrary"),
        name="moe_dispatch",
    )(last_tile, pos.reshape(nt, 1, tc * TOP_K), hn3)


def _expert_kernel(te_ref, nused_ref, xs_ref, wgu_ref, bgu_ref, wd_ref, bd_ref, o_ref, wgu_s, wd_s):
    i = pl.program_id(0)
    prev = te_ref[jnp.maximum(i - 1, 0)]

    @pl.when((i == 0) | (te_ref[i] != prev))
    def _():
        wgu_s[...] = wgu_ref[0].astype(BF16)
        wd_s[...] = wd_ref[0].astype(BF16)

    @pl.when(i < nused_ref[0])
    def _():
        x = _load_token_rows(xs_ref, 0, xs_ref.shape[0] // TOK_ROWS).astype(BF16)
        gu = jnp.dot(x, wgu_s[...], preferred_element_type=F32) + bgu_ref[0]
        gate = jnp.minimum(gu[:, :D_FF], SWIGLU_LIMIT)
        up = jnp.clip(gu[:, D_FF:], -SWIGLU_LIMIT, SWIGLU_LIMIT)
        act = gate * jax.nn.sigmoid(SWIGLU_ALPHA * gate) * (up + 1.0)
        y = jnp.dot(act.astype(BF16), wd_s[...], preferred_element_type=F32) + bd_ref[0]
        _store_token_rows(o_ref, y)

    @pl.when(i >= nused_ref[0])
    def _():
        o_ref[...] = jnp.zeros_like(o_ref)


def _experts(xs3, tile_expert, n_used, w_gate_up, b_gate_up, w_down, b_down):
    P = xs3.shape[0] // TOK_ROWS
    tm = MOE_TM
    nt = P // tm
    live = lambda i, te, nu: jnp.minimum(i, nu[0] - 1)
    return pl.pallas_call(
        _expert_kernel,
        grid_spec=pltpu.PrefetchScalarGridSpec(
            num_scalar_prefetch=2,
            grid=(nt,),
            in_specs=[
                pl.BlockSpec((tm * TOK_ROWS, LANES), lambda i, te, nu: (live(i, te, nu), 0)),
                pl.BlockSpec((1, D_MODEL, 2 * D_FF), lambda i, te, nu: (te[i], 0, 0)),
                pl.BlockSpec((1, 1, 2 * D_FF), lambda i, te, nu: (te[i], 0, 0)),
                pl.BlockSpec((1, D_FF, D_MODEL), lambda i, te, nu: (te[i], 0, 0)),
                pl.BlockSpec((1, 1, D_MODEL), lambda i, te, nu: (te[i], 0, 0)),
            ],
            out_specs=pl.BlockSpec((tm * TOK_ROWS, LANES), lambda i, te, nu: (i, 0)),
            scratch_shapes=[pltpu.VMEM((D_MODEL, 2 * D_FF), BF16), pltpu.VMEM((D_FF, D_MODEL), BF16)],
        ),
        out_shape=jax.ShapeDtypeStruct((P * TOK_ROWS, LANES), F32),
        compiler_params=_cparams("arbitrary"),
        name="moe_experts",
    )(tile_expert, n_used, xs3, w_gate_up, b_gate_up.reshape(N_EXPERTS, 1, 2 * D_FF), w_down,
      b_down.reshape(N_EXPERTS, 1, D_MODEL))


def _combine_kernel(pos_ref, posn_ref, wts_ref, h1_ref, nf_ref, ys_ref, o_ref, buf, sems):
    i = pl.program_id(0)
    n = pl.num_programs(0)
    tc = o_ref.shape[0]
    slot = lax.rem(i, 2)

    def gather(p_ref, s):
        def issue(r, c):
            for j in range(TOP_K):
                src = pl.multiple_of(p_ref[0, 0, r * TOP_K + j] * TOK_ROWS, TOK_ROWS)
                dst = pl.multiple_of((j * tc + r) * TOK_ROWS, TOK_ROWS)
                pltpu.make_async_copy(ys_ref.at[pl.ds(src, TOK_ROWS)], buf.at[s, pl.ds(dst, TOK_ROWS)],
                                      sems.at[s]).start(priority=j % 2)
            return c
        lax.fori_loop(0, tc, issue, 0, unroll=DMA_UNROLL // TOP_K)

    @pl.when(i == 0)
    def _():
        gather(pos_ref, 0)

    @pl.when(i + 1 < n)
    def _():
        gather(posn_ref, 1 - slot)

    pltpu.make_async_copy(ys_ref.at[pl.ds(0, TOP_K * tc * TOK_ROWS)], buf.at[slot], sems.at[slot]).wait()

    w = wts_ref[...]
    h = h1_ref[...]
    for j in range(TOP_K):
        h = h + w[:, j:j + 1] * _load_token_rows(buf.at[slot], j * tc, tc)
    o_ref[...] = h * lax.rsqrt(jnp.mean(h * h, axis=-1, keepdims=True) + EPS) * nf_ref[...]


def _combine(pos, wts, h1, norm_f_w, ys3):
    T = h1.shape[0]
    tc = min(MOE_TC, T)
    nt = T // tc
    pos3 = pos.reshape(nt, 1, tc * TOP_K)
    return pl.pallas_call(
        _combine_kernel,
        grid=(nt,),
        in_specs=[
            pl.BlockSpec((1, 1, tc * TOP_K), lambda i: (i, 0, 0), memory_space=pltpu.SMEM),
            pl.BlockSpec((1, 1, tc * TOP_K), lambda i: (jnp.minimum(i + 1, nt - 1), 0, 0), memory_space=pltpu.SMEM),
            pl.BlockSpec((tc, ROUTE_PAD), lambda i: (i, 0)),
            pl.BlockSpec((tc, D_MODEL), lambda i: (i, 0)),
            _const_spec((1, D_MODEL)),
            pl.BlockSpec(memory_space=pl.ANY),
        ],
        out_specs=pl.BlockSpec((tc, D_MODEL), lambda i: (i, 0)),
        out_shape=jax.ShapeDtypeStruct((T, D_MODEL), F32),
        scratch_shapes=[pltpu.VMEM((2, TOP_K * tc * TOK_ROWS, LANES), F32), pltpu.SemaphoreType.DMA((2,))],
        compiler_params=_cparams("arbitrary"),
        name="moe_combine",
    )(pos3, pos3, wts, h1, norm_f_w.reshape(1, D_MODEL), ys3)


def _route_plan(top_idx, T):
    tm = MOE_TM
    nt = (T * TOP_K) // tm + N_EXPERTS
    e = top_idx.reshape(-1)
    oh = (e[:, None] == jnp.arange(N_EXPERTS, dtype=jnp.int32)[None, :]).astype(jnp.int32)
    csum = jnp.cumsum(oh, axis=0)
    rank = jnp.sum(csum * oh, axis=1) - 1
    counts = csum[-1]
    tiles = (counts + tm - 1) // tm
    tile_end = jnp.cumsum(tiles)
    offs = (tile_end - tiles) * tm
    pos = jnp.sum(oh * offs[None, :], axis=1) + rank
    tile_id = jnp.arange(nt, dtype=jnp.int32)
    tile_expert = jnp.minimum(jnp.sum((tile_end[None, :] <= tile_id[:, None]).astype(jnp.int32), axis=1),
                              N_EXPERTS - 1).astype(jnp.int32)
    tail = tile_end[-1] + jnp.arange(N_EXPERTS, dtype=jnp.int32)
    last_tile = jnp.concatenate([jnp.where(tiles > 0, tile_end - 1, -1), jnp.where(tail < nt, tail, -1)]).astype(jnp.int32)
    n_used = tile_end[-1:].astype(jnp.int32)
    return pos.astype(jnp.int32), tile_expert, last_tile, n_used, nt * tm


def kernel(x, norm1_w, w_in, gla_gate_w2, gla_gate_b, gla_norm_w, hy_conv_w, hy_conv_b, hy_freq, hy_ffn_w1, hy_ffn_b1,
           hy_ffn_w2, hy_ffn_b2, hy_ffn_w3, hy_decay, hy_bias, w_branch_gla, w_branch_hy, w_out, norm2_w, router_w,
           router_b, w_gate_up, b_gate_up, w_down, b_down, norm_f_w):
    B, L, D = x.shape
    T = B * L
    x2 = x.reshape(T, D)

    s_a = N_QKVR
    s_h = s_a + 2 * GLA_RANK
    w_cat = jnp.concatenate(
        [w_in[:, :s_a], w_in[:, s_h:], w_in[:, s_a:s_h], jnp.zeros((D, ALOW_PAD - 2 * GLA_RANK), w_in.dtype)],
        axis=1).astype(BF16)
    qkvr, hy, gates, alow = _in_proj(x2, norm1_w, w_cat)

    w2 = jnp.zeros((2, ALOW_PAD, GLA_KEY), F32)
    w2 = w2.at[0, :GLA_RANK].set(gla_gate_w2[0]).at[1, GLA_RANK:2 * GLA_RANK].set(gla_gate_w2[1])
    o_gla = _gla(qkvr, alow, w2, gla_gate_b.reshape(2, 1, GLA_KEY), gla_norm_w, B, L)

    taps = _hy_filter(L, hy_freq, hy_ffn_w1, hy_ffn_b1, hy_ffn_w2, hy_ffn_b2, hy_ffn_w3, hy_decay)
    tabs = _dft_tables(L)
    sd = jnp.stack([taps[0] + taps[1], taps[0] - taps[1]]).astype(BF16)
    dummy = jnp.zeros((tabs["kfp"], HY_WIDTH), F32)
    fr, fi = _dft_fwd(tabs, sd, dummy, dummy, False, F32)
    z, x0c = _hy_pre(hy, hy_conv_w, hy_conv_b, B, L)
    z3 = z.reshape(B, L, HY_WIDTH)
    yr, yi = _dft_fwd(tabs, z3, fr[0], fi[1], True, BF16)
    o_hy = _dft_inv(tabs, yr, yi, z3, x0c.reshape(B, L, HY_WIDTH), hy_bias).reshape(T, HY_WIDTH)

    rw_pad = jnp.zeros((D, ROUTE_PAD), F32).at[:, :N_EXPERTS].set(router_w)
    rb_pad = jnp.zeros((1, ROUTE_PAD), F32).at[0, :N_EXPERTS].set(router_b)
    h1, hn3, idx_pad, wts_pad = _merge(o_gla, o_hy, gates, x2, w_branch_gla.astype(BF16), w_branch_hy.astype(BF16),
                                       w_out.astype(BF16), norm2_w, rw_pad, rb_pad)

    pos, tile_expert, last_tile, n_used, P = _route_plan(idx_pad[:, :TOP_K], T)
    xs3 = _dispatch(hn3, pos, last_tile, P)
    ys3 = _experts(xs3, tile_expert, n_used, w_gate_up, b_gate_up, w_down, b_down)
    out = _combine(pos, wts_pad, h1, norm_f_w, ys3)
    return out.reshape(B, L, D)
```

```python
import functools
import math

import numpy as np
import jax
import jax.numpy as jnp
from jax import lax
from jax.experimental import pallas as pl
from jax.experimental.pallas import tpu as pltpu

D_MODEL = 1024
EPS = 1e-6
GLA_HEADS = 4
GLA_DK = 128
GLA_DV = 256
GLA_KEY = GLA_HEADS * GLA_DK
GLA_VAL = GLA_HEADS * GLA_DV
GLA_RANK = 16
GLA_GATE_TEMP = 16.0
GLA_CHUNK = 64
HY_WIDTH = D_MODEL
HY_BANDS = 16
HY_EMB = 1 + 2 * HY_BANDS
HY_FFN = 64
N_EXPERTS = 32
TOP_K = 4
D_FF = D_MODEL
SWIGLU_LIMIT = 7.0
SWIGLU_ALPHA = 1.702

LANES = 128
ALOW_PAD = LANES
ROUTE_PAD = LANES
VMEM_LIMIT = 56 * 1024 * 1024

F32 = jnp.float32
BF16 = jnp.bfloat16
HIGHEST = lax.Precision.HIGHEST


def _cparams(*sem):
    return pltpu.CompilerParams(dimension_semantics=sem, vmem_limit_bytes=VMEM_LIMIT)


def _const_spec(shape):
    nd = len(shape)
    return pl.BlockSpec(shape, lambda *_: (0,) * nd, pipeline_mode=pl.Buffered(1))


TOK_ROWS = D_MODEL // LANES


def _store_token_rows(ref, val):
    n = val.shape[0]
    for s_ in range(TOK_ROWS):
        ref[pl.ds(s_, n, stride=TOK_ROWS), :] = val[:, s_ * LANES:(s_ + 1) * LANES]


def _load_token_rows(ref, first, n):
    return jnp.concatenate(
        [ref[pl.ds(first * TOK_ROWS + s_, n, stride=TOK_ROWS), :] for s_ in range(TOK_ROWS)], axis=1)


N_QKVR = 2 * GLA_KEY + 2 * GLA_VAL
N_HY = 3 * HY_WIDTH
N_GATES = 2 * D_MODEL
IN_COLS = N_QKVR + N_HY + N_GATES + ALOW_PAD


def _inproj_kernel(x_ref, nw_ref, w_ref, qkvr_ref, hy_ref, gates_ref, alow_ref):
    x = x_ref[...]
    ms = jnp.mean(x * x, axis=-1, keepdims=True)
    xn = (x * lax.rsqrt(ms + EPS) * nw_ref[...]).astype(BF16)
    c0, c1, c2 = N_QKVR, N_QKVR + N_HY, N_QKVR + N_HY + N_GATES
    qkvr_ref[...] = jnp.dot(xn, w_ref[:, :c0], preferred_element_type=F32).astype(BF16)
    hy_ref[...] = jnp.dot(xn, w_ref[:, c0:c1], preferred_element_type=F32).astype(BF16)
    gates_ref[...] = jnp.dot(xn, w_ref[:, c1:c2], preferred_element_type=F32).astype(BF16)
    alow_ref[...] = jnp.dot(xn, w_ref[:, c2:], preferred_element_type=F32)


def _in_proj(x2, norm1_w, w_cat):
    T = x2.shape[0]
    tm = min(512, T)
    return pl.pallas_call(
        _inproj_kernel,
        grid=(T // tm,),
        in_specs=[
            pl.BlockSpec((tm, D_MODEL), lambda i: (i, 0)),
            _const_spec((1, D_MODEL)),
            _const_spec((D_MODEL, IN_COLS)),
        ],
        out_specs=[
            pl.BlockSpec((tm, N_QKVR), lambda i: (i, 0)),
            pl.BlockSpec((tm, N_HY), lambda i: (i, 0)),
            pl.BlockSpec((tm, N_GATES), lambda i: (i, 0)),
            pl.BlockSpec((tm, ALOW_PAD), lambda i: (i, 0)),
        ],
        out_shape=[
            jax.ShapeDtypeStruct((T, N_QKVR), BF16),
            jax.ShapeDtypeStruct((T, N_HY), BF16),
            jax.ShapeDtypeStruct((T, N_GATES), BF16),
            jax.ShapeDtypeStruct((T, ALOW_PAD), F32),
        ],
        compiler_params=_cparams("parallel"),
        name="in_proj",
    )(x2, norm1_w.reshape(1, D_MODEL), w_cat)


GLA_PREP_ROWS = 256
GLA_SCAN_UNROLL = 4


def _log_sigmoid(x):
    return jnp.minimum(x, 0.0) - jnp.log1p(jnp.exp(-jnp.abs(x)))


def _gla_kernel(q_ref, k_ref, v_ref, r_ref, alow_ref, w2_ref, gb_ref, nw_ref, o_ref,
                qtf, kef, qtb, keb, decf, decb, of_s, ob_s, stf, stb):
    L = q_ref.shape[0]
    C = GLA_CHUNK
    DK = GLA_DK
    n_chunks = L // C
    R = min(GLA_PREP_ROWS, L)
    cpr = R // C
    scale = DK ** -0.5
    nt = (((1,), (1,)), ((), ()))
    tn = (((0,), (0,)), ((), ()))

    ri = lax.broadcasted_iota(jnp.int32, (R, R), 0)
    ci = lax.broadcasted_iota(jnp.int32, (R, R), 1)
    same = (ri // C) == (ci // C)
    mask_f = same & (ri >= ci)
    mask_b = same & (ri < ci)
    tri = jnp.where(mask_f, 1.0, 0.0).astype(BF16)
    w2 = jnp.concatenate([w2_ref[0], w2_ref[1]], axis=1).astype(BF16)
    gb = jnp.concatenate([gb_ref[0], gb_ref[1]], axis=1)

    def prep(g, carry):
        rows = pl.ds(pl.multiple_of(g * R, R), R)
        logits = jnp.dot(alow_ref[rows, :].astype(BF16), w2, preferred_element_type=F32) + gb
        lg = _log_sigmoid(logits) * (1.0 / GLA_GATE_TEMP)
        hi = lg.astype(BF16)
        lo = (lg - hi.astype(F32)).astype(BF16)
        b = jnp.dot(tri, hi, preferred_element_type=F32) + jnp.dot(tri, lo, preferred_element_type=F32)
        tot = jnp.concatenate(
            [jnp.broadcast_to(b[c * C + C - 1:c * C + C, :], (C, 2 * DK)) for c in range(cpr)], axis=0)
        q = q_ref[rows, :].astype(F32) * scale
        k = k_ref[rows, :].astype(F32)
        vb = v_ref[rows, :]
        dtot = jnp.exp(tot)
        for d, (qt, ke, dec, mask, out) in enumerate(((qtf, kef, decf, mask_f, of_s), (qtb, keb, decb, mask_b, ob_s))):
            sl = slice(d * DK, (d + 1) * DK)
            bd, td, ld = b[:, sl], tot[:, sl], lg[:, sl]
            if d == 0:
                e_q, e_end = bd, td - bd
            else:
                e_q, e_end = td - bd + ld, bd - ld
            qtv = (q * jnp.exp(e_q)).astype(BF16)
            ktv = (k * jnp.exp(-e_q)).astype(BF16)
            qt[rows, :] = qtv
            ke[rows, :] = (k * jnp.exp(e_end)).astype(BF16)
            s = lax.dot_general(qtv, ktv, nt, preferred_element_type=F32)
            s = jnp.where(mask, s, 0.0).astype(BF16)
            out[rows, :] = jnp.dot(s, vb, preferred_element_type=F32)
            for c in range(cpr):
                dec[pl.ds(g * cpr + c, 1), :] = dtot[c * C:c * C + 1, sl]
        return carry

    lax.fori_loop(0, L // R, prep, 0, unroll=2)

    def chunk(n, st, qt, ke, dec, out):
        rows = pl.ds(pl.multiple_of(n * C, C), C)
        s = st[...]
        out[rows, :] += lax.dot_general(qt[rows, :], s.astype(BF16), nt, preferred_element_type=F32)
        upd = lax.dot_general(v_ref[rows, :], ke[rows, :], tn, preferred_element_type=F32)
        st[...] = dec[pl.ds(n, 1), :] * s + upd

    def step(i, carry):
        chunk(i, stf, qtf, kef, decf, of_s)
        chunk(n_chunks - 1 - i, stb, qtb, keb, decb, ob_s)
        return carry

    stf[...] = jnp.zeros_like(stf)
    stb[...] = jnp.zeros_like(stb)
    lax.fori_loop(0, n_chunks, step, 0, unroll=GLA_SCAN_UNROLL)

    o = of_s[...] + ob_s[...]
    o = o * lax.rsqrt(jnp.mean(o * o, axis=-1, keepdims=True) + EPS) * nw_ref[...]
    r = r_ref[...].astype(F32)
    o_ref[...] = (o * (r * jax.nn.sigmoid(r))).astype(BF16)


def _gla(qkvr, alow, w2, gb, norm_w, B, L):
    T = B * L
    H = GLA_HEADS
    kq = GLA_KEY // GLA_DK
    kv = 2 * GLA_KEY // GLA_DV
    kr = (2 * GLA_KEY + GLA_VAL) // GLA_DV
    return pl.pallas_call(
        _gla_kernel,
        grid=(B, H),
        in_specs=[
            pl.BlockSpec((L, GLA_DK), lambda b, h: (b, h)),
            pl.BlockSpec((L, GLA_DK), lambda b, h: (b, kq + h)),
            pl.BlockSpec((L, GLA_DV), lambda b, h: (b, kv + h)),
            pl.BlockSpec((L, GLA_DV), lambda b, h: (b, kr + h)),
            pl.BlockSpec((L, ALOW_PAD), lambda b, h: (b, 0)),
            pl.BlockSpec((2, ALOW_PAD, GLA_DK), lambda b, h: (0, 0, h)),
            pl.BlockSpec((2, 1, GLA_DK), lambda b, h: (0, 0, h)),
            pl.BlockSpec((1, GLA_DV), lambda b, h: (0, 0)),
        ],
        out_specs=pl.BlockSpec((L, GLA_DV), lambda b, h: (b, h)),
        out_shape=jax.ShapeDtypeStruct((T, GLA_VAL), BF16),
        scratch_shapes=[pltpu.VMEM((L, GLA_DK), BF16)] * 4
        + [pltpu.VMEM((L // GLA_CHUNK, GLA_DK), F32)] * 2
        + [pltpu.VMEM((L, GLA_DV), F32)] * 2
        + [pltpu.VMEM((GLA_DV, GLA_DK), F32)] * 2,
        compiler_params=_cparams("parallel", "parallel"),
        name="gla",
    )(qkvr, qkvr, qkvr, qkvr, alow, w2, gb, norm_w.reshape(1, GLA_DV))


HY_WT = 256


def _hy_filter_kernel(z_ref, t_ref, fr_ref, w1_ref, b1_ref, w2_ref, b2_ref, w3f_ref, w3b_ref, dec_ref, o_ref, h_s):
    @pl.when(pl.program_id(0) == 0)
    def _():
        fr = fr_ref[...]
        h1 = jnp.sin(fr * (jnp.dot(z_ref[...], w1_ref[...], preferred_element_type=F32, precision=HIGHEST) + b1_ref[...]))
        h_s[...] = jnp.sin(fr * (jnp.dot(h1, w2_ref[...], preferred_element_type=F32, precision=HIGHEST) + b2_ref[...]))

    h = h_s[...]
    t = t_ref[...]
    row = lax.broadcasted_iota(jnp.int32, (z_ref.shape[0], 1), 0)
    hf = jnp.dot(h, w3f_ref[...], preferred_element_type=F32, precision=HIGHEST)
    hf = hf * jnp.exp(-t * jnp.abs(dec_ref[0:1, :]))
    hb = jnp.dot(h, w3b_ref[...], preferred_element_type=F32, precision=HIGHEST)
    hb = hb * jnp.exp(-t * jnp.abs(dec_ref[1:2, :]))
    hb = jnp.where(row == 0, 0.0, hb)
    ss = jnp.sum(hf * hf, axis=0, keepdims=True) + jnp.sum(hb * hb, axis=0, keepdims=True)
    inv = lax.rsqrt(ss + EPS)
    o_ref[0] = hf * inv
    o_ref[1] = hb * inv


def _hy_filter(L, hy_freq, w1, b1, w2, b2, w3, hy_decay):
    W = HY_WIDTH
    t = np.linspace(0.0, 1.0, L, dtype=np.float32)[:, None]
    omega = (np.float32(2.0 * math.pi) * np.arange(L, dtype=np.float32) / np.float32(L)).astype(np.float32)
    f = np.linspace(1e-4, HY_BANDS - 1, HY_BANDS, dtype=np.float32)
    ang = (omega[:, None] * f[None, :]).astype(np.float32)
    z = np.concatenate([t, np.cos(ang), -np.sin(ang)], axis=-1).astype(np.float32)
    nw = W // HY_WT
    return pl.pallas_call(
        _hy_filter_kernel,
        grid=(nw,),
        in_specs=[
            _const_spec((L, HY_EMB)),
            _const_spec((L, 1)),
            _const_spec((1, HY_FFN)),
            _const_spec((HY_EMB, HY_FFN)),
            _const_spec((1, HY_FFN)),
            _const_spec((HY_FFN, HY_FFN)),
            _const_spec((1, HY_FFN)),
            pl.BlockSpec((HY_FFN, HY_WT), lambda j: (0, j)),
            pl.BlockSpec((HY_FFN, HY_WT), lambda j: (0, nw + j)),
            pl.BlockSpec((2, HY_WT), lambda j: (0, j)),
        ],
        out_specs=pl.BlockSpec((2, L, HY_WT), lambda j: (0, 0, j)),
        out_shape=jax.ShapeDtypeStruct((2, L, W), F32),
        scratch_shapes=[pltpu.VMEM((L, HY_FFN), F32)],
        compiler_params=_cparams("arbitrary"),
        name="hy_filter",
    )(jnp.asarray(z), jnp.asarray(t), hy_freq.reshape(1, HY_FFN), w1, b1.reshape(1, HY_FFN), w2,
      b2.reshape(1, HY_FFN), w3, w3, hy_decay)


def _hy_pre_kernel(x0_ref, x1_ref, v_ref, w0_ref, w1_ref, wv_ref, b0_ref, b1_ref, bv_ref, z_ref, x0o_ref):
    L = x0_ref.shape[0]
    row = lax.broadcasted_iota(jnp.int32, (L, 1), 0)

    def conv3(u_ref, w_ref, b_ref):
        u = u_ref[...].astype(F32)
        prev = jnp.where(row == 0, 0.0, pltpu.roll(u, 1, 0))
        nxt = jnp.where(row == L - 1, 0.0, pltpu.roll(u, L - 1, 0))
        return w_ref[0:1, :] * prev + w_ref[1:2, :] * u + w_ref[2:3, :] * nxt + b_ref[...]

    x0o_ref[...] = conv3(x0_ref, w0_ref, b0_ref).astype(BF16)
    z_ref[...] = (conv3(v_ref, wv_ref, bv_ref) * conv3(x1_ref, w1_ref, b1_ref)).astype(BF16)


def _hy_pre(hy, conv_w, conv_b, B, L):
    W = HY_WIDTH
    nw = W // HY_WT
    cb = conv_b.reshape(1, 3 * W)
    blk = lambda off: pl.BlockSpec((L, HY_WT), lambda b, j: (b, off + j))
    wblk = lambda off: pl.BlockSpec((3, HY_WT), lambda b, j: (0, off + j))
    bblk = lambda off: pl.BlockSpec((1, HY_WT), lambda b, j: (0, off + j))
    return pl.pallas_call(
        _hy_pre_kernel,
        grid=(B, nw),
        in_specs=[blk(0), blk(nw), blk(2 * nw), wblk(0), wblk(nw), wblk(2 * nw), bblk(0), bblk(nw), bblk(2 * nw)],
        out_specs=[pl.BlockSpec((L, HY_WT), lambda b, j: (b, j))] * 2,
        out_shape=[jax.ShapeDtypeStruct((B * L, W), BF16)] * 2,
        compiler_params=_cparams("parallel", "parallel"),
        name="hy_pre",
    )(hy, hy, hy, conv_w, conv_w, conv_w, cb, cb, cb)


DFT_TK = 256
DFT_TT = 512
DFT_WT = 512


def _dft_tables(L):
    N = 2 * L
    nf = L + 1
    kfp = -(-nf // DFT_TK) * DFT_TK
    ang = lambda a, b: (2.0 * np.pi / N) * ((np.outer(a, b)) % N)
    i = np.arange(DFT_TK)
    n = np.arange(L)
    k = np.arange(kfp)
    k0 = np.arange(0, kfp, DFT_TK)
    t0 = np.arange(0, L, DFT_TK)
    alpha = np.where((k == 0) | (k == L), 1.0, 2.0) / N
    alpha = np.where(k <= L, alpha, 0.0)
    f32 = lambda a: jnp.asarray(a.astype(np.float32))
    return dict(
        kfp=kfp,
        fwd_e=(f32(np.cos(ang(i, n))), f32(np.sin(ang(i, n)))),
        fwd_r=(f32(np.cos(ang(k0, n))), f32(np.sin(ang(k0, n)))),
        inv_e=(f32(np.cos(ang(i, k))), f32(np.sin(ang(i, k)))),
        inv_r=(f32(np.cos(ang(t0, k))), f32(np.sin(ang(t0, k)))),
        alpha=f32(alpha[None, :]),
    )


def _dft_fwd_kernel(ec_ref, es_ref, rc_ref, rs_ref, z_ref, kr_ref, ki_ref, yr_ref, yi_ref, fc_s, fs_s, *, use_filter):
    @pl.when(pl.program_id(1) == 0)
    def _():
        ec, es, rc, rs = ec_ref[...], es_ref[...], rc_ref[0], rs_ref[0]
        fc_s[...] = (ec * rc - es * rs).astype(BF16)
        fs_s[...] = (es * rc + ec * rs).astype(BF16)

    z = z_ref[...]
    if use_filter:
        xr = jnp.dot(fc_s[...], z, preferred_element_type=F32)
        xi = -jnp.dot(fs_s[...], z, preferred_element_type=F32)
        kr, ki = kr_ref[...], ki_ref[...]
        yr_ref[...] = (xr * kr - xi * ki).astype(yr_ref.dtype)
        yi_ref[...] = (xr * ki + xi * kr).astype(yi_ref.dtype)
    else:
        @pl.when(pl.program_id(1) == 0)
        def _():
            yr_ref[...] = jnp.dot(fc_s[...], z, preferred_element_type=F32).astype(yr_ref.dtype)
            yi_ref[...] = jnp.zeros_like(yi_ref)

        @pl.when(pl.program_id(1) != 0)
        def _():
            yr_ref[...] = jnp.zeros_like(yr_ref)
            yi_ref[...] = (-jnp.dot(fs_s[...], z, preferred_element_type=F32)).astype(yi_ref.dtype)


def _dft_fwd(tabs, z3, kr, ki, use_filter, out_dtype):
    Bz, L, W = z3.shape
    kfp = tabs["kfp"]
    nk = kfp // DFT_TK
    (ec, es), (rc, rs) = tabs["fwd_e"], tabs["fwd_r"]
    return pl.pallas_call(
        functools.partial(_dft_fwd_kernel, use_filter=use_filter),
        grid=(nk, Bz),
        in_specs=[
            _const_spec((DFT_TK, L)),
            _const_spec((DFT_TK, L)),
            pl.BlockSpec((1, 1, L), lambda k, b: (k, 0, 0)),
            pl.BlockSpec((1, 1, L), lambda k, b: (k, 0, 0)),
            pl.BlockSpec((None, L, W), lambda k, b: (b, 0, 0)),
            pl.BlockSpec((DFT_TK, W), lambda k, b: (k, 0)),
            pl.BlockSpec((DFT_TK, W), lambda k, b: (k, 0)),
        ],
        out_specs=[pl.BlockSpec((None, DFT_TK, W), lambda k, b: (b, k, 0))] * 2,
        out_shape=[jax.ShapeDtypeStruct((Bz, kfp, W), out_dtype)] * 2,
        scratch_shapes=[pltpu.VMEM((DFT_TK, L), BF16)] * 2,
        compiler_params=_cparams("arbitrary", "arbitrary"),
        name="dft_fwd_filter" if use_filter else "dft_fwd",
    )(ec, es, rc.reshape(nk, 1, L), rs.reshape(nk, 1, L), z3, kr, ki)


def _dft_inv_kernel(ec_ref, es_ref, rc_ref, rs_ref, al_ref, yr_ref, yi_ref, z_ref, x0_ref, bias_ref, o_ref, gc_s, gs_s):
    @pl.when((pl.program_id(1) == 0) & (pl.program_id(2) == 0))
    def _():
        ec, es, al = ec_ref[...], es_ref[...], al_ref[...]
        for s in range(DFT_TT // DFT_TK):
            rc, rs = rc_ref[s], rs_ref[s]
            rows = slice(s * DFT_TK, (s + 1) * DFT_TK)
            gc_s[rows, :] = (al * (ec * rc - es * rs)).astype(BF16)
            gs_s[rows, :] = (-al * (es * rc + ec * rs)).astype(BF16)

    y = jnp.dot(gc_s[...], yr_ref[...], preferred_element_type=F32)
    y = y + jnp.dot(gs_s[...], yi_ref[...], preferred_element_type=F32)
    z = z_ref[...].astype(F32)
    o_ref[...] = ((y + z * bias_ref[...]) * x0_ref[...].astype(F32)).astype(BF16)


def _dft_inv(tabs, yr, yi, z3, x03, hy_bias):
    B, L, W = z3.shape
    kfp = tabs["kfp"]
    (ec, es), (rc, rs) = tabs["inv_e"], tabs["inv_r"]
    tt = min(DFT_TT, L)
    sub = tt // DFT_TK
    nt = L // tt
    wt = DFT_WT
    return pl.pallas_call(
        _dft_inv_kernel,
        grid=(nt, B, W // wt),
        in_specs=[
            _const_spec((DFT_TK, kfp)),
            _const_spec((DFT_TK, kfp)),
            pl.BlockSpec((sub, 1, kfp), lambda t, b, w: (t, 0, 0)),
            pl.BlockSpec((sub, 1, kfp), lambda t, b, w: (t, 0, 0)),
            _const_spec((1, kfp)),
            pl.BlockSpec((None, kfp, wt), lambda t, b, w: (b, 0, w)),
            pl.BlockSpec((None, kfp, wt), lambda t, b, w: (b, 0, w)),
            pl.BlockSpec((None, tt, wt), lambda t, b, w: (b, t, w)),
            pl.BlockSpec((None, tt, wt), lambda t, b, w: (b, t, w)),
            pl.BlockSpec((1, wt), lambda t, b, w: (0, w)),
        ],
        out_specs=pl.BlockSpec((None, tt, wt), lambda t, b, w: (b, t, w)),
        out_shape=jax.ShapeDtypeStruct((B, L, W), BF16),
        scratch_shapes=[pltpu.VMEM((tt, kfp), BF16)] * 2,
        compiler_params=_cparams("arbitrary", "arbitrary", "arbitrary"),
        name="dft_inv",
    )(ec, es, rc.reshape(L // DFT_TK, 1, kfp), rs.reshape(L // DFT_TK, 1, kfp), tabs["alpha"], yr, yi, z3, x03,
      hy_bias.reshape(1, W))


def _merge_kernel(og_ref, oh_ref, gates_ref, x_ref, wbg_ref, wbh_ref, wo_ref, n2_ref, rw_ref, rb_ref,
                  h1_ref, hn3_ref, idx_ref, wts_ref):
    mg = jnp.dot(og_ref[...], wbg_ref[...], preferred_element_type=F32)
    mh = jnp.dot(oh_ref[...], wbh_ref[...], preferred_element_type=F32)
    gg = jax.nn.sigmoid(gates_ref[:, :D_MODEL].astype(F32))
    gh = jax.nn.sigmoid(gates_ref[:, D_MODEL:].astype(F32))
    merged = (gg * mg + gh * mh).astype(BF16)
    h1 = x_ref[...] + jnp.dot(merged, wo_ref[...], preferred_element_type=F32)
    h1_ref[...] = h1
    hn = h1 * lax.rsqrt(jnp.mean(h1 * h1, axis=-1, keepdims=True) + EPS) * n2_ref[...]
    _store_token_rows(hn3_ref, hn)
    hh = hn.astype(BF16)
    hl = (hn - hh.astype(F32)).astype(BF16)
    p1 = jnp.dot(hh, rw_ref[...], preferred_element_type=F32)
    p2 = jnp.dot(hl, rw_ref[:, :ROUTE_PAD], preferred_element_type=F32)
    logits = p1[:, :ROUTE_PAD] + p1[:, ROUTE_PAD:] + p2 + rb_ref[...]
    lane = lax.broadcasted_iota(jnp.int32, logits.shape, 1)
    logits = jnp.where(lane < N_EXPERTS, logits, -jnp.inf)
    idx_out = jnp.zeros(logits.shape, jnp.int32)
    val_out = jnp.zeros(logits.shape, F32)
    v0 = None
    for j in range(TOP_K):
        m = jnp.max(logits, axis=-1, keepdims=True)
        sel = jnp.min(jnp.where(logits == m, lane, ROUTE_PAD), axis=-1, keepdims=True)
        if j == 0:
            v0 = m
        idx_out = jnp.where(lane == j, sel, idx_out)
        val_out = jnp.where(lane == j, jnp.exp(m - v0), val_out)
        logits = jnp.where(lane == sel, -jnp.inf, logits)
    idx_ref[...] = idx_out
    wts_ref[...] = val_out / jnp.sum(val_out, axis=-1, keepdims=True)


def _merge(o_gla, o_hy, gates, x2, wbg, wbh, wo, norm2_w, rw_pad, rb_pad):
    T = x2.shape[0]
    tm = min(512, T)
    row = lambda n: pl.BlockSpec((tm, n), lambda i: (i, 0))
    return pl.pallas_call(
        _merge_kernel,
        grid=(T // tm,),
        in_specs=[row(GLA_VAL), row(HY_WIDTH), row(N_GATES), row(D_MODEL),
                  _const_spec((GLA_VAL, D_MODEL)), _const_spec((HY_WIDTH, D_MODEL)), _const_spec((D_MODEL, D_MODEL)),
                  _const_spec((1, D_MODEL)), _const_spec((D_MODEL, 2 * ROUTE_PAD)), _const_spec((1, ROUTE_PAD))],
        out_specs=[row(D_MODEL), pl.BlockSpec((tm * TOK_ROWS, LANES), lambda i: (i, 0)), row(ROUTE_PAD), row(ROUTE_PAD)],
        out_shape=[jax.ShapeDtypeStruct((T, D_MODEL), F32), jax.ShapeDtypeStruct((T * TOK_ROWS, LANES), F32),
                   jax.ShapeDtypeStruct((T, ROUTE_PAD), jnp.int32), jax.ShapeDtypeStruct((T, ROUTE_PAD), F32)],
        compiler_params=_cparams("parallel"),
        name="merge_route",
    )(o_gla, o_hy, gates, x2, wbg, wbh, wo, norm2_w.reshape(1, D_MODEL), rw_pad, rb_pad)


MOE_TM = 512
MOE_TC = 256
DMA_UNROLL = 8


def _dispatch_kernel(last_ref, pos_ref, hn_ref, xs_ref, zero_s, sem, sems):
    i = pl.program_id(0)
    n = pl.num_programs(0)
    tc = pos_ref.shape[2] // TOP_K
    zrows = zero_s.shape[0]
    slot = lax.rem(i, 2)

    def wait_step(s):
        for _ in range(TOP_K):
            pltpu.make_async_copy(hn_ref.at[pl.ds(0, tc * TOK_ROWS)], xs_ref.at[pl.ds(0, tc * TOK_ROWS)],
                                  sems.at[s]).wait()

    @pl.when(i == 0)
    def _():
        zero_s[...] = jnp.zeros_like(zero_s)
        n_fill = last_ref.shape[0]

        def zfill(e, c):
            @pl.when(last_ref[e] >= 0)
            def _():
                dst = pl.multiple_of(last_ref[e] * zrows, zrows)
                pltpu.make_async_copy(zero_s, xs_ref.at[pl.ds(dst, zrows)], sem).start()
            return c
        lax.fori_loop(0, n_fill, zfill, 0)

        def zwait(e, c):
            @pl.when(last_ref[e] >= 0)
            def _():
                pltpu.make_async_copy(zero_s, xs_ref.at[pl.ds(0, zrows)], sem).wait()
            return c
        lax.fori_loop(0, n_fill, zwait, 0)

    base = i * tc

    def issue(r, c):
        src = hn_ref.at[pl.ds(pl.multiple_of((base + r) * TOK_ROWS, TOK_ROWS), TOK_ROWS)]
        for j in range(TOP_K):
            dst = pl.multiple_of(pos_ref[0, 0, r * TOP_K + j] * TOK_ROWS, TOK_ROWS)
            pltpu.make_async_copy(src, xs_ref.at[pl.ds(dst, TOK_ROWS)], sems.at[slot]).start(priority=j % 2)
        return c
    lax.fori_loop(0, tc, issue, 0, unroll=DMA_UNROLL // TOP_K)

    @pl.when(i > 0)
    def _():
        wait_step(1 - slot)

    @pl.when(i == n - 1)
    def _():
        wait_step(slot)


def _dispatch(hn3, pos, last_tile, P):
    T = hn3.shape[0] // TOK_ROWS
    tc = min(MOE_TC, T)
    nt = T // tc
    return pl.pallas_call(
        _dispatch_kernel,
        grid_spec=pltpu.PrefetchScalarGridSpec(
            num_scalar_prefetch=1,
            grid=(nt,),
            in_specs=[pl.BlockSpec((1, 1, tc * TOP_K), lambda i, lt: (i, 0, 0), memory_space=pltpu.SMEM),
                      pl.BlockSpec(memory_space=pl.ANY)],
            out_specs=pl.BlockSpec(memory_space=pl.ANY),
            scratch_shapes=[pltpu.VMEM((MOE_TM * TOK_ROWS, LANES), F32), pltpu.SemaphoreType.DMA(()),
                            pltpu.SemaphoreType.DMA((2,))],
        ),
        out_shape=jax.ShapeDtypeStruct((P * TOK_ROWS, LANES), F32),
        compiler_params=_cparams("arbitrary"),
        name="moe_dispatch",
    )(last_tile, pos.reshape(nt, 1, tc * TOP_K), hn3)


def _expert_kernel(te_ref, nused_ref, xs_ref, wgu_ref, bgu_ref, wd_ref, bd_ref, o_ref, wgu_s, wd_s):
    i = pl.program_id(0)
    prev = te_ref[jnp.maximum(i - 1, 0)]

    @pl.when((i == 0) | (te_ref[i] != prev))
    def _():
        wgu_s[...] = wgu_ref[0].astype(BF16)
        wd_s[...] = wd_ref[0].astype(BF16)

    @pl.when(i < nused_ref[0])
    def _():
        x = _load_token_rows(xs_ref, 0, xs_ref.shape[0] // TOK_ROWS).astype(BF16)
        gu = jnp.dot(x, wgu_s[...], preferred_element_type=F32) + bgu_ref[0]
        gate = jnp.minimum(gu[:, :D_FF], SWIGLU_LIMIT)
        up = jnp.clip(gu[:, D_FF:], -SWIGLU_LIMIT, SWIGLU_LIMIT)
        act = gate * jax.nn.sigmoid(SWIGLU_ALPHA * gate) * (up + 1.0)
        y = jnp.dot(act.astype(BF16), wd_s[...], preferred_element_type=F32) + bd_ref[0]
        _store_token_rows(o_ref, y)

    @pl.when(i >= nused_ref[0])
    def _():
        o_ref[...] = jnp.zeros_like(o_ref)


def _experts(xs3, tile_expert, n_used, w_gate_up, b_gate_up, w_down, b_down):
    P = xs3.shape[0] // TOK_ROWS
    tm = MOE_TM
    nt = P // tm
    live = lambda i, te, nu: jnp.minimum(i, nu[0] - 1)
    return pl.pallas_call(
        _expert_kernel,
        grid_spec=pltpu.PrefetchScalarGridSpec(
            num_scalar_prefetch=2,
            grid=(nt,),
            in_specs=[
                pl.BlockSpec((tm * TOK_ROWS, LANES), lambda i, te, nu: (live(i, te, nu), 0)),
                pl.BlockSpec((1, D_MODEL, 2 * D_FF), lambda i, te, nu: (te[i], 0, 0)),
                pl.BlockSpec((1, 1, 2 * D_FF), lambda i, te, nu: (te[i], 0, 0)),
                pl.BlockSpec((1, D_FF, D_MODEL), lambda i, te, nu: (te[i], 0, 0)),
                pl.BlockSpec((1, 1, D_MODEL), lambda i, te, nu: (te[i], 0, 0)),
            ],
            out_specs=pl.BlockSpec((tm * TOK_ROWS, LANES), lambda i, te, nu: (i, 0)),
            scratch_shapes=[pltpu.VMEM((D_MODEL, 2 * D_FF), BF16), pltpu.VMEM((D_FF, D_MODEL), BF16)],
        ),
        out_shape=jax.ShapeDtypeStruct((P * TOK_ROWS, LANES), F32),
        compiler_params=_cparams("arbitrary"),
        name="moe_experts",
    )(tile_expert, n_used, xs3, w_gate_up, b_gate_up.reshape(N_EXPERTS, 1, 2 * D_FF), w_down,
      b_down.reshape(N_EXPERTS, 1, D_MODEL))


def _combine_kernel(pos_ref, posn_ref, wts_ref, h1_ref, nf_ref, ys_ref, o_ref, buf, sems):
    i = pl.program_id(0)
    n = pl.num_programs(0)
    tc = o_ref.shape[0]
    slot = lax.rem(i, 2)

    def gather(p_ref, s):
        def issue(r, c):
            for j in range(TOP_K):
                src = pl.multiple_of(p_ref[0, 0, r * TOP_K + j] * TOK_ROWS, TOK_ROWS)
                dst = pl.multiple_of((j * tc + r) * TOK_ROWS, TOK_ROWS)
                pltpu.make_async_copy(ys_ref.at[pl.ds(src, TOK_ROWS)], buf.at[s, pl.ds(dst, TOK_ROWS)],
                                      sems.at[s]).start(priority=j % 2)
            return c
        lax.fori_loop(0, tc, issue, 0, unroll=DMA_UNROLL // TOP_K)

    @pl.when(i == 0)
    def _():
        gather(pos_ref, 0)

    @pl.when(i + 1 < n)
    def _():
        gather(posn_ref, 1 - slot)

    pltpu.make_async_copy(ys_ref.at[pl.ds(0, TOP_K * tc * TOK_ROWS)], buf.at[slot], sems.at[slot]).wait()

    w = wts_ref[...]
    h = h1_ref[...]
    for j in range(TOP_K):
        h = h + w[:, j:j + 1] * _load_token_rows(buf.at[slot], j * tc, tc)
    o_ref[...] = h * lax.rsqrt(jnp.mean(h * h, axis=-1, keepdims=True) + EPS) * nf_ref[...]


def _combine(pos, wts, h1, norm_f_w, ys3):
    T = h1.shape[0]
    tc = min(MOE_TC, T)
    nt = T // tc
    pos3 = pos.reshape(nt, 1, tc * TOP_K)
    return pl.pallas_call(
        _combine_kernel,
        grid=(nt,),
        in_specs=[
            pl.BlockSpec((1, 1, tc * TOP_K), lambda i: (i, 0, 0), memory_space=pltpu.SMEM),
            pl.BlockSpec((1, 1, tc * TOP_K), lambda i: (jnp.minimum(i + 1, nt - 1), 0, 0), memory_space=pltpu.SMEM),
            pl.BlockSpec((tc, ROUTE_PAD), lambda i: (i, 0)),
            pl.BlockSpec((tc, D_MODEL), lambda i: (i, 0)),
            _const_spec((1, D_MODEL)),
            pl.BlockSpec(memory_space=pl.ANY),
        ],
        out_specs=pl.BlockSpec((tc, D_MODEL), lambda i: (i, 0)),
        out_shape=jax.ShapeDtypeStruct((T, D_MODEL), F32),
        scratch_shapes=[pltpu.VMEM((2, TOP_K * tc * TOK_ROWS, LANES), F32), pltpu.SemaphoreType.DMA((2,))],
        compiler_params=_cparams("arbitrary"),
        name="moe_combine",
    )(pos3, pos3, wts, h1, norm_f_w.reshape(1, D_MODEL), ys3)


def _route_plan(top_idx, T):
    tm = MOE_TM
    nt = (T * TOP_K) // tm + N_EXPERTS
    e = top_idx.reshape(-1)
    oh = (e[:, None] == jnp.arange(N_EXPERTS, dtype=jnp.int32)[None, :]).astype(jnp.int32)
    csum = jnp.cumsum(oh, axis=0)
    rank = jnp.sum(csum * oh, axis=1) - 1
    counts = csum[-1]
    tiles = (counts + tm - 1) // tm
    tile_end = jnp.cumsum(tiles)
    offs = (tile_end - tiles) * tm
    pos = jnp.sum(oh * offs[None, :], axis=1) + rank
    tile_id = jnp.arange(nt, dtype=jnp.int32)
    tile_expert = jnp.minimum(jnp.sum((tile_end[None, :] <= tile_id[:, None]).astype(jnp.int32), axis=1),
                              N_EXPERTS - 1).astype(jnp.int32)
    tail = tile_end[-1] + jnp.arange(N_EXPERTS, dtype=jnp.int32)
    last_tile = jnp.concatenate([jnp.where(tiles > 0, tile_end - 1, -1), jnp.where(tail < nt, tail, -1)]).astype(jnp.int32)
    n_used = tile_end[-1:].astype(jnp.int32)
    return pos.astype(jnp.int32), tile_expert, last_tile, n_used, nt * tm


def kernel(x, norm1_w, w_in, gla_gate_w2, gla_gate_b, gla_norm_w, hy_conv_w, hy_conv_b, hy_freq, hy_ffn_w1, hy_ffn_b1,
           hy_ffn_w2, hy_ffn_b2, hy_ffn_w3, hy_decay, hy_bias, w_branch_gla, w_branch_hy, w_out, norm2_w, router_w,
           router_b, w_gate_up, b_gate_up, w_down, b_down, norm_f_w):
    B, L, D = x.shape
    T = B * L
    x2 = x.reshape(T, D)

    s_a = N_QKVR
    s_h = s_a + 2 * GLA_RANK
    w_cat = jnp.concatenate(
        [w_in[:, :s_a], w_in[:, s_h:], w_in[:, s_a:s_h], jnp.zeros((D, ALOW_PAD - 2 * GLA_RANK), w_in.dtype)],
        axis=1).astype(BF16)
    qkvr, hy, gates, alow = _in_proj(x2, norm1_w, w_cat)

    w2 = jnp.zeros((2, ALOW_PAD, GLA_KEY), F32)
    w2 = w2.at[0, :GLA_RANK].set(gla_gate_w2[0]).at[1, GLA_RANK:2 * GLA_RANK].set(gla_gate_w2[1])
    o_gla = _gla(qkvr, alow, w2, gla_gate_b.reshape(2, 1, GLA_KEY), gla_norm_w, B, L)

    taps = _hy_filter(L, hy_freq, hy_ffn_w1, hy_ffn_b1, hy_ffn_w2, hy_ffn_b2, hy_ffn_w3, hy_decay)
    tabs = _dft_tables(L)
    sd = jnp.stack([taps[0] + taps[1], taps[0] - taps[1]]).astype(BF16)
    dummy = jnp.zeros((tabs["kfp"], HY_WIDTH), F32)
    fr, fi = _dft_fwd(tabs, sd, dummy, dummy, False, F32)
    z, x0c = _hy_pre(hy, hy_conv_w, hy_conv_b, B, L)
    z3 = z.reshape(B, L, HY_WIDTH)
    yr, yi = _dft_fwd(tabs, z3, fr[0], fi[1], True, BF16)
    o_hy = _dft_inv(tabs, yr, yi, z3, x0c.reshape(B, L, HY_WIDTH), hy_bias).reshape(T, HY_WIDTH)

    rw_f = jnp.zeros((D, ROUTE_PAD), F32).at[:, :N_EXPERTS].set(router_w)
    rw_hi = rw_f.astype(BF16)
    rw_pad = jnp.concatenate([rw_hi, (rw_f - rw_hi.astype(F32)).astype(BF16)], axis=1)
    rb_pad = jnp.zeros((1, ROUTE_PAD), F32).at[0, :N_EXPERTS].set(router_b)
    h1, hn3, idx_pad, wts_pad = _merge(o_gla, o_hy, gates, x2, w_branch_gla.astype(BF16), w_branch_hy.astype(BF16),
                                       w_out.astype(BF16), norm2_w, rw_pad, rb_pad)

    pos, tile_expert, last_tile, n_used, P = _route_plan(idx_pad[:, :TOP_K], T)
    xs3 = _dispatch(hn3, pos, last_tile, P)
    ys3 = _experts(xs3, tile_expert, n_used, w_gate_up, b_gate_up, w_down, b_down)
    out = _combine(pos, wts_pad, h1, norm_f_w, ys3)
    return out.reshape(B, L, D)
```

```python
import functools
import math

import numpy as np
import jax
import jax.numpy as jnp
from jax import lax
from jax.experimental import pallas as pl
from jax.experimental.pallas import tpu as pltpu

D_MODEL = 1024
EPS = 1e-6
GLA_HEADS = 4
GLA_DK = 128
GLA_DV = 256
GLA_KEY = GLA_HEADS * GLA_DK
GLA_VAL = GLA_HEADS * GLA_DV
GLA_RANK = 16
GLA_GATE_TEMP = 16.0
GLA_CHUNK = 64
HY_WIDTH = D_MODEL
HY_BANDS = 16
HY_EMB = 1 + 2 * HY_BANDS
HY_FFN = 64
N_EXPERTS = 32
TOP_K = 4
D_FF = D_MODEL
SWIGLU_LIMIT = 7.0
SWIGLU_ALPHA = 1.702

LANES = 128
ALOW_PAD = LANES
ROUTE_PAD = LANES
VMEM_LIMIT = 56 * 1024 * 1024

F32 = jnp.float32
BF16 = jnp.bfloat16
HIGHEST = lax.Precision.HIGHEST


def _cparams(*sem):
    return pltpu.CompilerParams(dimension_semantics=sem, vmem_limit_bytes=VMEM_LIMIT)


def _const_spec(shape):
    nd = len(shape)
    return pl.BlockSpec(shape, lambda *_: (0,) * nd, pipeline_mode=pl.Buffered(1))


TOK_ROWS = D_MODEL // LANES


def _store_token_rows(ref, val):
    n = val.shape[0]
    for s_ in range(TOK_ROWS):
        ref[pl.ds(s_, n, stride=TOK_ROWS), :] = val[:, s_ * LANES:(s_ + 1) * LANES]


def _load_token_rows(ref, first, n):
    return jnp.concatenate(
        [ref[pl.ds(first * TOK_ROWS + s_, n, stride=TOK_ROWS), :] for s_ in range(TOK_ROWS)], axis=1)


N_QKVR = 2 * GLA_KEY + 2 * GLA_VAL
N_HY = 3 * HY_WIDTH
N_GATES = 2 * D_MODEL
IN_COLS = N_QKVR + N_HY + N_GATES + ALOW_PAD


def _inproj_kernel(x_ref, nw_ref, w_ref, qkvr_ref, hy_ref, gates_ref, alow_ref):
    x = x_ref[...]
    ms = jnp.mean(x * x, axis=-1, keepdims=True)
    xn = (x * lax.rsqrt(ms + EPS) * nw_ref[...]).astype(BF16)
    c0, c1, c2 = N_QKVR, N_QKVR + N_HY, N_QKVR + N_HY + N_GATES
    qkvr_ref[...] = jnp.dot(xn, w_ref[:, :c0], preferred_element_type=F32).astype(BF16)
    hy_ref[...] = jnp.dot(xn, w_ref[:, c0:c1], preferred_element_type=F32).astype(BF16)
    gates_ref[...] = jnp.dot(xn, w_ref[:, c1:c2], preferred_element_type=F32).astype(BF16)
    alow_ref[...] = jnp.dot(xn, w_ref[:, c2:], preferred_element_type=F32)


def _in_proj(x2, norm1_w, w_cat):
    T = x2.shape[0]
    tm = min(512, T)
    return pl.pallas_call(
        _inproj_kernel,
        grid=(T // tm,),
        in_specs=[
            pl.BlockSpec((tm, D_MODEL), lambda i: (i, 0)),
            _const_spec((1, D_MODEL)),
            _const_spec((D_MODEL, IN_COLS)),
        ],
        out_specs=[
            pl.BlockSpec((tm, N_QKVR), lambda i: (i, 0)),
            pl.BlockSpec((tm, N_HY), lambda i: (i, 0)),
            pl.BlockSpec((tm, N_GATES), lambda i: (i, 0)),
            pl.BlockSpec((tm, ALOW_PAD), lambda i: (i, 0)),
        ],
        out_shape=[
            jax.ShapeDtypeStruct((T, N_QKVR), BF16),
            jax.ShapeDtypeStruct((T, N_HY), BF16),
            jax.ShapeDtypeStruct((T, N_GATES), BF16),
            jax.ShapeDtypeStruct((T, ALOW_PAD), F32),
        ],
        compiler_params=_cparams("parallel"),
        name="in_proj",
    )(x2, norm1_w.reshape(1, D_MODEL), w_cat)


GLA_PREP_ROWS = 256
GLA_SCAN_UNROLL = 4


def _log_sigmoid(x):
    return jnp.minimum(x, 0.0) - jnp.log1p(jnp.exp(-jnp.abs(x)))


def _gla_kernel(q_ref, k_ref, v_ref, r_ref, alow_ref, w2_ref, gb_ref, nw_ref, o_ref,
                qtf, kef, qtb, keb, decf, decb, of_s, ob_s, stf, stb):
    L = q_ref.shape[0]
    C = GLA_CHUNK
    DK = GLA_DK
    n_chunks = L // C
    R = min(GLA_PREP_ROWS, L)
    cpr = R // C
    scale = DK ** -0.5
    nt = (((1,), (1,)), ((), ()))
    tn = (((0,), (0,)), ((), ()))

    ri = lax.broadcasted_iota(jnp.int32, (R, R), 0)
    ci = lax.broadcasted_iota(jnp.int32, (R, R), 1)
    same = (ri // C) == (ci // C)
    mask_f = same & (ri >= ci)
    mask_b = same & (ri < ci)
    tri = jnp.where(mask_f, 1.0, 0.0).astype(BF16)
    w2 = jnp.concatenate([w2_ref[0], w2_ref[1]], axis=1).astype(BF16)
    gb = jnp.concatenate([gb_ref[0], gb_ref[1]], axis=1)

    def prep(g, carry):
        rows = pl.ds(pl.multiple_of(g * R, R), R)
        logits = jnp.dot(alow_ref[rows, :].astype(BF16), w2, preferred_element_type=F32) + gb
        lg = _log_sigmoid(logits) * (1.0 / GLA_GATE_TEMP)
        hi = lg.astype(BF16)
        lo = (lg - hi.astype(F32)).astype(BF16)
        b = jnp.dot(tri, hi, preferred_element_type=F32) + jnp.dot(tri, lo, preferred_element_type=F32)
        tot = jnp.concatenate(
            [jnp.broadcast_to(b[c * C + C - 1:c * C + C, :], (C, 2 * DK)) for c in range(cpr)], axis=0)
        q = q_ref[rows, :].astype(F32) * scale
        k = k_ref[rows, :].astype(F32)
        vb = v_ref[rows, :]
        dtot = jnp.exp(tot)
        for d, (qt, ke, dec, mask, out) in enumerate(((qtf, kef, decf, mask_f, of_s), (qtb, keb, decb, mask_b, ob_s))):
            sl = slice(d * DK, (d + 1) * DK)
            bd, td, ld = b[:, sl], tot[:, sl], lg[:, sl]
            if d == 0:
                e_q, e_end = bd, td - bd
            else:
                e_q, e_end = td - bd + ld, bd - ld
            qtv = (q * jnp.exp(e_q)).astype(BF16)
            ktv = (k * jnp.exp(-e_q)).astype(BF16)
            qt[rows, :] = qtv
            ke[rows, :] = (k * jnp.exp(e_end)).astype(BF16)
            s = lax.dot_general(qtv, ktv, nt, preferred_element_type=F32)
            s = jnp.where(mask, s, 0.0).astype(BF16)
            out[rows, :] = jnp.dot(s, vb, preferred_element_type=F32)
            for c in range(cpr):
                dec[pl.ds(g * cpr + c, 1), :] = dtot[c * C:c * C + 1, sl]
        return carry

    lax.fori_loop(0, L // R, prep, 0, unroll=2)

    def chunk(n, st, qt, ke, dec, out):
        rows = pl.ds(pl.multiple_of(n * C, C), C)
        s = st[...]
        out[rows, :] += lax.dot_general(qt[rows, :], s.astype(BF16), nt, preferred_element_type=F32)
        upd = lax.dot_general(v_ref[rows, :], ke[rows, :], tn, preferred_element_type=F32)
        st[...] = dec[pl.ds(n, 1), :] * s + upd

    def step(i, carry):
        chunk(i, stf, qtf, kef, decf, of_s)
        chunk(n_chunks - 1 - i, stb, qtb, keb, decb, ob_s)
        return carry

    stf[...] = jnp.zeros_like(stf)
    stb[...] = jnp.zeros_like(stb)
    lax.fori_loop(0, n_chunks, step, 0, unroll=GLA_SCAN_UNROLL)

    o = of_s[...] + ob_s[...]
    o = o * lax.rsqrt(jnp.mean(o * o, axis=-1, keepdims=True) + EPS) * nw_ref[...]
    r = r_ref[...].astype(F32)
    o_ref[...] = (o * (r * jax.nn.sigmoid(r))).astype(BF16)


def _gla(qkvr, alow, w2, gb, norm_w, B, L):
    T = B * L
    H = GLA_HEADS
    kq = GLA_KEY // GLA_DK
    kv = 2 * GLA_KEY // GLA_DV
    kr = (2 * GLA_KEY + GLA_VAL) // GLA_DV
    return pl.pallas_call(
        _gla_kernel,
        grid=(B, H),
        in_specs=[
            pl.BlockSpec((L, GLA_DK), lambda b, h: (b, h)),
            pl.BlockSpec((L, GLA_DK), lambda b, h: (b, kq + h)),
            pl.BlockSpec((L, GLA_DV), lambda b, h: (b, kv + h)),
            pl.BlockSpec((L, GLA_DV), lambda b, h: (b, kr + h)),
            pl.BlockSpec((L, ALOW_PAD), lambda b, h: (b, 0)),
            pl.BlockSpec((2, ALOW_PAD, GLA_DK), lambda b, h: (0, 0, h)),
            pl.BlockSpec((2, 1, GLA_DK), lambda b, h: (0, 0, h)),
            pl.BlockSpec((1, GLA_DV), lambda b, h: (0, 0)),
        ],
        out_specs=pl.BlockSpec((L, GLA_DV), lambda b, h: (b, h)),
        out_shape=jax.ShapeDtypeStruct((T, GLA_VAL), BF16),
        scratch_shapes=[pltpu.VMEM((L, GLA_DK), BF16)] * 4
        + [pltpu.VMEM((L // GLA_CHUNK, GLA_DK), F32)] * 2
        + [pltpu.VMEM((L, GLA_DV), F32)] * 2
        + [pltpu.VMEM((GLA_DV, GLA_DK), F32)] * 2,
        compiler_params=_cparams("parallel", "parallel"),
        name="gla",
    )(qkvr, qkvr, qkvr, qkvr, alow, w2, gb, norm_w.reshape(1, GLA_DV))


HY_WT = 256


def _hy_filter_kernel(z_ref, t_ref, fr_ref, w1_ref, b1_ref, w2_ref, b2_ref, w3f_ref, w3b_ref, dec_ref, o_ref, h_s):
    @pl.when(pl.program_id(0) == 0)
    def _():
        fr = fr_ref[...]
        h1 = jnp.sin(fr * (jnp.dot(z_ref[...], w1_ref[...], preferred_element_type=F32, precision=HIGHEST) + b1_ref[...]))
        h_s[...] = jnp.sin(fr * (jnp.dot(h1, w2_ref[...], preferred_element_type=F32, precision=HIGHEST) + b2_ref[...]))

    h = h_s[...]
    t = t_ref[...]
    row = lax.broadcasted_iota(jnp.int32, (z_ref.shape[0], 1), 0)
    hf = jnp.dot(h, w3f_ref[...], preferred_element_type=F32, precision=HIGHEST)
    hf = hf * jnp.exp(-t * jnp.abs(dec_ref[0:1, :]))
    hb = jnp.dot(h, w3b_ref[...], preferred_element_type=F32, precision=HIGHEST)
    hb = hb * jnp.exp(-t * jnp.abs(dec_ref[1:2, :]))
    hb = jnp.where(row == 0, 0.0, hb)
    ss = jnp.sum(hf * hf, axis=0, keepdims=True) + jnp.sum(hb * hb, axis=0, keepdims=True)
    inv = lax.rsqrt(ss + EPS)
    o_ref[0] = hf * inv
    o_ref[1] = hb * inv


def _hy_filter(L, hy_freq, w1, b1, w2, b2, w3, hy_decay):
    W = HY_WIDTH
    t = np.linspace(0.0, 1.0, L, dtype=np.float32)[:, None]
    omega = (np.float32(2.0 * math.pi) * np.arange(L, dtype=np.float32) / np.float32(L)).astype(np.float32)
    f = np.linspace(1e-4, HY_BANDS - 1, HY_BANDS, dtype=np.float32)
    ang = (omega[:, None] * f[None, :]).astype(np.float32)
    z = np.concatenate([t, np.cos(ang), -np.sin(ang)], axis=-1).astype(np.float32)
    nw = W // HY_WT
    return pl.pallas_call(
        _hy_filter_kernel,
        grid=(nw,),
        in_specs=[
            _const_spec((L, HY_EMB)),
            _const_spec((L, 1)),
            _const_spec((1, HY_FFN)),
            _const_spec((HY_EMB, HY_FFN)),
            _const_spec((1, HY_FFN)),
            _const_spec((HY_FFN, HY_FFN)),
            _const_spec((1, HY_FFN)),
            pl.BlockSpec((HY_FFN, HY_WT), lambda j: (0, j)),
            pl.BlockSpec((HY_FFN, HY_WT), lambda j: (0, nw + j)),
            pl.BlockSpec((2, HY_WT), lambda j: (0, j)),
        ],
        out_specs=pl.BlockSpec((2, L, HY_WT), lambda j: (0, 0, j)),
        out_shape=jax.ShapeDtypeStruct((2, L, W), F32),
        scratch_shapes=[pltpu.VMEM((L, HY_FFN), F32)],
        compiler_params=_cparams("arbitrary"),
        name="hy_filter",
    )(jnp.asarray(z), jnp.asarray(t), hy_freq.reshape(1, HY_FFN), w1, b1.reshape(1, HY_FFN), w2,
      b2.reshape(1, HY_FFN), w3, w3, hy_decay)


def _hy_pre_kernel(x0_ref, x1_ref, v_ref, w0_ref, w1_ref, wv_ref, b0_ref, b1_ref, bv_ref, z_ref, x0o_ref):
    L = x0_ref.shape[0]
    row = lax.broadcasted_iota(jnp.int32, (L, 1), 0)

    def conv3(u_ref, w_ref, b_ref):
        u = u_ref[...].astype(F32)
        prev = jnp.where(row == 0, 0.0, pltpu.roll(u, 1, 0))
        nxt = jnp.where(row == L - 1, 0.0, pltpu.roll(u, L - 1, 0))
        return w_ref[0:1, :] * prev + w_ref[1:2, :] * u + w_ref[2:3, :] * nxt + b_ref[...]

    x0o_ref[...] = conv3(x0_ref, w0_ref, b0_ref).astype(BF16)
    z_ref[...] = (conv3(v_ref, wv_ref, bv_ref) * conv3(x1_ref, w1_ref, b1_ref)).astype(BF16)


def _hy_pre(hy, conv_w, conv_b, B, L):
    W = HY_WIDTH
    nw = W // HY_WT
    cb = conv_b.reshape(1, 3 * W)
    blk = lambda off: pl.BlockSpec((L, HY_WT), lambda b, j: (b, off + j))
    wblk = lambda off: pl.BlockSpec((3, HY_WT), lambda b, j: (0, off + j))
    bblk = lambda off: pl.BlockSpec((1, HY_WT), lambda b, j: (0, off + j))
    return pl.pallas_call(
        _hy_pre_kernel,
        grid=(B, nw),
        in_specs=[blk(0), blk(nw), blk(2 * nw), wblk(0), wblk(nw), wblk(2 * nw), bblk(0), bblk(nw), bblk(2 * nw)],
        out_specs=[pl.BlockSpec((L, HY_WT), lambda b, j: (b, j))] * 2,
        out_shape=[jax.ShapeDtypeStruct((B * L, W), BF16)] * 2,
        compiler_params=_cparams("parallel", "parallel"),
        name="hy_pre",
    )(hy, hy, hy, conv_w, conv_w, conv_w, cb, cb, cb)


DFT_TK = 256
DFT_TT = 512
DFT_WT = 512


def _dft_tables(L):
    N = 2 * L
    nf = L + 1
    kfp = -(-nf // DFT_TK) * DFT_TK
    ang = lambda a, b: (2.0 * np.pi / N) * ((np.outer(a, b)) % N)
    i = np.arange(DFT_TK)
    n = np.arange(L)
    k = np.arange(kfp)
    k0 = np.arange(0, kfp, DFT_TK)
    t0 = np.arange(0, L, DFT_TK)
    alpha = np.where((k == 0) | (k == L), 1.0, 2.0) / N
    alpha = np.where(k <= L, alpha, 0.0)
    f32 = lambda a: jnp.asarray(a.astype(np.float32))
    return dict(
        kfp=kfp,
        fwd_e=(f32(np.cos(ang(i, n))), f32(np.sin(ang(i, n)))),
        fwd_r=(f32(np.cos(ang(k0, n))), f32(np.sin(ang(k0, n)))),
        inv_e=(f32(np.cos(ang(i, k))), f32(np.sin(ang(i, k)))),
        inv_r=(f32(np.cos(ang(t0, k))), f32(np.sin(ang(t0, k)))),
        alpha=f32(alpha[None, :]),
    )


def _dft_fwd_kernel(ec_ref, es_ref, rc_ref, rs_ref, z_ref, kr_ref, ki_ref, yr_ref, yi_ref, fc_s, fs_s, *, use_filter):
    @pl.when(pl.program_id(1) == 0)
    def _():
        ec, es, rc, rs = ec_ref[...], es_ref[...], rc_ref[0], rs_ref[0]
        fc_s[...] = (ec * rc - es * rs).astype(BF16)
        fs_s[...] = (es * rc + ec * rs).astype(BF16)

    z = z_ref[...]
    if use_filter:
        xr = jnp.dot(fc_s[...], z, preferred_element_type=F32)
        xi = -jnp.dot(fs_s[...], z, preferred_element_type=F32)
        kr, ki = kr_ref[...], ki_ref[...]
        yr_ref[...] = (xr * kr - xi * ki).astype(yr_ref.dtype)
        yi_ref[...] = (xr * ki + xi * kr).astype(yi_ref.dtype)
    else:
        @pl.when(pl.program_id(1) == 0)
        def _():
            yr_ref[...] = jnp.dot(fc_s[...], z, preferred_element_type=F32).astype(yr_ref.dtype)
            yi_ref[...] = jnp.zeros_like(yi_ref)

        @pl.when(pl.program_id(1) != 0)
        def _():
            yr_ref[...] = jnp.zeros_like(yr_ref)
            yi_ref[...] = (-jnp.dot(fs_s[...], z, preferred_element_type=F32)).astype(yi_ref.dtype)


def _dft_fwd(tabs, z3, kr, ki, use_filter, out_dtype):
    Bz, L, W = z3.shape
    kfp = tabs["kfp"]
    nk = kfp // DFT_TK
    (ec, es), (rc, rs) = tabs["fwd_e"], tabs["fwd_r"]
    return pl.pallas_call(
        functools.partial(_dft_fwd_kernel, use_filter=use_filter),
        grid=(nk, Bz),
        in_specs=[
            _const_spec((DFT_TK, L)),
            _const_spec((DFT_TK, L)),
            pl.BlockSpec((1, 1, L), lambda k, b: (k, 0, 0)),
            pl.BlockSpec((1, 1, L), lambda k, b: (k, 0, 0)),
            pl.BlockSpec((None, L, W), lambda k, b: (b, 0, 0)),
            pl.BlockSpec((DFT_TK, W), lambda k, b: (k, 0)),
            pl.BlockSpec((DFT_TK, W), lambda k, b: (k, 0)),
        ],
        out_specs=[pl.BlockSpec((None, DFT_TK, W), lambda k, b: (b, k, 0))] * 2,
        out_shape=[jax.ShapeDtypeStruct((Bz, kfp, W), out_dtype)] * 2,
        scratch_shapes=[pltpu.VMEM((DFT_TK, L), BF16)] * 2,
        compiler_params=_cparams("arbitrary", "arbitrary"),
        name="dft_fwd_filter" if use_filter else "dft_fwd",
    )(ec, es, rc.reshape(nk, 1, L), rs.reshape(nk, 1, L), z3, kr, ki)


def _dft_inv_kernel(ec_ref, es_ref, rc_ref, rs_ref, al_ref, yr_ref, yi_ref, z_ref, x0_ref, bias_ref, o_ref, gc_s, gs_s):
    @pl.when((pl.program_id(1) == 0) & (pl.program_id(2) == 0))
    def _():
        ec, es, al = ec_ref[...], es_ref[...], al_ref[...]
        for s in range(DFT_TT // DFT_TK):
            rc, rs = rc_ref[s], rs_ref[s]
            rows = slice(s * DFT_TK, (s + 1) * DFT_TK)
            gc_s[rows, :] = (al * (ec * rc - es * rs)).astype(BF16)
            gs_s[rows, :] = (-al * (es * rc + ec * rs)).astype(BF16)

    y = jnp.dot(gc_s[...], yr_ref[...], preferred_element_type=F32)
    y = y + jnp.dot(gs_s[...], yi_ref[...], preferred_element_type=F32)
    z = z_ref[...].astype(F32)
    o_ref[...] = ((y + z * bias_ref[...]) * x0_ref[...].astype(F32)).astype(BF16)


def _dft_inv(tabs, yr, yi, z3, x03, hy_bias):
    B, L, W = z3.shape
    kfp = tabs["kfp"]
    (ec, es), (rc, rs) = tabs["inv_e"], tabs["inv_r"]
    tt = min(DFT_TT, L)
    sub = tt // DFT_TK
    nt = L // tt
    wt = DFT_WT
    return pl.pallas_call(
        _dft_inv_kernel,
        grid=(nt, B, W // wt),
        in_specs=[
            _const_spec((DFT_TK, kfp)),
            _const_spec((DFT_TK, kfp)),
            pl.BlockSpec((sub, 1, kfp), lambda t, b, w: (t, 0, 0)),
            pl.BlockSpec((sub, 1, kfp), lambda t, b, w: (t, 0, 0)),
            _const_spec((1, kfp)),
            pl.BlockSpec((None, kfp, wt), lambda t, b, w: (b, 0, w)),
            pl.BlockSpec((None, kfp, wt), lambda t, b, w: (b, 0, w)),
            pl.BlockSpec((None, tt, wt), lambda t, b, w: (b, t, w)),
            pl.BlockSpec((None, tt, wt), lambda t, b, w: (b, t, w)),
            pl.BlockSpec((1, wt), lambda t, b, w: (0, w)),
        ],
        out_specs=pl.BlockSpec((None, tt, wt), lambda t, b, w: (b, t, w)),
        out_shape=jax.ShapeDtypeStruct((B, L, W), BF16),
        scratch_shapes=[pltpu.VMEM((tt, kfp), BF16)] * 2,
        compiler_params=_cparams("arbitrary", "arbitrary", "arbitrary"),
        name="dft_inv",
    )(ec, es, rc.reshape(L // DFT_TK, 1, kfp), rs.reshape(L // DFT_TK, 1, kfp), tabs["alpha"], yr, yi, z3, x03,
      hy_bias.reshape(1, W))


def _merge_kernel(og_ref, oh_ref, gates_ref, x_ref, wbg_ref, wbh_ref, wo_ref, n2_ref, rw_ref, rb_ref,
                  h1_ref, hn3_ref, idx_ref, wts_ref):
    mg = jnp.dot(og_ref[...], wbg_ref[...], preferred_element_type=F32)
    mh = jnp.dot(oh_ref[...], wbh_ref[...], preferred_element_type=F32)
    gg = jax.nn.sigmoid(gates_ref[:, :D_MODEL].astype(F32))
    gh = jax.nn.sigmoid(gates_ref[:, D_MODEL:].astype(F32))
    merged = (gg * mg + gh * mh).astype(BF16)
    h1 = x_ref[...] + jnp.dot(merged, wo_ref[...], preferred_element_type=F32)
    h1_ref[...] = h1
    hn = h1 * lax.rsqrt(jnp.mean(h1 * h1, axis=-1, keepdims=True) + EPS) * n2_ref[...]
    _store_token_rows(hn3_ref, hn)
    hh = hn.astype(BF16)
    hl = (hn - hh.astype(F32)).astype(BF16)
    p1 = jnp.dot(hh, rw_ref[...], preferred_element_type=F32)
    p2 = jnp.dot(hl, rw_ref[:, :ROUTE_PAD], preferred_element_type=F32)
    logits = p1[:, :ROUTE_PAD] + p1[:, ROUTE_PAD:] + p2 + rb_ref[...]
    lane = lax.broadcasted_iota(jnp.int32, logits.shape, 1)
    logits = jnp.where(lane < N_EXPERTS, logits, -jnp.inf)
    idx_out = jnp.zeros(logits.shape, jnp.int32)
    val_out = jnp.zeros(logits.shape, F32)
    v0 = None
    for j in range(TOP_K):
        m = jnp.max(logits, axis=-1, keepdims=True)
        sel = jnp.min(jnp.where(logits == m, lane, ROUTE_PAD), axis=-1, keepdims=True)
        if j == 0:
            v0 = m
        idx_out = jnp.where(lane == j, sel, idx_out)
        val_out = jnp.where(lane == j, jnp.exp(m - v0), val_out)
        logits = jnp.where(lane == sel, -jnp.inf, logits)
    idx_ref[...] = idx_out
    wts_ref[...] = val_out / jnp.sum(val_out, axis=-1, keepdims=True)


def _merge(o_gla, o_hy, gates, x2, wbg, wbh, wo, norm2_w, rw_pad, rb_pad):
    T = x2.shape[0]
    tm = min(512, T)
    row = lambda n: pl.BlockSpec((tm, n), lambda i: (i, 0))
    return pl.pallas_call(
        _merge_kernel,
        grid=(T // tm,),
        in_specs=[row(GLA_VAL), row(HY_WIDTH), row(N_GATES), row(D_MODEL),
                  _const_spec((GLA_VAL, D_MODEL)), _const_spec((HY_WIDTH, D_MODEL)), _const_spec((D_MODEL, D_MODEL)),
                  _const_spec((1, D_MODEL)), _const_spec((D_MODEL, 2 * ROUTE_PAD)), _const_spec((1, ROUTE_PAD))],
        out_specs=[row(D_MODEL), pl.BlockSpec((tm * TOK_ROWS, LANES), lambda i: (i, 0)), row(ROUTE_PAD), row(ROUTE_PAD)],
        out_shape=[jax.ShapeDtypeStruct((T, D_MODEL), F32), jax.ShapeDtypeStruct((T * TOK_ROWS, LANES), F32),
                   jax.ShapeDtypeStruct((T, ROUTE_PAD), jnp.int32), jax.ShapeDtypeStruct((T, ROUTE_PAD), F32)],
        compiler_params=_cparams("parallel"),
        name="merge_route",
    )(o_gla, o_hy, gates, x2, wbg, wbh, wo, norm2_w.reshape(1, D_MODEL), rw_pad, rb_pad)


MOE_TM = 512
MOE_TC = 256
DMA_UNROLL = 8


DISPATCH_SLOTS = 3


def _dispatch_kernel(last_ref, pos_ref, hn_ref, xs_ref, zero_s, buf, sem, lsems, ssems):
    i = pl.program_id(0)
    n = pl.num_programs(0)
    tc = pos_ref.shape[2] // TOP_K
    rows = tc * TOK_ROWS
    zrows = zero_s.shape[0]
    slot = lax.rem(i, DISPATCH_SLOTS)
    prev_slot = lax.rem(i + DISPATCH_SLOTS - 1, DISPATCH_SLOTS)

    def load(blk, s):
        src = hn_ref.at[pl.ds(pl.multiple_of(blk * rows, rows), rows)]
        return pltpu.make_async_copy(src, buf.at[s], lsems.at[s])

    def wait_copies(s):
        for _ in range(TOP_K):
            pltpu.make_async_copy(buf.at[s], xs_ref.at[pl.ds(0, rows)], ssems.at[s]).wait()

    @pl.when(i == 0)
    def _():
        load(0, 0).start()

        @pl.when(n > 1)
        def _():
            load(1, 1).start()

        zero_s[...] = jnp.zeros_like(zero_s)
        n_fill = last_ref.shape[0]

        def zfill(e, c):
            @pl.when(last_ref[e] >= 0)
            def _():
                dst = pl.multiple_of(last_ref[e] * zrows, zrows)
                pltpu.make_async_copy(zero_s, xs_ref.at[pl.ds(dst, zrows)], sem).start()
            return c
        lax.fori_loop(0, n_fill, zfill, 0)

        def zwait(e, c):
            @pl.when(last_ref[e] >= 0)
            def _():
                pltpu.make_async_copy(zero_s, xs_ref.at[pl.ds(0, zrows)], sem).wait()
            return c
        lax.fori_loop(0, n_fill, zwait, 0)

    load(i, slot).wait()

    def issue(r, c):
        src = buf.at[slot, pl.ds(pl.multiple_of(r * TOK_ROWS, TOK_ROWS), TOK_ROWS)]
        for j in range(TOP_K):
            dst = pl.multiple_of(pos_ref[0, 0, r * TOP_K + j] * TOK_ROWS, TOK_ROWS)
            pltpu.make_async_copy(src, xs_ref.at[pl.ds(dst, TOK_ROWS)], ssems.at[slot]).start(priority=j % 2)
        return c
    lax.fori_loop(0, tc, issue, 0, unroll=DMA_UNROLL // TOP_K)

    @pl.when(i > 0)
    def _():
        wait_copies(prev_slot)

    @pl.when(i + 2 < n)
    def _():
        load(i + 2, prev_slot).start()

    @pl.when(i == n - 1)
    def _():
        wait_copies(slot)


def _dispatch(hn3, pos, last_tile, P):
    T = hn3.shape[0] // TOK_ROWS
    tc = min(MOE_TC, T)
    nt = T // tc
    return pl.pallas_call(
        _dispatch_kernel,
        grid_spec=pltpu.PrefetchScalarGridSpec(
            num_scalar_prefetch=1,
            grid=(nt,),
            in_specs=[pl.BlockSpec((1, 1, tc * TOP_K), lambda i, lt: (i, 0, 0), memory_space=pltpu.SMEM),
                      pl.BlockSpec(memory_space=pl.ANY)],
            out_specs=pl.BlockSpec(memory_space=pl.ANY),
            scratch_shapes=[pltpu.VMEM((MOE_TM * TOK_ROWS, LANES), F32),
                            pltpu.VMEM((DISPATCH_SLOTS, tc * TOK_ROWS, LANES), F32),
                            pltpu.SemaphoreType.DMA(()),
                            pltpu.SemaphoreType.DMA((DISPATCH_SLOTS,)), pltpu.SemaphoreType.DMA((DISPATCH_SLOTS,))],
        ),
        out_shape=jax.ShapeDtypeStruct((P * TOK_ROWS, LANES), F32),
        compiler_params=_cparams("arbitrary"),
        name="moe_dispatch",
    )(last_tile, pos.reshape(nt, 1, tc * TOP_K), hn3)


def _expert_kernel(te_ref, nused_ref, xs_ref, wgu_ref, bgu_ref, wd_ref, bd_ref, o_ref, wgu_s, wd_s):
    i = pl.program_id(0)
    prev = te_ref[jnp.maximum(i - 1, 0)]

    @pl.when((i == 0) | (te_ref[i] != prev))
    def _():
        wgu_s[...] = wgu_ref[0].astype(BF16)
        wd_s[...] = wd_ref[0].astype(BF16)

    @pl.when(i < nused_ref[0])
    def _():
        x = _load_token_rows(xs_ref, 0, xs_ref.shape[0] // TOK_ROWS).astype(BF16)
        gu = jnp.dot(x, wgu_s[...], preferred_element_type=F32) + bgu_ref[0]
        gate = jnp.minimum(gu[:, :D_FF], SWIGLU_LIMIT)
        up = jnp.clip(gu[:, D_FF:], -SWIGLU_LIMIT, SWIGLU_LIMIT)
        act = gate * jax.nn.sigmoid(SWIGLU_ALPHA * gate) * (up + 1.0)
        y = jnp.dot(act.astype(BF16), wd_s[...], preferred_element_type=F32) + bd_ref[0]
        _store_token_rows(o_ref, y)

    @pl.when(i >= nused_ref[0])
    def _():
        o_ref[...] = jnp.zeros_like(o_ref)


def _experts(xs3, tile_expert, n_used, w_gate_up, b_gate_up, w_down, b_down):
    P = xs3.shape[0] // TOK_ROWS
    tm = MOE_TM
    nt = P // tm
    live = lambda i, te, nu: jnp.minimum(i, nu[0] - 1)
    return pl.pallas_call(
        _expert_kernel,
        grid_spec=pltpu.PrefetchScalarGridSpec(
            num_scalar_prefetch=2,
            grid=(nt,),
            in_specs=[
                pl.BlockSpec((tm * TOK_ROWS, LANES), lambda i, te, nu: (live(i, te, nu), 0)),
                pl.BlockSpec((1, D_MODEL, 2 * D_FF), lambda i, te, nu: (te[i], 0, 0)),
                pl.BlockSpec((1, 1, 2 * D_FF), lambda i, te, nu: (te[i], 0, 0)),
                pl.BlockSpec((1, D_FF, D_MODEL), lambda i, te, nu: (te[i], 0, 0)),
                pl.BlockSpec((1, 1, D_MODEL), lambda i, te, nu: (te[i], 0, 0)),
            ],
            out_specs=pl.BlockSpec((tm * TOK_ROWS, LANES), lambda i, te, nu: (i, 0)),
            scratch_shapes=[pltpu.VMEM((D_MODEL, 2 * D_FF), BF16), pltpu.VMEM((D_FF, D_MODEL), BF16)],
        ),
        out_shape=jax.ShapeDtypeStruct((P * TOK_ROWS, LANES), F32),
        compiler_params=_cparams("arbitrary"),
        name="moe_experts",
    )(tile_expert, n_used, xs3, w_gate_up, b_gate_up.reshape(N_EXPERTS, 1, 2 * D_FF), w_down,
      b_down.reshape(N_EXPERTS, 1, D_MODEL))


def _combine_kernel(pos_ref, posn_ref, wts_ref, h1_ref, nf_ref, ys_ref, o_ref, buf, sems):
    i = pl.program_id(0)
    n = pl.num_programs(0)
    tc = o_ref.shape[0]
    slot = lax.rem(i, 2)

    def gather(p_ref, s):
        def issue(r, c):
            for j in range(TOP_K):
                src = pl.multiple_of(p_ref[0, 0, r * TOP_K + j] * TOK_ROWS, TOK_ROWS)
                dst = pl.multiple_of((j * tc + r) * TOK_ROWS, TOK_ROWS)
                pltpu.make_async_copy(ys_ref.at[pl.ds(src, TOK_ROWS)], buf.at[s, pl.ds(dst, TOK_ROWS)],
                                      sems.at[s]).start(priority=j % 2)
            return c
        lax.fori_loop(0, tc, issue, 0, unroll=DMA_UNROLL // TOP_K)

    @pl.when(i == 0)
    def _():
        gather(pos_ref, 0)

    @pl.when(i + 1 < n)
    def _():
        gather(posn_ref, 1 - slot)

    pltpu.make_async_copy(ys_ref.at[pl.ds(0, TOP_K * tc * TOK_ROWS)], buf.at[slot], sems.at[slot]).wait()

    w = wts_ref[...]
    h = h1_ref[...]
    for j in range(TOP_K):
        h = h + w[:, j:j + 1] * _load_token_rows(buf.at[slot], j * tc, tc)
    o_ref[...] = h * lax.rsqrt(jnp.mean(h * h, axis=-1, keepdims=True) + EPS) * nf_ref[...]


def _combine(pos, wts, h1, norm_f_w, ys3):
    T = h1.shape[0]
    tc = min(MOE_TC, T)
    nt = T // tc
    pos3 = pos.reshape(nt, 1, tc * TOP_K)
    return pl.pallas_call(
        _combine_kernel,
        grid=(nt,),
        in_specs=[
            pl.BlockSpec((1, 1, tc * TOP_K), lambda i: (i, 0, 0), memory_space=pltpu.SMEM),
            pl.BlockSpec((1, 1, tc * TOP_K), lambda i: (jnp.minimum(i + 1, nt - 1), 0, 0), memory_space=pltpu.SMEM),
            pl.BlockSpec((tc, ROUTE_PAD), lambda i: (i, 0)),
            pl.BlockSpec((tc, D_MODEL), lambda i: (i, 0)),
            _const_spec((1, D_MODEL)),
            pl.BlockSpec(memory_space=pl.ANY),
        ],
        out_specs=pl.BlockSpec((tc, D_MODEL), lambda i: (i, 0)),
        out_shape=jax.ShapeDtypeStruct((T, D_MODEL), F32),
        scratch_shapes=[pltpu.VMEM((2, TOP_K * tc * TOK_ROWS, LANES), F32), pltpu.SemaphoreType.DMA((2,))],
        compiler_params=_cparams("arbitrary"),
        name="moe_combine",
    )(pos3, pos3, wts, h1, norm_f_w.reshape(1, D_MODEL), ys3)


def _route_plan(top_idx, T):
    tm = MOE_TM
    nt = (T * TOP_K) // tm + N_EXPERTS
    e = top_idx.reshape(-1)
    oh = (e[:, None] == jnp.arange(N_EXPERTS, dtype=jnp.int32)[None, :]).astype(jnp.int32)
    csum = jnp.cumsum(oh, axis=0)
    rank = jnp.sum(csum * oh, axis=1) - 1
    counts = csum[-1]
    tiles = (counts + tm - 1) // tm
    tile_end = jnp.cumsum(tiles)
    offs = (tile_end - tiles) * tm
    pos = jnp.sum(oh * offs[None, :], axis=1) + rank
    tile_id = jnp.arange(nt, dtype=jnp.int32)
    tile_expert = jnp.minimum(jnp.sum((tile_end[None, :] <= tile_id[:, None]).astype(jnp.int32), axis=1),
                              N_EXPERTS - 1).astype(jnp.int32)
    tail = tile_end[-1] + jnp.arange(N_EXPERTS, dtype=jnp.int32)
    last_tile = jnp.concatenate([jnp.where(tiles > 0, tile_end - 1, -1), jnp.where(tail < nt, tail, -1)]).astype(jnp.int32)
    n_used = tile_end[-1:].astype(jnp.int32)
    return pos.astype(jnp.int32), tile_expert, last_tile, n_used, nt * tm


def kernel(x, norm1_w, w_in, gla_gate_w2, gla_gate_b, gla_norm_w, hy_conv_w, hy_conv_b, hy_freq, hy_ffn_w1, hy_ffn_b1,
           hy_ffn_w2, hy_ffn_b2, hy_ffn_w3, hy_decay, hy_bias, w_branch_gla, w_branch_hy, w_out, norm2_w, router_w,
           router_b, w_gate_up, b_gate_up, w_down, b_down, norm_f_w):
    B, L, D = x.shape
    T = B * L
    x2 = x.reshape(T, D)

    s_a = N_QKVR
    s_h = s_a + 2 * GLA_RANK
    w_cat = jnp.concatenate(
        [w_in[:, :s_a], w_in[:, s_h:], w_in[:, s_a:s_h], jnp.zeros((D, ALOW_PAD - 2 * GLA_RANK), w_in.dtype)],
        axis=1).astype(BF16)
    qkvr, hy, gates, alow = _in_proj(x2, norm1_w, w_cat)

    w2 = jnp.zeros((2, ALOW_PAD, GLA_KEY), F32)
    w2 = w2.at[0, :GLA_RANK].set(gla_gate_w2[0]).at[1, GLA_RANK:2 * GLA_RANK].set(gla_gate_w2[1])
    o_gla = _gla(qkvr, alow, w2, gla_gate_b.reshape(2, 1, GLA_KEY), gla_norm_w, B, L)

    taps = _hy_filter(L, hy_freq, hy_ffn_w1, hy_ffn_b1, hy_ffn_w2, hy_ffn_b2, hy_ffn_w3, hy_decay)
    tabs = _dft_tables(L)
    sd = jnp.stack([taps[0] + taps[1], taps[0] - taps[1]]).astype(BF16)
    dummy = jnp.zeros((tabs["kfp"], HY_WIDTH), F32)
    fr, fi = _dft_fwd(tabs, sd, dummy, dummy, False, F32)
    z, x0c = _hy_pre(hy, hy_conv_w, hy_conv_b, B, L)
    z3 = z.reshape(B, L, HY_WIDTH)
    yr, yi = _dft_fwd(tabs, z3, fr[0], fi[1], True, BF16)
    o_hy = _dft_inv(tabs, yr, yi, z3, x0c.reshape(B, L, HY_WIDTH), hy_bias).reshape(T, HY_WIDTH)

    rw_f = jnp.zeros((D, ROUTE_PAD), F32).at[:, :N_EXPERTS].set(router_w)
    rw_hi = rw_f.astype(BF16)
    rw_pad = jnp.concatenate([rw_hi, (rw_f - rw_hi.astype(F32)).astype(BF16)], axis=1)
    rb_pad = jnp.zeros((1, ROUTE_PAD), F32).at[0, :N_EXPERTS].set(router_b)
    h1, hn3, idx_pad, wts_pad = _merge(o_gla, o_hy, gates, x2, w_branch_gla.astype(BF16), w_branch_hy.astype(BF16),
                                       w_out.astype(BF16), norm2_w, rw_pad, rb_pad)

    pos, tile_expert, last_tile, n_used, P = _route_plan(idx_pad[:, :TOP_K], T)
    xs3 = _dispatch(hn3, pos, last_tile, P)
    ys3 = _experts(xs3, tile_expert, n_used, w_gate_up, b_gate_up, w_down, b_down)
    out = _combine(pos, wts_pad, h1, norm_f_w, ys3)
    return out.reshape(B, L, D)
```

```python
import functools
import math

import numpy as np
import jax
import jax.numpy as jnp
from jax import lax
from jax.experimental import pallas as pl
from jax.experimental.pallas import tpu as pltpu

D_MODEL = 1024
EPS = 1e-6
GLA_HEADS = 4
GLA_DK = 128
GLA_DV = 256
GLA_KEY = GLA_HEADS * GLA_DK
GLA_VAL = GLA_HEADS * GLA_DV
GLA_RANK = 16
GLA_GATE_TEMP = 16.0
GLA_CHUNK = 64
HY_WIDTH = D_MODEL
HY_BANDS = 16
HY_EMB = 1 + 2 * HY_BANDS
HY_FFN = 64
N_EXPERTS = 32
TOP_K = 4
D_FF = D_MODEL
SWIGLU_LIMIT = 7.0
SWIGLU_ALPHA = 1.702

LANES = 128
ALOW_PAD = LANES
ROUTE_PAD = LANES
VMEM_LIMIT = 56 * 1024 * 1024

F32 = jnp.float32
BF16 = jnp.bfloat16
HIGHEST = lax.Precision.HIGHEST


def _cparams(*sem):
    return pltpu.CompilerParams(dimension_semantics=sem, vmem_limit_bytes=VMEM_LIMIT)


def _const_spec(shape):
    nd = len(shape)
    return pl.BlockSpec(shape, lambda *_: (0,) * nd, pipeline_mode=pl.Buffered(1))


TOK_ROWS = D_MODEL // LANES


def _store_token_rows(ref, val):
    n = val.shape[0]
    for s_ in range(TOK_ROWS):
        ref[pl.ds(s_, n, stride=TOK_ROWS), :] = val[:, s_ * LANES:(s_ + 1) * LANES]


def _load_token_rows(ref, first, n):
    return jnp.concatenate(
        [ref[pl.ds(first * TOK_ROWS + s_, n, stride=TOK_ROWS), :] for s_ in range(TOK_ROWS)], axis=1)


N_QKVR = 2 * GLA_KEY + 2 * GLA_VAL
N_HY = 3 * HY_WIDTH
N_GATES = 2 * D_MODEL
IN_COLS = N_QKVR + N_HY + N_GATES + ALOW_PAD


def _inproj_kernel(x_ref, nw_ref, w_ref, qkvr_ref, hy_ref, gates_ref, alow_ref):
    x = x_ref[...]
    ms = jnp.mean(x * x, axis=-1, keepdims=True)
    xn = (x * lax.rsqrt(ms + EPS) * nw_ref[...]).astype(BF16)
    c0, c1, c2 = N_QKVR, N_QKVR + N_HY, N_QKVR + N_HY + N_GATES
    qkvr_ref[...] = jnp.dot(xn, w_ref[:, :c0], preferred_element_type=F32).astype(BF16)
    hy_ref[...] = jnp.dot(xn, w_ref[:, c0:c1], preferred_element_type=F32).astype(BF16)
    gates_ref[...] = jnp.dot(xn, w_ref[:, c1:c2], preferred_element_type=F32).astype(BF16)
    alow_ref[...] = jnp.dot(xn, w_ref[:, c2:], preferred_element_type=F32)


def _in_proj(x2, norm1_w, w_cat):
    T = x2.shape[0]
    tm = min(512, T)
    return pl.pallas_call(
        _inproj_kernel,
        grid=(T // tm,),
        in_specs=[
            pl.BlockSpec((tm, D_MODEL), lambda i: (i, 0)),
            _const_spec((1, D_MODEL)),
            _const_spec((D_MODEL, IN_COLS)),
        ],
        out_specs=[
            pl.BlockSpec((tm, N_QKVR), lambda i: (i, 0)),
            pl.BlockSpec((tm, N_HY), lambda i: (i, 0)),
            pl.BlockSpec((tm, N_GATES), lambda i: (i, 0)),
            pl.BlockSpec((tm, ALOW_PAD), lambda i: (i, 0)),
        ],
        out_shape=[
            jax.ShapeDtypeStruct((T, N_QKVR), BF16),
            jax.ShapeDtypeStruct((T, N_HY), BF16),
            jax.ShapeDtypeStruct((T, N_GATES), BF16),
            jax.ShapeDtypeStruct((T, ALOW_PAD), F32),
        ],
        compiler_params=_cparams("parallel"),
        name="in_proj",
    )(x2, norm1_w.reshape(1, D_MODEL), w_cat)


GLA_PREP_ROWS = 256
GLA_SCAN_UNROLL = 4


def _log_sigmoid(x):
    return jnp.minimum(x, 0.0) - jnp.log1p(jnp.exp(-jnp.abs(x)))


def _gla_kernel(q_ref, k_ref, v_ref, r_ref, alow_ref, w2_ref, gb_ref, nw_ref, o_ref,
                qtf, kef, qtb, keb, decf, decb, of_s, ob_s, stf, stb):
    L = q_ref.shape[0]
    C = GLA_CHUNK
    DK = GLA_DK
    n_chunks = L // C
    R = min(GLA_PREP_ROWS, L)
    cpr = R // C
    scale = DK ** -0.5
    nt = (((1,), (1,)), ((), ()))
    tn = (((0,), (0,)), ((), ()))

    ri = lax.broadcasted_iota(jnp.int32, (R, R), 0)
    ci = lax.broadcasted_iota(jnp.int32, (R, R), 1)
    same = (ri // C) == (ci // C)
    mask_f = same & (ri >= ci)
    mask_b = same & (ri < ci)
    tri = jnp.where(mask_f, 1.0, 0.0).astype(BF16)
    w2 = jnp.concatenate([w2_ref[0], w2_ref[1]], axis=1).astype(BF16)
    gb = jnp.concatenate([gb_ref[0], gb_ref[1]], axis=1)

    def prep(g, carry):
        rows = pl.ds(pl.multiple_of(g * R, R), R)
        logits = jnp.dot(alow_ref[rows, :].astype(BF16), w2, preferred_element_type=F32) + gb
        lg = _log_sigmoid(logits) * (1.0 / GLA_GATE_TEMP)
        hi = lg.astype(BF16)
        lo = (lg - hi.astype(F32)).astype(BF16)
        b = jnp.dot(tri, hi, preferred_element_type=F32) + jnp.dot(tri, lo, preferred_element_type=F32)
        tot = jnp.concatenate(
            [jnp.broadcast_to(b[c * C + C - 1:c * C + C, :], (C, 2 * DK)) for c in range(cpr)], axis=0)
        q = q_ref[rows, :].astype(F32) * scale
        k = k_ref[rows, :].astype(F32)
        vb = v_ref[rows, :]
        dtot = jnp.exp(tot)
        for d, (qt, ke, dec, mask, out) in enumerate(((qtf, kef, decf, mask_f, of_s), (qtb, keb, decb, mask_b, ob_s))):
            sl = slice(d * DK, (d + 1) * DK)
            bd, td, ld = b[:, sl], tot[:, sl], lg[:, sl]
            if d == 0:
                e_q, e_end = bd, td - bd
            else:
                e_q, e_end = td - bd + ld, bd - ld
            qtv = (q * jnp.exp(e_q)).astype(BF16)
            ktv = (k * jnp.exp(-e_q)).astype(BF16)
            qt[rows, :] = qtv
            ke[rows, :] = (k * jnp.exp(e_end)).astype(BF16)
            s = lax.dot_general(qtv, ktv, nt, preferred_element_type=F32)
            s = jnp.where(mask, s, 0.0).astype(BF16)
            out[rows, :] = jnp.dot(s, vb, preferred_element_type=F32)
            for c in range(cpr):
                dec[pl.ds(g * cpr + c, 1), :] = dtot[c * C:c * C + 1, sl]
        return carry

    lax.fori_loop(0, L // R, prep, 0, unroll=2)

    def chunk(n, st, qt, ke, dec, out):
        rows = pl.ds(pl.multiple_of(n * C, C), C)
        s = st[...]
        out[rows, :] += lax.dot_general(qt[rows, :], s.astype(BF16), nt, preferred_element_type=F32)
        upd = lax.dot_general(v_ref[rows, :], ke[rows, :], tn, preferred_element_type=F32)
        st[...] = dec[pl.ds(n, 1), :] * s + upd

    def step(i, carry):
        chunk(i, stf, qtf, kef, decf, of_s)
        chunk(n_chunks - 1 - i, stb, qtb, keb, decb, ob_s)
        return carry

    stf[...] = jnp.zeros_like(stf)
    stb[...] = jnp.zeros_like(stb)
    lax.fori_loop(0, n_chunks, step, 0, unroll=GLA_SCAN_UNROLL)

    o = of_s[...] + ob_s[...]
    o = o * lax.rsqrt(jnp.mean(o * o, axis=-1, keepdims=True) + EPS) * nw_ref[...]
    r = r_ref[...].astype(F32)
    o_ref[...] = (o * (r * jax.nn.sigmoid(r))).astype(BF16)


def _gla(qkvr, alow, w2, gb, norm_w, B, L):
    T = B * L
    H = GLA_HEADS
    kq = GLA_KEY // GLA_DK
    kv = 2 * GLA_KEY // GLA_DV
    kr = (2 * GLA_KEY + GLA_VAL) // GLA_DV
    return pl.pallas_call(
        _gla_kernel,
        grid=(B, H),
        in_specs=[
            pl.BlockSpec((L, GLA_DK), lambda b, h: (b, h)),
            pl.BlockSpec((L, GLA_DK), lambda b, h: (b, kq + h)),
            pl.BlockSpec((L, GLA_DV), lambda b, h: (b, kv + h)),
            pl.BlockSpec((L, GLA_DV), lambda b, h: (b, kr + h)),
            pl.BlockSpec((L, ALOW_PAD), lambda b, h: (b, 0)),
            pl.BlockSpec((2, ALOW_PAD, GLA_DK), lambda b, h: (0, 0, h)),
            pl.BlockSpec((2, 1, GLA_DK), lambda b, h: (0, 0, h)),
            pl.BlockSpec((1, GLA_DV), lambda b, h: (0, 0)),
        ],
        out_specs=pl.BlockSpec((L, GLA_DV), lambda b, h: (b, h)),
        out_shape=jax.ShapeDtypeStruct((T, GLA_VAL), BF16),
        scratch_shapes=[pltpu.VMEM((L, GLA_DK), BF16)] * 4
        + [pltpu.VMEM((L // GLA_CHUNK, GLA_DK), F32)] * 2
        + [pltpu.VMEM((L, GLA_DV), F32)] * 2
        + [pltpu.VMEM((GLA_DV, GLA_DK), F32)] * 2,
        compiler_params=_cparams("parallel", "parallel"),
        name="gla",
    )(qkvr, qkvr, qkvr, qkvr, alow, w2, gb, norm_w.reshape(1, GLA_DV))


HY_WT = 256


def _hy_filter_kernel(z_ref, t_ref, fr_ref, w1_ref, b1_ref, w2_ref, b2_ref, w3f_ref, w3b_ref, dec_ref, o_ref, h_s):
    @pl.when(pl.program_id(0) == 0)
    def _():
        fr = fr_ref[...]
        h1 = jnp.sin(fr * (jnp.dot(z_ref[...], w1_ref[...], preferred_element_type=F32, precision=HIGHEST) + b1_ref[...]))
        h_s[...] = jnp.sin(fr * (jnp.dot(h1, w2_ref[...], preferred_element_type=F32, precision=HIGHEST) + b2_ref[...]))

    h = h_s[...]
    t = t_ref[...]
    row = lax.broadcasted_iota(jnp.int32, (z_ref.shape[0], 1), 0)
    hf = jnp.dot(h, w3f_ref[...], preferred_element_type=F32, precision=HIGHEST)
    hf = hf * jnp.exp(-t * jnp.abs(dec_ref[0:1, :]))
    hb = jnp.dot(h, w3b_ref[...], preferred_element_type=F32, precision=HIGHEST)
    hb = hb * jnp.exp(-t * jnp.abs(dec_ref[1:2, :]))
    hb = jnp.where(row == 0, 0.0, hb)
    ss = jnp.sum(hf * hf, axis=0, keepdims=True) + jnp.sum(hb * hb, axis=0, keepdims=True)
    inv = lax.rsqrt(ss + EPS)
    o_ref[0] = hf * inv
    o_ref[1] = hb * inv


def _hy_filter(L, hy_freq, w1, b1, w2, b2, w3, hy_decay):
    W = HY_WIDTH
    t = np.linspace(0.0, 1.0, L, dtype=np.float32)[:, None]
    omega = (np.float32(2.0 * math.pi) * np.arange(L, dtype=np.float32) / np.float32(L)).astype(np.float32)
    f = np.linspace(1e-4, HY_BANDS - 1, HY_BANDS, dtype=np.float32)
    ang = (omega[:, None] * f[None, :]).astype(np.float32)
    z = np.concatenate([t, np.cos(ang), -np.sin(ang)], axis=-1).astype(np.float32)
    nw = W // HY_WT
    return pl.pallas_call(
        _hy_filter_kernel,
        grid=(nw,),
        in_specs=[
            _const_spec((L, HY_EMB)),
            _const_spec((L, 1)),
            _const_spec((1, HY_FFN)),
            _const_spec((HY_EMB, HY_FFN)),
            _const_spec((1, HY_FFN)),
            _const_spec((HY_FFN, HY_FFN)),
            _const_spec((1, HY_FFN)),
            pl.BlockSpec((HY_FFN, HY_WT), lambda j: (0, j)),
            pl.BlockSpec((HY_FFN, HY_WT), lambda j: (0, nw + j)),
            pl.BlockSpec((2, HY_WT), lambda j: (0, j)),
        ],
        out_specs=pl.BlockSpec((2, L, HY_WT), lambda j: (0, 0, j)),
        out_shape=jax.ShapeDtypeStruct((2, L, W), F32),
        scratch_shapes=[pltpu.VMEM((L, HY_FFN), F32)],
        compiler_params=_cparams("arbitrary"),
        name="hy_filter",
    )(jnp.asarray(z), jnp.asarray(t), hy_freq.reshape(1, HY_FFN), w1, b1.reshape(1, HY_FFN), w2,
      b2.reshape(1, HY_FFN), w3, w3, hy_decay)


def _hy_pre_kernel(x0_ref, x1_ref, v_ref, w0_ref, w1_ref, wv_ref, b0_ref, b1_ref, bv_ref, z_ref, x0o_ref):
    L = x0_ref.shape[0]
    row = lax.broadcasted_iota(jnp.int32, (L, 1), 0)

    def conv3(u_ref, w_ref, b_ref):
        u = u_ref[...].astype(F32)
        prev = jnp.where(row == 0, 0.0, pltpu.roll(u, 1, 0))
        nxt = jnp.where(row == L - 1, 0.0, pltpu.roll(u, L - 1, 0))
        return w_ref[0:1, :] * prev + w_ref[1:2, :] * u + w_ref[2:3, :] * nxt + b_ref[...]

    x0o_ref[...] = conv3(x0_ref, w0_ref, b0_ref).astype(BF16)
    z_ref[...] = conv3(v_ref, wv_ref, bv_ref) * conv3(x1_ref, w1_ref, b1_ref)


def _hy_pre(hy, conv_w, conv_b, B, L):
    W = HY_WIDTH
    nw = W // HY_WT
    cb = conv_b.reshape(1, 3 * W)
    blk = lambda off: pl.BlockSpec((L, HY_WT), lambda b, j: (b, off + j))
    wblk = lambda off: pl.BlockSpec((3, HY_WT), lambda b, j: (0, off + j))
    bblk = lambda off: pl.BlockSpec((1, HY_WT), lambda b, j: (0, off + j))
    return pl.pallas_call(
        _hy_pre_kernel,
        grid=(B, nw),
        in_specs=[blk(0), blk(nw), blk(2 * nw), wblk(0), wblk(nw), wblk(2 * nw), bblk(0), bblk(nw), bblk(2 * nw)],
        out_specs=[pl.BlockSpec((L, HY_WT), lambda b, j: (b, j))] * 2,
        out_shape=[jax.ShapeDtypeStruct((B * L, W), F32), jax.ShapeDtypeStruct((B * L, W), BF16)],
        compiler_params=_cparams("parallel", "parallel"),
        name="hy_pre",
    )(hy, hy, hy, conv_w, conv_w, conv_w, cb, cb, cb)


FFT_L = LANES
FFT_LB = 16
FFT_KB = 8
FFT_WT = 2 * LANES


def _fft_tables(L):
    N = 2 * L
    C = L // FFT_L
    n1 = N // FFT_L
    l = np.arange(FFT_L)
    c = np.arange(C)
    k2 = np.arange(n1)
    ang1 = 2 * np.pi * (((k2[None, :, None] * c[None, None, :]) % n1) / n1
                        + ((l[:, None, None] * k2[None, :, None]) % N) / N)
    f1 = np.concatenate([np.cos(ang1), -np.sin(ang1)], axis=1)
    g1 = np.stack([np.cos(ang1), -np.sin(ang1)], axis=-1)
    g1 = np.transpose(g1, (0, 2, 1, 3)).reshape(FFT_L, C, 2 * n1) / N
    a2 = 2 * np.pi * np.outer(l, l) / FFT_L
    cs, sn = np.cos(a2), np.sin(a2)
    f2 = np.block([[cs, sn], [-sn, cs]])
    g2 = np.block([[cs, -sn], [sn, cs]])
    bf = lambda a: jnp.asarray(a.astype(np.float32)).astype(BF16)
    return dict(C=C, n1=n1, f1=bf(f1), g1=bf(g1), f2=bf(f2), g2=bf(g2))


def _fft_s1_kernel(za_ref, zb_ref, f1_ref, a_ref, za, zb):
    C, lb = za_ref.shape[0], za_ref.shape[1]
    za[...] = za_ref[...].reshape(C * lb, LANES)
    zb[...] = zb_ref[...].reshape(C * lb, LANES)
    for i in range(lb):
        x = jnp.concatenate([za[pl.ds(i, C, stride=lb), :], zb[pl.ds(i, C, stride=lb), :]], axis=1)
        a_ref[i] = jnp.dot(f1_ref[i], x.astype(BF16), preferred_element_type=F32)


def _fft_stage1(tabs, z4):
    B, C, _, W = z4.shape
    n1 = tabs["n1"]
    lb = FFT_LB
    hw = FFT_WT // LANES
    return pl.pallas_call(
        _fft_s1_kernel,
        grid=(B, FFT_L // lb, W // FFT_WT),
        in_specs=[pl.BlockSpec((None, C, lb, LANES), lambda b, l, w: (b, 0, l, hw * w)),
                  pl.BlockSpec((None, C, lb, LANES), lambda b, l, w: (b, 0, l, hw * w + 1)),
                  pl.BlockSpec((lb, 2 * n1, C), lambda b, l, w: (l, 0, 0))],
        out_specs=pl.BlockSpec((None, lb, 2 * n1, FFT_WT), lambda b, l, w: (b, l, 0, w)),
        out_shape=jax.ShapeDtypeStruct((B, FFT_L, 2 * n1, W), F32),
        scratch_shapes=[pltpu.VMEM((C * lb, LANES), F32)] * 2,
        compiler_params=_cparams("parallel", "parallel", "parallel"),
        name="fft_stage1",
    )(z4, z4, tabs["f1"])


def _fft_s2_kernel(ara_ref, arb_ref, aia_ref, aib_ref, f2_ref, g2_ref, kf_ref, o_ref, s0, s1, s2, s3, *, use_filter):
    kb = ara_ref.shape[1]
    refs = (s0, s1, s2, s3)
    for src, dst in zip((ara_ref, arb_ref, aia_ref, aib_ref), refs):
        dst[...] = src[...].reshape(FFT_L * kb, LANES)
    for i in range(kb):
        ra, rb, ia, ib = [r[pl.ds(i, FFT_L, stride=kb), :] for r in refs]
        x = jnp.concatenate([jnp.concatenate([ra, rb], axis=1), jnp.concatenate([ia, ib], axis=1)], axis=0)
        s = jnp.dot(f2_ref[...], x.astype(BF16), preferred_element_type=F32)
        if use_filter:
            xr, xi = s[:FFT_L], s[FFT_L:]
            kr, ki = kf_ref[i, 0], kf_ref[i, 1]
            y = jnp.concatenate([xr * kr - xi * ki, xr * ki + xi * kr], axis=0).astype(BF16)
            s = jnp.dot(g2_ref[...], y, preferred_element_type=F32)
        o_ref[i, 0] = s[:FFT_L]
        o_ref[i, 1] = s[FFT_L:]


def _fft_stage2(tabs, a, kf, use_filter):
    B, _, _, W = a.shape
    n1 = tabs["n1"]
    kb = min(FFT_KB, n1)
    nkb = n1 // kb
    hw = FFT_WT // LANES
    ablk = lambda im, h: pl.BlockSpec((None, FFT_L, kb, LANES), lambda k, w, b: (b, 0, im * nkb + k, hw * w + h))
    return pl.pallas_call(
        functools.partial(_fft_s2_kernel, use_filter=use_filter),
        grid=(nkb, W // FFT_WT, B),
        in_specs=[ablk(0, 0), ablk(0, 1), ablk(1, 0), ablk(1, 1),
                  _const_spec((2 * FFT_L, 2 * FFT_L)), _const_spec((2 * FFT_L, 2 * FFT_L)),
                  pl.BlockSpec((kb, 2, FFT_L, FFT_WT), lambda k, w, b: (k, 0, 0, w))],
        out_specs=pl.BlockSpec((None, kb, 2, FFT_L, FFT_WT), lambda k, w, b: (b, k, 0, 0, w)),
        out_shape=jax.ShapeDtypeStruct((B, n1, 2, FFT_L, W), F32),
        scratch_shapes=[pltpu.VMEM((FFT_L * kb, LANES), F32)] * 4,
        compiler_params=_cparams("parallel", "parallel", "arbitrary"),
        name="fft_stage2_filter" if use_filter else "fft_stage2",
    )(a, a, a, a, tabs["f2"], tabs["g2"], kf)


def _fft_s3_kernel(b_ref, g1_ref, y_ref, bm, ym):
    n1, _, lb, _ = b_ref.shape
    C = y_ref.shape[0]
    bm[...] = b_ref[...].reshape(n1 * 2 * lb, LANES)
    for i in range(lb):
        x = bm[pl.ds(i, 2 * n1, stride=lb), :].astype(BF16)
        ym[pl.ds(i, C, stride=lb), :] = jnp.dot(g1_ref[i], x, preferred_element_type=F32)
    y_ref[...] = ym[...].reshape(C, lb, LANES)


def _fft_stage3(tabs, bsp):
    B, n1, _, _, W = bsp.shape
    C = tabs["C"]
    lb = FFT_LB
    return pl.pallas_call(
        _fft_s3_kernel,
        grid=(B, FFT_L // lb, W // LANES),
        in_specs=[pl.BlockSpec((None, n1, 2, lb, LANES), lambda b, l, w: (b, 0, 0, l, w)),
                  pl.BlockSpec((lb, C, 2 * n1), lambda b, l, w: (l, 0, 0))],
        out_specs=pl.BlockSpec((None, C, lb, LANES), lambda b, l, w: (b, 0, l, w)),
        out_shape=jax.ShapeDtypeStruct((B, C, FFT_L, W), F32),
        scratch_shapes=[pltpu.VMEM((n1 * 2 * lb, LANES), F32), pltpu.VMEM((C * lb, LANES), F32)],
        compiler_params=_cparams("parallel", "parallel", "parallel"),
        name="fft_stage3",
    )(bsp, tabs["g1"])


def _fft_conv(z, taps):
    B, L, W = z.shape
    tabs = _fft_tables(L)
    C, n1 = tabs["C"], tabs["n1"]
    unused = jnp.zeros((n1, 2, FFT_L, W), F32)
    ts = _fft_stage2(tabs, _fft_stage1(tabs, taps.reshape(2, C, FFT_L, W)), unused, False)
    kf = jnp.stack([ts[0, :, 0] + ts[1, :, 0], ts[0, :, 1] - ts[1, :, 1]], axis=1)
    bsp = _fft_stage2(tabs, _fft_stage1(tabs, z.reshape(B, C, FFT_L, W)), kf, True)
    return _fft_stage3(tabs, bsp).reshape(B, L, W)


def _merge_kernel(og_ref, y_ref, z_ref, x0_ref, hb_ref, gates_ref, x_ref, wbg_ref, wbh_ref, wo_ref, n2_ref, rw_ref, rb_ref,
                  h1_ref, hn3_ref, idx_ref, wts_ref):
    mg = jnp.dot(og_ref[...], wbg_ref[...], preferred_element_type=F32)
    o_hy = ((y_ref[...] + z_ref[...] * hb_ref[...]) * x0_ref[...].astype(F32)).astype(BF16)
    mh = jnp.dot(o_hy, wbh_ref[...], preferred_element_type=F32)
    gg = jax.nn.sigmoid(gates_ref[:, :D_MODEL].astype(F32))
    gh = jax.nn.sigmoid(gates_ref[:, D_MODEL:].astype(F32))
    merged = (gg * mg + gh * mh).astype(BF16)
    h1 = x_ref[...] + jnp.dot(merged, wo_ref[...], preferred_element_type=F32)
    h1_ref[...] = h1
    hn = h1 * lax.rsqrt(jnp.mean(h1 * h1, axis=-1, keepdims=True) + EPS) * n2_ref[...]
    _store_token_rows(hn3_ref, hn)
    hh = hn.astype(BF16)
    hl = (hn - hh.astype(F32)).astype(BF16)
    p1 = jnp.dot(hh, rw_ref[...], preferred_element_type=F32)
    p2 = jnp.dot(hl, rw_ref[:, :ROUTE_PAD], preferred_element_type=F32)
    logits = p1[:, :ROUTE_PAD] + p1[:, ROUTE_PAD:] + p2 + rb_ref[...]
    lane = lax.broadcasted_iota(jnp.int32, logits.shape, 1)
    logits = jnp.where(lane < N_EXPERTS, logits, -jnp.inf)
    idx_out = jnp.zeros(logits.shape, jnp.int32)
    val_out = jnp.zeros(logits.shape, F32)
    v0 = None
    for j in range(TOP_K):
        m = jnp.max(logits, axis=-1, keepdims=True)
        sel = jnp.min(jnp.where(logits == m, lane, ROUTE_PAD), axis=-1, keepdims=True)
        if j == 0:
            v0 = m
        idx_out = jnp.where(lane == j, sel, idx_out)
        val_out = jnp.where(lane == j, jnp.exp(m - v0), val_out)
        logits = jnp.where(lane == sel, -jnp.inf, logits)
    idx_ref[...] = idx_out
    wts_ref[...] = val_out / jnp.sum(val_out, axis=-1, keepdims=True)


def _merge(o_gla, y, z, x0c, hy_bias, gates, x2, wbg, wbh, wo, norm2_w, rw_pad, rb_pad):
    T = x2.shape[0]
    tm = min(512, T)
    row = lambda n: pl.BlockSpec((tm, n), lambda i: (i, 0))
    return pl.pallas_call(
        _merge_kernel,
        grid=(T // tm,),
        in_specs=[row(GLA_VAL), row(HY_WIDTH), row(HY_WIDTH), row(HY_WIDTH), _const_spec((1, HY_WIDTH)), row(N_GATES), row(D_MODEL),
                  _const_spec((GLA_VAL, D_MODEL)), _const_spec((HY_WIDTH, D_MODEL)), _const_spec((D_MODEL, D_MODEL)),
                  _const_spec((1, D_MODEL)), _const_spec((D_MODEL, 2 * ROUTE_PAD)), _const_spec((1, ROUTE_PAD))],
        out_specs=[row(D_MODEL), pl.BlockSpec((tm * TOK_ROWS, LANES), lambda i: (i, 0)), row(ROUTE_PAD), row(ROUTE_PAD)],
        out_shape=[jax.ShapeDtypeStruct((T, D_MODEL), F32), jax.ShapeDtypeStruct((T * TOK_ROWS, LANES), F32),
                   jax.ShapeDtypeStruct((T, ROUTE_PAD), jnp.int32), jax.ShapeDtypeStruct((T, ROUTE_PAD), F32)],
        compiler_params=_cparams("parallel"),
        name="merge_route",
    )(o_gla, y, z, x0c, hy_bias.reshape(1, HY_WIDTH), gates, x2, wbg, wbh, wo, norm2_w.reshape(1, D_MODEL), rw_pad, rb_pad)


MOE_TM = 512
MOE_TC = 256
DMA_UNROLL = 8


DISPATCH_SLOTS = 3


def _dispatch_kernel(last_ref, pos_ref, hn_ref, xs_ref, zero_s, buf, sem, lsems, ssems):
    i = pl.program_id(0)
    n = pl.num_programs(0)
    tc = pos_ref.shape[2] // TOP_K
    rows = tc * TOK_ROWS
    zrows = zero_s.shape[0]
    slot = lax.rem(i, DISPATCH_SLOTS)
    prev_slot = lax.rem(i + DISPATCH_SLOTS - 1, DISPATCH_SLOTS)

    def load(blk, s):
        src = hn_ref.at[pl.ds(pl.multiple_of(blk * rows, rows), rows)]
        return pltpu.make_async_copy(src, buf.at[s], lsems.at[s])

    def wait_copies(s):
        for _ in range(TOP_K):
            pltpu.make_async_copy(buf.at[s], xs_ref.at[pl.ds(0, rows)], ssems.at[s]).wait()

    @pl.when(i == 0)
    def _():
        load(0, 0).start()

        @pl.when(n > 1)
        def _():
            load(1, 1).start()

        zero_s[...] = jnp.zeros_like(zero_s)
        n_fill = last_ref.shape[0]

        def zfill(e, c):
            @pl.when(last_ref[e] >= 0)
            def _():
                dst = pl.multiple_of(last_ref[e] * zrows, zrows)
                pltpu.make_async_copy(zero_s, xs_ref.at[pl.ds(dst, zrows)], sem).start()
            return c
        lax.fori_loop(0, n_fill, zfill, 0)

        def zwait(e, c):
            @pl.when(last_ref[e] >= 0)
            def _():
                pltpu.make_async_copy(zero_s, xs_ref.at[pl.ds(0, zrows)], sem).wait()
            return c
        lax.fori_loop(0, n_fill, zwait, 0)

    load(i, slot).wait()

    def issue(r, c):
        src = buf.at[slot, pl.ds(pl.multiple_of(r * TOK_ROWS, TOK_ROWS), TOK_ROWS)]
        for j in range(TOP_K):
            dst = pl.multiple_of(pos_ref[0, 0, r * TOP_K + j] * TOK_ROWS, TOK_ROWS)
            pltpu.make_async_copy(src, xs_ref.at[pl.ds(dst, TOK_ROWS)], ssems.at[slot]).start(priority=j % 2)
        return c
    lax.fori_loop(0, tc, issue, 0, unroll=DMA_UNROLL // TOP_K)

    @pl.when(i > 0)
    def _():
        wait_copies(prev_slot)

    @pl.when(i + 2 < n)
    def _():
        load(i + 2, prev_slot).start()

    @pl.when(i == n - 1)
    def _():
        wait_copies(slot)


def _dispatch(hn3, pos, last_tile, P):
    T = hn3.shape[0] // TOK_ROWS
    tc = min(MOE_TC, T)
    nt = T // tc
    return pl.pallas_call(
        _dispatch_kernel,
        grid_spec=pltpu.PrefetchScalarGridSpec(
            num_scalar_prefetch=1,
            grid=(nt,),
            in_specs=[pl.BlockSpec((1, 1, tc * TOP_K), lambda i, lt: (i, 0, 0), memory_space=pltpu.SMEM),
                      pl.BlockSpec(memory_space=pl.ANY)],
            out_specs=pl.BlockSpec(memory_space=pl.ANY),
            scratch_shapes=[pltpu.VMEM((MOE_TM * TOK_ROWS, LANES), F32),
                            pltpu.VMEM((DISPATCH_SLOTS, tc * TOK_ROWS, LANES), F32),
                            pltpu.SemaphoreType.DMA(()),
                            pltpu.SemaphoreType.DMA((DISPATCH_SLOTS,)), pltpu.SemaphoreType.DMA((DISPATCH_SLOTS,))],
        ),
        out_shape=jax.ShapeDtypeStruct((P * TOK_ROWS, LANES), F32),
        compiler_params=_cparams("arbitrary"),
        name="moe_dispatch",
    )(last_tile, pos.reshape(nt, 1, tc * TOP_K), hn3)


def _expert_kernel(te_ref, nused_ref, xs_ref, wgu_ref, bgu_ref, wd_ref, bd_ref, o_ref, wgu_s, wd_s):
    i = pl.program_id(0)
    prev = te_ref[jnp.maximum(i - 1, 0)]

    @pl.when((i == 0) | (te_ref[i] != prev))
    def _():
        wgu_s[...] = wgu_ref[0].astype(BF16)
        wd_s[...] = wd_ref[0].astype(BF16)

    @pl.when(i < nused_ref[0])
    def _():
        x = _load_token_rows(xs_ref, 0, xs_ref.shape[0] // TOK_ROWS).astype(BF16)
        gu = jnp.dot(x, wgu_s[...], preferred_element_type=F32) + bgu_ref[0]
        gate = jnp.minimum(gu[:, :D_FF], SWIGLU_LIMIT)
        up = jnp.clip(gu[:, D_FF:], -SWIGLU_LIMIT, SWIGLU_LIMIT)
        act = gate * jax.nn.sigmoid(SWIGLU_ALPHA * gate) * (up + 1.0)
        y = jnp.dot(act.astype(BF16), wd_s[...], preferred_element_type=F32) + bd_ref[0]
        _store_token_rows(o_ref, y)

    @pl.when(i >= nused_ref[0])
    def _():
        o_ref[...] = jnp.zeros_like(o_ref)


def _experts(xs3, tile_expert, n_used, w_gate_up, b_gate_up, w_down, b_down):
    P = xs3.shape[0] // TOK_ROWS
    tm = MOE_TM
    nt = P // tm
    live = lambda i, te, nu: jnp.minimum(i, nu[0] - 1)
    return pl.pallas_call(
        _expert_kernel,
        grid_spec=pltpu.PrefetchScalarGridSpec(
            num_scalar_prefetch=2,
            grid=(nt,),
            in_specs=[
                pl.BlockSpec((tm * TOK_ROWS, LANES), lambda i, te, nu: (live(i, te, nu), 0)),
                pl.BlockSpec((1, D_MODEL, 2 * D_FF), lambda i, te, nu: (te[i], 0, 0)),
                pl.BlockSpec((1, 1, 2 * D_FF), lambda i, te, nu: (te[i], 0, 0)),
                pl.BlockSpec((1, D_FF, D_MODEL), lambda i, te, nu: (te[i], 0, 0)),
                pl.BlockSpec((1, 1, D_MODEL), lambda i, te, nu: (te[i], 0, 0)),
            ],
            out_specs=pl.BlockSpec((tm * TOK_ROWS, LANES), lambda i, te, nu: (i, 0)),
            scratch_shapes=[pltpu.VMEM((D_MODEL, 2 * D_FF), BF16), pltpu.VMEM((D_FF, D_MODEL), BF16)],
        ),
        out_shape=jax.ShapeDtypeStruct((P * TOK_ROWS, LANES), F32),
        compiler_params=_cparams("arbitrary"),
        name="moe_experts",
    )(tile_expert, n_used, xs3, w_gate_up, b_gate_up.reshape(N_EXPERTS, 1, 2 * D_FF), w_down,
      b_down.reshape(N_EXPERTS, 1, D_MODEL))


def _combine_kernel(pos_ref, posn_ref, wts_ref, h1_ref, nf_ref, ys_ref, o_ref, buf, sems):
    i = pl.program_id(0)
    n = pl.num_programs(0)
    tc = o_ref.shape[0]
    slot = lax.rem(i, 2)

    def gather(p_ref, s):
        def issue(r, c):
            for j in range(TOP_K):
                src = pl.multiple_of(p_ref[0, 0, r * TOP_K + j] * TOK_ROWS, TOK_ROWS)
                dst = pl.multiple_of((j * tc + r) * TOK_ROWS, TOK_ROWS)
                pltpu.make_async_copy(ys_ref.at[pl.ds(src, TOK_ROWS)], buf.at[s, pl.ds(dst, TOK_ROWS)],
                                      sems.at[s]).start(priority=j % 2)
            return c
        lax.fori_loop(0, tc, issue, 0, unroll=DMA_UNROLL // TOP_K)

    @pl.when(i == 0)
    def _():
        gather(pos_ref, 0)

    @pl.when(i + 1 < n)
    def _():
        gather(posn_ref, 1 - slot)

    pltpu.make_async_copy(ys_ref.at[pl.ds(0, TOP_K * tc * TOK_ROWS)], buf.at[slot], sems.at[slot]).wait()

    w = wts_ref[...]
    h = h1_ref[...]
    for j in range(TOP_K):
        h = h + w[:, j:j + 1] * _load_token_rows(buf.at[slot], j * tc, tc)
    o_ref[...] = h * lax.rsqrt(jnp.mean(h * h, axis=-1, keepdims=True) + EPS) * nf_ref[...]


def _combine(pos, wts, h1, norm_f_w, ys3):
    T = h1.shape[0]
    tc = min(MOE_TC, T)
    nt = T // tc
    pos3 = pos.reshape(nt, 1, tc * TOP_K)
    return pl.pallas_call(
        _combine_kernel,
        grid=(nt,),
        in_specs=[
            pl.BlockSpec((1, 1, tc * TOP_K), lambda i: (i, 0, 0), memory_space=pltpu.SMEM),
            pl.BlockSpec((1, 1, tc * TOP_K), lambda i: (jnp.minimum(i + 1, nt - 1), 0, 0), memory_space=pltpu.SMEM),
            pl.BlockSpec((tc, ROUTE_PAD), lambda i: (i, 0)),
            pl.BlockSpec((tc, D_MODEL), lambda i: (i, 0)),
            _const_spec((1, D_MODEL)),
            pl.BlockSpec(memory_space=pl.ANY),
        ],
        out_specs=pl.BlockSpec((tc, D_MODEL), lambda i: (i, 0)),
        out_shape=jax.ShapeDtypeStruct((T, D_MODEL), F32),
        scratch_shapes=[pltpu.VMEM((2, TOP_K * tc * TOK_ROWS, LANES), F32), pltpu.SemaphoreType.DMA((2,))],
        compiler_params=_cparams("arbitrary"),
        name="moe_combine",
    )(pos3, pos3, wts, h1, norm_f_w.reshape(1, D_MODEL), ys3)


def _route_plan(top_idx, T):
    tm = MOE_TM
    nt = (T * TOP_K) // tm + N_EXPERTS
    e = top_idx.reshape(-1)
    oh = (e[:, None] == jnp.arange(N_EXPERTS, dtype=jnp.int32)[None, :]).astype(jnp.int32)
    csum = jnp.cumsum(oh, axis=0)
    rank = jnp.sum(csum * oh, axis=1) - 1
    counts = csum[-1]
    tiles = (counts + tm - 1) // tm
    tile_end = jnp.cumsum(tiles)
    offs = (tile_end - tiles) * tm
    pos = jnp.sum(oh * offs[None, :], axis=1) + rank
    tile_id = jnp.arange(nt, dtype=jnp.int32)
    tile_expert = jnp.minimum(jnp.sum((tile_end[None, :] <= tile_id[:, None]).astype(jnp.int32), axis=1),
                              N_EXPERTS - 1).astype(jnp.int32)
    tail = tile_end[-1] + jnp.arange(N_EXPERTS, dtype=jnp.int32)
    last_tile = jnp.concatenate([jnp.where(tiles > 0, tile_end - 1, -1), jnp.where(tail < nt, tail, -1)]).astype(jnp.int32)
    n_used = tile_end[-1:].astype(jnp.int32)
    return pos.astype(jnp.int32), tile_expert, last_tile, n_used, nt * tm


def kernel(x, norm1_w, w_in, gla_gate_w2, gla_gate_b, gla_norm_w, hy_conv_w, hy_conv_b, hy_freq, hy_ffn_w1, hy_ffn_b1,
           hy_ffn_w2, hy_ffn_b2, hy_ffn_w3, hy_decay, hy_bias, w_branch_gla, w_branch_hy, w_out, norm2_w, router_w,
           router_b, w_gate_up, b_gate_up, w_down, b_down, norm_f_w):
    B, L, D = x.shape
    T = B * L
    x2 = x.reshape(T, D)

    s_a = N_QKVR
    s_h = s_a + 2 * GLA_RANK
    w_cat = jnp.concatenate(
        [w_in[:, :s_a], w_in[:, s_h:], w_in[:, s_a:s_h], jnp.zeros((D, ALOW_PAD - 2 * GLA_RANK), w_in.dtype)],
        axis=1).astype(BF16)
    qkvr, hy, gates, alow = _in_proj(x2, norm1_w, w_cat)

    w2 = jnp.zeros((2, ALOW_PAD, GLA_KEY), F32)
    w2 = w2.at[0, :GLA_RANK].set(gla_gate_w2[0]).at[1, GLA_RANK:2 * GLA_RANK].set(gla_gate_w2[1])
    o_gla = _gla(qkvr, alow, w2, gla_gate_b.reshape(2, 1, GLA_KEY), gla_norm_w, B, L)

    taps = _hy_filter(L, hy_freq, hy_ffn_w1, hy_ffn_b1, hy_ffn_w2, hy_ffn_b2, hy_ffn_w3, hy_decay)
    z, x0c = _hy_pre(hy, hy_conv_w, hy_conv_b, B, L)
    y = _fft_conv(z.reshape(B, L, HY_WIDTH), taps).reshape(T, HY_WIDTH)

    rw_f = jnp.zeros((D, ROUTE_PAD), F32).at[:, :N_EXPERTS].set(router_w)
    rw_hi = rw_f.astype(BF16)
    rw_pad = jnp.concatenate([rw_hi, (rw_f - rw_hi.astype(F32)).astype(BF16)], axis=1)
    rb_pad = jnp.zeros((1, ROUTE_PAD), F32).at[0, :N_EXPERTS].set(router_b)
    h1, hn3, idx_pad, wts_pad = _merge(o_gla, y, z, x0c, hy_bias, gates, x2, w_branch_gla.astype(BF16), w_branch_hy.astype(BF16),
                                       w_out.astype(BF16), norm2_w, rw_pad, rb_pad)

    pos, tile_expert, last_tile, n_used, P = _route_plan(idx_pad[:, :TOP_K], T)
    xs3 = _dispatch(hn3, pos, last_tile, P)
    ys3 = _experts(xs3, tile_expert, n_used, w_gate_up, b_gate_up, w_down, b_down)
    out = _combine(pos, wts_pad, h1, norm_f_w, ys3)
    return out.reshape(B, L, D)
```

```python
import functools
import math

import numpy as np
import jax
import jax.numpy as jnp
from jax import lax
from jax.experimental import pallas as pl
from jax.experimental.pallas import tpu as pltpu

D_MODEL = 1024
EPS = 1e-6
GLA_HEADS = 4
GLA_DK = 128
GLA_DV = 256
GLA_KEY = GLA_HEADS * GLA_DK
GLA_VAL = GLA_HEADS * GLA_DV
GLA_RANK = 16
GLA_GATE_TEMP = 16.0
GLA_CHUNK = 64
HY_WIDTH = D_MODEL
HY_BANDS = 16
HY_EMB = 1 + 2 * HY_BANDS
HY_FFN = 64
N_EXPERTS = 32
TOP_K = 4
D_FF = D_MODEL
SWIGLU_LIMIT = 7.0
SWIGLU_ALPHA = 1.702

LANES = 128
ALOW_PAD = LANES
ROUTE_PAD = LANES
VMEM_LIMIT = 56 * 1024 * 1024

F32 = jnp.float32
BF16 = jnp.bfloat16
HIGHEST = lax.Precision.HIGHEST


def _cparams(*sem):
    return pltpu.CompilerParams(dimension_semantics=sem, vmem_limit_bytes=VMEM_LIMIT)


def _const_spec(shape):
    nd = len(shape)
    return pl.BlockSpec(shape, lambda *_: (0,) * nd, pipeline_mode=pl.Buffered(1))


TOK_ROWS = D_MODEL // LANES


def _store_token_rows(ref, val):
    n = val.shape[0]
    for s_ in range(TOK_ROWS):
        ref[pl.ds(s_, n, stride=TOK_ROWS), :] = val[:, s_ * LANES:(s_ + 1) * LANES]


def _load_token_rows(ref, first, n):
    return jnp.concatenate(
        [ref[pl.ds(first * TOK_ROWS + s_, n, stride=TOK_ROWS), :] for s_ in range(TOK_ROWS)], axis=1)


N_QKVR = 2 * GLA_KEY + 2 * GLA_VAL
N_HY = 3 * HY_WIDTH
N_GATES = 2 * D_MODEL
IN_COLS = N_QKVR + N_HY + N_GATES + ALOW_PAD


def _inproj_kernel(x_ref, nw_ref, w_ref, qkvr_ref, hy_ref, gates_ref, alow_ref):
    x = x_ref[...]
    ms = jnp.mean(x * x, axis=-1, keepdims=True)
    xn = (x * lax.rsqrt(ms + EPS) * nw_ref[...]).astype(BF16)
    c0, c1, c2 = N_QKVR, N_QKVR + N_HY, N_QKVR + N_HY + N_GATES
    qkvr_ref[...] = jnp.dot(xn, w_ref[:, :c0], preferred_element_type=F32).astype(BF16)
    hy_ref[...] = jnp.dot(xn, w_ref[:, c0:c1], preferred_element_type=F32).astype(BF16)
    gates_ref[...] = jnp.dot(xn, w_ref[:, c1:c2], preferred_element_type=F32).astype(BF16)
    alow_ref[...] = jnp.dot(xn, w_ref[:, c2:], preferred_element_type=F32)


def _in_proj(x2, norm1_w, w_cat):
    T = x2.shape[0]
    tm = min(512, T)
    return pl.pallas_call(
        _inproj_kernel,
        grid=(T // tm,),
        in_specs=[
            pl.BlockSpec((tm, D_MODEL), lambda i: (i, 0)),
            _const_spec((1, D_MODEL)),
            _const_spec((D_MODEL, IN_COLS)),
        ],
        out_specs=[
            pl.BlockSpec((tm, N_QKVR), lambda i: (i, 0)),
            pl.BlockSpec((tm, N_HY), lambda i: (i, 0)),
            pl.BlockSpec((tm, N_GATES), lambda i: (i, 0)),
            pl.BlockSpec((tm, ALOW_PAD), lambda i: (i, 0)),
        ],
        out_shape=[
            jax.ShapeDtypeStruct((T, N_QKVR), BF16),
            jax.ShapeDtypeStruct((T, N_HY), BF16),
            jax.ShapeDtypeStruct((T, N_GATES), BF16),
            jax.ShapeDtypeStruct((T, ALOW_PAD), F32),
        ],
        compiler_params=_cparams("parallel"),
        name="in_proj",
    )(x2, norm1_w.reshape(1, D_MODEL), w_cat)


GLA_PREP_ROWS = 256
GLA_SCAN_UNROLL = 4


def _log_sigmoid(x):
    return jnp.minimum(x, 0.0) - jnp.log1p(jnp.exp(-jnp.abs(x)))


def _gla_kernel(q_ref, k_ref, v_ref, r_ref, alow_ref, w2_ref, gb_ref, nw_ref, o_ref,
                qtf, kef, qtb, keb, decf, decb, of_s, ob_s, stf, stb):
    L = q_ref.shape[0]
    C = GLA_CHUNK
    DK = GLA_DK
    n_chunks = L // C
    R = min(GLA_PREP_ROWS, L)
    cpr = R // C
    scale = DK ** -0.5
    nt = (((1,), (1,)), ((), ()))
    tn = (((0,), (0,)), ((), ()))

    ri = lax.broadcasted_iota(jnp.int32, (R, R), 0)
    ci = lax.broadcasted_iota(jnp.int32, (R, R), 1)
    same = (ri // C) == (ci // C)
    mask_f = same & (ri >= ci)
    mask_b = same & (ri < ci)
    tri = jnp.where(mask_f, 1.0, 0.0).astype(BF16)
    w2 = jnp.concatenate([w2_ref[0], w2_ref[1]], axis=1).astype(BF16)
    gb = jnp.concatenate([gb_ref[0], gb_ref[1]], axis=1)

    def prep(g, carry):
        rows = pl.ds(pl.multiple_of(g * R, R), R)
        logits = jnp.dot(alow_ref[rows, :].astype(BF16), w2, preferred_element_type=F32) + gb
        lg = _log_sigmoid(logits) * (1.0 / GLA_GATE_TEMP)
        hi = lg.astype(BF16)
        lo = (lg - hi.astype(F32)).astype(BF16)
        b = jnp.dot(tri, hi, preferred_element_type=F32) + jnp.dot(tri, lo, preferred_element_type=F32)
        tot = jnp.concatenate(
            [jnp.broadcast_to(b[c * C + C - 1:c * C + C, :], (C, 2 * DK)) for c in range(cpr)], axis=0)
        q = q_ref[rows, :].astype(F32) * scale
        k = k_ref[rows, :].astype(F32)
        vb = v_ref[rows, :]
        dtot = jnp.exp(tot)
        for d, (qt, ke, dec, mask, out) in enumerate(((qtf, kef, decf, mask_f, of_s), (qtb, keb, decb, mask_b, ob_s))):
            sl = slice(d * DK, (d + 1) * DK)
            bd, td, ld = b[:, sl], tot[:, sl], lg[:, sl]
            if d == 0:
                e_q, e_end = bd, td - bd
            else:
                e_q, e_end = td - bd + ld, bd - ld
            qtv = (q * jnp.exp(e_q)).astype(BF16)
            ktv = (k * jnp.exp(-e_q)).astype(BF16)
            qt[rows, :] = qtv
            ke[rows, :] = (k * jnp.exp(e_end)).astype(BF16)
            s = lax.dot_general(qtv, ktv, nt, preferred_element_type=F32)
            s = jnp.where(mask, s, 0.0).astype(BF16)
            out[rows, :] = jnp.dot(s, vb, preferred_element_type=F32)
            for c in range(cpr):
                dec[pl.ds(g * cpr + c, 1), :] = dtot[c * C:c * C + 1, sl]
        return carry

    lax.fori_loop(0, L // R, prep, 0, unroll=2)

    def chunk(n, st, qt, ke, dec, out):
        rows = pl.ds(pl.multiple_of(n * C, C), C)
        s = st[...]
        out[rows, :] += lax.dot_general(qt[rows, :], s.astype(BF16), nt, preferred_element_type=F32)
        upd = lax.dot_general(v_ref[rows, :], ke[rows, :], tn, preferred_element_type=F32)
        st[...] = dec[pl.ds(n, 1), :] * s + upd

    def step(i, carry):
        chunk(i, stf, qtf, kef, decf, of_s)
        chunk(n_chunks - 1 - i, stb, qtb, keb, decb, ob_s)
        return carry

    stf[...] = jnp.zeros_like(stf)
    stb[...] = jnp.zeros_like(stb)
    lax.fori_loop(0, n_chunks, step, 0, unroll=GLA_SCAN_UNROLL)

    o = of_s[...] + ob_s[...]
    o = o * lax.rsqrt(jnp.mean(o * o, axis=-1, keepdims=True) + EPS) * nw_ref[...]
    r = r_ref[...].astype(F32)
    o_ref[...] = (o * (r * jax.nn.sigmoid(r))).astype(BF16)


def _gla(qkvr, alow, w2, gb, norm_w, B, L):
    T = B * L
    H = GLA_HEADS
    kq = GLA_KEY // GLA_DK
    kv = 2 * GLA_KEY // GLA_DV
    kr = (2 * GLA_KEY + GLA_VAL) // GLA_DV
    return pl.pallas_call(
        _gla_kernel,
        grid=(B, H),
        in_specs=[
            pl.BlockSpec((L, GLA_DK), lambda b, h: (b, h)),
            pl.BlockSpec((L, GLA_DK), lambda b, h: (b, kq + h)),
            pl.BlockSpec((L, GLA_DV), lambda b, h: (b, kv + h)),
            pl.BlockSpec((L, GLA_DV), lambda b, h: (b, kr + h)),
            pl.BlockSpec((L, ALOW_PAD), lambda b, h: (b, 0)),
            pl.BlockSpec((2, ALOW_PAD, GLA_DK), lambda b, h: (0, 0, h)),
            pl.BlockSpec((2, 1, GLA_DK), lambda b, h: (0, 0, h)),
            pl.BlockSpec((1, GLA_DV), lambda b, h: (0, 0)),
        ],
        out_specs=pl.BlockSpec((L, GLA_DV), lambda b, h: (b, h)),
        out_shape=jax.ShapeDtypeStruct((T, GLA_VAL), BF16),
        scratch_shapes=[pltpu.VMEM((L, GLA_DK), BF16)] * 4
        + [pltpu.VMEM((L // GLA_CHUNK, GLA_DK), F32)] * 2
        + [pltpu.VMEM((L, GLA_DV), F32)] * 2
        + [pltpu.VMEM((GLA_DV, GLA_DK), F32)] * 2,
        compiler_params=_cparams("parallel", "parallel"),
        name="gla",
    )(qkvr, qkvr, qkvr, qkvr, alow, w2, gb, norm_w.reshape(1, GLA_DV))


HY_WT = 256


def _hy_filter_kernel(z_ref, t_ref, fr_ref, w1_ref, b1_ref, w2_ref, b2_ref, w3f_ref, w3b_ref, dec_ref, o_ref, h_s):
    @pl.when(pl.program_id(0) == 0)
    def _():
        fr = fr_ref[...]
        h1 = jnp.sin(fr * (jnp.dot(z_ref[...], w1_ref[...], preferred_element_type=F32, precision=HIGHEST) + b1_ref[...]))
        h_s[...] = jnp.sin(fr * (jnp.dot(h1, w2_ref[...], preferred_element_type=F32, precision=HIGHEST) + b2_ref[...]))

    h = h_s[...]
    t = t_ref[...]
    row = lax.broadcasted_iota(jnp.int32, (z_ref.shape[0], 1), 0)
    hf = jnp.dot(h, w3f_ref[...], preferred_element_type=F32, precision=HIGHEST)
    hf = hf * jnp.exp(-t * jnp.abs(dec_ref[0:1, :]))
    hb = jnp.dot(h, w3b_ref[...], preferred_element_type=F32, precision=HIGHEST)
    hb = hb * jnp.exp(-t * jnp.abs(dec_ref[1:2, :]))
    hb = jnp.where(row == 0, 0.0, hb)
    ss = jnp.sum(hf * hf, axis=0, keepdims=True) + jnp.sum(hb * hb, axis=0, keepdims=True)
    inv = lax.rsqrt(ss + EPS)
    o_ref[0] = hf * inv
    o_ref[1] = hb * inv


def _hy_filter(L, hy_freq, w1, b1, w2, b2, w3, hy_decay):
    W = HY_WIDTH
    t = np.linspace(0.0, 1.0, L, dtype=np.float32)[:, None]
    omega = (np.float32(2.0 * math.pi) * np.arange(L, dtype=np.float32) / np.float32(L)).astype(np.float32)
    f = np.linspace(1e-4, HY_BANDS - 1, HY_BANDS, dtype=np.float32)
    ang = (omega[:, None] * f[None, :]).astype(np.float32)
    z = np.concatenate([t, np.cos(ang), -np.sin(ang)], axis=-1).astype(np.float32)
    nw = W // HY_WT
    return pl.pallas_call(
        _hy_filter_kernel,
        grid=(nw,),
        in_specs=[
            _const_spec((L, HY_EMB)),
            _const_spec((L, 1)),
            _const_spec((1, HY_FFN)),
            _const_spec((HY_EMB, HY_FFN)),
            _const_spec((1, HY_FFN)),
            _const_spec((HY_FFN, HY_FFN)),
            _const_spec((1, HY_FFN)),
            pl.BlockSpec((HY_FFN, HY_WT), lambda j: (0, j)),
            pl.BlockSpec((HY_FFN, HY_WT), lambda j: (0, nw + j)),
            pl.BlockSpec((2, HY_WT), lambda j: (0, j)),
        ],
        out_specs=pl.BlockSpec((2, L, HY_WT), lambda j: (0, 0, j)),
        out_shape=jax.ShapeDtypeStruct((2, L, W), F32),
        scratch_shapes=[pltpu.VMEM((L, HY_FFN), F32)],
        compiler_params=_cparams("arbitrary"),
        name="hy_filter",
    )(jnp.asarray(z), jnp.asarray(t), hy_freq.reshape(1, HY_FFN), w1, b1.reshape(1, HY_FFN), w2,
      b2.reshape(1, HY_FFN), w3, w3, hy_decay)


def _hy_pre_kernel(x0_ref, x1_ref, v_ref, w0_ref, w1_ref, wv_ref, b0_ref, b1_ref, bv_ref, z_ref, x0o_ref):
    L = x0_ref.shape[0]
    row = lax.broadcasted_iota(jnp.int32, (L, 1), 0)

    def conv3(u_ref, w_ref, b_ref):
        u = u_ref[...].astype(F32)
        prev = jnp.where(row == 0, 0.0, pltpu.roll(u, 1, 0))
        nxt = jnp.where(row == L - 1, 0.0, pltpu.roll(u, L - 1, 0))
        return w_ref[0:1, :] * prev + w_ref[1:2, :] * u + w_ref[2:3, :] * nxt + b_ref[...]

    x0o_ref[...] = conv3(x0_ref, w0_ref, b0_ref).astype(BF16)
    z_ref[...] = conv3(v_ref, wv_ref, bv_ref) * conv3(x1_ref, w1_ref, b1_ref)


def _hy_pre(hy, conv_w, conv_b, B, L):
    W = HY_WIDTH
    nw = W // HY_WT
    cb = conv_b.reshape(1, 3 * W)
    blk = lambda off: pl.BlockSpec((L, HY_WT), lambda b, j: (b, off + j))
    wblk = lambda off: pl.BlockSpec((3, HY_WT), lambda b, j: (0, off + j))
    bblk = lambda off: pl.BlockSpec((1, HY_WT), lambda b, j: (0, off + j))
    return pl.pallas_call(
        _hy_pre_kernel,
        grid=(B, nw),
        in_specs=[blk(0), blk(nw), blk(2 * nw), wblk(0), wblk(nw), wblk(2 * nw), bblk(0), bblk(nw), bblk(2 * nw)],
        out_specs=[pl.BlockSpec((L, HY_WT), lambda b, j: (b, j))] * 2,
        out_shape=[jax.ShapeDtypeStruct((B * L, W), F32), jax.ShapeDtypeStruct((B * L, W), BF16)],
        compiler_params=_cparams("parallel", "parallel"),
        name="hy_pre",
    )(hy, hy, hy, conv_w, conv_w, conv_w, cb, cb, cb)


FFT_L = LANES
FFT_PITCH = FFT_L + 8
FFT_UNROLL = 8


def _fft_tables(L):
    N = 2 * L
    C = L // FFT_L
    n1 = N // FFT_L
    l = np.arange(FFT_L)
    c = np.arange(C)
    k2 = np.arange(n1)
    ang1 = 2 * np.pi * (((k2[None, :, None] * c[None, None, :]) % n1) / n1
                        + ((l[:, None, None] * k2[None, :, None]) % N) / N)
    f1 = np.concatenate([np.cos(ang1), -np.sin(ang1)], axis=1)
    g1 = np.stack([np.cos(ang1), -np.sin(ang1)], axis=-1)
    g1 = np.transpose(g1, (0, 2, 1, 3)).reshape(FFT_L, C, 2 * n1) / N
    a2 = 2 * np.pi * np.outer(l, l) / FFT_L
    cs, sn = np.cos(a2), np.sin(a2)
    f2 = np.block([[cs, sn], [-sn, cs]])
    g2 = np.block([[cs, -sn], [sn, cs]])
    bf = lambda a: jnp.asarray(a.astype(np.float32)).astype(BF16)
    return dict(C=C, n1=n1, f1=bf(f1), g1=bf(g1), f2=bf(f2), g2=bf(g2))


def _fft_kernel(z_ref, f1_ref, f2_ref, g2_ref, kf_ref, g1_ref, o_ref, a_s, b_s, *, spectrum_only):
    C = f1_ref.shape[2]
    n1 = f1_ref.shape[1] // 2
    P = FFT_PITCH

    def stage1(l, carry):
        x = z_ref[pl.ds(l, C, stride=FFT_L), :].astype(BF16)
        a_s[pl.ds(pl.multiple_of(l * P, 8), 2 * n1), :] = jnp.dot(f1_ref[l], x, preferred_element_type=F32)
        return carry
    lax.fori_loop(0, FFT_L, stage1, 0, unroll=FFT_UNROLL)

    def stage2(k2, carry):
        ar = a_s[pl.ds(k2, FFT_L, stride=P), :]
        ai = a_s[pl.ds(n1 + k2, FFT_L, stride=P), :]
        x = jnp.concatenate([ar, ai], axis=0).astype(BF16)
        s = jnp.dot(f2_ref[...], x, preferred_element_type=F32)
        if spectrum_only:
            o_ref[k2, 0] = s[:FFT_L]
            o_ref[k2, 1] = s[FFT_L:]
        else:
            xr, xi = s[:FFT_L], s[FFT_L:]
            kr, ki = kf_ref[k2, 0].astype(F32), kf_ref[k2, 1].astype(F32)
            y = jnp.concatenate([xr * kr - xi * ki, xr * ki + xi * kr], axis=0).astype(BF16)
            s = jnp.dot(g2_ref[...], y, preferred_element_type=F32)
            b_s[pl.ds(pl.multiple_of(2 * k2 * P, 8), FFT_L), :] = s[:FFT_L]
            b_s[pl.ds(pl.multiple_of((2 * k2 + 1) * P, 8), FFT_L), :] = s[FFT_L:]
        return carry
    lax.fori_loop(0, n1, stage2, 0, unroll=FFT_UNROLL)

    if not spectrum_only:
        def stage3(l, carry):
            x = b_s[pl.ds(l, 2 * n1, stride=P), :].astype(BF16)
            o_ref[pl.ds(l, C, stride=FFT_L), :] = jnp.dot(g1_ref[l], x, preferred_element_type=F32)
            return carry
        lax.fori_loop(0, FFT_L, stage3, 0, unroll=FFT_UNROLL)


def _fft(tabs, z, kf, spectrum_only):
    B, L, W = z.shape
    C, n1 = tabs["C"], tabs["n1"]
    if spectrum_only:
        out_spec = pl.BlockSpec((None, n1, 2, FFT_L, LANES), lambda w, b: (b, 0, 0, 0, w))
        out_shape = jax.ShapeDtypeStruct((B, n1, 2, FFT_L, W), F32)
    else:
        out_spec = pl.BlockSpec((None, L, LANES), lambda w, b: (b, 0, w))
        out_shape = jax.ShapeDtypeStruct((B, L, W), F32)
    return pl.pallas_call(
        functools.partial(_fft_kernel, spectrum_only=spectrum_only),
        grid=(W // LANES, B),
        in_specs=[pl.BlockSpec((None, L, LANES), lambda w, b: (b, 0, w)),
                  _const_spec((FFT_L, 2 * n1, C)),
                  _const_spec((2 * FFT_L, 2 * FFT_L)), _const_spec((2 * FFT_L, 2 * FFT_L)),
                  pl.BlockSpec((n1, 2, FFT_L, LANES), lambda w, b: (0, 0, 0, w)),
                  _const_spec((FFT_L, C, 2 * n1))],
        out_specs=out_spec,
        out_shape=out_shape,
        scratch_shapes=[pltpu.VMEM((FFT_L * FFT_PITCH, LANES), F32), pltpu.VMEM((2 * n1 * FFT_PITCH, LANES), F32)],
        compiler_params=_cparams("arbitrary", "arbitrary"),
        name="fft_spectrum" if spectrum_only else "fft_conv",
    )(z, tabs["f1"], tabs["f2"], tabs["g2"], kf, tabs["g1"])


def _fft_conv(z, taps):
    tabs = _fft_tables(z.shape[1])
    unused = jnp.zeros((tabs["n1"], 2, FFT_L, z.shape[2]), BF16)
    ts = _fft(tabs, taps, unused, True)
    kf = jnp.stack([ts[0, :, 0] + ts[1, :, 0], ts[0, :, 1] - ts[1, :, 1]], axis=1).astype(BF16)
    return _fft(tabs, z, kf, False)


def _merge_kernel(og_ref, y_ref, z_ref, x0_ref, hb_ref, gates_ref, x_ref, wbg_ref, wbh_ref, wo_ref, n2_ref, rw_ref, rb_ref,
                  h1_ref, hn3_ref, idx_ref, wts_ref):
    mg = jnp.dot(og_ref[...], wbg_ref[...], preferred_element_type=F32)
    o_hy = ((y_ref[...] + z_ref[...] * hb_ref[...]) * x0_ref[...].astype(F32)).astype(BF16)
    mh = jnp.dot(o_hy, wbh_ref[...], preferred_element_type=F32)
    gg = jax.nn.sigmoid(gates_ref[:, :D_MODEL].astype(F32))
    gh = jax.nn.sigmoid(gates_ref[:, D_MODEL:].astype(F32))
    merged = (gg * mg + gh * mh).astype(BF16)
    h1 = x_ref[...] + jnp.dot(merged, wo_ref[...], preferred_element_type=F32)
    h1_ref[...] = h1
    hn = h1 * lax.rsqrt(jnp.mean(h1 * h1, axis=-1, keepdims=True) + EPS) * n2_ref[...]
    _store_token_rows(hn3_ref, hn)
    hh = hn.astype(BF16)
    hl = (hn - hh.astype(F32)).astype(BF16)
    p1 = jnp.dot(hh, rw_ref[...], preferred_element_type=F32)
    p2 = jnp.dot(hl, rw_ref[:, :ROUTE_PAD], preferred_element_type=F32)
    logits = p1[:, :ROUTE_PAD] + p1[:, ROUTE_PAD:] + p2 + rb_ref[...]
    lane = lax.broadcasted_iota(jnp.int32, logits.shape, 1)
    logits = jnp.where(lane < N_EXPERTS, logits, -jnp.inf)
    idx_out = jnp.zeros(logits.shape, jnp.int32)
    val_out = jnp.zeros(logits.shape, F32)
    v0 = None
    for j in range(TOP_K):
        m = jnp.max(logits, axis=-1, keepdims=True)
        sel = jnp.min(jnp.where(logits == m, lane, ROUTE_PAD), axis=-1, keepdims=True)
        if j == 0:
            v0 = m
        idx_out = jnp.where(lane == j, sel, idx_out)
        val_out = jnp.where(lane == j, jnp.exp(m - v0), val_out)
        logits = jnp.where(lane == sel, -jnp.inf, logits)
    idx_ref[...] = idx_out
    wts_ref[...] = val_out / jnp.sum(val_out, axis=-1, keepdims=True)


def _merge(o_gla, y, z, x0c, hy_bias, gates, x2, wbg, wbh, wo, norm2_w, rw_pad, rb_pad):
    T = x2.shape[0]
    tm = min(512, T)
    row = lambda n: pl.BlockSpec((tm, n), lambda i: (i, 0))
    return pl.pallas_call(
        _merge_kernel,
        grid=(T // tm,),
        in_specs=[row(GLA_VAL), row(HY_WIDTH), row(HY_WIDTH), row(HY_WIDTH), _const_spec((1, HY_WIDTH)), row(N_GATES), row(D_MODEL),
                  _const_spec((GLA_VAL, D_MODEL)), _const_spec((HY_WIDTH, D_MODEL)), _const_spec((D_MODEL, D_MODEL)),
                  _const_spec((1, D_MODEL)), _const_spec((D_MODEL, 2 * ROUTE_PAD)), _const_spec((1, ROUTE_PAD))],
        out_specs=[row(D_MODEL), pl.BlockSpec((tm * TOK_ROWS, LANES), lambda i: (i, 0)), row(ROUTE_PAD), row(ROUTE_PAD)],
        out_shape=[jax.ShapeDtypeStruct((T, D_MODEL), F32), jax.ShapeDtypeStruct((T * TOK_ROWS, LANES), F32),
                   jax.ShapeDtypeStruct((T, ROUTE_PAD), jnp.int32), jax.ShapeDtypeStruct((T, ROUTE_PAD), F32)],
        compiler_params=_cparams("parallel"),
        name="merge_route",
    )(o_gla, y, z, x0c, hy_bias.reshape(1, HY_WIDTH), gates, x2, wbg, wbh, wo, norm2_w.reshape(1, D_MODEL), rw_pad, rb_pad)


MOE_TM = 512
MOE_TC = 256
DMA_UNROLL = 8


DISPATCH_SLOTS = 3


def _dispatch_kernel(last_ref, pos_ref, hn_ref, xs_ref, zero_s, buf, sem, lsems, ssems):
    i = pl.program_id(0)
    n = pl.num_programs(0)
    tc = pos_ref.shape[2] // TOP_K
    rows = tc * TOK_ROWS
    zrows = zero_s.shape[0]
    slot = lax.rem(i, DISPATCH_SLOTS)
    prev_slot = lax.rem(i + DISPATCH_SLOTS - 1, DISPATCH_SLOTS)

    def load(blk, s):
        src = hn_ref.at[pl.ds(pl.multiple_of(blk * rows, rows), rows)]
        return pltpu.make_async_copy(src, buf.at[s], lsems.at[s])

    def wait_copies(s):
        for _ in range(TOP_K):
            pltpu.make_async_copy(buf.at[s], xs_ref.at[pl.ds(0, rows)], ssems.at[s]).wait()

    @pl.when(i == 0)
    def _():
        load(0, 0).start()

        @pl.when(n > 1)
        def _():
            load(1, 1).start()

        zero_s[...] = jnp.zeros_like(zero_s)
        n_fill = last_ref.shape[0]

        def zfill(e, c):
            @pl.when(last_ref[e] >= 0)
            def _():
                dst = pl.multiple_of(last_ref[e] * zrows, zrows)
                pltpu.make_async_copy(zero_s, xs_ref.at[pl.ds(dst, zrows)], sem).start()
            return c
        lax.fori_loop(0, n_fill, zfill, 0)

        def zwait(e, c):
            @pl.when(last_ref[e] >= 0)
            def _():
                pltpu.make_async_copy(zero_s, xs_ref.at[pl.ds(0, zrows)], sem).wait()
            return c
        lax.fori_loop(0, n_fill, zwait, 0)

    load(i, slot).wait()

    def issue(r, c):
        src = buf.at[slot, pl.ds(pl.multiple_of(r * TOK_ROWS, TOK_ROWS), TOK_ROWS)]
        for j in range(TOP_K):
            dst = pl.multiple_of(pos_ref[0, 0, r * TOP_K + j] * TOK_ROWS, TOK_ROWS)
            pltpu.make_async_copy(src, xs_ref.at[pl.ds(dst, TOK_ROWS)], ssems.at[slot]).start(priority=j % 2)
        return c
    lax.fori_loop(0, tc, issue, 0, unroll=DMA_UNROLL // TOP_K)

    @pl.when(i > 0)
    def _():
        wait_copies(prev_slot)

    @pl.when(i + 2 < n)
    def _():
        load(i + 2, prev_slot).start()

    @pl.when(i == n - 1)
    def _():
        wait_copies(slot)


def _dispatch(hn3, pos, last_tile, P):
    T = hn3.shape[0] // TOK_ROWS
    tc = min(MOE_TC, T)
    nt = T // tc
    return pl.pallas_call(
        _dispatch_kernel,
        grid_spec=pltpu.PrefetchScalarGridSpec(
            num_scalar_prefetch=1,
            grid=(nt,),
            in_specs=[pl.BlockSpec((1, 1, tc * TOP_K), lambda i, lt: (i, 0, 0), memory_space=pltpu.SMEM),
                      pl.BlockSpec(memory_space=pl.ANY)],
            out_specs=pl.BlockSpec(memory_space=pl.ANY),
            scratch_shapes=[pltpu.VMEM((MOE_TM * TOK_ROWS, LANES), F32),
                            pltpu.VMEM((DISPATCH_SLOTS, tc * TOK_ROWS, LANES), F32),
                            pltpu.SemaphoreType.DMA(()),
                            pltpu.SemaphoreType.DMA((DISPATCH_SLOTS,)), pltpu.SemaphoreType.DMA((DISPATCH_SLOTS,))],
        ),
        out_shape=jax.ShapeDtypeStruct((P * TOK_ROWS, LANES), F32),
        compiler_params=_cparams("arbitrary"),
        name="moe_dispatch",
    )(last_tile, pos.reshape(nt, 1, tc * TOP_K), hn3)


def _expert_kernel(te_ref, nused_ref, xs_ref, wgu_ref, bgu_ref, wd_ref, bd_ref, o_ref, wgu_s, wd_s):
    i = pl.program_id(0)
    prev = te_ref[jnp.maximum(i - 1, 0)]

    @pl.when((i == 0) | (te_ref[i] != prev))
    def _():
        wgu_s[...] = wgu_ref[0].astype(BF16)
        wd_s[...] = wd_ref[0].astype(BF16)

    @pl.when(i < nused_ref[0])
    def _():
        x = _load_token_rows(xs_ref, 0, xs_ref.shape[0] // TOK_ROWS).astype(BF16)
        gu = jnp.dot(x, wgu_s[...], preferred_element_type=F32) + bgu_ref[0]
        gate = jnp.minimum(gu[:, :D_FF], SWIGLU_LIMIT)
        up = jnp.clip(gu[:, D_FF:], -SWIGLU_LIMIT, SWIGLU_LIMIT)
        act = gate * jax.nn.sigmoid(SWIGLU_ALPHA * gate) * (up + 1.0)
        y = jnp.dot(act.astype(BF16), wd_s[...], preferred_element_type=F32) + bd_ref[0]
        _store_token_rows(o_ref, y)

    @pl.when(i >= nused_ref[0])
    def _():
        o_ref[...] = jnp.zeros_like(o_ref)


def _experts(xs3, tile_expert, n_used, w_gate_up, b_gate_up, w_down, b_down):
    P = xs3.shape[0] // TOK_ROWS
    tm = MOE_TM
    nt = P // tm
    live = lambda i, te, nu: jnp.minimum(i, nu[0] - 1)
    return pl.pallas_call(
        _expert_kernel,
        grid_spec=pltpu.PrefetchScalarGridSpec(
            num_scalar_prefetch=2,
            grid=(nt,),
            in_specs=[
                pl.BlockSpec((tm * TOK_ROWS, LANES), lambda i, te, nu: (live(i, te, nu), 0)),
                pl.BlockSpec((1, D_MODEL, 2 * D_FF), lambda i, te, nu: (te[i], 0, 0)),
                pl.BlockSpec((1, 1, 2 * D_FF), lambda i, te, nu: (te[i], 0, 0)),
                pl.BlockSpec((1, D_FF, D_MODEL), lambda i, te, nu: (te[i], 0, 0)),
                pl.BlockSpec((1, 1, D_MODEL), lambda i, te, nu: (te[i], 0, 0)),
            ],
            out_specs=pl.BlockSpec((tm * TOK_ROWS, LANES), lambda i, te, nu: (i, 0)),
            scratch_shapes=[pltpu.VMEM((D_MODEL, 2 * D_FF), BF16), pltpu.VMEM((D_FF, D_MODEL), BF16)],
        ),
        out_shape=jax.ShapeDtypeStruct((P * TOK_ROWS, LANES), F32),
        compiler_params=_cparams("arbitrary"),
        name="moe_experts",
    )(tile_expert, n_used, xs3, w_gate_up, b_gate_up.reshape(N_EXPERTS, 1, 2 * D_FF), w_down,
      b_down.reshape(N_EXPERTS, 1, D_MODEL))


def _combine_kernel(pos_ref, posn_ref, wts_ref, h1_ref, nf_ref, ys_ref, o_ref, buf, sems):
    i = pl.program_id(0)
    n = pl.num_programs(0)
    tc = o_ref.shape[0]
    slot = lax.rem(i, 2)

    def gather(p_ref, s):
        def issue(r, c):
            for j in range(TOP_K):
                src = pl.multiple_of(p_ref[0, 0, r * TOP_K + j] * TOK_ROWS, TOK_ROWS)
                dst = pl.multiple_of((j * tc + r) * TOK_ROWS, TOK_ROWS)
                pltpu.make_async_copy(ys_ref.at[pl.ds(src, TOK_ROWS)], buf.at[s, pl.ds(dst, TOK_ROWS)],
                                      sems.at[s]).start(priority=j % 2)
            return c
        lax.fori_loop(0, tc, issue, 0, unroll=DMA_UNROLL // TOP_K)

    @pl.when(i == 0)
    def _():
        gather(pos_ref, 0)

    @pl.when(i + 1 < n)
    def _():
        gather(posn_ref, 1 - slot)

    pltpu.make_async_copy(ys_ref.at[pl.ds(0, TOP_K * tc * TOK_ROWS)], buf.at[slot], sems.at[slot]).wait()

    w = wts_ref[...]
    h = h1_ref[...]
    for j in range(TOP_K):
        h = h + w[:, j:j + 1] * _load_token_rows(buf.at[slot], j * tc, tc)
    o_ref[...] = h * lax.rsqrt(jnp.mean(h * h, axis=-1, keepdims=True) + EPS) * nf_ref[...]


def _combine(pos, wts, h1, norm_f_w, ys3):
    T = h1.shape[0]
    tc = min(MOE_TC, T)
    nt = T // tc
    pos3 = pos.reshape(nt, 1, tc * TOP_K)
    return pl.pallas_call(
        _combine_kernel,
        grid=(nt,),
        in_specs=[
            pl.BlockSpec((1, 1, tc * TOP_K), lambda i: (i, 0, 0), memory_space=pltpu.SMEM),
            pl.BlockSpec((1, 1, tc * TOP_K), lambda i: (jnp.minimum(i + 1, nt - 1), 0, 0), memory_space=pltpu.SMEM),
            pl.BlockSpec((tc, ROUTE_PAD), lambda i: (i, 0)),
            pl.BlockSpec((tc, D_MODEL), lambda i: (i, 0)),
            _const_spec((1, D_MODEL)),
            pl.BlockSpec(memory_space=pl.ANY),
        ],
        out_specs=pl.BlockSpec((tc, D_MODEL), lambda i: (i, 0)),
        out_shape=jax.ShapeDtypeStruct((T, D_MODEL), F32),
        scratch_shapes=[pltpu.VMEM((2, TOP_K * tc * TOK_ROWS, LANES), F32), pltpu.SemaphoreType.DMA((2,))],
        compiler_params=_cparams("arbitrary"),
        name="moe_combine",
    )(pos3, pos3, wts, h1, norm_f_w.reshape(1, D_MODEL), ys3)


def _route_plan(top_idx, T):
    tm = MOE_TM
    nt = (T * TOP_K) // tm + N_EXPERTS
    e = top_idx.reshape(-1)
    oh = (e[:, None] == jnp.arange(N_EXPERTS, dtype=jnp.int32)[None, :]).astype(jnp.int32)
    csum = jnp.cumsum(oh, axis=0)
    rank = jnp.sum(csum * oh, axis=1) - 1
    counts = csum[-1]
    tiles = (counts + tm - 1) // tm
    tile_end = jnp.cumsum(tiles)
    offs = (tile_end - tiles) * tm
    pos = jnp.sum(oh * offs[None, :], axis=1) + rank
    tile_id = jnp.arange(nt, dtype=jnp.int32)
    tile_expert = jnp.minimum(jnp.sum((tile_end[None, :] <= tile_id[:, None]).astype(jnp.int32), axis=1),
                              N_EXPERTS - 1).astype(jnp.int32)
    tail = tile_end[-1] + jnp.arange(N_EXPERTS, dtype=jnp.int32)
    last_tile = jnp.concatenate([jnp.where(tiles > 0, tile_end - 1, -1), jnp.where(tail < nt, tail, -1)]).astype(jnp.int32)
    n_used = tile_end[-1:].astype(jnp.int32)
    return pos.astype(jnp.int32), tile_expert, last_tile, n_used, nt * tm


def kernel(x, norm1_w, w_in, gla_gate_w2, gla_gate_b, gla_norm_w, hy_conv_w, hy_conv_b, hy_freq, hy_ffn_w1, hy_ffn_b1,
           hy_ffn_w2, hy_ffn_b2, hy_ffn_w3, hy_decay, hy_bias, w_branch_gla, w_branch_hy, w_out, norm2_w, router_w,
           router_b, w_gate_up, b_gate_up, w_down, b_down, norm_f_w):
    B, L, D = x.shape
    T = B * L
    x2 = x.reshape(T, D)

    s_a = N_QKVR
    s_h = s_a + 2 * GLA_RANK
    w_cat = jnp.concatenate(
        [w_in[:, :s_a], w_in[:, s_h:], w_in[:, s_a:s_h], jnp.zeros((D, ALOW_PAD - 2 * GLA_RANK), w_in.dtype)],
        axis=1).astype(BF16)
    qkvr, hy, gates, alow = _in_proj(x2, norm1_w, w_cat)

    w2 = jnp.zeros((2, ALOW_PAD, GLA_KEY), F32)
    w2 = w2.at[0, :GLA_RANK].set(gla_gate_w2[0]).at[1, GLA_RANK:2 * GLA_RANK].set(gla_gate_w2[1])
    o_gla = _gla(qkvr, alow, w2, gla_gate_b.reshape(2, 1, GLA_KEY), gla_norm_w, B, L)

    taps = _hy_filter(L, hy_freq, hy_ffn_w1, hy_ffn_b1, hy_ffn_w2, hy_ffn_b2, hy_ffn_w3, hy_decay)
    z, x0c = _hy_pre(hy, hy_conv_w, hy_conv_b, B, L)
    y = _fft_conv(z.reshape(B, L, HY_WIDTH), taps).reshape(T, HY_WIDTH)

    rw_f = jnp.zeros((D, ROUTE_PAD), F32).at[:, :N_EXPERTS].set(router_w)
    rw_hi = rw_f.astype(BF16)
    rw_pad = jnp.concatenate([rw_hi, (rw_f - rw_hi.astype(F32)).astype(BF16)], axis=1)
    rb_pad = jnp.zeros((1, ROUTE_PAD), F32).at[0, :N_EXPERTS].set(router_b)
    h1, hn3, idx_pad, wts_pad = _merge(o_gla, y, z, x0c, hy_bias, gates, x2, w_branch_gla.astype(BF16), w_branch_hy.astype(BF16),
                                       w_out.astype(BF16), norm2_w, rw_pad, rb_pad)

    pos, tile_expert, last_tile, n_used, P = _route_plan(idx_pad[:, :TOP_K], T)
    xs3 = _dispatch(hn3, pos, last_tile, P)
    ys3 = _experts(xs3, tile_expert, n_used, w_gate_up, b_gate_up, w_down, b_down)
    out = _combine(pos, wts_pad, h1, norm_f_w, ys3)
    return out.reshape(B, L, D)
```

```python
import functools
import math

import numpy as np
import jax
import jax.numpy as jnp
from jax import lax
from jax.experimental import pallas as pl
from jax.experimental.pallas import tpu as pltpu

D_MODEL = 1024
EPS = 1e-6
GLA_HEADS = 4
GLA_DK = 128
GLA_DV = 256
GLA_KEY = GLA_HEADS * GLA_DK
GLA_VAL = GLA_HEADS * GLA_DV
GLA_RANK = 16
GLA_GATE_TEMP = 16.0
GLA_CHUNK = 64
HY_WIDTH = D_MODEL
HY_BANDS = 16
HY_EMB = 1 + 2 * HY_BANDS
HY_FFN = 64
N_EXPERTS = 32
TOP_K = 4
D_FF = D_MODEL
SWIGLU_LIMIT = 7.0
SWIGLU_ALPHA = 1.702

LANES = 128
ALOW_PAD = LANES
ROUTE_PAD = LANES
VMEM_LIMIT = 56 * 1024 * 1024

F32 = jnp.float32
BF16 = jnp.bfloat16
HIGHEST = lax.Precision.HIGHEST


def _cparams(*sem):
    return pltpu.CompilerParams(dimension_semantics=sem, vmem_limit_bytes=VMEM_LIMIT)


def _const_spec(shape):
    nd = len(shape)
    return pl.BlockSpec(shape, lambda *_: (0,) * nd, pipeline_mode=pl.Buffered(1))


TOK_ROWS = D_MODEL // LANES


def _store_token_rows(ref, val):
    n = val.shape[0]
    for s_ in range(TOK_ROWS):
        ref[pl.ds(s_, n, stride=TOK_ROWS), :] = val[:, s_ * LANES:(s_ + 1) * LANES]


def _load_token_rows(ref, first, n):
    return jnp.concatenate(
        [ref[pl.ds(first * TOK_ROWS + s_, n, stride=TOK_ROWS), :] for s_ in range(TOK_ROWS)], axis=1)


N_QKVR = 2 * GLA_KEY + 2 * GLA_VAL
N_HY = 3 * HY_WIDTH
N_GATES = 2 * D_MODEL
IN_COLS = N_QKVR + N_HY + N_GATES + ALOW_PAD


def _inproj_kernel(x_ref, nw_ref, w_ref, qkvr_ref, hy_ref, gates_ref, alow_ref):
    x = x_ref[...]
    ms = jnp.mean(x * x, axis=-1, keepdims=True)
    xn = (x * lax.rsqrt(ms + EPS) * nw_ref[...]).astype(BF16)
    c0, c1, c2 = N_QKVR, N_QKVR + N_HY, N_QKVR + N_HY + N_GATES
    qkvr_ref[...] = jnp.dot(xn, w_ref[:, :c0], preferred_element_type=F32).astype(BF16)
    hy_ref[...] = jnp.dot(xn, w_ref[:, c0:c1], preferred_element_type=F32).astype(BF16)
    gates_ref[...] = jnp.dot(xn, w_ref[:, c1:c2], preferred_element_type=F32).astype(BF16)
    alow_ref[...] = jnp.dot(xn, w_ref[:, c2:], preferred_element_type=F32)


def _in_proj(x2, norm1_w, w_cat):
    T = x2.shape[0]
    tm = min(512, T)
    return pl.pallas_call(
        _inproj_kernel,
        grid=(T // tm,),
        in_specs=[
            pl.BlockSpec((tm, D_MODEL), lambda i: (i, 0)),
            _const_spec((1, D_MODEL)),
            _const_spec((D_MODEL, IN_COLS)),
        ],
        out_specs=[
            pl.BlockSpec((tm, N_QKVR), lambda i: (i, 0)),
            pl.BlockSpec((tm, N_HY), lambda i: (i, 0)),
            pl.BlockSpec((tm, N_GATES), lambda i: (i, 0)),
            pl.BlockSpec((tm, ALOW_PAD), lambda i: (i, 0)),
        ],
        out_shape=[
            jax.ShapeDtypeStruct((T, N_QKVR), BF16),
            jax.ShapeDtypeStruct((T, N_HY), BF16),
            jax.ShapeDtypeStruct((T, N_GATES), BF16),
            jax.ShapeDtypeStruct((T, ALOW_PAD), F32),
        ],
        compiler_params=_cparams("parallel"),
        name="in_proj",
    )(x2, norm1_w.reshape(1, D_MODEL), w_cat)


GLA_PREP_ROWS = 256
GLA_PREP_UNROLL = 4
GLA_SCAN_UNROLL = 16


def _log_sigmoid(x):
    return jnp.minimum(x, 0.0) - jnp.log1p(jnp.exp(-jnp.abs(x)))


def _gla_kernel(q_ref, k_ref, v_ref, r_ref, alow_ref, w2_ref, gb_ref, nw_ref, o_ref,
                qtf, kef, qtb, keb, decf, decb, of_s, ob_s, stf, stb):
    L = q_ref.shape[0]
    C = GLA_CHUNK
    DK = GLA_DK
    n_chunks = L // C
    R = min(GLA_PREP_ROWS, L)
    cpr = R // C
    scale = DK ** -0.5
    nt = (((1,), (1,)), ((), ()))
    tn = (((0,), (0,)), ((), ()))

    ri = lax.broadcasted_iota(jnp.int32, (R, R), 0)
    ci = lax.broadcasted_iota(jnp.int32, (R, R), 1)
    same = (ri // C) == (ci // C)
    mask_f = same & (ri >= ci)
    mask_b = same & (ri < ci)
    tri = jnp.where(mask_f, 1.0, 0.0).astype(BF16)
    w2 = jnp.concatenate([w2_ref[0], w2_ref[1]], axis=1).astype(BF16)
    gb = jnp.concatenate([gb_ref[0], gb_ref[1]], axis=1)

    def prep(g, carry):
        rows = pl.ds(pl.multiple_of(g * R, R), R)
        logits = jnp.dot(alow_ref[rows, :].astype(BF16), w2, preferred_element_type=F32) + gb
        lg = _log_sigmoid(logits) * (1.0 / GLA_GATE_TEMP)
        hi = lg.astype(BF16)
        lo = (lg - hi.astype(F32)).astype(BF16)
        b = jnp.dot(tri, hi, preferred_element_type=F32) + jnp.dot(tri, lo, preferred_element_type=F32)
        tot = jnp.concatenate(
            [jnp.broadcast_to(b[c * C + C - 1:c * C + C, :], (C, 2 * DK)) for c in range(cpr)], axis=0)
        q = q_ref[rows, :].astype(F32) * scale
        k = k_ref[rows, :].astype(F32)
        vb = v_ref[rows, :]
        dtot = jnp.exp(tot)
        for d, (qt, ke, dec, mask, out) in enumerate(((qtf, kef, decf, mask_f, of_s), (qtb, keb, decb, mask_b, ob_s))):
            sl = slice(d * DK, (d + 1) * DK)
            bd, td, ld = b[:, sl], tot[:, sl], lg[:, sl]
            if d == 0:
                e_q, e_end = bd, td - bd
            else:
                e_q, e_end = td - bd + ld, bd - ld
            qtv = (q * jnp.exp(e_q)).astype(BF16)
            ktv = (k * jnp.exp(-e_q)).astype(BF16)
            qt[rows, :] = qtv
            ke[rows, :] = (k * jnp.exp(e_end)).astype(BF16)
            s = lax.dot_general(qtv, ktv, nt, preferred_element_type=F32)
            s = jnp.where(mask, s, 0.0).astype(BF16)
            out[rows, :] = jnp.dot(s, vb, preferred_element_type=F32)
            for c in range(cpr):
                dec[pl.ds(g * cpr + c, 1), :] = dtot[c * C:c * C + 1, sl]
        return carry

    lax.fori_loop(0, L // R, prep, 0, unroll=min(GLA_PREP_UNROLL, L // R))

    def chunk(n, st, qt, ke, dec, out):
        rows = pl.ds(pl.multiple_of(n * C, C), C)
        s = st[...]
        out[rows, :] += lax.dot_general(qt[rows, :], s.astype(BF16), nt, preferred_element_type=F32)
        upd = lax.dot_general(v_ref[rows, :], ke[rows, :], tn, preferred_element_type=F32)
        st[...] = dec[pl.ds(n, 1), :] * s + upd

    def step(i, carry):
        chunk(i, stf, qtf, kef, decf, of_s)
        chunk(n_chunks - 1 - i, stb, qtb, keb, decb, ob_s)
        return carry

    stf[...] = jnp.zeros_like(stf)
    stb[...] = jnp.zeros_like(stb)
    lax.fori_loop(0, n_chunks, step, 0, unroll=min(GLA_SCAN_UNROLL, n_chunks))

    o = of_s[...] + ob_s[...]
    o = o * lax.rsqrt(jnp.mean(o * o, axis=-1, keepdims=True) + EPS) * nw_ref[...]
    r = r_ref[...].astype(F32)
    o_ref[...] = (o * (r * jax.nn.sigmoid(r))).astype(BF16)


def _gla(qkvr, alow, w2, gb, norm_w, B, L):
    T = B * L
    H = GLA_HEADS
    kq = GLA_KEY // GLA_DK
    kv = 2 * GLA_KEY // GLA_DV
    kr = (2 * GLA_KEY + GLA_VAL) // GLA_DV
    return pl.pallas_call(
        _gla_kernel,
        grid=(B, H),
        in_specs=[
            pl.BlockSpec((L, GLA_DK), lambda b, h: (b, h)),
            pl.BlockSpec((L, GLA_DK), lambda b, h: (b, kq + h)),
            pl.BlockSpec((L, GLA_DV), lambda b, h: (b, kv + h)),
            pl.BlockSpec((L, GLA_DV), lambda b, h: (b, kr + h)),
            pl.BlockSpec((L, ALOW_PAD), lambda b, h: (b, 0)),
            pl.BlockSpec((2, ALOW_PAD, GLA_DK), lambda b, h: (0, 0, h)),
            pl.BlockSpec((2, 1, GLA_DK), lambda b, h: (0, 0, h)),
            pl.BlockSpec((1, GLA_DV), lambda b, h: (0, 0)),
        ],
        out_specs=pl.BlockSpec((L, GLA_DV), lambda b, h: (b, h)),
        out_shape=jax.ShapeDtypeStruct((T, GLA_VAL), BF16),
        scratch_shapes=[pltpu.VMEM((L, GLA_DK), BF16)] * 4
        + [pltpu.VMEM((L // GLA_CHUNK, GLA_DK), F32)] * 2
        + [pltpu.VMEM((L, GLA_DV), F32)] * 2
        + [pltpu.VMEM((GLA_DV, GLA_DK), F32)] * 2,
        compiler_params=_cparams("parallel", "parallel"),
        name="gla",
    )(qkvr, qkvr, qkvr, qkvr, alow, w2, gb, norm_w.reshape(1, GLA_DV))


HY_WT = 256


def _hy_filter_kernel(z_ref, t_ref, fr_ref, w1_ref, b1_ref, w2_ref, b2_ref, w3f_ref, w3b_ref, dec_ref, o_ref, h_s):
    @pl.when(pl.program_id(0) == 0)
    def _():
        fr = fr_ref[...]
        h1 = jnp.sin(fr * (jnp.dot(z_ref[...], w1_ref[...], preferred_element_type=F32, precision=HIGHEST) + b1_ref[...]))
        h_s[...] = jnp.sin(fr * (jnp.dot(h1, w2_ref[...], preferred_element_type=F32, precision=HIGHEST) + b2_ref[...]))

    h = h_s[...]
    t = t_ref[...]
    row = lax.broadcasted_iota(jnp.int32, (z_ref.shape[0], 1), 0)
    hf = jnp.dot(h, w3f_ref[...], preferred_element_type=F32, precision=HIGHEST)
    hf = hf * jnp.exp(-t * jnp.abs(dec_ref[0:1, :]))
    hb = jnp.dot(h, w3b_ref[...], preferred_element_type=F32, precision=HIGHEST)
    hb = hb * jnp.exp(-t * jnp.abs(dec_ref[1:2, :]))
    hb = jnp.where(row == 0, 0.0, hb)
    ss = jnp.sum(hf * hf, axis=0, keepdims=True) + jnp.sum(hb * hb, axis=0, keepdims=True)
    inv = lax.rsqrt(ss + EPS)
    o_ref[0] = hf * inv
    o_ref[1] = hb * inv


def _hy_filter(L, hy_freq, w1, b1, w2, b2, w3, hy_decay):
    W = HY_WIDTH
    t = np.linspace(0.0, 1.0, L, dtype=np.float32)[:, None]
    omega = (np.float32(2.0 * math.pi) * np.arange(L, dtype=np.float32) / np.float32(L)).astype(np.float32)
    f = np.linspace(1e-4, HY_BANDS - 1, HY_BANDS, dtype=np.float32)
    ang = (omega[:, None] * f[None, :]).astype(np.float32)
    z = np.concatenate([t, np.cos(ang), -np.sin(ang)], axis=-1).astype(np.float32)
    nw = W // HY_WT
    return pl.pallas_call(
        _hy_filter_kernel,
        grid=(nw,),
        in_specs=[
            _const_spec((L, HY_EMB)),
            _const_spec((L, 1)),
            _const_spec((1, HY_FFN)),
            _const_spec((HY_EMB, HY_FFN)),
            _const_spec((1, HY_FFN)),
            _const_spec((HY_FFN, HY_FFN)),
            _const_spec((1, HY_FFN)),
            pl.BlockSpec((HY_FFN, HY_WT), lambda j: (0, j)),
            pl.BlockSpec((HY_FFN, HY_WT), lambda j: (0, nw + j)),
            pl.BlockSpec((2, HY_WT), lambda j: (0, j)),
        ],
        out_specs=pl.BlockSpec((2, L, HY_WT), lambda j: (0, 0, j)),
        out_shape=jax.ShapeDtypeStruct((2, L, W), F32),
        scratch_shapes=[pltpu.VMEM((L, HY_FFN), F32)],
        compiler_params=_cparams("arbitrary"),
        name="hy_filter",
    )(jnp.asarray(z), jnp.asarray(t), hy_freq.reshape(1, HY_FFN), w1, b1.reshape(1, HY_FFN), w2,
      b2.reshape(1, HY_FFN), w3, w3, hy_decay)


def _hy_pre_kernel(x0_ref, x1_ref, v_ref, w0_ref, w1_ref, wv_ref, b0_ref, b1_ref, bv_ref, z_ref, x0o_ref):
    L = x0_ref.shape[0]
    row = lax.broadcasted_iota(jnp.int32, (L, 1), 0)

    def conv3(u_ref, w_ref, b_ref):
        u = u_ref[...].astype(F32)
        prev = jnp.where(row == 0, 0.0, pltpu.roll(u, 1, 0))
        nxt = jnp.where(row == L - 1, 0.0, pltpu.roll(u, L - 1, 0))
        return w_ref[0:1, :] * prev + w_ref[1:2, :] * u + w_ref[2:3, :] * nxt + b_ref[...]

    x0o_ref[...] = conv3(x0_ref, w0_ref, b0_ref).astype(BF16)
    z_ref[...] = conv3(v_ref, wv_ref, bv_ref) * conv3(x1_ref, w1_ref, b1_ref)


def _hy_pre(hy, conv_w, conv_b, B, L):
    W = HY_WIDTH
    nw = W // HY_WT
    cb = conv_b.reshape(1, 3 * W)
    blk = lambda off: pl.BlockSpec((L, HY_WT), lambda b, j: (b, off + j))
    wblk = lambda off: pl.BlockSpec((3, HY_WT), lambda b, j: (0, off + j))
    bblk = lambda off: pl.BlockSpec((1, HY_WT), lambda b, j: (0, off + j))
    return pl.pallas_call(
        _hy_pre_kernel,
        grid=(B, nw),
        in_specs=[blk(0), blk(nw), blk(2 * nw), wblk(0), wblk(nw), wblk(2 * nw), bblk(0), bblk(nw), bblk(2 * nw)],
        out_specs=[pl.BlockSpec((L, HY_WT), lambda b, j: (b, j))] * 2,
        out_shape=[jax.ShapeDtypeStruct((B * L, W), F32), jax.ShapeDtypeStruct((B * L, W), BF16)],
        compiler_params=_cparams("parallel", "parallel"),
        name="hy_pre",
    )(hy, hy, hy, conv_w, conv_w, conv_w, cb, cb, cb)


FFT_L = LANES
FFT_PITCH = FFT_L + 8
FFT_UNROLL = 8


def _fft_tables(L):
    N = 2 * L
    C = L // FFT_L
    n1 = N // FFT_L
    kh = min(n1, -(-(n1 // 2 + 1) // 8) * 8)
    l = np.arange(FFT_L)
    c = np.arange(C)
    k2 = np.arange(kh)
    pair_w = np.where((k2 == 0) | (k2 == n1 // 2), 1.0, np.where(k2 < n1 // 2, 2.0, 0.0))
    ang1 = 2 * np.pi * (((k2[None, :, None] * c[None, None, :]) % n1) / n1
                        + ((l[:, None, None] * k2[None, :, None]) % N) / N)
    f1 = np.concatenate([np.cos(ang1), -np.sin(ang1)], axis=1)
    g1 = np.stack([np.cos(ang1), -np.sin(ang1)], axis=-1) * pair_w[None, :, None, None]
    g1 = np.transpose(g1, (0, 2, 1, 3)).reshape(FFT_L, C, 2 * kh) / N
    a2 = 2 * np.pi * np.outer(l, l) / FFT_L
    cs, sn = np.cos(a2), np.sin(a2)
    f2 = np.block([[cs, sn], [-sn, cs]])
    g2 = np.block([[cs, -sn], [sn, cs]])
    bf = lambda a: jnp.asarray(a.astype(np.float32)).astype(BF16)
    return dict(C=C, n1=kh, f1=bf(f1), g1=bf(g1), f2=bf(f2), g2=bf(g2))


def _fft_kernel(z_ref, f1_ref, f2_ref, g2_ref, kf_ref, g1_ref, o_ref, a_s, b_s, *, spectrum_only):
    C = f1_ref.shape[2]
    n1 = f1_ref.shape[1] // 2
    P = FFT_PITCH

    def stage1(l, carry):
        x = z_ref[pl.ds(l, C, stride=FFT_L), :].astype(BF16)
        a_s[pl.ds(pl.multiple_of(l * P, 8), 2 * n1), :] = jnp.dot(f1_ref[l], x, preferred_element_type=F32)
        return carry
    lax.fori_loop(0, FFT_L, stage1, 0, unroll=FFT_UNROLL)

    def stage2(k2, carry):
        ar = a_s[pl.ds(k2, FFT_L, stride=P), :]
        ai = a_s[pl.ds(n1 + k2, FFT_L, stride=P), :]
        x = jnp.concatenate([ar, ai], axis=0).astype(BF16)
        s = jnp.dot(f2_ref[...], x, preferred_element_type=F32)
        if spectrum_only:
            o_ref[k2, 0] = s[:FFT_L]
            o_ref[k2, 1] = s[FFT_L:]
        else:
            xr, xi = s[:FFT_L], s[FFT_L:]
            kr, ki = kf_ref[k2, 0].astype(F32), kf_ref[k2, 1].astype(F32)
            y = jnp.concatenate([xr * kr - xi * ki, xr * ki + xi * kr], axis=0).astype(BF16)
            s = jnp.dot(g2_ref[...], y, preferred_element_type=F32)
            b_s[pl.ds(pl.multiple_of(2 * k2 * P, 8), FFT_L), :] = s[:FFT_L]
            b_s[pl.ds(pl.multiple_of((2 * k2 + 1) * P, 8), FFT_L), :] = s[FFT_L:]
        return carry
    lax.fori_loop(0, n1, stage2, 0, unroll=FFT_UNROLL)

    if not spectrum_only:
        def stage3(l, carry):
            x = b_s[pl.ds(l, 2 * n1, stride=P), :].astype(BF16)
            o_ref[pl.ds(l, C, stride=FFT_L), :] = jnp.dot(g1_ref[l], x, preferred_element_type=F32)
            return carry
        lax.fori_loop(0, FFT_L, stage3, 0, unroll=FFT_UNROLL)


def _fft(tabs, z, kf, spectrum_only):
    B, L, W = z.shape
    C, n1 = tabs["C"], tabs["n1"]
    if spectrum_only:
        out_spec = pl.BlockSpec((None, n1, 2, FFT_L, LANES), lambda w, b: (b, 0, 0, 0, w))
        out_shape = jax.ShapeDtypeStruct((B, n1, 2, FFT_L, W), F32)
    else:
        out_spec = pl.BlockSpec((None, L, LANES), lambda w, b: (b, 0, w))
        out_shape = jax.ShapeDtypeStruct((B, L, W), F32)
    return pl.pallas_call(
        functools.partial(_fft_kernel, spectrum_only=spectrum_only),
        grid=(W // LANES, B),
        in_specs=[pl.BlockSpec((None, L, LANES), lambda w, b: (b, 0, w)),
                  _const_spec((FFT_L, 2 * n1, C)),
                  _const_spec((2 * FFT_L, 2 * FFT_L)), _const_spec((2 * FFT_L, 2 * FFT_L)),
                  pl.BlockSpec((n1, 2, FFT_L, LANES), lambda w, b: (0, 0, 0, w)),
                  _const_spec((FFT_L, C, 2 * n1))],
        out_specs=out_spec,
        out_shape=out_shape,
        scratch_shapes=[pltpu.VMEM((FFT_L * FFT_PITCH, LANES), F32), pltpu.VMEM((2 * n1 * FFT_PITCH, LANES), F32)],
        compiler_params=_cparams("arbitrary", "arbitrary"),
        name="fft_spectrum" if spectrum_only else "fft_conv",
    )(z, tabs["f1"], tabs["f2"], tabs["g2"], kf, tabs["g1"])


def _fft_conv(z, taps):
    tabs = _fft_tables(z.shape[1])
    unused = jnp.zeros((tabs["n1"], 2, FFT_L, z.shape[2]), BF16)
    ts = _fft(tabs, taps, unused, True)
    kf = jnp.stack([ts[0, :, 0] + ts[1, :, 0], ts[0, :, 1] - ts[1, :, 1]], axis=1).astype(BF16)
    return _fft(tabs, z, kf, False)


def _merge_kernel(og_ref, y_ref, z_ref, x0_ref, hb_ref, gates_ref, x_ref, wbg_ref, wbh_ref, wo_ref, n2_ref, rw_ref, rb_ref,
                  h1_ref, hn3_ref, idx_ref, wts_ref):
    mg = jnp.dot(og_ref[...], wbg_ref[...], preferred_element_type=F32)
    o_hy = ((y_ref[...] + z_ref[...] * hb_ref[...]) * x0_ref[...].astype(F32)).astype(BF16)
    mh = jnp.dot(o_hy, wbh_ref[...], preferred_element_type=F32)
    gg = jax.nn.sigmoid(gates_ref[:, :D_MODEL].astype(F32))
    gh = jax.nn.sigmoid(gates_ref[:, D_MODEL:].astype(F32))
    merged = (gg * mg + gh * mh).astype(BF16)
    h1 = x_ref[...] + jnp.dot(merged, wo_ref[...], preferred_element_type=F32)
    h1_ref[...] = h1
    hn = h1 * lax.rsqrt(jnp.mean(h1 * h1, axis=-1, keepdims=True) + EPS) * n2_ref[...]
    _store_token_rows(hn3_ref, hn)
    hh = hn.astype(BF16)
    hl = (hn - hh.astype(F32)).astype(BF16)
    p1 = jnp.dot(hh, rw_ref[...], preferred_element_type=F32)
    p2 = jnp.dot(hl, rw_ref[:, :ROUTE_PAD], preferred_element_type=F32)
    logits = p1[:, :ROUTE_PAD] + p1[:, ROUTE_PAD:] + p2 + rb_ref[...]
    lane = lax.broadcasted_iota(jnp.int32, logits.shape, 1)
    logits = jnp.where(lane < N_EXPERTS, logits, -jnp.inf)
    idx_out = jnp.zeros(logits.shape, jnp.int32)
    val_out = jnp.zeros(logits.shape, F32)
    v0 = None
    for j in range(TOP_K):
        m = jnp.max(logits, axis=-1, keepdims=True)
        sel = jnp.min(jnp.where(logits == m, lane, ROUTE_PAD), axis=-1, keepdims=True)
        if j == 0:
            v0 = m
        idx_out = jnp.where(lane == j, sel, idx_out)
        val_out = jnp.where(lane == j, jnp.exp(m - v0), val_out)
        logits = jnp.where(lane == sel, -jnp.inf, logits)
    idx_ref[...] = idx_out
    wts_ref[...] = val_out / jnp.sum(val_out, axis=-1, keepdims=True)


def _merge(o_gla, y, z, x0c, hy_bias, gates, x2, wbg, wbh, wo, norm2_w, rw_pad, rb_pad):
    T = x2.shape[0]
    tm = min(512, T)
    row = lambda n: pl.BlockSpec((tm, n), lambda i: (i, 0))
    return pl.pallas_call(
        _merge_kernel,
        grid=(T // tm,),
        in_specs=[row(GLA_VAL), row(HY_WIDTH), row(HY_WIDTH), row(HY_WIDTH), _const_spec((1, HY_WIDTH)), row(N_GATES), row(D_MODEL),
                  _const_spec((GLA_VAL, D_MODEL)), _const_spec((HY_WIDTH, D_MODEL)), _const_spec((D_MODEL, D_MODEL)),
                  _const_spec((1, D_MODEL)), _const_spec((D_MODEL, 2 * ROUTE_PAD)), _const_spec((1, ROUTE_PAD))],
        out_specs=[row(D_MODEL), pl.BlockSpec((tm * TOK_ROWS, LANES), lambda i: (i, 0)), row(ROUTE_PAD), row(ROUTE_PAD)],
        out_shape=[jax.ShapeDtypeStruct((T, D_MODEL), F32), jax.ShapeDtypeStruct((T * TOK_ROWS, LANES), F32),
                   jax.ShapeDtypeStruct((T, ROUTE_PAD), jnp.int32), jax.ShapeDtypeStruct((T, ROUTE_PAD), F32)],
        compiler_params=_cparams("parallel"),
        name="merge_route",
    )(o_gla, y, z, x0c, hy_bias.reshape(1, HY_WIDTH), gates, x2, wbg, wbh, wo, norm2_w.reshape(1, D_MODEL), rw_pad, rb_pad)


MOE_TM = 512
MOE_TC = 256
DMA_UNROLL = 8


DISPATCH_SLOTS = 3


def _dispatch_kernel(last_ref, pos_ref, hn_ref, xs_ref, zero_s, buf, sem, lsems, ssems):
    i = pl.program_id(0)
    n = pl.num_programs(0)
    tc = pos_ref.shape[2] // TOP_K
    rows = tc * TOK_ROWS
    zrows = zero_s.shape[0]
    slot = lax.rem(i, DISPATCH_SLOTS)
    prev_slot = lax.rem(i + DISPATCH_SLOTS - 1, DISPATCH_SLOTS)

    def load(blk, s):
        src = hn_ref.at[pl.ds(pl.multiple_of(blk * rows, rows), rows)]
        return pltpu.make_async_copy(src, buf.at[s], lsems.at[s])

    def wait_copies(s):
        for _ in range(TOP_K):
            pltpu.make_async_copy(buf.at[s], xs_ref.at[pl.ds(0, rows)], ssems.at[s]).wait()

    @pl.when(i == 0)
    def _():
        load(0, 0).start()

        @pl.when(n > 1)
        def _():
            load(1, 1).start()

        zero_s[...] = jnp.zeros_like(zero_s)
        n_fill = last_ref.shape[0]

        def zfill(e, c):
            @pl.when(last_ref[e] >= 0)
            def _():
                dst = pl.multiple_of(last_ref[e] * zrows, zrows)
                pltpu.make_async_copy(zero_s, xs_ref.at[pl.ds(dst, zrows)], sem).start()
            return c
        lax.fori_loop(0, n_fill, zfill, 0)

        def zwait(e, c):
            @pl.when(last_ref[e] >= 0)
            def _():
                pltpu.make_async_copy(zero_s, xs_ref.at[pl.ds(0, zrows)], sem).wait()
            return c
        lax.fori_loop(0, n_fill, zwait, 0)

    load(i, slot).wait()

    def issue(r, c):
        src = buf.at[slot, pl.ds(pl.multiple_of(r * TOK_ROWS, TOK_ROWS), TOK_ROWS)]
        for j in range(TOP_K):
            dst = pl.multiple_of(pos_ref[0, 0, r * TOP_K + j] * TOK_ROWS, TOK_ROWS)
            pltpu.make_async_copy(src, xs_ref.at[pl.ds(dst, TOK_ROWS)], ssems.at[slot]).start(priority=j % 2)
        return c
    lax.fori_loop(0, tc, issue, 0, unroll=DMA_UNROLL // TOP_K)

    @pl.when(i > 0)
    def _():
        wait_copies(prev_slot)

    @pl.when(i + 2 < n)
    def _():
        load(i + 2, prev_slot).start()

    @pl.when(i == n - 1)
    def _():
        wait_copies(slot)


def _dispatch(hn3, pos, last_tile, P):
    T = hn3.shape[0] // TOK_ROWS
    tc = min(MOE_TC, T)
    nt = T // tc
    return pl.pallas_call(
        _dispatch_kernel,
        grid_spec=pltpu.PrefetchScalarGridSpec(
            num_scalar_prefetch=1,
            grid=(nt,),
            in_specs=[pl.BlockSpec((1, 1, tc * TOP_K), lambda i, lt: (i, 0, 0), memory_space=pltpu.SMEM),
                      pl.BlockSpec(memory_space=pl.ANY)],
            out_specs=pl.BlockSpec(memory_space=pl.ANY),
            scratch_shapes=[pltpu.VMEM((MOE_TM * TOK_ROWS, LANES), F32),
                            pltpu.VMEM((DISPATCH_SLOTS, tc * TOK_ROWS, LANES), F32),
                            pltpu.SemaphoreType.DMA(()),
                            pltpu.SemaphoreType.DMA((DISPATCH_SLOTS,)), pltpu.SemaphoreType.DMA((DISPATCH_SLOTS,))],
        ),
        out_shape=jax.ShapeDtypeStruct((P * TOK_ROWS, LANES), F32),
        compiler_params=_cparams("arbitrary"),
        name="moe_dispatch",
    )(last_tile, pos.reshape(nt, 1, tc * TOP_K), hn3)


def _expert_kernel(te_ref, nused_ref, xs_ref, wgu_ref, bgu_ref, wd_ref, bd_ref, o_ref, wgu_s, wd_s):
    i = pl.program_id(0)
    prev = te_ref[jnp.maximum(i - 1, 0)]

    @pl.when((i == 0) | (te_ref[i] != prev))
    def _():
        wgu_s[...] = wgu_ref[0].astype(BF16)
        wd_s[...] = wd_ref[0].astype(BF16)

    @pl.when(i < nused_ref[0])
    def _():
        x = _load_token_rows(xs_ref, 0, xs_ref.shape[0] // TOK_ROWS).astype(BF16)
        gu = jnp.dot(x, wgu_s[...], preferred_element_type=F32) + bgu_ref[0]
        gate = jnp.minimum(gu[:, :D_FF], SWIGLU_LIMIT)
        up = jnp.clip(gu[:, D_FF:], -SWIGLU_LIMIT, SWIGLU_LIMIT)
        act = gate * jax.nn.sigmoid(SWIGLU_ALPHA * gate) * (up + 1.0)
        y = jnp.dot(act.astype(BF16), wd_s[...], preferred_element_type=F32) + bd_ref[0]
        _store_token_rows(o_ref, y)

    @pl.when(i >= nused_ref[0])
    def _():
        o_ref[...] = jnp.zeros_like(o_ref)


def _experts(xs3, tile_expert, n_used, w_gate_up, b_gate_up, w_down, b_down):
    P = xs3.shape[0] // TOK_ROWS
    tm = MOE_TM
    nt = P // tm
    live = lambda i, te, nu: jnp.minimum(i, nu[0] - 1)
    return pl.pallas_call(
        _expert_kernel,
        grid_spec=pltpu.PrefetchScalarGridSpec(
            num_scalar_prefetch=2,
            grid=(nt,),
            in_specs=[
                pl.BlockSpec((tm * TOK_ROWS, LANES), lambda i, te, nu: (live(i, te, nu), 0)),
                pl.BlockSpec((1, D_MODEL, 2 * D_FF), lambda i, te, nu: (te[i], 0, 0)),
                pl.BlockSpec((1, 1, 2 * D_FF), lambda i, te, nu: (te[i], 0, 0)),
                pl.BlockSpec((1, D_FF, D_MODEL), lambda i, te, nu: (te[i], 0, 0)),
                pl.BlockSpec((1, 1, D_MODEL), lambda i, te, nu: (te[i], 0, 0)),
            ],
            out_specs=pl.BlockSpec((tm * TOK_ROWS, LANES), lambda i, te, nu: (i, 0)),
            scratch_shapes=[pltpu.VMEM((D_MODEL, 2 * D_FF), BF16), pltpu.VMEM((D_FF, D_MODEL), BF16)],
        ),
        out_shape=jax.ShapeDtypeStruct((P * TOK_ROWS, LANES), F32),
        compiler_params=_cparams("arbitrary"),
        name="moe_experts",
    )(tile_expert, n_used, xs3, w_gate_up, b_gate_up.reshape(N_EXPERTS, 1, 2 * D_FF), w_down,
      b_down.reshape(N_EXPERTS, 1, D_MODEL))


def _combine_kernel(pos_ref, posn_ref, wts_ref, h1_ref, nf_ref, ys_ref, o_ref, buf, sems):
    i = pl.program_id(0)
    n = pl.num_programs(0)
    tc = o_ref.shape[0]
    slot = lax.rem(i, 2)

    def gather(p_ref, s):
        def issue(r, c):
            for j in range(TOP_K):
                src = pl.multiple_of(p_ref[0, 0, r * TOP_K + j] * TOK_ROWS, TOK_ROWS)
                dst = pl.multiple_of((j * tc + r) * TOK_ROWS, TOK_ROWS)
                pltpu.make_async_copy(ys_ref.at[pl.ds(src, TOK_ROWS)], buf.at[s, pl.ds(dst, TOK_ROWS)],
                                      sems.at[s]).start(priority=j % 2)
            return c
        lax.fori_loop(0, tc, issue, 0, unroll=DMA_UNROLL // TOP_K)

    @pl.when(i == 0)
    def _():
        gather(pos_ref, 0)

    @pl.when(i + 1 < n)
    def _():
        gather(posn_ref, 1 - slot)

    pltpu.make_async_copy(ys_ref.at[pl.ds(0, TOP_K * tc * TOK_ROWS)], buf.at[slot], sems.at[slot]).wait()

    w = wts_ref[...]
    h = h1_ref[...]
    for j in range(TOP_K):
        h = h + w[:, j:j + 1] * _load_token_rows(buf.at[slot], j * tc, tc)
    o_ref[...] = h * lax.rsqrt(jnp.mean(h * h, axis=-1, keepdims=True) + EPS) * nf_ref[...]


def _combine(pos, wts, h1, norm_f_w, ys3):
    T = h1.shape[0]
    tc = min(MOE_TC, T)
    nt = T // tc
    pos3 = pos.reshape(nt, 1, tc * TOP_K)
    return pl.pallas_call(
        _combine_kernel,
        grid=(nt,),
        in_specs=[
            pl.BlockSpec((1, 1, tc * TOP_K), lambda i: (i, 0, 0), memory_space=pltpu.SMEM),
            pl.BlockSpec((1, 1, tc * TOP_K), lambda i: (jnp.minimum(i + 1, nt - 1), 0, 0), memory_space=pltpu.SMEM),
            pl.BlockSpec((tc, ROUTE_PAD), lambda i: (i, 0)),
            pl.BlockSpec((tc, D_MODEL), lambda i: (i, 0)),
            _const_spec((1, D_MODEL)),
            pl.BlockSpec(memory_space=pl.ANY),
        ],
        out_specs=pl.BlockSpec((tc, D_MODEL), lambda i: (i, 0)),
        out_shape=jax.ShapeDtypeStruct((T, D_MODEL), F32),
        scratch_shapes=[pltpu.VMEM((2, TOP_K * tc * TOK_ROWS, LANES), F32), pltpu.SemaphoreType.DMA((2,))],
        compiler_params=_cparams("arbitrary"),
        name="moe_combine",
    )(pos3, pos3, wts, h1, norm_f_w.reshape(1, D_MODEL), ys3)


def _route_plan(top_idx, T):
    tm = MOE_TM
    nt = (T * TOP_K) // tm + N_EXPERTS
    e = top_idx.reshape(-1)
    oh = (e[:, None] == jnp.arange(N_EXPERTS, dtype=jnp.int32)[None, :]).astype(jnp.int32)
    csum = jnp.cumsum(oh, axis=0)
    rank = jnp.sum(csum * oh, axis=1) - 1
    counts = csum[-1]
    tiles = (counts + tm - 1) // tm
    tile_end = jnp.cumsum(tiles)
    offs = (tile_end - tiles) * tm
    pos = jnp.sum(oh * offs[None, :], axis=1) + rank
    tile_id = jnp.arange(nt, dtype=jnp.int32)
    tile_expert = jnp.minimum(jnp.sum((tile_end[None, :] <= tile_id[:, None]).astype(jnp.int32), axis=1),
                              N_EXPERTS - 1).astype(jnp.int32)
    tail = tile_end[-1] + jnp.arange(N_EXPERTS, dtype=jnp.int32)
    last_tile = jnp.concatenate([jnp.where(tiles > 0, tile_end - 1, -1), jnp.where(tail < nt, tail, -1)]).astype(jnp.int32)
    n_used = tile_end[-1:].astype(jnp.int32)
    return pos.astype(jnp.int32), tile_expert, last_tile, n_used, nt * tm


def kernel(x, norm1_w, w_in, gla_gate_w2, gla_gate_b, gla_norm_w, hy_conv_w, hy_conv_b, hy_freq, hy_ffn_w1, hy_ffn_b1,
           hy_ffn_w2, hy_ffn_b2, hy_ffn_w3, hy_decay, hy_bias, w_branch_gla, w_branch_hy, w_out, norm2_w, router_w,
           router_b, w_gate_up, b_gate_up, w_down, b_down, norm_f_w):
    B, L, D = x.shape
    T = B * L
    x2 = x.reshape(T, D)

    s_a = N_QKVR
    s_h = s_a + 2 * GLA_RANK
    w_cat = jnp.concatenate(
        [w_in[:, :s_a], w_in[:, s_h:], w_in[:, s_a:s_h], jnp.zeros((D, ALOW_PAD - 2 * GLA_RANK), w_in.dtype)],
        axis=1).astype(BF16)
    qkvr, hy, gates, alow = _in_proj(x2, norm1_w, w_cat)

    w2 = jnp.zeros((2, ALOW_PAD, GLA_KEY), F32)
    w2 = w2.at[0, :GLA_RANK].set(gla_gate_w2[0]).at[1, GLA_RANK:2 * GLA_RANK].set(gla_gate_w2[1])
    o_gla = _gla(qkvr, alow, w2, gla_gate_b.reshape(2, 1, GLA_KEY), gla_norm_w, B, L)

    taps = _hy_filter(L, hy_freq, hy_ffn_w1, hy_ffn_b1, hy_ffn_w2, hy_ffn_b2, hy_ffn_w3, hy_decay)
    z, x0c = _hy_pre(hy, hy_conv_w, hy_conv_b, B, L)
    y = _fft_conv(z.reshape(B, L, HY_WIDTH), taps).reshape(T, HY_WIDTH)

    rw_f = jnp.zeros((D, ROUTE_PAD), F32).at[:, :N_EXPERTS].set(router_w)
    rw_hi = rw_f.astype(BF16)
    rw_pad = jnp.concatenate([rw_hi, (rw_f - rw_hi.astype(F32)).astype(BF16)], axis=1)
    rb_pad = jnp.zeros((1, ROUTE_PAD), F32).at[0, :N_EXPERTS].set(router_b)
    h1, hn3, idx_pad, wts_pad = _merge(o_gla, y, z, x0c, hy_bias, gates, x2, w_branch_gla.astype(BF16), w_branch_hy.astype(BF16),
                                       w_out.astype(BF16), norm2_w, rw_pad, rb_pad)

    pos, tile_expert, last_tile, n_used, P = _route_plan(idx_pad[:, :TOP_K], T)
    xs3 = _dispatch(hn3, pos, last_tile, P)
    ys3 = _experts(xs3, tile_expert, n_used, w_gate_up, b_gate_up, w_down, b_down)
    out = _combine(pos, wts_pad, h1, norm_f_w, ys3)
    return out.reshape(B, L, D)
```

```python
import functools
import math

import numpy as np
import jax
import jax.numpy as jnp
from jax import lax
from jax.experimental import pallas as pl
from jax.experimental.pallas import tpu as pltpu

D_MODEL = 1024
EPS = 1e-6
GLA_HEADS = 4
GLA_DK = 128
GLA_DV = 256
GLA_KEY = GLA_HEADS * GLA_DK
GLA_VAL = GLA_HEADS * GLA_DV
GLA_RANK = 16
GLA_GATE_TEMP = 16.0
GLA_CHUNK = 64
HY_WIDTH = D_MODEL
HY_BANDS = 16
HY_EMB = 1 + 2 * HY_BANDS
HY_FFN = 64
N_EXPERTS = 32
TOP_K = 4
D_FF = D_MODEL
SWIGLU_LIMIT = 7.0
SWIGLU_ALPHA = 1.702

LANES = 128
ALOW_PAD = LANES
ROUTE_PAD = LANES
VMEM_LIMIT = 56 * 1024 * 1024

F32 = jnp.float32
BF16 = jnp.bfloat16
HIGHEST = lax.Precision.HIGHEST


def _cparams(*sem):
    return pltpu.CompilerParams(dimension_semantics=sem, vmem_limit_bytes=VMEM_LIMIT)


def _const_spec(shape):
    nd = len(shape)
    return pl.BlockSpec(shape, lambda *_: (0,) * nd, pipeline_mode=pl.Buffered(1))


TOK_ROWS = D_MODEL // LANES


def _store_token_rows(ref, val):
    n = val.shape[0]
    for s_ in range(TOK_ROWS):
        ref[pl.ds(s_, n, stride=TOK_ROWS), :] = val[:, s_ * LANES:(s_ + 1) * LANES]


def _load_token_rows(ref, first, n):
    return jnp.concatenate(
        [ref[pl.ds(first * TOK_ROWS + s_, n, stride=TOK_ROWS), :] for s_ in range(TOK_ROWS)], axis=1)


N_QKVR = 2 * GLA_KEY + 2 * GLA_VAL
N_HY = 3 * HY_WIDTH
N_GATES = 2 * D_MODEL
IN_COLS = N_QKVR + N_HY + N_GATES + ALOW_PAD


def _inproj_kernel(x_ref, nw_ref, w_ref, qkvr_ref, hy_ref, gates_ref, alow_ref):
    x = x_ref[...]
    ms = jnp.mean(x * x, axis=-1, keepdims=True)
    xn = (x * lax.rsqrt(ms + EPS) * nw_ref[...]).astype(BF16)
    c0, c1, c2 = N_QKVR, N_QKVR + N_HY, N_QKVR + N_HY + N_GATES
    qkvr_ref[...] = jnp.dot(xn, w_ref[:, :c0], preferred_element_type=F32).astype(BF16)
    hy_ref[...] = jnp.dot(xn, w_ref[:, c0:c1], preferred_element_type=F32).astype(BF16)
    gates_ref[...] = jnp.dot(xn, w_ref[:, c1:c2], preferred_element_type=F32).astype(BF16)
    alow_ref[...] = jnp.dot(xn, w_ref[:, c2:], preferred_element_type=F32)


def _in_proj(x2, norm1_w, w_cat):
    T = x2.shape[0]
    tm = min(512, T)
    return pl.pallas_call(
        _inproj_kernel,
        grid=(T // tm,),
        in_specs=[
            pl.BlockSpec((tm, D_MODEL), lambda i: (i, 0)),
            _const_spec((1, D_MODEL)),
            _const_spec((D_MODEL, IN_COLS)),
        ],
        out_specs=[
            pl.BlockSpec((tm, N_QKVR), lambda i: (i, 0)),
            pl.BlockSpec((tm, N_HY), lambda i: (i, 0)),
            pl.BlockSpec((tm, N_GATES), lambda i: (i, 0)),
            pl.BlockSpec((tm, ALOW_PAD), lambda i: (i, 0)),
        ],
        out_shape=[
            jax.ShapeDtypeStruct((T, N_QKVR), BF16),
            jax.ShapeDtypeStruct((T, N_HY), BF16),
            jax.ShapeDtypeStruct((T, N_GATES), BF16),
            jax.ShapeDtypeStruct((T, ALOW_PAD), F32),
        ],
        compiler_params=_cparams("parallel"),
        name="in_proj",
    )(x2, norm1_w.reshape(1, D_MODEL), w_cat)


GLA_PREP_ROWS = 256
GLA_PREP_UNROLL = 4
GLA_SCAN_UNROLL = 16


def _log_sigmoid(x):
    return jnp.minimum(x, 0.0) - jnp.log1p(jnp.exp(-jnp.abs(x)))


def _gla_kernel(q_ref, k_ref, v_ref, r_ref, alow_ref, w2_ref, gb_ref, nw_ref, o_ref,
                qtf, kef, qtb, keb, decf, decb, of_s, ob_s, stf, stb):
    L = q_ref.shape[0]
    C = GLA_CHUNK
    DK = GLA_DK
    n_chunks = L // C
    R = min(GLA_PREP_ROWS, L)
    cpr = R // C
    scale = DK ** -0.5
    nt = (((1,), (1,)), ((), ()))
    tn = (((0,), (0,)), ((), ()))

    ri = lax.broadcasted_iota(jnp.int32, (R, R), 0)
    ci = lax.broadcasted_iota(jnp.int32, (R, R), 1)
    same = (ri // C) == (ci // C)
    mask_f = same & (ri >= ci)
    mask_b = same & (ri < ci)
    tri = jnp.where(mask_f, 1.0, 0.0).astype(BF16)
    w2 = jnp.concatenate([w2_ref[0], w2_ref[1]], axis=1).astype(BF16)
    gb = jnp.concatenate([gb_ref[0], gb_ref[1]], axis=1)

    def prep(g, carry):
        rows = pl.ds(pl.multiple_of(g * R, R), R)
        logits = jnp.dot(alow_ref[rows, :].astype(BF16), w2, preferred_element_type=F32) + gb
        lg = _log_sigmoid(logits) * (1.0 / GLA_GATE_TEMP)
        hi = lg.astype(BF16)
        lo = (lg - hi.astype(F32)).astype(BF16)
        b = jnp.dot(tri, hi, preferred_element_type=F32) + jnp.dot(tri, lo, preferred_element_type=F32)
        tot = jnp.concatenate(
            [jnp.broadcast_to(b[c * C + C - 1:c * C + C, :], (C, 2 * DK)) for c in range(cpr)], axis=0)
        q = q_ref[rows, :].astype(F32) * scale
        k = k_ref[rows, :].astype(F32)
        vb = v_ref[rows, :]
        dtot = jnp.exp(tot)
        for d, (qt, ke, dec, mask, out) in enumerate(((qtf, kef, decf, mask_f, of_s), (qtb, keb, decb, mask_b, ob_s))):
            sl = slice(d * DK, (d + 1) * DK)
            bd, td, ld = b[:, sl], tot[:, sl], lg[:, sl]
            if d == 0:
                e_q, e_end = bd, td - bd
            else:
                e_q, e_end = td - bd + ld, bd - ld
            qtv = (q * jnp.exp(e_q)).astype(BF16)
            ktv = (k * jnp.exp(-e_q)).astype(BF16)
            qt[rows, :] = qtv
            ke[rows, :] = (k * jnp.exp(e_end)).astype(BF16)
            s = lax.dot_general(qtv, ktv, nt, preferred_element_type=F32)
            s = jnp.where(mask, s, 0.0).astype(BF16)
            out[rows, :] = jnp.dot(s, vb, preferred_element_type=F32)
            for c in range(cpr):
                dec[pl.ds(g * cpr + c, 1), :] = dtot[c * C:c * C + 1, sl]
        return carry

    lax.fori_loop(0, L // R, prep, 0, unroll=min(GLA_PREP_UNROLL, L // R))

    def chunk(n, st, qt, ke, dec, out):
        rows = pl.ds(pl.multiple_of(n * C, C), C)
        s = st[...]
        out[rows, :] += lax.dot_general(qt[rows, :], s.astype(BF16), nt, preferred_element_type=F32)
        upd = lax.dot_general(v_ref[rows, :], ke[rows, :], tn, preferred_element_type=F32)
        st[...] = dec[pl.ds(n, 1), :] * s + upd

    def step(i, carry):
        chunk(i, stf, qtf, kef, decf, of_s)
        chunk(n_chunks - 1 - i, stb, qtb, keb, decb, ob_s)
        return carry

    stf[...] = jnp.zeros_like(stf)
    stb[...] = jnp.zeros_like(stb)
    lax.fori_loop(0, n_chunks, step, 0, unroll=min(GLA_SCAN_UNROLL, n_chunks))

    o = of_s[...] + ob_s[...]
    o = o * lax.rsqrt(jnp.mean(o * o, axis=-1, keepdims=True) + EPS) * nw_ref[...]
    r = r_ref[...].astype(F32)
    o_ref[...] = (o * (r * jax.nn.sigmoid(r))).astype(BF16)


def _gla(qkvr, alow, w2, gb, norm_w, B, L):
    T = B * L
    H = GLA_HEADS
    kq = GLA_KEY // GLA_DK
    kv = 2 * GLA_KEY // GLA_DV
    kr = (2 * GLA_KEY + GLA_VAL) // GLA_DV
    return pl.pallas_call(
        _gla_kernel,
        grid=(B, H),
        in_specs=[
            pl.BlockSpec((L, GLA_DK), lambda b, h: (b, h)),
            pl.BlockSpec((L, GLA_DK), lambda b, h: (b, kq + h)),
            pl.BlockSpec((L, GLA_DV), lambda b, h: (b, kv + h)),
            pl.BlockSpec((L, GLA_DV), lambda b, h: (b, kr + h)),
            pl.BlockSpec((L, ALOW_PAD), lambda b, h: (b, 0)),
            pl.BlockSpec((2, ALOW_PAD, GLA_DK), lambda b, h: (0, 0, h)),
            pl.BlockSpec((2, 1, GLA_DK), lambda b, h: (0, 0, h)),
            pl.BlockSpec((1, GLA_DV), lambda b, h: (0, 0)),
        ],
        out_specs=pl.BlockSpec((L, GLA_DV), lambda b, h: (b, h)),
        out_shape=jax.ShapeDtypeStruct((T, GLA_VAL), BF16),
        scratch_shapes=[pltpu.VMEM((L, GLA_DK), BF16)] * 4
        + [pltpu.VMEM((L // GLA_CHUNK, GLA_DK), F32)] * 2
        + [pltpu.VMEM((L, GLA_DV), F32)] * 2
        + [pltpu.VMEM((GLA_DV, GLA_DK), F32)] * 2,
        compiler_params=_cparams("parallel", "parallel"),
        name="gla",
    )(qkvr, qkvr, qkvr, qkvr, alow, w2, gb, norm_w.reshape(1, GLA_DV))


HY_WT = 256


def _hy_filter_kernel(z_ref, t_ref, fr_ref, w1_ref, b1_ref, w2_ref, b2_ref, w3f_ref, w3b_ref, dec_ref, o_ref, h_s):
    @pl.when(pl.program_id(0) == 0)
    def _():
        fr = fr_ref[...]
        h1 = jnp.sin(fr * (jnp.dot(z_ref[...], w1_ref[...], preferred_element_type=F32, precision=HIGHEST) + b1_ref[...]))
        h_s[...] = jnp.sin(fr * (jnp.dot(h1, w2_ref[...], preferred_element_type=F32, precision=HIGHEST) + b2_ref[...]))

    h = h_s[...]
    t = t_ref[...]
    row = lax.broadcasted_iota(jnp.int32, (z_ref.shape[0], 1), 0)
    hq = h.astype(BF16)
    hf = jnp.dot(hq, w3f_ref[...].astype(BF16), preferred_element_type=F32)
    hf = hf * jnp.exp(-t * jnp.abs(dec_ref[0:1, :]))
    hb = jnp.dot(hq, w3b_ref[...].astype(BF16), preferred_element_type=F32)
    hb = hb * jnp.exp(-t * jnp.abs(dec_ref[1:2, :]))
    hb = jnp.where(row == 0, 0.0, hb)
    ss = jnp.sum(hf * hf, axis=0, keepdims=True) + jnp.sum(hb * hb, axis=0, keepdims=True)
    inv = lax.rsqrt(ss + EPS)
    o_ref[0] = hf * inv
    o_ref[1] = hb * inv


def _hy_filter(L, hy_freq, w1, b1, w2, b2, w3, hy_decay):
    W = HY_WIDTH
    t = np.linspace(0.0, 1.0, L, dtype=np.float32)[:, None]
    omega = (np.float32(2.0 * math.pi) * np.arange(L, dtype=np.float32) / np.float32(L)).astype(np.float32)
    f = np.linspace(1e-4, HY_BANDS - 1, HY_BANDS, dtype=np.float32)
    ang = (omega[:, None] * f[None, :]).astype(np.float32)
    z = np.concatenate([t, np.cos(ang), -np.sin(ang)], axis=-1).astype(np.float32)
    nw = W // HY_WT
    return pl.pallas_call(
        _hy_filter_kernel,
        grid=(nw,),
        in_specs=[
            _const_spec((L, HY_EMB)),
            _const_spec((L, 1)),
            _const_spec((1, HY_FFN)),
            _const_spec((HY_EMB, HY_FFN)),
            _const_spec((1, HY_FFN)),
            _const_spec((HY_FFN, HY_FFN)),
            _const_spec((1, HY_FFN)),
            pl.BlockSpec((HY_FFN, HY_WT), lambda j: (0, j)),
            pl.BlockSpec((HY_FFN, HY_WT), lambda j: (0, nw + j)),
            pl.BlockSpec((2, HY_WT), lambda j: (0, j)),
        ],
        out_specs=pl.BlockSpec((2, L, HY_WT), lambda j: (0, 0, j)),
        out_shape=jax.ShapeDtypeStruct((2, L, W), F32),
        scratch_shapes=[pltpu.VMEM((L, HY_FFN), F32)],
        compiler_params=_cparams("arbitrary"),
        name="hy_filter",
    )(jnp.asarray(z), jnp.asarray(t), hy_freq.reshape(1, HY_FFN), w1, b1.reshape(1, HY_FFN), w2,
      b2.reshape(1, HY_FFN), w3, w3, hy_decay)


def _hy_pre_kernel(x0_ref, x1_ref, v_ref, w0_ref, w1_ref, wv_ref, b0_ref, b1_ref, bv_ref, z_ref, x0o_ref):
    L = x0_ref.shape[0]
    row = lax.broadcasted_iota(jnp.int32, (L, 1), 0)

    def conv3(u_ref, w_ref, b_ref):
        u = u_ref[...].astype(F32)
        prev = jnp.where(row == 0, 0.0, pltpu.roll(u, 1, 0))
        nxt = jnp.where(row == L - 1, 0.0, pltpu.roll(u, L - 1, 0))
        return w_ref[0:1, :] * prev + w_ref[1:2, :] * u + w_ref[2:3, :] * nxt + b_ref[...]

    x0o_ref[...] = conv3(x0_ref, w0_ref, b0_ref).astype(BF16)
    z_ref[...] = conv3(v_ref, wv_ref, bv_ref) * conv3(x1_ref, w1_ref, b1_ref)


def _hy_pre(hy, conv_w, conv_b, B, L):
    W = HY_WIDTH
    nw = W // HY_WT
    cb = conv_b.reshape(1, 3 * W)
    blk = lambda off: pl.BlockSpec((L, HY_WT), lambda b, j: (b, off + j))
    wblk = lambda off: pl.BlockSpec((3, HY_WT), lambda b, j: (0, off + j))
    bblk = lambda off: pl.BlockSpec((1, HY_WT), lambda b, j: (0, off + j))
    return pl.pallas_call(
        _hy_pre_kernel,
        grid=(B, nw),
        in_specs=[blk(0), blk(nw), blk(2 * nw), wblk(0), wblk(nw), wblk(2 * nw), bblk(0), bblk(nw), bblk(2 * nw)],
        out_specs=[pl.BlockSpec((L, HY_WT), lambda b, j: (b, j))] * 2,
        out_shape=[jax.ShapeDtypeStruct((B * L, W), F32), jax.ShapeDtypeStruct((B * L, W), BF16)],
        compiler_params=_cparams("parallel", "parallel"),
        name="hy_pre",
    )(hy, hy, hy, conv_w, conv_w, conv_w, cb, cb, cb)


FFT_L = LANES
FFT_PITCH = FFT_L + 8
FFT_UNROLL = 8


def _fft_tables(L):
    N = 2 * L
    C = L // FFT_L
    n1 = N // FFT_L
    kh = min(n1, -(-(n1 // 2 + 1) // 8) * 8)
    l = np.arange(FFT_L)
    c = np.arange(C)
    k2 = np.arange(kh)
    pair_w = np.where((k2 == 0) | (k2 == n1 // 2), 1.0, np.where(k2 < n1 // 2, 2.0, 0.0))
    ang1 = 2 * np.pi * ((k2[:, None] * c[None, :]) % n1) / n1
    f1 = np.concatenate([np.cos(ang1), -np.sin(ang1)], axis=0)
    g1 = (np.stack([np.cos(ang1), -np.sin(ang1)], axis=-1) * pair_w[:, None, None]).transpose(1, 0, 2).reshape(C, 2 * kh) / N
    ang2 = 2 * np.pi * (((l[None, :, None] * l[None, None, :]) % FFT_L) / FFT_L
                        + ((k2[:, None, None] * l[None, None, :]) % N) / N)
    cs, sn = np.cos(ang2), np.sin(ang2)
    f2 = np.concatenate([np.concatenate([cs, sn], axis=2), np.concatenate([-sn, cs], axis=2)], axis=1)
    cst, snt = cs.transpose(0, 2, 1), sn.transpose(0, 2, 1)
    g2 = np.concatenate([np.concatenate([cst, -snt], axis=2), np.concatenate([snt, cst], axis=2)], axis=1)
    bf = lambda a: jnp.asarray(a.astype(np.float32)).astype(BF16)
    return dict(C=C, n1=kh, f1=bf(f1), g1=bf(g1), f2=bf(f2), g2=bf(g2))


def _fft_kernel(z_ref, f1_ref, f2_ref, g2_ref, kf_ref, g1_ref, o_ref, a_s, b_s, y_s, *, spectrum_only):
    C = f1_ref.shape[1]
    n1 = f1_ref.shape[0] // 2
    P = FFT_PITCH
    blk = lambda j: pl.ds(pl.multiple_of(j * P, 8), FFT_L)

    def stage1(j, carry):
        x = jnp.concatenate([z_ref[pl.ds(2 * j, C, stride=FFT_L), :], z_ref[pl.ds(2 * j + 1, C, stride=FFT_L), :]], axis=1)
        r = jnp.dot(f1_ref[...], x.astype(BF16), preferred_element_type=F32)
        a_s[pl.ds(pl.multiple_of(2 * j * P, 8), 2 * n1), :] = r[:, :LANES]
        a_s[pl.ds(pl.multiple_of((2 * j + 1) * P, 8), 2 * n1), :] = r[:, LANES:]
        return carry
    lax.fori_loop(0, FFT_L // 2, stage1, 0, unroll=FFT_UNROLL)

    def stage2(k2, carry):
        ar = a_s[pl.ds(k2, FFT_L, stride=P), :]
        ai = a_s[pl.ds(n1 + k2, FFT_L, stride=P), :]
        x = jnp.concatenate([ar, ai], axis=0).astype(BF16)
        s = jnp.dot(f2_ref[k2], x, preferred_element_type=F32)
        xr, xi = s[:FFT_L], s[FFT_L:]
        if spectrum_only:
            first = pl.program_id(1) == 0

            @pl.when(first)
            def _():
                o_ref[k2, 0] = xr
                o_ref[k2, 1] = xi

            @pl.when(jnp.logical_not(first))
            def _():
                o_ref[k2, 0] += xr
                o_ref[k2, 1] -= xi
        else:
            kr, ki = kf_ref[k2, 0], kf_ref[k2, 1]
            y = jnp.concatenate([xr * kr - xi * ki, xr * ki + xi * kr], axis=0).astype(BF16)
            s = jnp.dot(g2_ref[k2], y, preferred_element_type=F32)
            b_s[blk(2 * k2), :] = s[:FFT_L]
            b_s[blk(2 * k2 + 1), :] = s[FFT_L:]
        return carry
    lax.fori_loop(0, n1, stage2, 0, unroll=FFT_UNROLL)

    if not spectrum_only:
        def stage3(j, carry):
            x = jnp.concatenate([b_s[pl.ds(2 * j, 2 * n1, stride=P), :], b_s[pl.ds(2 * j + 1, 2 * n1, stride=P), :]], axis=1)
            r = jnp.dot(g1_ref[...], x.astype(BF16), preferred_element_type=F32)
            y_s[pl.ds(2 * j, C, stride=P), :] = r[:, :LANES]
            y_s[pl.ds(2 * j + 1, C, stride=P), :] = r[:, LANES:]
            return carry
        lax.fori_loop(0, FFT_L // 2, stage3, 0, unroll=FFT_UNROLL)
        for c in range(C):
            o_ref[c * FFT_L:(c + 1) * FFT_L, :] = y_s[c * P:c * P + FFT_L, :]


def _fft(tabs, z, kf, spectrum_only):
    B, L, W = z.shape
    C, n1 = tabs["C"], tabs["n1"]
    if spectrum_only:
        out_spec = pl.BlockSpec((n1, 2, FFT_L, LANES), lambda w, b: (0, 0, 0, w))
        out_shape = jax.ShapeDtypeStruct((n1, 2, FFT_L, W), F32)
    else:
        out_spec = pl.BlockSpec((None, L, LANES), lambda w, b: (b, 0, w))
        out_shape = jax.ShapeDtypeStruct((B, L, W), F32)
    return pl.pallas_call(
        functools.partial(_fft_kernel, spectrum_only=spectrum_only),
        grid=(W // LANES, B),
        in_specs=[pl.BlockSpec((None, L, LANES), lambda w, b: (b, 0, w)),
                  _const_spec((2 * n1, C)),
                  _const_spec((n1, 2 * FFT_L, 2 * FFT_L)), _const_spec((n1, 2 * FFT_L, 2 * FFT_L)),
                  pl.BlockSpec((n1, 2, FFT_L, LANES), lambda w, b: (0, 0, 0, w)),
                  _const_spec((C, 2 * n1))],
        out_specs=out_spec,
        out_shape=out_shape,
        scratch_shapes=[pltpu.VMEM((FFT_L * FFT_PITCH, LANES), F32), pltpu.VMEM((2 * n1 * FFT_PITCH, LANES), F32),
                        pltpu.VMEM((C * FFT_PITCH, LANES), F32)],
        compiler_params=_cparams("arbitrary", "arbitrary"),
        name="fft_spectrum" if spectrum_only else "fft_conv",
    )(z, tabs["f1"], tabs["f2"], tabs["g2"], kf, tabs["g1"])


def _fft_conv(z, taps):
    tabs = _fft_tables(z.shape[1])
    unused = jnp.zeros((tabs["n1"], 2, FFT_L, z.shape[2]), F32)
    kf = _fft(tabs, taps, unused, True)
    return _fft(tabs, z, kf, False)


def _merge_kernel(og_ref, y_ref, z_ref, x0_ref, hb_ref, gates_ref, x_ref, wbg_ref, wbh_ref, wo_ref, n2_ref, rw_ref, rb_ref,
                  h1_ref, hn3_ref, idx_ref, wts_ref):
    mg = jnp.dot(og_ref[...], wbg_ref[...], preferred_element_type=F32)
    o_hy = ((y_ref[...] + z_ref[...] * hb_ref[...]) * x0_ref[...].astype(F32)).astype(BF16)
    mh = jnp.dot(o_hy, wbh_ref[...], preferred_element_type=F32)
    gg = jax.nn.sigmoid(gates_ref[:, :D_MODEL].astype(F32))
    gh = jax.nn.sigmoid(gates_ref[:, D_MODEL:].astype(F32))
    merged = (gg * mg + gh * mh).astype(BF16)
    h1 = x_ref[...] + jnp.dot(merged, wo_ref[...], preferred_element_type=F32)
    h1_ref[...] = h1
    hn = h1 * lax.rsqrt(jnp.mean(h1 * h1, axis=-1, keepdims=True) + EPS) * n2_ref[...]
    _store_token_rows(hn3_ref, hn)
    hh = hn.astype(BF16)
    hl = (hn - hh.astype(F32)).astype(BF16)
    p1 = jnp.dot(hh, rw_ref[...], preferred_element_type=F32)
    p2 = jnp.dot(hl, rw_ref[:, :ROUTE_PAD], preferred_element_type=F32)
    logits = p1[:, :ROUTE_PAD] + p1[:, ROUTE_PAD:] + p2 + rb_ref[...]
    lane = lax.broadcasted_iota(jnp.int32, logits.shape, 1)
    logits = jnp.where(lane < N_EXPERTS, logits, -jnp.inf)
    idx_out = jnp.zeros(logits.shape, jnp.int32)
    val_out = jnp.zeros(logits.shape, F32)
    v0 = None
    for j in range(TOP_K):
        m = jnp.max(logits, axis=-1, keepdims=True)
        sel = jnp.min(jnp.where(logits == m, lane, ROUTE_PAD), axis=-1, keepdims=True)
        if j == 0:
            v0 = m
        idx_out = jnp.where(lane == j, sel, idx_out)
        val_out = jnp.where(lane == j, jnp.exp(m - v0), val_out)
        logits = jnp.where(lane == sel, -jnp.inf, logits)
    idx_ref[...] = idx_out
    wts_ref[...] = val_out / jnp.sum(val_out, axis=-1, keepdims=True)


def _merge(o_gla, y, z, x0c, hy_bias, gates, x2, wbg, wbh, wo, norm2_w, rw_pad, rb_pad):
    T = x2.shape[0]
    tm = min(512, T)
    row = lambda n: pl.BlockSpec((tm, n), lambda i: (i, 0))
    return pl.pallas_call(
        _merge_kernel,
        grid=(T // tm,),
        in_specs=[row(GLA_VAL), row(HY_WIDTH), row(HY_WIDTH), row(HY_WIDTH), _const_spec((1, HY_WIDTH)), row(N_GATES), row(D_MODEL),
                  _const_spec((GLA_VAL, D_MODEL)), _const_spec((HY_WIDTH, D_MODEL)), _const_spec((D_MODEL, D_MODEL)),
                  _const_spec((1, D_MODEL)), _const_spec((D_MODEL, 2 * ROUTE_PAD)), _const_spec((1, ROUTE_PAD))],
        out_specs=[row(D_MODEL), pl.BlockSpec((tm * TOK_ROWS, LANES), lambda i: (i, 0)), row(ROUTE_PAD), row(ROUTE_PAD)],
        out_shape=[jax.ShapeDtypeStruct((T, D_MODEL), F32), jax.ShapeDtypeStruct((T * TOK_ROWS, LANES), F32),
                   jax.ShapeDtypeStruct((T, ROUTE_PAD), jnp.int32), jax.ShapeDtypeStruct((T, ROUTE_PAD), F32)],
        compiler_params=_cparams("parallel"),
        name="merge_route",
    )(o_gla, y, z, x0c, hy_bias.reshape(1, HY_WIDTH), gates, x2, wbg, wbh, wo, norm2_w.reshape(1, D_MODEL), rw_pad, rb_pad)


MOE_TM = 512
MOE_TC = 256
DMA_UNROLL = 8


DISPATCH_SLOTS = 3


def _dispatch_kernel(last_ref, pos_ref, hn_ref, xs_ref, zero_s, buf, sem, lsems, ssems):
    i = pl.program_id(0)
    n = pl.num_programs(0)
    tc = pos_ref.shape[2] // TOP_K
    rows = tc * TOK_ROWS
    zrows = zero_s.shape[0]
    slot = lax.rem(i, DISPATCH_SLOTS)
    prev_slot = lax.rem(i + DISPATCH_SLOTS - 1, DISPATCH_SLOTS)

    def load(blk, s):
        src = hn_ref.at[pl.ds(pl.multiple_of(blk * rows, rows), rows)]
        return pltpu.make_async_copy(src, buf.at[s], lsems.at[s])

    def wait_copies(s):
        for _ in range(TOP_K):
            pltpu.make_async_copy(buf.at[s], xs_ref.at[pl.ds(0, rows)], ssems.at[s]).wait()

    @pl.when(i == 0)
    def _():
        load(0, 0).start()

        @pl.when(n > 1)
        def _():
            load(1, 1).start()

        zero_s[...] = jnp.zeros_like(zero_s)
        n_fill = last_ref.shape[0]

        def zfill(e, c):
            @pl.when(last_ref[e] >= 0)
            def _():
                dst = pl.multiple_of(last_ref[e] * zrows, zrows)
                pltpu.make_async_copy(zero_s, xs_ref.at[pl.ds(dst, zrows)], sem).start()
            return c
        lax.fori_loop(0, n_fill, zfill, 0)

        def zwait(e, c):
            @pl.when(last_ref[e] >= 0)
            def _():
                pltpu.make_async_copy(zero_s, xs_ref.at[pl.ds(0, zrows)], sem).wait()
            return c
        lax.fori_loop(0, n_fill, zwait, 0)

    load(i, slot).wait()

    def issue(r, c):
        src = buf.at[slot, pl.ds(pl.multiple_of(r * TOK_ROWS, TOK_ROWS), TOK_ROWS)]
        for j in range(TOP_K):
            dst = pl.multiple_of(pos_ref[0, 0, r * TOP_K + j] * TOK_ROWS, TOK_ROWS)
            pltpu.make_async_copy(src, xs_ref.at[pl.ds(dst, TOK_ROWS)], ssems.at[slot]).start(priority=j % 2)
        return c
    lax.fori_loop(0, tc, issue, 0, unroll=DMA_UNROLL // TOP_K)

    @pl.when(i > 0)
    def _():
        wait_copies(prev_slot)

    @pl.when(i + 2 < n)
    def _():
        load(i + 2, prev_slot).start()

    @pl.when(i == n - 1)
    def _():
        wait_copies(slot)


def _dispatch(hn3, pos, last_tile, P):
    T = hn3.shape[0] // TOK_ROWS
    tc = min(MOE_TC, T)
    nt = T // tc
    return pl.pallas_call(
        _dispatch_kernel,
        grid_spec=pltpu.PrefetchScalarGridSpec(
            num_scalar_prefetch=1,
            grid=(nt,),
            in_specs=[pl.BlockSpec((1, 1, tc * TOP_K), lambda i, lt: (i, 0, 0), memory_space=pltpu.SMEM),
                      pl.BlockSpec(memory_space=pl.ANY)],
            out_specs=pl.BlockSpec(memory_space=pl.ANY),
            scratch_shapes=[pltpu.VMEM((MOE_TM * TOK_ROWS, LANES), F32),
                            pltpu.VMEM((DISPATCH_SLOTS, tc * TOK_ROWS, LANES), F32),
                            pltpu.SemaphoreType.DMA(()),
                            pltpu.SemaphoreType.DMA((DISPATCH_SLOTS,)), pltpu.SemaphoreType.DMA((DISPATCH_SLOTS,))],
        ),
        out_shape=jax.ShapeDtypeStruct((P * TOK_ROWS, LANES), F32),
        compiler_params=_cparams("arbitrary"),
        name="moe_dispatch",
    )(last_tile, pos.reshape(nt, 1, tc * TOP_K), hn3)


def _expert_kernel(te_ref, nused_ref, xs_ref, wgu_ref, bgu_ref, wd_ref, bd_ref, o_ref, wgu_s, wd_s):
    i = pl.program_id(0)
    prev = te_ref[jnp.maximum(i - 1, 0)]

    @pl.when((i == 0) | (te_ref[i] != prev))
    def _():
        wgu_s[...] = wgu_ref[0].astype(BF16)
        wd_s[...] = wd_ref[0].astype(BF16)

    @pl.when(i < nused_ref[0])
    def _():
        x = _load_token_rows(xs_ref, 0, xs_ref.shape[0] // TOK_ROWS).astype(BF16)
        gu = jnp.dot(x, wgu_s[...], preferred_element_type=F32) + bgu_ref[0]
        gate = jnp.minimum(gu[:, :D_FF], SWIGLU_LIMIT)
        up = jnp.clip(gu[:, D_FF:], -SWIGLU_LIMIT, SWIGLU_LIMIT)
        act = gate * jax.nn.sigmoid(SWIGLU_ALPHA * gate) * (up + 1.0)
        y = jnp.dot(act.astype(BF16), wd_s[...], preferred_element_type=F32) + bd_ref[0]
        _store_token_rows(o_ref, y)

    @pl.when(i >= nused_ref[0])
    def _():
        o_ref[...] = jnp.zeros_like(o_ref)


def _experts(xs3, tile_expert, n_used, w_gate_up, b_gate_up, w_down, b_down):
    P = xs3.shape[0] // TOK_ROWS
    tm = MOE_TM
    nt = P // tm
    live = lambda i, te, nu: jnp.minimum(i, nu[0] - 1)
    return pl.pallas_call(
        _expert_kernel,
        grid_spec=pltpu.PrefetchScalarGridSpec(
            num_scalar_prefetch=2,
            grid=(nt,),
            in_specs=[
                pl.BlockSpec((tm * TOK_ROWS, LANES), lambda i, te, nu: (live(i, te, nu), 0)),
                pl.BlockSpec((1, D_MODEL, 2 * D_FF), lambda i, te, nu: (te[i], 0, 0)),
                pl.BlockSpec((1, 1, 2 * D_FF), lambda i, te, nu: (te[i], 0, 0)),
                pl.BlockSpec((1, D_FF, D_MODEL), lambda i, te, nu: (te[i], 0, 0)),
                pl.BlockSpec((1, 1, D_MODEL), lambda i, te, nu: (te[i], 0, 0)),
            ],
            out_specs=pl.BlockSpec((tm * TOK_ROWS, LANES), lambda i, te, nu: (i, 0)),
            scratch_shapes=[pltpu.VMEM((D_MODEL, 2 * D_FF), BF16), pltpu.VMEM((D_FF, D_MODEL), BF16)],
        ),
        out_shape=jax.ShapeDtypeStruct((P * TOK_ROWS, LANES), F32),
        compiler_params=_cparams("arbitrary"),
        name="moe_experts",
    )(tile_expert, n_used, xs3, w_gate_up, b_gate_up.reshape(N_EXPERTS, 1, 2 * D_FF), w_down,
      b_down.reshape(N_EXPERTS, 1, D_MODEL))


def _combine_kernel(pos_ref, posn_ref, wts_ref, h1_ref, nf_ref, ys_ref, o_ref, buf, sems):
    i = pl.program_id(0)
    n = pl.num_programs(0)
    tc = o_ref.shape[0]
    slot = lax.rem(i, 2)

    def gather(p_ref, s):
        def issue(r, c):
            for j in range(TOP_K):
                src = pl.multiple_of(p_ref[0, 0, r * TOP_K + j] * TOK_ROWS, TOK_ROWS)
                dst = pl.multiple_of((j * tc + r) * TOK_ROWS, TOK_ROWS)
                pltpu.make_async_copy(ys_ref.at[pl.ds(src, TOK_ROWS)], buf.at[s, pl.ds(dst, TOK_ROWS)],
                                      sems.at[s]).start(priority=j % 2)
            return c
        lax.fori_loop(0, tc, issue, 0, unroll=DMA_UNROLL // TOP_K)

    @pl.when(i == 0)
    def _():
        gather(pos_ref, 0)

    @pl.when(i + 1 < n)
    def _():
        gather(posn_ref, 1 - slot)

    pltpu.make_async_copy(ys_ref.at[pl.ds(0, TOP_K * tc * TOK_ROWS)], buf.at[slot], sems.at[slot]).wait()

    w = wts_ref[...]
    h = h1_ref[...]
    for j in range(TOP_K):
        h = h + w[:, j:j + 1] * _load_token_rows(buf.at[slot], j * tc, tc)
    o_ref[...] = h * lax.rsqrt(jnp.mean(h * h, axis=-1, keepdims=True) + EPS) * nf_ref[...]


def _combine(pos, wts, h1, norm_f_w, ys3):
    T = h1.shape[0]
    tc = min(MOE_TC, T)
    nt = T // tc
    pos3 = pos.reshape(nt, 1, tc * TOP_K)
    return pl.pallas_call(
        _combine_kernel,
        grid=(nt,),
        in_specs=[
            pl.BlockSpec((1, 1, tc * TOP_K), lambda i: (i, 0, 0), memory_space=pltpu.SMEM),
            pl.BlockSpec((1, 1, tc * TOP_K), lambda i: (jnp.minimum(i + 1, nt - 1), 0, 0), memory_space=pltpu.SMEM),
            pl.BlockSpec((tc, ROUTE_PAD), lambda i: (i, 0)),
            pl.BlockSpec((tc, D_MODEL), lambda i: (i, 0)),
            _const_spec((1, D_MODEL)),
            pl.BlockSpec(memory_space=pl.ANY),
        ],
        out_specs=pl.BlockSpec((tc, D_MODEL), lambda i: (i, 0)),
        out_shape=jax.ShapeDtypeStruct((T, D_MODEL), F32),
        scratch_shapes=[pltpu.VMEM((2, TOP_K * tc * TOK_ROWS, LANES), F32), pltpu.SemaphoreType.DMA((2,))],
        compiler_params=_cparams("arbitrary"),
        name="moe_combine",
    )(pos3, pos3, wts, h1, norm_f_w.reshape(1, D_MODEL), ys3)


def _route_plan(top_idx, T):
    tm = MOE_TM
    nt = (T * TOP_K) // tm + N_EXPERTS
    e = top_idx.reshape(-1)
    oh = (e[:, None] == jnp.arange(N_EXPERTS, dtype=jnp.int32)[None, :]).astype(jnp.int32)
    csum = jnp.cumsum(oh, axis=0)
    rank = jnp.sum(csum * oh, axis=1) - 1
    counts = csum[-1]
    tiles = (counts + tm - 1) // tm
    tile_end = jnp.cumsum(tiles)
    offs = (tile_end - tiles) * tm
    pos = jnp.sum(oh * offs[None, :], axis=1) + rank
    tile_id = jnp.arange(nt, dtype=jnp.int32)
    tile_expert = jnp.minimum(jnp.sum((tile_end[None, :] <= tile_id[:, None]).astype(jnp.int32), axis=1),
                              N_EXPERTS - 1).astype(jnp.int32)
    tail = tile_end[-1] + jnp.arange(N_EXPERTS, dtype=jnp.int32)
    last_tile = jnp.concatenate([jnp.where(tiles > 0, tile_end - 1, -1), jnp.where(tail < nt, tail, -1)]).astype(jnp.int32)
    n_used = tile_end[-1:].astype(jnp.int32)
    return pos.astype(jnp.int32), tile_expert, last_tile, n_used, nt * tm


def kernel(x, norm1_w, w_in, gla_gate_w2, gla_gate_b, gla_norm_w, hy_conv_w, hy_conv_b, hy_freq, hy_ffn_w1, hy_ffn_b1,
           hy_ffn_w2, hy_ffn_b2, hy_ffn_w3, hy_decay, hy_bias, w_branch_gla, w_branch_hy, w_out, norm2_w, router_w,
           router_b, w_gate_up, b_gate_up, w_down, b_down, norm_f_w):
    B, L, D = x.shape
    T = B * L
    x2 = x.reshape(T, D)

    s_a = N_QKVR
    s_h = s_a + 2 * GLA_RANK
    w_cat = jnp.concatenate(
        [w_in[:, :s_a], w_in[:, s_h:], w_in[:, s_a:s_h], jnp.zeros((D, ALOW_PAD - 2 * GLA_RANK), w_in.dtype)],
        axis=1).astype(BF16)
    qkvr, hy, gates, alow = _in_proj(x2, norm1_w, w_cat)

    w2 = jnp.zeros((2, ALOW_PAD, GLA_KEY), F32)
    w2 = w2.at[0, :GLA_RANK].set(gla_gate_w2[0]).at[1, GLA_RANK:2 * GLA_RANK].set(gla_gate_w2[1])
    o_gla = _gla(qkvr, alow, w2, gla_gate_b.reshape(2, 1, GLA_KEY), gla_norm_w, B, L)

    taps = _hy_filter(L, hy_freq, hy_ffn_w1, hy_ffn_b1, hy_ffn_w2, hy_ffn_b2, hy_ffn_w3, hy_decay)
    z, x0c = _hy_pre(hy, hy_conv_w, hy_conv_b, B, L)
    y = _fft_conv(z.reshape(B, L, HY_WIDTH), taps).reshape(T, HY_WIDTH)

    rw_f = jnp.zeros((D, ROUTE_PAD), F32).at[:, :N_EXPERTS].set(router_w)
    rw_hi = rw_f.astype(BF16)
    rw_pad = jnp.concatenate([rw_hi, (rw_f - rw_hi.astype(F32)).astype(BF16)], axis=1)
    rb_pad = jnp.zeros((1, ROUTE_PAD), F32).at[0, :N_EXPERTS].set(router_b)
    h1, hn3, idx_pad, wts_pad = _merge(o_gla, y, z, x0c, hy_bias, gates, x2, w_branch_gla.astype(BF16), w_branch_hy.astype(BF16),
                                       w_out.astype(BF16), norm2_w, rw_pad, rb_pad)

    pos, tile_expert, last_tile, n_used, P = _route_plan(idx_pad[:, :TOP_K], T)
    xs3 = _dispatch(hn3, pos, last_tile, P)
    ys3 = _experts(xs3, tile_expert, n_used, w_gate_up, b_gate_up, w_down, b_down)
    out = _combine(pos, wts_pad, h1, norm_f_w, ys3)
    return out.reshape(B, L, D)
```

```python
import functools
import math

import numpy as np
import jax
import jax.numpy as jnp
from jax import lax
from jax.experimental import pallas as pl
from jax.experimental.pallas import tpu as pltpu

D_MODEL = 1024
EPS = 1e-6
GLA_HEADS = 4
GLA_DK = 128
GLA_DV = 256
GLA_KEY = GLA_HEADS * GLA_DK
GLA_VAL = GLA_HEADS * GLA_DV
GLA_RANK = 16
GLA_GATE_TEMP = 16.0
GLA_CHUNK = 64
HY_WIDTH = D_MODEL
HY_BANDS = 16
HY_EMB = 1 + 2 * HY_BANDS
HY_FFN = 64
N_EXPERTS = 32
TOP_K = 4
D_FF = D_MODEL
SWIGLU_LIMIT = 7.0
SWIGLU_ALPHA = 1.702

LANES = 128
ALOW_PAD = LANES
ROUTE_PAD = LANES
VMEM_LIMIT = 56 * 1024 * 1024

F32 = jnp.float32
BF16 = jnp.bfloat16
HIGHEST = lax.Precision.HIGHEST


def _cparams(*sem):
    return pltpu.CompilerParams(dimension_semantics=sem, vmem_limit_bytes=VMEM_LIMIT)


def _const_spec(shape):
    nd = len(shape)
    return pl.BlockSpec(shape, lambda *_: (0,) * nd, pipeline_mode=pl.Buffered(1))


TOK_ROWS = D_MODEL // LANES


def _store_token_rows(ref, val):
    n = val.shape[0]
    for s_ in range(TOK_ROWS):
        ref[pl.ds(s_, n, stride=TOK_ROWS), :] = val[:, s_ * LANES:(s_ + 1) * LANES]


def _load_token_rows(ref, first, n):
    return jnp.concatenate(
        [ref[pl.ds(first * TOK_ROWS + s_, n, stride=TOK_ROWS), :] for s_ in range(TOK_ROWS)], axis=1)


N_QKVR = 2 * GLA_KEY + 2 * GLA_VAL
N_HY = 3 * HY_WIDTH
N_GATES = 2 * D_MODEL
IN_COLS = N_QKVR + N_HY + N_GATES + ALOW_PAD


def _inproj_kernel(x_ref, nw_ref, w_ref, qkvr_ref, hy_ref, gates_ref, alow_ref):
    x = x_ref[...]
    ms = jnp.mean(x * x, axis=-1, keepdims=True)
    xn = (x * lax.rsqrt(ms + EPS) * nw_ref[...]).astype(BF16)
    c0, c1, c2 = N_QKVR, N_QKVR + N_HY, N_QKVR + N_HY + N_GATES
    qkvr_ref[...] = jnp.dot(xn, w_ref[:, :c0], preferred_element_type=F32).astype(BF16)
    hy_ref[...] = jnp.dot(xn, w_ref[:, c0:c1], preferred_element_type=F32).astype(BF16)
    gates_ref[...] = jnp.dot(xn, w_ref[:, c1:c2], preferred_element_type=F32).astype(BF16)
    alow_ref[...] = jnp.dot(xn, w_ref[:, c2:], preferred_element_type=F32)


def _in_proj(x2, norm1_w, w_cat):
    T = x2.shape[0]
    tm = min(512, T)
    return pl.pallas_call(
        _inproj_kernel,
        grid=(T // tm,),
        in_specs=[
            pl.BlockSpec((tm, D_MODEL), lambda i: (i, 0)),
            _const_spec((1, D_MODEL)),
            _const_spec((D_MODEL, IN_COLS)),
        ],
        out_specs=[
            pl.BlockSpec((tm, N_QKVR), lambda i: (i, 0)),
            pl.BlockSpec((tm, N_HY), lambda i: (i, 0)),
            pl.BlockSpec((tm, N_GATES), lambda i: (i, 0)),
            pl.BlockSpec((tm, ALOW_PAD), lambda i: (i, 0)),
        ],
        out_shape=[
            jax.ShapeDtypeStruct((T, N_QKVR), BF16),
            jax.ShapeDtypeStruct((T, N_HY), BF16),
            jax.ShapeDtypeStruct((T, N_GATES), BF16),
            jax.ShapeDtypeStruct((T, ALOW_PAD), F32),
        ],
        compiler_params=_cparams("parallel"),
        name="in_proj",
    )(x2, norm1_w.reshape(1, D_MODEL), w_cat)


GLA_PREP_ROWS = 256
GLA_PREP_UNROLL = 4
GLA_SCAN_UNROLL = 16


def _log_sigmoid(x):
    return jnp.minimum(x, 0.0) - jnp.log1p(jnp.exp(-jnp.abs(x)))


def _gla_kernel(q_ref, k_ref, v_ref, r_ref, alow_ref, w2_ref, gb_ref, nw_ref, o_ref,
                qtf, kef, qtb, keb, decf, decb, of_s, ob_s, stf, stb):
    L = q_ref.shape[0]
    C = GLA_CHUNK
    DK = GLA_DK
    n_chunks = L // C
    R = min(GLA_PREP_ROWS, L)
    cpr = R // C
    scale = DK ** -0.5
    nt = (((1,), (1,)), ((), ()))
    tn = (((0,), (0,)), ((), ()))

    ri = lax.broadcasted_iota(jnp.int32, (R, R), 0)
    ci = lax.broadcasted_iota(jnp.int32, (R, R), 1)
    same = (ri // C) == (ci // C)
    mask_f = same & (ri >= ci)
    mask_b = same & (ri < ci)
    tri = jnp.where(mask_f, 1.0, 0.0).astype(BF16)
    w2 = jnp.concatenate([w2_ref[0], w2_ref[1]], axis=1).astype(BF16)
    gb = jnp.concatenate([gb_ref[0], gb_ref[1]], axis=1)

    def prep(g, carry):
        rows = pl.ds(pl.multiple_of(g * R, R), R)
        logits = jnp.dot(alow_ref[rows, :].astype(BF16), w2, preferred_element_type=F32) + gb
        lg = _log_sigmoid(logits) * (1.0 / GLA_GATE_TEMP)
        hi = lg.astype(BF16)
        lo = (lg - hi.astype(F32)).astype(BF16)
        b = jnp.dot(tri, hi, preferred_element_type=F32) + jnp.dot(tri, lo, preferred_element_type=F32)
        tot = jnp.concatenate(
            [jnp.broadcast_to(b[c * C + C - 1:c * C + C, :], (C, 2 * DK)) for c in range(cpr)], axis=0)
        q = q_ref[rows, :].astype(F32) * scale
        k = k_ref[rows, :].astype(F32)
        vb = v_ref[rows, :]
        dtot = jnp.exp(tot)
        for d, (qt, ke, dec, mask, out) in enumerate(((qtf, kef, decf, mask_f, of_s), (qtb, keb, decb, mask_b, ob_s))):
            sl = slice(d * DK, (d + 1) * DK)
            bd, td, ld = b[:, sl], tot[:, sl], lg[:, sl]
            if d == 0:
                e_q, e_end = bd, td - bd
            else:
                e_q, e_end = td - bd + ld, bd - ld
            qtv = (q * jnp.exp(e_q)).astype(BF16)
            ktv = (k * jnp.exp(-e_q)).astype(BF16)
            qt[rows, :] = qtv
            ke[rows, :] = (k * jnp.exp(e_end)).astype(BF16)
            s = lax.dot_general(qtv, ktv, nt, preferred_element_type=F32)
            s = jnp.where(mask, s, 0.0).astype(BF16)
            out[rows, :] = jnp.dot(s, vb, preferred_element_type=F32)
            for c in range(cpr):
                dec[pl.ds(g * cpr + c, 1), :] = dtot[c * C:c * C + 1, sl]
        return carry

    lax.fori_loop(0, L // R, prep, 0, unroll=min(GLA_PREP_UNROLL, L // R))

    def chunk(n, st, qt, ke, dec, out):
        rows = pl.ds(pl.multiple_of(n * C, C), C)
        s = st[...]
        out[rows, :] += lax.dot_general(qt[rows, :], s.astype(BF16), nt, preferred_element_type=F32)
        upd = lax.dot_general(v_ref[rows, :], ke[rows, :], tn, preferred_element_type=F32)
        st[...] = dec[pl.ds(n, 1), :] * s + upd

    def step(i, carry):
        chunk(i, stf, qtf, kef, decf, of_s)
        chunk(n_chunks - 1 - i, stb, qtb, keb, decb, ob_s)
        return carry

    stf[...] = jnp.zeros_like(stf)
    stb[...] = jnp.zeros_like(stb)
    lax.fori_loop(0, n_chunks, step, 0, unroll=min(GLA_SCAN_UNROLL, n_chunks))

    o = of_s[...] + ob_s[...]
    o = o * lax.rsqrt(jnp.mean(o * o, axis=-1, keepdims=True) + EPS) * nw_ref[...]
    r = r_ref[...].astype(F32)
    o_ref[...] = (o * (r * jax.nn.sigmoid(r))).astype(BF16)


def _gla(qkvr, alow, w2, gb, norm_w, B, L):
    T = B * L
    H = GLA_HEADS
    kq = GLA_KEY // GLA_DK
    kv = 2 * GLA_KEY // GLA_DV
    kr = (2 * GLA_KEY + GLA_VAL) // GLA_DV
    return pl.pallas_call(
        _gla_kernel,
        grid=(B, H),
        in_specs=[
            pl.BlockSpec((L, GLA_DK), lambda b, h: (b, h)),
            pl.BlockSpec((L, GLA_DK), lambda b, h: (b, kq + h)),
            pl.BlockSpec((L, GLA_DV), lambda b, h: (b, kv + h)),
            pl.BlockSpec((L, GLA_DV), lambda b, h: (b, kr + h)),
            pl.BlockSpec((L, ALOW_PAD), lambda b, h: (b, 0)),
            pl.BlockSpec((2, ALOW_PAD, GLA_DK), lambda b, h: (0, 0, h)),
            pl.BlockSpec((2, 1, GLA_DK), lambda b, h: (0, 0, h)),
            pl.BlockSpec((1, GLA_DV), lambda b, h: (0, 0)),
        ],
        out_specs=pl.BlockSpec((L, GLA_DV), lambda b, h: (b, h)),
        out_shape=jax.ShapeDtypeStruct((T, GLA_VAL), BF16),
        scratch_shapes=[pltpu.VMEM((L, GLA_DK), BF16)] * 4
        + [pltpu.VMEM((L // GLA_CHUNK, GLA_DK), F32)] * 2
        + [pltpu.VMEM((L, GLA_DV), F32)] * 2
        + [pltpu.VMEM((GLA_DV, GLA_DK), F32)] * 2,
        compiler_params=_cparams("parallel", "parallel"),
        name="gla",
    )(qkvr, qkvr, qkvr, qkvr, alow, w2, gb, norm_w.reshape(1, GLA_DV))


HY_WT = 256


def _hy_filter_kernel(z_ref, t_ref, fr_ref, w1_ref, b1_ref, w2_ref, b2_ref, w3f_ref, w3b_ref, dec_ref, o_ref, h_s):
    @pl.when(pl.program_id(0) == 0)
    def _():
        fr = fr_ref[...]
        h1 = jnp.sin(fr * (jnp.dot(z_ref[...], w1_ref[...], preferred_element_type=F32, precision=HIGHEST) + b1_ref[...]))
        h_s[...] = jnp.sin(fr * (jnp.dot(h1, w2_ref[...], preferred_element_type=F32, precision=HIGHEST) + b2_ref[...]))

    h = h_s[...]
    t = t_ref[...]
    row = lax.broadcasted_iota(jnp.int32, (z_ref.shape[0], 1), 0)
    hq = h.astype(BF16)
    hf = jnp.dot(hq, w3f_ref[...].astype(BF16), preferred_element_type=F32)
    hf = hf * jnp.exp(-t * jnp.abs(dec_ref[0:1, :]))
    hb = jnp.dot(hq, w3b_ref[...].astype(BF16), preferred_element_type=F32)
    hb = hb * jnp.exp(-t * jnp.abs(dec_ref[1:2, :]))
    hb = jnp.where(row == 0, 0.0, hb)
    ss = jnp.sum(hf * hf, axis=0, keepdims=True) + jnp.sum(hb * hb, axis=0, keepdims=True)
    inv = lax.rsqrt(ss + EPS)
    o_ref[0] = hf * inv
    o_ref[1] = hb * inv


def _hy_filter(L, hy_freq, w1, b1, w2, b2, w3, hy_decay):
    W = HY_WIDTH
    t = np.linspace(0.0, 1.0, L, dtype=np.float32)[:, None]
    omega = (np.float32(2.0 * math.pi) * np.arange(L, dtype=np.float32) / np.float32(L)).astype(np.float32)
    f = np.linspace(1e-4, HY_BANDS - 1, HY_BANDS, dtype=np.float32)
    ang = (omega[:, None] * f[None, :]).astype(np.float32)
    z = np.concatenate([t, np.cos(ang), -np.sin(ang)], axis=-1).astype(np.float32)
    nw = W // HY_WT
    return pl.pallas_call(
        _hy_filter_kernel,
        grid=(nw,),
        in_specs=[
            _const_spec((L, HY_EMB)),
            _const_spec((L, 1)),
            _const_spec((1, HY_FFN)),
            _const_spec((HY_EMB, HY_FFN)),
            _const_spec((1, HY_FFN)),
            _const_spec((HY_FFN, HY_FFN)),
            _const_spec((1, HY_FFN)),
            pl.BlockSpec((HY_FFN, HY_WT), lambda j: (0, j)),
            pl.BlockSpec((HY_FFN, HY_WT), lambda j: (0, nw + j)),
            pl.BlockSpec((2, HY_WT), lambda j: (0, j)),
        ],
        out_specs=pl.BlockSpec((2, L, HY_WT), lambda j: (0, 0, j)),
        out_shape=jax.ShapeDtypeStruct((2, L, W), F32),
        scratch_shapes=[pltpu.VMEM((L, HY_FFN), F32)],
        compiler_params=_cparams("arbitrary"),
        name="hy_filter",
    )(jnp.asarray(z), jnp.asarray(t), hy_freq.reshape(1, HY_FFN), w1, b1.reshape(1, HY_FFN), w2,
      b2.reshape(1, HY_FFN), w3, w3, hy_decay)


FFT_L = LANES
FFT_PITCH = FFT_L + 8
FFT_UNROLL = 8


def _fft_tables(L):
    N = 2 * L
    C = L // FFT_L
    n1 = N // FFT_L
    kh = min(n1, -(-(n1 // 2 + 1) // 8) * 8)
    l = np.arange(FFT_L)
    c = np.arange(C)
    k2 = np.arange(kh)
    pair_w = np.where((k2 == 0) | (k2 == n1 // 2), 1.0, np.where(k2 < n1 // 2, 2.0, 0.0))
    ang1 = 2 * np.pi * ((k2[:, None] * c[None, :]) % n1) / n1
    f1 = np.concatenate([np.cos(ang1), -np.sin(ang1)], axis=0)
    g1 = (np.stack([np.cos(ang1), -np.sin(ang1)], axis=-1) * pair_w[:, None, None]).transpose(1, 0, 2).reshape(C, 2 * kh) / N
    ang2 = 2 * np.pi * (((l[None, :, None] * l[None, None, :]) % FFT_L) / FFT_L
                        + ((k2[:, None, None] * l[None, None, :]) % N) / N)
    cs, sn = np.cos(ang2), np.sin(ang2)
    f2 = np.concatenate([np.concatenate([cs, sn], axis=2), np.concatenate([-sn, cs], axis=2)], axis=1)
    cst, snt = cs.transpose(0, 2, 1), sn.transpose(0, 2, 1)
    g2 = np.concatenate([np.concatenate([cst, -snt], axis=2), np.concatenate([snt, cst], axis=2)], axis=1)
    bf = lambda a: jnp.asarray(a.astype(np.float32)).astype(BF16)
    return dict(C=C, n1=kh, f1=bf(f1), g1=bf(g1), f2=bf(f2), g2=bf(g2))


def _fft_kernel(z_ref, f1_ref, f2_ref, g2_ref, kf_ref, g1_ref, o_ref, a_s, b_s, y_s, *, spectrum_only, epilogue=None):
    C = f1_ref.shape[1]
    n1 = f1_ref.shape[0] // 2
    P = FFT_PITCH
    blk = lambda j: pl.ds(pl.multiple_of(j * P, 8), FFT_L)

    def stage1(j, carry):
        x = jnp.concatenate([z_ref[pl.ds(2 * j, C, stride=FFT_L), :], z_ref[pl.ds(2 * j + 1, C, stride=FFT_L), :]], axis=1)
        r = jnp.dot(f1_ref[...], x.astype(BF16), preferred_element_type=F32)
        a_s[pl.ds(pl.multiple_of(2 * j * P, 8), 2 * n1), :] = r[:, :LANES]
        a_s[pl.ds(pl.multiple_of((2 * j + 1) * P, 8), 2 * n1), :] = r[:, LANES:]
        return carry
    lax.fori_loop(0, FFT_L // 2, stage1, 0, unroll=FFT_UNROLL)

    def stage2(k2, carry, *, accumulate=False):
        ar = a_s[pl.ds(k2, FFT_L, stride=P), :]
        ai = a_s[pl.ds(n1 + k2, FFT_L, stride=P), :]
        x = jnp.concatenate([ar, ai], axis=0).astype(BF16)
        s = jnp.dot(f2_ref[k2], x, preferred_element_type=F32)
        xr, xi = s[:FFT_L], s[FFT_L:]
        if not spectrum_only:
            kr, ki = kf_ref[k2, 0], kf_ref[k2, 1]
            y = jnp.concatenate([xr * kr - xi * ki, xr * ki + xi * kr], axis=0).astype(BF16)
            s = jnp.dot(g2_ref[k2], y, preferred_element_type=F32)
            b_s[blk(2 * k2), :] = s[:FFT_L]
            b_s[blk(2 * k2 + 1), :] = s[FFT_L:]
        elif accumulate:
            o_ref[k2, 0] += xr
            o_ref[k2, 1] -= xi
        else:
            o_ref[k2, 0] = xr
            o_ref[k2, 1] = xi
        return carry

    if spectrum_only:
        @pl.when(pl.program_id(1) == 0)
        def _():
            lax.fori_loop(0, n1, stage2, 0, unroll=FFT_UNROLL)

        @pl.when(pl.program_id(1) != 0)
        def _():
            lax.fori_loop(0, n1, functools.partial(stage2, accumulate=True), 0, unroll=FFT_UNROLL)
    else:
        lax.fori_loop(0, n1, stage2, 0, unroll=FFT_UNROLL)

    if not spectrum_only:
        def stage3(j, carry):
            x = jnp.concatenate([b_s[pl.ds(2 * j, 2 * n1, stride=P), :], b_s[pl.ds(2 * j + 1, 2 * n1, stride=P), :]], axis=1)
            r = jnp.dot(g1_ref[...], x.astype(BF16), preferred_element_type=F32)
            y_s[pl.ds(2 * j, C, stride=P), :] = r[:, :LANES]
            y_s[pl.ds(2 * j + 1, C, stride=P), :] = r[:, LANES:]
            return carry
        lax.fori_loop(0, FFT_L // 2, stage3, 0, unroll=FFT_UNROLL)
        for c in range(C):
            rows = slice(c * FFT_L, (c + 1) * FFT_L)
            o_ref[rows, :] = epilogue(rows, y_s[c * P:c * P + FFT_L, :])


def _hyena_kernel(x0_ref, x1_ref, v_ref, w0_ref, w1_ref, wv_ref, b0_ref, b1_ref, bv_ref, hb_ref,
                  f1_ref, f2_ref, g2_ref, kf_ref, g1_ref, o_ref, z_s, x0_s, a_s, b_s, y_s):
    L = x0_ref.shape[0]
    row = lax.broadcasted_iota(jnp.int32, (L, 1), 0)

    def conv3(u_ref, w_ref, b_ref):
        u = u_ref[...].astype(F32)
        prev = jnp.where(row == 0, 0.0, pltpu.roll(u, 1, 0))
        nxt = jnp.where(row == L - 1, 0.0, pltpu.roll(u, L - 1, 0))
        return w_ref[0:1, :] * prev + w_ref[1:2, :] * u + w_ref[2:3, :] * nxt + b_ref[...]

    z_s[...] = conv3(v_ref, wv_ref, bv_ref) * conv3(x1_ref, w1_ref, b1_ref)
    x0_s[...] = conv3(x0_ref, w0_ref, b0_ref)

    def epilogue(rows, y):
        return ((y + z_s[rows, :] * hb_ref[...]) * x0_s[rows, :]).astype(BF16)

    _fft_kernel(z_s, f1_ref, f2_ref, g2_ref, kf_ref, g1_ref, o_ref, a_s, b_s, y_s, spectrum_only=False, epilogue=epilogue)


def _fft_specs(tabs):
    C, n1 = tabs["C"], tabs["n1"]
    specs = [_const_spec((2 * n1, C)),
             _const_spec((n1, 2 * FFT_L, 2 * FFT_L)), _const_spec((n1, 2 * FFT_L, 2 * FFT_L)),
             pl.BlockSpec((n1, 2, FFT_L, LANES), lambda w, b: (0, 0, 0, w)),
             _const_spec((C, 2 * n1))]
    scratch = [pltpu.VMEM((FFT_L * FFT_PITCH, LANES), F32), pltpu.VMEM((2 * n1 * FFT_PITCH, LANES), F32),
               pltpu.VMEM((C * FFT_PITCH, LANES), F32)]
    return specs, scratch


def _filter_spectrum(tabs, taps):
    _, L, W = taps.shape
    n1 = tabs["n1"]
    specs, scratch = _fft_specs(tabs)
    unused = jnp.zeros((n1, 2, FFT_L, W), F32)
    return pl.pallas_call(
        functools.partial(_fft_kernel, spectrum_only=True),
        grid=(W // LANES, 2),
        in_specs=[pl.BlockSpec((None, L, LANES), lambda w, b: (b, 0, w))] + specs,
        out_specs=pl.BlockSpec((n1, 2, FFT_L, LANES), lambda w, b: (0, 0, 0, w)),
        out_shape=jax.ShapeDtypeStruct((n1, 2, FFT_L, W), F32),
        scratch_shapes=scratch,
        compiler_params=_cparams("arbitrary", "arbitrary"),
        name="fft_spectrum",
    )(taps, tabs["f1"], tabs["f2"], tabs["g2"], unused, tabs["g1"])


def _hyena(hy, conv_w, conv_b, hy_bias, taps, B, L):
    W = HY_WIDTH
    nw = W // LANES
    tabs = _fft_tables(L)
    kf = _filter_spectrum(tabs, taps)
    specs, scratch = _fft_specs(tabs)
    cb = conv_b.reshape(1, 3 * W)
    blk = lambda off: pl.BlockSpec((L, LANES), lambda w, b: (b, off + w))
    wblk = lambda off: pl.BlockSpec((3, LANES), lambda w, b: (0, off + w))
    bblk = lambda off: pl.BlockSpec((1, LANES), lambda w, b: (0, off + w))
    return pl.pallas_call(
        _hyena_kernel,
        grid=(nw, B),
        in_specs=[blk(0), blk(nw), blk(2 * nw), wblk(0), wblk(nw), wblk(2 * nw), bblk(0), bblk(nw), bblk(2 * nw),
                  bblk(0)] + specs,
        out_specs=pl.BlockSpec((L, LANES), lambda w, b: (b, w)),
        out_shape=jax.ShapeDtypeStruct((B * L, W), BF16),
        scratch_shapes=[pltpu.VMEM((L, LANES), F32), pltpu.VMEM((L, LANES), F32)] + scratch,
        compiler_params=_cparams("arbitrary", "arbitrary"),
        name="hyena",
    )(hy, hy, hy, conv_w, conv_w, conv_w, cb, cb, cb, hy_bias.reshape(1, W), tabs["f1"], tabs["f2"], tabs["g2"], kf,
      tabs["g1"])


def _merge_kernel(og_ref, oh_ref, gates_ref, x_ref, wbg_ref, wbh_ref, wo_ref, n2_ref, rw_ref, rb_ref,
                  h1_ref, hn3_ref, idx_ref, wts_ref):
    mg = jnp.dot(og_ref[...], wbg_ref[...], preferred_element_type=F32)
    mh = jnp.dot(oh_ref[...], wbh_ref[...], preferred_element_type=F32)
    gg = jax.nn.sigmoid(gates_ref[:, :D_MODEL].astype(F32))
    gh = jax.nn.sigmoid(gates_ref[:, D_MODEL:].astype(F32))
    merged = (gg * mg + gh * mh).astype(BF16)
    h1 = x_ref[...] + jnp.dot(merged, wo_ref[...], preferred_element_type=F32)
    h1_ref[...] = h1
    hn = h1 * lax.rsqrt(jnp.mean(h1 * h1, axis=-1, keepdims=True) + EPS) * n2_ref[...]
    _store_token_rows(hn3_ref, hn)
    hh = hn.astype(BF16)
    hl = (hn - hh.astype(F32)).astype(BF16)
    p1 = jnp.dot(hh, rw_ref[...], preferred_element_type=F32)
    p2 = jnp.dot(hl, rw_ref[:, :ROUTE_PAD], preferred_element_type=F32)
    logits = p1[:, :ROUTE_PAD] + p1[:, ROUTE_PAD:] + p2 + rb_ref[...]
    lane = lax.broadcasted_iota(jnp.int32, logits.shape, 1)
    logits = jnp.where(lane < N_EXPERTS, logits, -jnp.inf)
    idx_out = jnp.zeros(logits.shape, jnp.int32)
    val_out = jnp.zeros(logits.shape, F32)
    v0 = None
    for j in range(TOP_K):
        m = jnp.max(logits, axis=-1, keepdims=True)
        sel = jnp.min(jnp.where(logits == m, lane, ROUTE_PAD), axis=-1, keepdims=True)
        if j == 0:
            v0 = m
        idx_out = jnp.where(lane == j, sel, idx_out)
        val_out = jnp.where(lane == j, jnp.exp(m - v0), val_out)
        logits = jnp.where(lane == sel, -jnp.inf, logits)
    idx_ref[...] = idx_out
    wts_ref[...] = val_out / jnp.sum(val_out, axis=-1, keepdims=True)


def _merge(o_gla, o_hy, gates, x2, wbg, wbh, wo, norm2_w, rw_pad, rb_pad):
    T = x2.shape[0]
    tm = min(512, T)
    row = lambda n: pl.BlockSpec((tm, n), lambda i: (i, 0))
    return pl.pallas_call(
        _merge_kernel,
        grid=(T // tm,),
        in_specs=[row(GLA_VAL), row(HY_WIDTH), row(N_GATES), row(D_MODEL),
                  _const_spec((GLA_VAL, D_MODEL)), _const_spec((HY_WIDTH, D_MODEL)), _const_spec((D_MODEL, D_MODEL)),
                  _const_spec((1, D_MODEL)), _const_spec((D_MODEL, 2 * ROUTE_PAD)), _const_spec((1, ROUTE_PAD))],
        out_specs=[row(D_MODEL), pl.BlockSpec((tm * TOK_ROWS, LANES), lambda i: (i, 0)), row(ROUTE_PAD), row(ROUTE_PAD)],
        out_shape=[jax.ShapeDtypeStruct((T, D_MODEL), F32), jax.ShapeDtypeStruct((T * TOK_ROWS, LANES), F32),
                   jax.ShapeDtypeStruct((T, ROUTE_PAD), jnp.int32), jax.ShapeDtypeStruct((T, ROUTE_PAD), F32)],
        compiler_params=_cparams("parallel"),
        name="merge_route",
    )(o_gla, o_hy, gates, x2, wbg, wbh, wo, norm2_w.reshape(1, D_MODEL), rw_pad, rb_pad)


MOE_TM = 512
MOE_TC = 256
DMA_UNROLL = 8


DISPATCH_SLOTS = 3


def _dispatch_kernel(last_ref, pos_ref, hn_ref, xs_ref, zero_s, buf, sem, lsems, ssems):
    i = pl.program_id(0)
    n = pl.num_programs(0)
    tc = pos_ref.shape[2] // TOP_K
    rows = tc * TOK_ROWS
    zrows = zero_s.shape[0]
    slot = lax.rem(i, DISPATCH_SLOTS)
    prev_slot = lax.rem(i + DISPATCH_SLOTS - 1, DISPATCH_SLOTS)

    def load(blk, s):
        src = hn_ref.at[pl.ds(pl.multiple_of(blk * rows, rows), rows)]
        return pltpu.make_async_copy(src, buf.at[s], lsems.at[s])

    def wait_copies(s):
        for _ in range(TOP_K):
            pltpu.make_async_copy(buf.at[s], xs_ref.at[pl.ds(0, rows)], ssems.at[s]).wait()

    @pl.when(i == 0)
    def _():
        load(0, 0).start()

        @pl.when(n > 1)
        def _():
            load(1, 1).start()

        zero_s[...] = jnp.zeros_like(zero_s)
        n_fill = last_ref.shape[0]

        def zfill(e, c):
            @pl.when(last_ref[e] >= 0)
            def _():
                dst = pl.multiple_of(last_ref[e] * zrows, zrows)
                pltpu.make_async_copy(zero_s, xs_ref.at[pl.ds(dst, zrows)], sem).start()
            return c
        lax.fori_loop(0, n_fill, zfill, 0)

        def zwait(e, c):
            @pl.when(last_ref[e] >= 0)
            def _():
                pltpu.make_async_copy(zero_s, xs_ref.at[pl.ds(0, zrows)], sem).wait()
            return c
        lax.fori_loop(0, n_fill, zwait, 0)

    load(i, slot).wait()

    def issue(r, c):
        src = buf.at[slot, pl.ds(pl.multiple_of(r * TOK_ROWS, TOK_ROWS), TOK_ROWS)]
        for j in range(TOP_K):
            dst = pl.multiple_of(pos_ref[0, 0, r * TOP_K + j] * TOK_ROWS, TOK_ROWS)
            pltpu.make_async_copy(src, xs_ref.at[pl.ds(dst, TOK_ROWS)], ssems.at[slot]).start(priority=j % 2)
        return c
    lax.fori_loop(0, tc, issue, 0, unroll=DMA_UNROLL // TOP_K)

    @pl.when(i > 0)
    def _():
        wait_copies(prev_slot)

    @pl.when(i + 2 < n)
    def _():
        load(i + 2, prev_slot).start()

    @pl.when(i == n - 1)
    def _():
        wait_copies(slot)


def _dispatch(hn3, pos, last_tile, P):
    T = hn3.shape[0] // TOK_ROWS
    tc = min(MOE_TC, T)
    nt = T // tc
    return pl.pallas_call(
        _dispatch_kernel,
        grid_spec=pltpu.PrefetchScalarGridSpec(
            num_scalar_prefetch=1,
            grid=(nt,),
            in_specs=[pl.BlockSpec((1, 1, tc * TOP_K), lambda i, lt: (i, 0, 0), memory_space=pltpu.SMEM),
                      pl.BlockSpec(memory_space=pl.ANY)],
            out_specs=pl.BlockSpec(memory_space=pl.ANY),
            scratch_shapes=[pltpu.VMEM((MOE_TM * TOK_ROWS, LANES), F32),
                            pltpu.VMEM((DISPATCH_SLOTS, tc * TOK_ROWS, LANES), F32),
                            pltpu.SemaphoreType.DMA(()),
                            pltpu.SemaphoreType.DMA((DISPATCH_SLOTS,)), pltpu.SemaphoreType.DMA((DISPATCH_SLOTS,))],
        ),
        out_shape=jax.ShapeDtypeStruct((P * TOK_ROWS, LANES), F32),
        compiler_params=_cparams("arbitrary"),
        name="moe_dispatch",
    )(last_tile, pos.reshape(nt, 1, tc * TOP_K), hn3)


def _expert_kernel(te_ref, nused_ref, xs_ref, wgu_ref, bgu_ref, wd_ref, bd_ref, o_ref, wgu_s, wd_s):
    i = pl.program_id(0)
    prev = te_ref[jnp.maximum(i - 1, 0)]

    @pl.when((i == 0) | (te_ref[i] != prev))
    def _():
        wgu_s[...] = wgu_ref[0].astype(BF16)
        wd_s[...] = wd_ref[0].astype(BF16)

    @pl.when(i < nused_ref[0])
    def _():
        x = _load_token_rows(xs_ref, 0, xs_ref.shape[0] // TOK_ROWS).astype(BF16)
        gu = jnp.dot(x, wgu_s[...], preferred_element_type=F32) + bgu_ref[0]
        gate = jnp.minimum(gu[:, :D_FF], SWIGLU_LIMIT)
        up = jnp.clip(gu[:, D_FF:], -SWIGLU_LIMIT, SWIGLU_LIMIT)
        act = gate * jax.nn.sigmoid(SWIGLU_ALPHA * gate) * (up + 1.0)
        y = jnp.dot(act.astype(BF16), wd_s[...], preferred_element_type=F32) + bd_ref[0]
        _store_token_rows(o_ref, y)

    @pl.when(i >= nused_ref[0])
    def _():
        o_ref[...] = jnp.zeros_like(o_ref)


def _experts(xs3, tile_expert, n_used, w_gate_up, b_gate_up, w_down, b_down):
    P = xs3.shape[0] // TOK_ROWS
    tm = MOE_TM
    nt = P // tm
    live = lambda i, te, nu: jnp.minimum(i, nu[0] - 1)
    return pl.pallas_call(
        _expert_kernel,
        grid_spec=pltpu.PrefetchScalarGridSpec(
            num_scalar_prefetch=2,
            grid=(nt,),
            in_specs=[
                pl.BlockSpec((tm * TOK_ROWS, LANES), lambda i, te, nu: (live(i, te, nu), 0)),
                pl.BlockSpec((1, D_MODEL, 2 * D_FF), lambda i, te, nu: (te[i], 0, 0)),
                pl.BlockSpec((1, 1, 2 * D_FF), lambda i, te, nu: (te[i], 0, 0)),
                pl.BlockSpec((1, D_FF, D_MODEL), lambda i, te, nu: (te[i], 0, 0)),
                pl.BlockSpec((1, 1, D_MODEL), lambda i, te, nu: (te[i], 0, 0)),
            ],
            out_specs=pl.BlockSpec((tm * TOK_ROWS, LANES), lambda i, te, nu: (i, 0)),
            scratch_shapes=[pltpu.VMEM((D_MODEL, 2 * D_FF), BF16), pltpu.VMEM((D_FF, D_MODEL), BF16)],
        ),
        out_shape=jax.ShapeDtypeStruct((P * TOK_ROWS, LANES), F32),
        compiler_params=_cparams("arbitrary"),
        name="moe_experts",
    )(tile_expert, n_used, xs3, w_gate_up, b_gate_up.reshape(N_EXPERTS, 1, 2 * D_FF), w_down,
      b_down.reshape(N_EXPERTS, 1, D_MODEL))


def _combine_kernel(pos_ref, posn_ref, wts_ref, h1_ref, nf_ref, ys_ref, o_ref, buf, sems):
    i = pl.program_id(0)
    n = pl.num_programs(0)
    tc = o_ref.shape[0]
    slot = lax.rem(i, 2)

    def gather(p_ref, s):
        def issue(r, c):
            for j in range(TOP_K):
                src = pl.multiple_of(p_ref[0, 0, r * TOP_K + j] * TOK_ROWS, TOK_ROWS)
                dst = pl.multiple_of((j * tc + r) * TOK_ROWS, TOK_ROWS)
                pltpu.make_async_copy(ys_ref.at[pl.ds(src, TOK_ROWS)], buf.at[s, pl.ds(dst, TOK_ROWS)],
                                      sems.at[s]).start(priority=j % 2)
            return c
        lax.fori_loop(0, tc, issue, 0, unroll=DMA_UNROLL // TOP_K)

    @pl.when(i == 0)
    def _():
        gather(pos_ref, 0)

    @pl.when(i + 1 < n)
    def _():
        gather(posn_ref, 1 - slot)

    pltpu.make_async_copy(ys_ref.at[pl.ds(0, TOP_K * tc * TOK_ROWS)], buf.at[slot], sems.at[slot]).wait()

    w = wts_ref[...]
    h = h1_ref[...]
    for j in range(TOP_K):
        h = h + w[:, j:j + 1] * _load_token_rows(buf.at[slot], j * tc, tc)
    o_ref[...] = h * lax.rsqrt(jnp.mean(h * h, axis=-1, keepdims=True) + EPS) * nf_ref[...]


def _combine(pos, wts, h1, norm_f_w, ys3):
    T = h1.shape[0]
    tc = min(MOE_TC, T)
    nt = T // tc
    pos3 = pos.reshape(nt, 1, tc * TOP_K)
    return pl.pallas_call(
        _combine_kernel,
        grid=(nt,),
        in_specs=[
            pl.BlockSpec((1, 1, tc * TOP_K), lambda i: (i, 0, 0), memory_space=pltpu.SMEM),
            pl.BlockSpec((1, 1, tc * TOP_K), lambda i: (jnp.minimum(i + 1, nt - 1), 0, 0), memory_space=pltpu.SMEM),
            pl.BlockSpec((tc, ROUTE_PAD), lambda i: (i, 0)),
            pl.BlockSpec((tc, D_MODEL), lambda i: (i, 0)),
            _const_spec((1, D_MODEL)),
            pl.BlockSpec(memory_space=pl.ANY),
        ],
        out_specs=pl.BlockSpec((tc, D_MODEL), lambda i: (i, 0)),
        out_shape=jax.ShapeDtypeStruct((T, D_MODEL), F32),
        scratch_shapes=[pltpu.VMEM((2, TOP_K * tc * TOK_ROWS, LANES), F32), pltpu.SemaphoreType.DMA((2,))],
        compiler_params=_cparams("arbitrary"),
        name="moe_combine",
    )(pos3, pos3, wts, h1, norm_f_w.reshape(1, D_MODEL), ys3)


def _route_plan(top_idx, T):
    tm = MOE_TM
    nt = (T * TOP_K) // tm + N_EXPERTS
    e = top_idx.reshape(-1)
    oh = (e[:, None] == jnp.arange(N_EXPERTS, dtype=jnp.int32)[None, :]).astype(jnp.int32)
    csum = jnp.cumsum(oh, axis=0)
    rank = jnp.sum(csum * oh, axis=1) - 1
    counts = csum[-1]
    tiles = (counts + tm - 1) // tm
    tile_end = jnp.cumsum(tiles)
    offs = (tile_end - tiles) * tm
    pos = jnp.sum(oh * offs[None, :], axis=1) + rank
    tile_id = jnp.arange(nt, dtype=jnp.int32)
    tile_expert = jnp.minimum(jnp.sum((tile_end[None, :] <= tile_id[:, None]).astype(jnp.int32), axis=1),
                              N_EXPERTS - 1).astype(jnp.int32)
    tail = tile_end[-1] + jnp.arange(N_EXPERTS, dtype=jnp.int32)
    last_tile = jnp.concatenate([jnp.where(tiles > 0, tile_end - 1, -1), jnp.where(tail < nt, tail, -1)]).astype(jnp.int32)
    n_used = tile_end[-1:].astype(jnp.int32)
    return pos.astype(jnp.int32), tile_expert, last_tile, n_used, nt * tm


def kernel(x, norm1_w, w_in, gla_gate_w2, gla_gate_b, gla_norm_w, hy_conv_w, hy_conv_b, hy_freq, hy_ffn_w1, hy_ffn_b1,
           hy_ffn_w2, hy_ffn_b2, hy_ffn_w3, hy_decay, hy_bias, w_branch_gla, w_branch_hy, w_out, norm2_w, router_w,
           router_b, w_gate_up, b_gate_up, w_down, b_down, norm_f_w):
    B, L, D = x.shape
    T = B * L
    x2 = x.reshape(T, D)

    s_a = N_QKVR
    s_h = s_a + 2 * GLA_RANK
    w_cat = jnp.concatenate(
        [w_in[:, :s_a], w_in[:, s_h:], w_in[:, s_a:s_h], jnp.zeros((D, ALOW_PAD - 2 * GLA_RANK), w_in.dtype)],
        axis=1).astype(BF16)
    qkvr, hy, gates, alow = _in_proj(x2, norm1_w, w_cat)

    w2 = jnp.zeros((2, ALOW_PAD, GLA_KEY), F32)
    w2 = w2.at[0, :GLA_RANK].set(gla_gate_w2[0]).at[1, GLA_RANK:2 * GLA_RANK].set(gla_gate_w2[1])
    o_gla = _gla(qkvr, alow, w2, gla_gate_b.reshape(2, 1, GLA_KEY), gla_norm_w, B, L)

    taps = _hy_filter(L, hy_freq, hy_ffn_w1, hy_ffn_b1, hy_ffn_w2, hy_ffn_b2, hy_ffn_w3, hy_decay)
    o_hy = _hyena(hy, hy_conv_w, hy_conv_b, hy_bias, taps, B, L)

    rw_f = jnp.zeros((D, ROUTE_PAD), F32).at[:, :N_EXPERTS].set(router_w)
    rw_hi = rw_f.astype(BF16)
    rw_pad = jnp.concatenate([rw_hi, (rw_f - rw_hi.astype(F32)).astype(BF16)], axis=1)
    rb_pad = jnp.zeros((1, ROUTE_PAD), F32).at[0, :N_EXPERTS].set(router_b)
    h1, hn3, idx_pad, wts_pad = _merge(o_gla, o_hy, gates, x2, w_branch_gla.astype(BF16), w_branch_hy.astype(BF16),
                                       w_out.astype(BF16), norm2_w, rw_pad, rb_pad)

    pos, tile_expert, last_tile, n_used, P = _route_plan(idx_pad[:, :TOP_K], T)
    xs3 = _dispatch(hn3, pos, last_tile, P)
    ys3 = _experts(xs3, tile_expert, n_used, w_gate_up, b_gate_up, w_down, b_down)
    out = _combine(pos, wts_pad, h1, norm_f_w, ys3)
    return out.reshape(B, L, D)
```

```python
import functools
import math

import numpy as np
import jax
import jax.numpy as jnp
from jax import lax
from jax.experimental import pallas as pl
from jax.experimental.pallas import tpu as pltpu

D_MODEL = 1024
EPS = 1e-6
GLA_HEADS = 4
GLA_DK = 128
GLA_DV = 256
GLA_KEY = GLA_HEADS * GLA_DK
GLA_VAL = GLA_HEADS * GLA_DV
GLA_RANK = 16
GLA_GATE_TEMP = 16.0
GLA_CHUNK = 64
HY_WIDTH = D_MODEL
HY_BANDS = 16
HY_EMB = 1 + 2 * HY_BANDS
HY_FFN = 64
N_EXPERTS = 32
TOP_K = 4
D_FF = D_MODEL
SWIGLU_LIMIT = 7.0
SWIGLU_ALPHA = 1.702

LANES = 128
ALOW_PAD = LANES
ROUTE_PAD = LANES
VMEM_LIMIT = 56 * 1024 * 1024

F32 = jnp.float32
BF16 = jnp.bfloat16
HIGHEST = lax.Precision.HIGHEST


def _cparams(*sem):
    return pltpu.CompilerParams(dimension_semantics=sem, vmem_limit_bytes=VMEM_LIMIT)


def _const_spec(shape):
    nd = len(shape)
    return pl.BlockSpec(shape, lambda *_: (0,) * nd, pipeline_mode=pl.Buffered(1))


TOK_ROWS = D_MODEL // LANES


def _store_token_rows(ref, val):
    n = val.shape[0]
    for s_ in range(TOK_ROWS):
        ref[pl.ds(s_, n, stride=TOK_ROWS), :] = val[:, s_ * LANES:(s_ + 1) * LANES]


def _load_token_rows(ref, first, n):
    return jnp.concatenate(
        [ref[pl.ds(first * TOK_ROWS + s_, n, stride=TOK_ROWS), :] for s_ in range(TOK_ROWS)], axis=1)


N_QKVR = 2 * GLA_KEY + 2 * GLA_VAL
N_HY = 3 * HY_WIDTH
N_GATES = 2 * D_MODEL
IN_COLS = N_QKVR + N_HY + N_GATES + ALOW_PAD


def _inproj_kernel(x_ref, nw_ref, w_ref, qkvr_ref, hy_ref, gates_ref, alow_ref):
    x = x_ref[...]
    ms = jnp.mean(x * x, axis=-1, keepdims=True)
    xn = (x * lax.rsqrt(ms + EPS) * nw_ref[...]).astype(BF16)
    c0, c1, c2 = N_QKVR, N_QKVR + N_HY, N_QKVR + N_HY + N_GATES
    qkvr_ref[...] = jnp.dot(xn, w_ref[:, :c0], preferred_element_type=F32).astype(BF16)
    hy_ref[...] = jnp.dot(xn, w_ref[:, c0:c1], preferred_element_type=F32).astype(BF16)
    gates_ref[...] = jnp.dot(xn, w_ref[:, c1:c2], preferred_element_type=F32).astype(BF16)
    alow_ref[...] = jnp.dot(xn, w_ref[:, c2:], preferred_element_type=F32)


def _in_proj(x2, norm1_w, w_cat):
    T = x2.shape[0]
    tm = min(512, T)
    return pl.pallas_call(
        _inproj_kernel,
        grid=(T // tm,),
        in_specs=[
            pl.BlockSpec((tm, D_MODEL), lambda i: (i, 0)),
            _const_spec((1, D_MODEL)),
            _const_spec((D_MODEL, IN_COLS)),
        ],
        out_specs=[
            pl.BlockSpec((tm, N_QKVR), lambda i: (i, 0)),
            pl.BlockSpec((tm, N_HY), lambda i: (i, 0)),
            pl.BlockSpec((tm, N_GATES), lambda i: (i, 0)),
            pl.BlockSpec((tm, ALOW_PAD), lambda i: (i, 0)),
        ],
        out_shape=[
            jax.ShapeDtypeStruct((T, N_QKVR), BF16),
            jax.ShapeDtypeStruct((T, N_HY), BF16),
            jax.ShapeDtypeStruct((T, N_GATES), BF16),
            jax.ShapeDtypeStruct((T, ALOW_PAD), F32),
        ],
        compiler_params=_cparams("parallel"),
        name="in_proj",
    )(x2, norm1_w.reshape(1, D_MODEL), w_cat)


GLA_PREP_ROWS = 256
GLA_PREP_UNROLL = 4
GLA_SCAN_UNROLL = 16


def _log_sigmoid(x):
    return jnp.minimum(x, 0.0) - jnp.log1p(jnp.exp(-jnp.abs(x)))


def _gla_kernel(q_ref, k_ref, v_ref, r_ref, alow_ref, w2_ref, gb_ref, nw_ref, o_ref,
                qtf, kef, qtb, keb, decf, decb, of_s, ob_s, stf, stb):
    L = q_ref.shape[0]
    C = GLA_CHUNK
    DK = GLA_DK
    n_chunks = L // C
    R = min(GLA_PREP_ROWS, L)
    cpr = R // C
    scale = DK ** -0.5
    nt = (((1,), (1,)), ((), ()))
    tn = (((0,), (0,)), ((), ()))

    ri = lax.broadcasted_iota(jnp.int32, (R, R), 0)
    ci = lax.broadcasted_iota(jnp.int32, (R, R), 1)
    same = (ri // C) == (ci // C)
    mask_f = same & (ri >= ci)
    mask_b = same & (ri < ci)
    tri = jnp.where(mask_f, 1.0, 0.0).astype(BF16)
    w2 = jnp.concatenate([w2_ref[0], w2_ref[1]], axis=1).astype(BF16)
    gb = jnp.concatenate([gb_ref[0], gb_ref[1]], axis=1)

    def prep(g, carry):
        rows = pl.ds(pl.multiple_of(g * R, R), R)
        logits = jnp.dot(alow_ref[rows, :].astype(BF16), w2, preferred_element_type=F32) + gb
        lg = _log_sigmoid(logits) * (1.0 / GLA_GATE_TEMP)
        hi = lg.astype(BF16)
        lo = (lg - hi.astype(F32)).astype(BF16)
        b = jnp.dot(tri, hi, preferred_element_type=F32) + jnp.dot(tri, lo, preferred_element_type=F32)
        tot = jnp.concatenate(
            [jnp.broadcast_to(b[c * C + C - 1:c * C + C, :], (C, 2 * DK)) for c in range(cpr)], axis=0)
        q = q_ref[rows, :].astype(F32) * scale
        k = k_ref[rows, :].astype(F32)
        vb = v_ref[rows, :]
        dtot = jnp.exp(tot)
        for d, (qt, ke, dec, mask, out) in enumerate(((qtf, kef, decf, mask_f, of_s), (qtb, keb, decb, mask_b, ob_s))):
            sl = slice(d * DK, (d + 1) * DK)
            bd, td, ld = b[:, sl], tot[:, sl], lg[:, sl]
            if d == 0:
                e_q, e_end = bd, td - bd
            else:
                e_q, e_end = td - bd + ld, bd - ld
            qtv = (q * jnp.exp(e_q)).astype(BF16)
            ktv = (k * jnp.exp(-e_q)).astype(BF16)
            qt[rows, :] = qtv
            ke[rows, :] = (k * jnp.exp(e_end)).astype(BF16)
            s = lax.dot_general(qtv, ktv, nt, preferred_element_type=F32)
            s = jnp.where(mask, s, 0.0).astype(BF16)
            out[rows, :] = jnp.dot(s, vb, preferred_element_type=F32)
            for c in range(cpr):
                dec[pl.ds(g * cpr + c, 1), :] = dtot[c * C:c * C + 1, sl]
        return carry

    lax.fori_loop(0, L // R, prep, 0, unroll=min(GLA_PREP_UNROLL, L // R))

    def chunk(n, st, qt, ke, dec, out):
        rows = pl.ds(pl.multiple_of(n * C, C), C)
        s = st[...]
        out[rows, :] += lax.dot_general(qt[rows, :], s.astype(BF16), nt, preferred_element_type=F32)
        upd = lax.dot_general(v_ref[rows, :], ke[rows, :], tn, preferred_element_type=F32)
        st[...] = dec[pl.ds(n, 1), :] * s + upd

    def step(i, carry):
        chunk(i, stf, qtf, kef, decf, of_s)
        chunk(n_chunks - 1 - i, stb, qtb, keb, decb, ob_s)
        return carry

    stf[...] = jnp.zeros_like(stf)
    stb[...] = jnp.zeros_like(stb)
    lax.fori_loop(0, n_chunks, step, 0, unroll=min(GLA_SCAN_UNROLL, n_chunks))

    o = of_s[...] + ob_s[...]
    o = o * lax.rsqrt(jnp.mean(o * o, axis=-1, keepdims=True) + EPS) * nw_ref[...]
    r = r_ref[...].astype(F32)
    o_ref[...] = (o * (r * jax.nn.sigmoid(r))).astype(BF16)


def _gla(qkvr, alow, w2, gb, norm_w, B, L):
    T = B * L
    H = GLA_HEADS
    kq = GLA_KEY // GLA_DK
    kv = 2 * GLA_KEY // GLA_DV
    kr = (2 * GLA_KEY + GLA_VAL) // GLA_DV
    return pl.pallas_call(
        _gla_kernel,
        grid=(B, H),
        in_specs=[
            pl.BlockSpec((L, GLA_DK), lambda b, h: (b, h)),
            pl.BlockSpec((L, GLA_DK), lambda b, h: (b, kq + h)),
            pl.BlockSpec((L, GLA_DV), lambda b, h: (b, kv + h)),
            pl.BlockSpec((L, GLA_DV), lambda b, h: (b, kr + h)),
            pl.BlockSpec((L, ALOW_PAD), lambda b, h: (b, 0)),
            pl.BlockSpec((2, ALOW_PAD, GLA_DK), lambda b, h: (0, 0, h)),
            pl.BlockSpec((2, 1, GLA_DK), lambda b, h: (0, 0, h)),
            pl.BlockSpec((1, GLA_DV), lambda b, h: (0, 0)),
        ],
        out_specs=pl.BlockSpec((L, GLA_DV), lambda b, h: (b, h)),
        out_shape=jax.ShapeDtypeStruct((T, GLA_VAL), BF16),
        scratch_shapes=[pltpu.VMEM((L, GLA_DK), BF16)] * 4
        + [pltpu.VMEM((L // GLA_CHUNK, GLA_DK), F32)] * 2
        + [pltpu.VMEM((L, GLA_DV), F32)] * 2
        + [pltpu.VMEM((GLA_DV, GLA_DK), F32)] * 2,
        compiler_params=_cparams("parallel", "parallel"),
        name="gla",
    )(qkvr, qkvr, qkvr, qkvr, alow, w2, gb, norm_w.reshape(1, GLA_DV))


HY_WT = 256


def _hy_filter_kernel(z_ref, t_ref, fr_ref, w1_ref, b1_ref, w2_ref, b2_ref, w3f_ref, w3b_ref, dec_ref, o_ref, h_s):
    @pl.when(pl.program_id(0) == 0)
    def _():
        fr = fr_ref[...]
        h1 = jnp.sin(fr * (jnp.dot(z_ref[...], w1_ref[...], preferred_element_type=F32, precision=HIGHEST) + b1_ref[...]))
        h_s[...] = jnp.sin(fr * (jnp.dot(h1, w2_ref[...], preferred_element_type=F32, precision=HIGHEST) + b2_ref[...]))

    h = h_s[...]
    t = t_ref[...]
    row = lax.broadcasted_iota(jnp.int32, (z_ref.shape[0], 1), 0)
    hq = h.astype(BF16)
    hf = jnp.dot(hq, w3f_ref[...].astype(BF16), preferred_element_type=F32)
    hf = hf * jnp.exp(-t * jnp.abs(dec_ref[0:1, :]))
    hb = jnp.dot(hq, w3b_ref[...].astype(BF16), preferred_element_type=F32)
    hb = hb * jnp.exp(-t * jnp.abs(dec_ref[1:2, :]))
    hb = jnp.where(row == 0, 0.0, hb)
    ss = jnp.sum(hf * hf, axis=0, keepdims=True) + jnp.sum(hb * hb, axis=0, keepdims=True)
    inv = lax.rsqrt(ss + EPS)
    o_ref[0] = hf * inv
    o_ref[1] = hb * inv


def _hy_filter(L, hy_freq, w1, b1, w2, b2, w3, hy_decay):
    W = HY_WIDTH
    t = np.linspace(0.0, 1.0, L, dtype=np.float32)[:, None]
    omega = (np.float32(2.0 * math.pi) * np.arange(L, dtype=np.float32) / np.float32(L)).astype(np.float32)
    f = np.linspace(1e-4, HY_BANDS - 1, HY_BANDS, dtype=np.float32)
    ang = (omega[:, None] * f[None, :]).astype(np.float32)
    z = np.concatenate([t, np.cos(ang), -np.sin(ang)], axis=-1).astype(np.float32)
    nw = W // HY_WT
    return pl.pallas_call(
        _hy_filter_kernel,
        grid=(nw,),
        in_specs=[
            _const_spec((L, HY_EMB)),
            _const_spec((L, 1)),
            _const_spec((1, HY_FFN)),
            _const_spec((HY_EMB, HY_FFN)),
            _const_spec((1, HY_FFN)),
            _const_spec((HY_FFN, HY_FFN)),
            _const_spec((1, HY_FFN)),
            pl.BlockSpec((HY_FFN, HY_WT), lambda j: (0, j)),
            pl.BlockSpec((HY_FFN, HY_WT), lambda j: (0, nw + j)),
            pl.BlockSpec((2, HY_WT), lambda j: (0, j)),
        ],
        out_specs=pl.BlockSpec((2, L, HY_WT), lambda j: (0, 0, j)),
        out_shape=jax.ShapeDtypeStruct((2, L, W), F32),
        scratch_shapes=[pltpu.VMEM((L, HY_FFN), F32)],
        compiler_params=_cparams("arbitrary"),
        name="hy_filter",
    )(jnp.asarray(z), jnp.asarray(t), hy_freq.reshape(1, HY_FFN), w1, b1.reshape(1, HY_FFN), w2,
      b2.reshape(1, HY_FFN), w3, w3, hy_decay)


FFT_L = LANES
FFT_PITCH = FFT_L + 8
FFT_UNROLL = 8


def _fft_tables(L):
    N = 2 * L
    C = L // FFT_L
    n1 = N // FFT_L
    kh = min(n1, -(-(n1 // 2 + 1) // 8) * 8)
    l = np.arange(FFT_L)
    c = np.arange(C)
    k2 = np.arange(kh)
    pair_w = np.where((k2 == 0) | (k2 == n1 // 2), 1.0, np.where(k2 < n1 // 2, 2.0, 0.0))
    ang1 = 2 * np.pi * ((k2[:, None] * c[None, :]) % n1) / n1
    f1 = np.concatenate([np.cos(ang1), -np.sin(ang1)], axis=0)
    g1 = (np.stack([np.cos(ang1), -np.sin(ang1)], axis=-1) * pair_w[:, None, None]).transpose(1, 0, 2).reshape(C, 2 * kh) / N
    ang2 = 2 * np.pi * (((l[None, :, None] * l[None, None, :]) % FFT_L) / FFT_L
                        + ((k2[:, None, None] * l[None, None, :]) % N) / N)
    cs, sn = np.cos(ang2), np.sin(ang2)
    f2 = np.concatenate([np.concatenate([cs, sn], axis=2), np.concatenate([-sn, cs], axis=2)], axis=1)
    cst, snt = cs.transpose(0, 2, 1), sn.transpose(0, 2, 1)
    g2 = np.concatenate([np.concatenate([cst, -snt], axis=2), np.concatenate([snt, cst], axis=2)], axis=1)
    bf = lambda a: jnp.asarray(a.astype(np.float32)).astype(BF16)
    return dict(C=C, n1=kh, f1=bf(f1), g1=bf(g1), f2=bf(f2), g2=bf(g2))


def _fft_kernel(z_ref, f1_ref, f2_ref, g2_ref, kf_ref, g1_ref, o_ref, a_s, b_s, y_s, *, spectrum_only, epilogue=None):
    C = f1_ref.shape[1]
    n1 = f1_ref.shape[0] // 2
    P = FFT_PITCH
    blk = lambda j: pl.ds(pl.multiple_of(j * P, 8), FFT_L)

    def stage1(j, carry):
        x = jnp.concatenate([z_ref[pl.ds(2 * j, C, stride=FFT_L), :], z_ref[pl.ds(2 * j + 1, C, stride=FFT_L), :]], axis=1)
        r = jnp.dot(f1_ref[...], x.astype(BF16), preferred_element_type=F32)
        a_s[pl.ds(pl.multiple_of(2 * j * P, 8), 2 * n1), :] = r[:, :LANES]
        a_s[pl.ds(pl.multiple_of((2 * j + 1) * P, 8), 2 * n1), :] = r[:, LANES:]
        return carry
    lax.fori_loop(0, FFT_L // 2, stage1, 0, unroll=FFT_UNROLL)

    def stage2(k2, carry, *, accumulate=False):
        ar = a_s[pl.ds(k2, FFT_L, stride=P), :]
        ai = a_s[pl.ds(n1 + k2, FFT_L, stride=P), :]
        x = jnp.concatenate([ar, ai], axis=0).astype(BF16)
        s = jnp.dot(f2_ref[k2], x, preferred_element_type=F32)
        xr, xi = s[:FFT_L], s[FFT_L:]
        if not spectrum_only:
            kr, ki = kf_ref[k2, 0], kf_ref[k2, 1]
            y = jnp.concatenate([xr * kr - xi * ki, xr * ki + xi * kr], axis=0).astype(BF16)
            s = jnp.dot(g2_ref[k2], y, preferred_element_type=F32)
            b_s[blk(2 * k2), :] = s[:FFT_L]
            b_s[blk(2 * k2 + 1), :] = s[FFT_L:]
        elif accumulate:
            o_ref[k2, 0] += xr
            o_ref[k2, 1] -= xi
        else:
            o_ref[k2, 0] = xr
            o_ref[k2, 1] = xi
        return carry

    if spectrum_only:
        @pl.when(pl.program_id(1) == 0)
        def _():
            lax.fori_loop(0, n1, stage2, 0, unroll=FFT_UNROLL)

        @pl.when(pl.program_id(1) != 0)
        def _():
            lax.fori_loop(0, n1, functools.partial(stage2, accumulate=True), 0, unroll=FFT_UNROLL)
    else:
        lax.fori_loop(0, n1, stage2, 0, unroll=FFT_UNROLL)

    if not spectrum_only:
        def stage3(j, carry):
            x = jnp.concatenate([b_s[pl.ds(2 * j, 2 * n1, stride=P), :], b_s[pl.ds(2 * j + 1, 2 * n1, stride=P), :]], axis=1)
            r = jnp.dot(g1_ref[...], x.astype(BF16), preferred_element_type=F32)
            y_s[pl.ds(2 * j, C, stride=P), :] = r[:, :LANES]
            y_s[pl.ds(2 * j + 1, C, stride=P), :] = r[:, LANES:]
            return carry
        lax.fori_loop(0, FFT_L // 2, stage3, 0, unroll=FFT_UNROLL)
        for c in range(C):
            rows = slice(c * FFT_L, (c + 1) * FFT_L)
            o_ref[rows, :] = epilogue(rows, y_s[c * P:c * P + FFT_L, :])


def _hyena_kernel(x0_ref, x1_ref, v_ref, w0_ref, w1_ref, wv_ref, b0_ref, b1_ref, bv_ref, hb_ref,
                  f1_ref, f2_ref, g2_ref, kf_ref, g1_ref, o_ref, z_s, x0_s, a_s, b_s, y_s):
    L = x0_ref.shape[0]
    row = lax.broadcasted_iota(jnp.int32, (L, 1), 0)

    def conv3(u_ref, w_ref, b_ref):
        u = u_ref[...].astype(F32)
        prev = jnp.where(row == 0, 0.0, pltpu.roll(u, 1, 0))
        nxt = jnp.where(row == L - 1, 0.0, pltpu.roll(u, L - 1, 0))
        return w_ref[0:1, :] * prev + w_ref[1:2, :] * u + w_ref[2:3, :] * nxt + b_ref[...]

    z_s[...] = conv3(v_ref, wv_ref, bv_ref) * conv3(x1_ref, w1_ref, b1_ref)
    x0_s[...] = conv3(x0_ref, w0_ref, b0_ref)

    def epilogue(rows, y):
        return ((y + z_s[rows, :] * hb_ref[...]) * x0_s[rows, :]).astype(BF16)

    _fft_kernel(z_s, f1_ref, f2_ref, g2_ref, kf_ref, g1_ref, o_ref, a_s, b_s, y_s, spectrum_only=False, epilogue=epilogue)


def _fft_specs(tabs):
    C, n1 = tabs["C"], tabs["n1"]
    specs = [_const_spec((2 * n1, C)),
             _const_spec((n1, 2 * FFT_L, 2 * FFT_L)), _const_spec((n1, 2 * FFT_L, 2 * FFT_L)),
             pl.BlockSpec((n1, 2, FFT_L, LANES), lambda w, b: (0, 0, 0, w)),
             _const_spec((C, 2 * n1))]
    scratch = [pltpu.VMEM((FFT_L * FFT_PITCH, LANES), F32), pltpu.VMEM((2 * n1 * FFT_PITCH, LANES), F32),
               pltpu.VMEM((C * FFT_PITCH, LANES), F32)]
    return specs, scratch


def _filter_spectrum(tabs, taps):
    _, L, W = taps.shape
    n1 = tabs["n1"]
    specs, scratch = _fft_specs(tabs)
    unused = jnp.zeros((n1, 2, FFT_L, W), F32)
    return pl.pallas_call(
        functools.partial(_fft_kernel, spectrum_only=True),
        grid=(W // LANES, 2),
        in_specs=[pl.BlockSpec((None, L, LANES), lambda w, b: (b, 0, w))] + specs,
        out_specs=pl.BlockSpec((n1, 2, FFT_L, LANES), lambda w, b: (0, 0, 0, w)),
        out_shape=jax.ShapeDtypeStruct((n1, 2, FFT_L, W), F32),
        scratch_shapes=scratch,
        compiler_params=_cparams("arbitrary", "arbitrary"),
        name="fft_spectrum",
    )(taps, tabs["f1"], tabs["f2"], tabs["g2"], unused, tabs["g1"])


def _hyena(hy, conv_w, conv_b, hy_bias, taps, B, L):
    W = HY_WIDTH
    nw = W // LANES
    tabs = _fft_tables(L)
    kf = _filter_spectrum(tabs, taps)
    specs, scratch = _fft_specs(tabs)
    cb = conv_b.reshape(1, 3 * W)
    blk = lambda off: pl.BlockSpec((L, LANES), lambda w, b: (b, off + w))
    wblk = lambda off: pl.BlockSpec((3, LANES), lambda w, b: (0, off + w))
    bblk = lambda off: pl.BlockSpec((1, LANES), lambda w, b: (0, off + w))
    return pl.pallas_call(
        _hyena_kernel,
        grid=(nw, B),
        in_specs=[blk(0), blk(nw), blk(2 * nw), wblk(0), wblk(nw), wblk(2 * nw), bblk(0), bblk(nw), bblk(2 * nw),
                  bblk(0)] + specs,
        out_specs=pl.BlockSpec((L, LANES), lambda w, b: (b, w)),
        out_shape=jax.ShapeDtypeStruct((B * L, W), BF16),
        scratch_shapes=[pltpu.VMEM((L, LANES), F32), pltpu.VMEM((L, LANES), F32)] + scratch,
        compiler_params=_cparams("arbitrary", "arbitrary"),
        name="hyena",
    )(hy, hy, hy, conv_w, conv_w, conv_w, cb, cb, cb, hy_bias.reshape(1, W), tabs["f1"], tabs["f2"], tabs["g2"], kf,
      tabs["g1"])


def _merge_kernel(og_ref, oh_ref, gates_ref, x_ref, wbg_ref, wbh_ref, wo_ref, n2_ref, rw_ref, rb_ref,
                  h1_ref, hn3_ref, idx_ref, wts_ref):
    mg = jnp.dot(og_ref[...], wbg_ref[...], preferred_element_type=F32)
    mh = jnp.dot(oh_ref[...], wbh_ref[...], preferred_element_type=F32)
    gg = jax.nn.sigmoid(gates_ref[:, :D_MODEL].astype(F32))
    gh = jax.nn.sigmoid(gates_ref[:, D_MODEL:].astype(F32))
    merged = (gg * mg + gh * mh).astype(BF16)
    h1 = x_ref[...] + jnp.dot(merged, wo_ref[...], preferred_element_type=F32)
    h1_ref[...] = h1
    hn = h1 * lax.rsqrt(jnp.mean(h1 * h1, axis=-1, keepdims=True) + EPS) * n2_ref[...]
    _store_token_rows(hn3_ref, hn)
    hh = hn.astype(BF16)
    hl = (hn - hh.astype(F32)).astype(BF16)
    p1 = jnp.dot(hh, rw_ref[...], preferred_element_type=F32)
    p2 = jnp.dot(hl, rw_ref[:, :ROUTE_PAD], preferred_element_type=F32)
    logits = p1[:, :ROUTE_PAD] + p1[:, ROUTE_PAD:] + p2 + rb_ref[...]
    lane = lax.broadcasted_iota(jnp.int32, logits.shape, 1)
    logits = jnp.where(lane < N_EXPERTS, logits, -jnp.inf)
    idx_out = jnp.zeros(logits.shape, jnp.int32)
    val_out = jnp.zeros(logits.shape, F32)
    v0 = None
    for j in range(TOP_K):
        m = jnp.max(logits, axis=-1, keepdims=True)
        sel = jnp.min(jnp.where(logits == m, lane, ROUTE_PAD), axis=-1, keepdims=True)
        if j == 0:
            v0 = m
        idx_out = jnp.where(lane == j, sel, idx_out)
        val_out = jnp.where(lane == j, jnp.exp(m - v0), val_out)
        logits = jnp.where(lane == sel, -jnp.inf, logits)
    idx_ref[...] = idx_out
    wts_ref[...] = val_out / jnp.sum(val_out, axis=-1, keepdims=True)


def _merge(o_gla, o_hy, gates, x2, wbg, wbh, wo, norm2_w, rw_pad, rb_pad):
    T = x2.shape[0]
    tm = min(512, T)
    row = lambda n: pl.BlockSpec((tm, n), lambda i: (i, 0))
    return pl.pallas_call(
        _merge_kernel,
        grid=(T // tm,),
        in_specs=[row(GLA_VAL), row(HY_WIDTH), row(N_GATES), row(D_MODEL),
                  _const_spec((GLA_VAL, D_MODEL)), _const_spec((HY_WIDTH, D_MODEL)), _const_spec((D_MODEL, D_MODEL)),
                  _const_spec((1, D_MODEL)), _const_spec((D_MODEL, 2 * ROUTE_PAD)), _const_spec((1, ROUTE_PAD))],
        out_specs=[row(D_MODEL), pl.BlockSpec((tm * TOK_ROWS, LANES), lambda i: (i, 0)), row(ROUTE_PAD), row(ROUTE_PAD)],
        out_shape=[jax.ShapeDtypeStruct((T, D_MODEL), F32), jax.ShapeDtypeStruct((T * TOK_ROWS, LANES), F32),
                   jax.ShapeDtypeStruct((T, ROUTE_PAD), jnp.int32), jax.ShapeDtypeStruct((T, ROUTE_PAD), F32)],
        compiler_params=_cparams("parallel"),
        name="merge_route",
    )(o_gla, o_hy, gates, x2, wbg, wbh, wo, norm2_w.reshape(1, D_MODEL), rw_pad, rb_pad)


MOE_TM = 512
MOE_TC = 256
DMA_UNROLL = 8


DISPATCH_SLOTS = 3


def _dispatch_kernel(last_ref, pos_ref, hn_ref, xs_ref, zero_s, buf, sem, lsems, ssems):
    i = pl.program_id(0)
    n = pl.num_programs(0)
    tc = pos_ref.shape[2] // TOP_K
    rows = tc * TOK_ROWS
    zrows = zero_s.shape[0]
    slot = lax.rem(i, DISPATCH_SLOTS)
    prev_slot = lax.rem(i + DISPATCH_SLOTS - 1, DISPATCH_SLOTS)

    def load(blk, s):
        src = hn_ref.at[pl.ds(pl.multiple_of(blk * rows, rows), rows)]
        return pltpu.make_async_copy(src, buf.at[s], lsems.at[s])

    def wait_copies(s):
        for _ in range(TOP_K):
            pltpu.make_async_copy(buf.at[s], xs_ref.at[pl.ds(0, rows)], ssems.at[s]).wait()

    @pl.when(i == 0)
    def _():
        load(0, 0).start()

        @pl.when(n > 1)
        def _():
            load(1, 1).start()

        zero_s[...] = jnp.zeros_like(zero_s)
        n_fill = last_ref.shape[0]

        def zfill(e, c):
            @pl.when(last_ref[e] >= 0)
            def _():
                dst = pl.multiple_of(last_ref[e] * zrows, zrows)
                pltpu.make_async_copy(zero_s, xs_ref.at[pl.ds(dst, zrows)], sem).start()
            return c
        lax.fori_loop(0, n_fill, zfill, 0)

        def zwait(e, c):
            @pl.when(last_ref[e] >= 0)
            def _():
                pltpu.make_async_copy(zero_s, xs_ref.at[pl.ds(0, zrows)], sem).wait()
            return c
        lax.fori_loop(0, n_fill, zwait, 0)

    load(i, slot).wait()

    def issue(r, c):
        src = buf.at[slot, pl.ds(pl.multiple_of(r * TOK_ROWS, TOK_ROWS), TOK_ROWS)]
        for j in range(TOP_K):
            dst = pl.multiple_of(pos_ref[0, 0, r * TOP_K + j] * TOK_ROWS, TOK_ROWS)
            pltpu.make_async_copy(src, xs_ref.at[pl.ds(dst, TOK_ROWS)], ssems.at[slot]).start(priority=j % 2)
        return c
    lax.fori_loop(0, tc, issue, 0, unroll=DMA_UNROLL // TOP_K)

    @pl.when(i > 0)
    def _():
        wait_copies(prev_slot)

    @pl.when(i + 2 < n)
    def _():
        load(i + 2, prev_slot).start()

    @pl.when(i == n - 1)
    def _():
        wait_copies(slot)


def _dispatch(hn3, pos, last_tile, P):
    T = hn3.shape[0] // TOK_ROWS
    tc = min(MOE_TC, T)
    nt = T // tc
    return pl.pallas_call(
        _dispatch_kernel,
        grid_spec=pltpu.PrefetchScalarGridSpec(
            num_scalar_prefetch=1,
            grid=(nt,),
            in_specs=[pl.BlockSpec((1, 1, tc * TOP_K), lambda i, lt: (i, 0, 0), memory_space=pltpu.SMEM),
                      pl.BlockSpec(memory_space=pl.ANY)],
            out_specs=pl.BlockSpec(memory_space=pl.ANY),
            scratch_shapes=[pltpu.VMEM((MOE_TM * TOK_ROWS, LANES), F32),
                            pltpu.VMEM((DISPATCH_SLOTS, tc * TOK_ROWS, LANES), F32),
                            pltpu.SemaphoreType.DMA(()),
                            pltpu.SemaphoreType.DMA((DISPATCH_SLOTS,)), pltpu.SemaphoreType.DMA((DISPATCH_SLOTS,))],
        ),
        out_shape=jax.ShapeDtypeStruct((P * TOK_ROWS, LANES), F32),
        compiler_params=_cparams("arbitrary"),
        name="moe_dispatch",
    )(last_tile, pos.reshape(nt, 1, tc * TOP_K), hn3)


def _expert_kernel(te_ref, nused_ref, xs_ref, wgu_ref, bgu_ref, wd_ref, bd_ref, o_ref, wgu_s, wd_s):
    i = pl.program_id(0)
    prev = te_ref[jnp.maximum(i - 1, 0)]

    @pl.when((i == 0) | (te_ref[i] != prev))
    def _():
        wgu_s[...] = wgu_ref[0].astype(BF16)
        wd_s[...] = wd_ref[0].astype(BF16)

    @pl.when(i < nused_ref[0])
    def _():
        x = _load_token_rows(xs_ref, 0, xs_ref.shape[0] // TOK_ROWS).astype(BF16)
        gu = jnp.dot(x, wgu_s[...], preferred_element_type=F32) + bgu_ref[0]
        gate = jnp.minimum(gu[:, :D_FF], SWIGLU_LIMIT)
        up = jnp.clip(gu[:, D_FF:], -SWIGLU_LIMIT, SWIGLU_LIMIT)
        act = gate * jax.nn.sigmoid(SWIGLU_ALPHA * gate) * (up + 1.0)
        y = jnp.dot(act.astype(BF16), wd_s[...], preferred_element_type=F32) + bd_ref[0]
        _store_token_rows(o_ref, y)

    @pl.when(i >= nused_ref[0])
    def _():
        o_ref[...] = jnp.zeros_like(o_ref)


def _experts(xs3, tile_expert, n_used, w_gate_up, b_gate_up, w_down, b_down):
    P = xs3.shape[0] // TOK_ROWS
    tm = MOE_TM
    nt = P // tm
    live = lambda i, te, nu: jnp.minimum(i, nu[0] - 1)
    return pl.pallas_call(
        _expert_kernel,
        grid_spec=pltpu.PrefetchScalarGridSpec(
            num_scalar_prefetch=2,
            grid=(nt,),
            in_specs=[
                pl.BlockSpec((tm * TOK_ROWS, LANES), lambda i, te, nu: (live(i, te, nu), 0)),
                pl.BlockSpec((1, D_MODEL, 2 * D_FF), lambda i, te, nu: (te[i], 0, 0)),
                pl.BlockSpec((1, 1, 2 * D_FF), lambda i, te, nu: (te[i], 0, 0)),
                pl.BlockSpec((1, D_FF, D_MODEL), lambda i, te, nu: (te[i], 0, 0)),
                pl.BlockSpec((1, 1, D_MODEL), lambda i, te, nu: (te[i], 0, 0)),
            ],
            out_specs=pl.BlockSpec((tm * TOK_ROWS, LANES), lambda i, te, nu: (i, 0)),
            scratch_shapes=[pltpu.VMEM((D_MODEL, 2 * D_FF), BF16), pltpu.VMEM((D_FF, D_MODEL), BF16)],
        ),
        out_shape=jax.ShapeDtypeStruct((P * TOK_ROWS, LANES), F32),
        compiler_params=_cparams("arbitrary"),
        name="moe_experts",
    )(tile_expert, n_used, xs3, w_gate_up, b_gate_up.reshape(N_EXPERTS, 1, 2 * D_FF), w_down,
      b_down.reshape(N_EXPERTS, 1, D_MODEL))


COMBINE_SLOTS = 3
COMBINE_GROUP = 8


def _combine_kernel(pos_ref, pos1_ref, pos2_ref, wts_ref, h1_ref, nf_ref, ys_ref, o_ref, buf, sems):
    i = pl.program_id(0)
    n = pl.num_programs(0)
    tc = o_ref.shape[0]
    G = COMBINE_GROUP
    slot = lax.rem(i, COMBINE_SLOTS)
    slot2 = lax.rem(i + 2, COMBINE_SLOTS)

    def start_rows(p_ref, s, r):
        for j in range(TOP_K):
            src = pl.multiple_of(p_ref[0, 0, r * TOP_K + j] * TOK_ROWS, TOK_ROWS)
            dst = pl.multiple_of((j * tc + r) * TOK_ROWS, TOK_ROWS)
            pltpu.make_async_copy(ys_ref.at[pl.ds(src, TOK_ROWS)], buf.at[s, pl.ds(dst, TOK_ROWS)],
                                  sems.at[s]).start(priority=j % 2)

    def gather(p_ref, s):
        def issue(r, c):
            start_rows(p_ref, s, r)
            return c
        lax.fori_loop(0, tc, issue, 0, unroll=DMA_UNROLL // TOP_K)

    def combine_group(g):
        rows = slice(g * G, (g + 1) * G)
        w = wts_ref[rows, :]
        h = h1_ref[rows, :]
        for j in range(TOP_K):
            h = h + w[:, j:j + 1] * _load_token_rows(buf.at[slot], j * tc + g * G, G)
        o_ref[rows, :] = h * lax.rsqrt(jnp.mean(h * h, axis=-1, keepdims=True) + EPS) * nf_ref[...]

    @pl.when(i == 0)
    def _():
        gather(pos_ref, 0)

        @pl.when(n > 1)
        def _():
            gather(pos1_ref, 1)

    pltpu.make_async_copy(ys_ref.at[pl.ds(0, TOP_K * tc * TOK_ROWS)], buf.at[slot], sems.at[slot]).wait()

    @pl.when(i + 2 < n)
    def _():
        for g in range(tc // G):
            for r in range(g * G, (g + 1) * G):
                start_rows(pos2_ref, slot2, r)
            combine_group(g)

    @pl.when(i + 2 >= n)
    def _():
        for g in range(tc // G):
            combine_group(g)


def _combine(pos, wts, h1, norm_f_w, ys3):
    T = h1.shape[0]
    tc = min(MOE_TC, T)
    nt = T // tc
    pos3 = pos.reshape(nt, 1, tc * TOP_K)
    ahead = lambda k: pl.BlockSpec((1, 1, tc * TOP_K), lambda i: (jnp.minimum(i + k, nt - 1), 0, 0), memory_space=pltpu.SMEM)
    return pl.pallas_call(
        _combine_kernel,
        grid=(nt,),
        in_specs=[
            ahead(0), ahead(1), ahead(2),
            pl.BlockSpec((tc, ROUTE_PAD), lambda i: (i, 0)),
            pl.BlockSpec((tc, D_MODEL), lambda i: (i, 0)),
            _const_spec((1, D_MODEL)),
            pl.BlockSpec(memory_space=pl.ANY),
        ],
        out_specs=pl.BlockSpec((tc, D_MODEL), lambda i: (i, 0)),
        out_shape=jax.ShapeDtypeStruct((T, D_MODEL), F32),
        scratch_shapes=[pltpu.VMEM((COMBINE_SLOTS, TOP_K * tc * TOK_ROWS, LANES), F32),
                        pltpu.SemaphoreType.DMA((COMBINE_SLOTS,))],
        compiler_params=_cparams("arbitrary"),
        name="moe_combine",
    )(pos3, pos3, pos3, wts, h1, norm_f_w.reshape(1, D_MODEL), ys3)


def _route_plan(top_idx, T):
    tm = MOE_TM
    nt = (T * TOP_K) // tm + N_EXPERTS
    e = top_idx.reshape(-1)
    oh = (e[:, None] == jnp.arange(N_EXPERTS, dtype=jnp.int32)[None, :]).astype(jnp.int32)
    csum = jnp.cumsum(oh, axis=0)
    rank = jnp.sum(csum * oh, axis=1) - 1
    counts = csum[-1]
    tiles = (counts + tm - 1) // tm
    tile_end = jnp.cumsum(tiles)
    offs = (tile_end - tiles) * tm
    pos = jnp.sum(oh * offs[None, :], axis=1) + rank
    tile_id = jnp.arange(nt, dtype=jnp.int32)
    tile_expert = jnp.minimum(jnp.sum((tile_end[None, :] <= tile_id[:, None]).astype(jnp.int32), axis=1),
                              N_EXPERTS - 1).astype(jnp.int32)
    tail = tile_end[-1] + jnp.arange(N_EXPERTS, dtype=jnp.int32)
    last_tile = jnp.concatenate([jnp.where(tiles > 0, tile_end - 1, -1), jnp.where(tail < nt, tail, -1)]).astype(jnp.int32)
    n_used = tile_end[-1:].astype(jnp.int32)
    return pos.astype(jnp.int32), tile_expert, last_tile, n_used, nt * tm


def kernel(x, norm1_w, w_in, gla_gate_w2, gla_gate_b, gla_norm_w, hy_conv_w, hy_conv_b, hy_freq, hy_ffn_w1, hy_ffn_b1,
           hy_ffn_w2, hy_ffn_b2, hy_ffn_w3, hy_decay, hy_bias, w_branch_gla, w_branch_hy, w_out, norm2_w, router_w,
           router_b, w_gate_up, b_gate_up, w_down, b_down, norm_f_w):
    B, L, D = x.shape
    T = B * L
    x2 = x.reshape(T, D)

    s_a = N_QKVR
    s_h = s_a + 2 * GLA_RANK
    w_cat = jnp.concatenate(
        [w_in[:, :s_a], w_in[:, s_h:], w_in[:, s_a:s_h], jnp.zeros((D, ALOW_PAD - 2 * GLA_RANK), w_in.dtype)],
        axis=1).astype(BF16)
    qkvr, hy, gates, alow = _in_proj(x2, norm1_w, w_cat)

    w2 = jnp.zeros((2, ALOW_PAD, GLA_KEY), F32)
    w2 = w2.at[0, :GLA_RANK].set(gla_gate_w2[0]).at[1, GLA_RANK:2 * GLA_RANK].set(gla_gate_w2[1])
    o_gla = _gla(qkvr, alow, w2, gla_gate_b.reshape(2, 1, GLA_KEY), gla_norm_w, B, L)

    taps = _hy_filter(L, hy_freq, hy_ffn_w1, hy_ffn_b1, hy_ffn_w2, hy_ffn_b2, hy_ffn_w3, hy_decay)
    o_hy = _hyena(hy, hy_conv_w, hy_conv_b, hy_bias, taps, B, L)

    rw_f = jnp.zeros((D, ROUTE_PAD), F32).at[:, :N_EXPERTS].set(router_w)
    rw_hi = rw_f.astype(BF16)
    rw_pad = jnp.concatenate([rw_hi, (rw_f - rw_hi.astype(F32)).astype(BF16)], axis=1)
    rb_pad = jnp.zeros((1, ROUTE_PAD), F32).at[0, :N_EXPERTS].set(router_b)
    h1, hn3, idx_pad, wts_pad = _merge(o_gla, o_hy, gates, x2, w_branch_gla.astype(BF16), w_branch_hy.astype(BF16),
                                       w_out.astype(BF16), norm2_w, rw_pad, rb_pad)

    pos, tile_expert, last_tile, n_used, P = _route_plan(idx_pad[:, :TOP_K], T)
    xs3 = _dispatch(hn3, pos, last_tile, P)
    ys3 = _experts(xs3, tile_expert, n_used, w_gate_up, b_gate_up, w_down, b_down)
    out = _combine(pos, wts_pad, h1, norm_f_w, ys3)
    return out.reshape(B, L, D)
```

```python
import functools
import math

import numpy as np
import jax
import jax.numpy as jnp
from jax import lax
from jax.experimental import pallas as pl
from jax.experimental.pallas import tpu as pltpu

D_MODEL = 1024
EPS = 1e-6
GLA_HEADS = 4
GLA_DK = 128
GLA_DV = 256
GLA_KEY = GLA_HEADS * GLA_DK
GLA_VAL = GLA_HEADS * GLA_DV
GLA_RANK = 16
GLA_GATE_TEMP = 16.0
GLA_CHUNK = 64
HY_WIDTH = D_MODEL
HY_BANDS = 16
HY_EMB = 1 + 2 * HY_BANDS
HY_FFN = 64
N_EXPERTS = 32
TOP_K = 4
D_FF = D_MODEL
SWIGLU_LIMIT = 7.0
SWIGLU_ALPHA = 1.702

LANES = 128
ALOW_PAD = LANES
ROUTE_PAD = LANES
VMEM_LIMIT = 56 * 1024 * 1024

F32 = jnp.float32
BF16 = jnp.bfloat16
HIGHEST = lax.Precision.HIGHEST


def _cparams(*sem):
    return pltpu.CompilerParams(dimension_semantics=sem, vmem_limit_bytes=VMEM_LIMIT)


def _const_spec(shape):
    nd = len(shape)
    return pl.BlockSpec(shape, lambda *_: (0,) * nd, pipeline_mode=pl.Buffered(1))


TOK_ROWS = D_MODEL // LANES


def _store_token_rows(ref, val):
    n = val.shape[0]
    for s_ in range(TOK_ROWS):
        ref[pl.ds(s_, n, stride=TOK_ROWS), :] = val[:, s_ * LANES:(s_ + 1) * LANES]


def _load_token_rows(ref, first, n):
    return jnp.concatenate(
        [ref[pl.ds(first * TOK_ROWS + s_, n, stride=TOK_ROWS), :] for s_ in range(TOK_ROWS)], axis=1)


N_QKVR = 2 * GLA_KEY + 2 * GLA_VAL
N_HY = 3 * HY_WIDTH
N_GATES = 2 * D_MODEL
IN_COLS = N_QKVR + N_HY + N_GATES + ALOW_PAD


def _inproj_kernel(x_ref, nw_ref, w_ref, qkvr_ref, hy_ref, gates_ref, alow_ref):
    x = x_ref[...]
    ms = jnp.mean(x * x, axis=-1, keepdims=True)
    xn = (x * lax.rsqrt(ms + EPS) * nw_ref[...]).astype(BF16)
    c0, c1, c2 = N_QKVR, N_QKVR + N_HY, N_QKVR + N_HY + N_GATES
    qkvr_ref[...] = jnp.dot(xn, w_ref[:, :c0], preferred_element_type=F32).astype(BF16)
    hy_ref[...] = jnp.dot(xn, w_ref[:, c0:c1], preferred_element_type=F32).astype(BF16)
    gates_ref[...] = jnp.dot(xn, w_ref[:, c1:c2], preferred_element_type=F32).astype(BF16)
    alow_ref[...] = jnp.dot(xn, w_ref[:, c2:], preferred_element_type=F32)


def _in_proj(x2, norm1_w, w_cat):
    T = x2.shape[0]
    tm = min(512, T)
    return pl.pallas_call(
        _inproj_kernel,
        grid=(T // tm,),
        in_specs=[
            pl.BlockSpec((tm, D_MODEL), lambda i: (i, 0)),
            _const_spec((1, D_MODEL)),
            _const_spec((D_MODEL, IN_COLS)),
        ],
        out_specs=[
            pl.BlockSpec((tm, N_QKVR), lambda i: (i, 0)),
            pl.BlockSpec((tm, N_HY), lambda i: (i, 0)),
            pl.BlockSpec((tm, N_GATES), lambda i: (i, 0)),
            pl.BlockSpec((tm, ALOW_PAD), lambda i: (i, 0)),
        ],
        out_shape=[
            jax.ShapeDtypeStruct((T, N_QKVR), BF16),
            jax.ShapeDtypeStruct((T, N_HY), BF16),
            jax.ShapeDtypeStruct((T, N_GATES), BF16),
            jax.ShapeDtypeStruct((T, ALOW_PAD), F32),
        ],
        compiler_params=_cparams("parallel"),
        name="in_proj",
    )(x2, norm1_w.reshape(1, D_MODEL), w_cat)


GLA_PREP_ROWS = 256
GLA_PREP_UNROLL = 4
GLA_SCAN_UNROLL = 16


def _log_sigmoid(x):
    return jnp.minimum(x, 0.0) - jnp.log1p(jnp.exp(-jnp.abs(x)))


def _gla_kernel(q_ref, k_ref, v_ref, r_ref, alow_ref, w2_ref, gb_ref, nw_ref, o_ref,
                qtf, kef, qtb, keb, decf, decb, of_s, ob_s, stf, stb):
    L = q_ref.shape[0]
    C = GLA_CHUNK
    DK = GLA_DK
    n_chunks = L // C
    R = min(GLA_PREP_ROWS, L)
    cpr = R // C
    scale = DK ** -0.5
    nt = (((1,), (1,)), ((), ()))
    tn = (((0,), (0,)), ((), ()))

    ri = lax.broadcasted_iota(jnp.int32, (R, R), 0)
    ci = lax.broadcasted_iota(jnp.int32, (R, R), 1)
    same = (ri // C) == (ci // C)
    mask_f = same & (ri >= ci)
    mask_b = same & (ri < ci)
    tri = jnp.where(mask_f, 1.0, 0.0).astype(BF16)
    w2 = jnp.concatenate([w2_ref[0], w2_ref[1]], axis=1).astype(BF16)
    gb = jnp.concatenate([gb_ref[0], gb_ref[1]], axis=1)

    def prep(g, carry):
        rows = pl.ds(pl.multiple_of(g * R, R), R)
        logits = jnp.dot(alow_ref[rows, :].astype(BF16), w2, preferred_element_type=F32) + gb
        lg = _log_sigmoid(logits) * (1.0 / GLA_GATE_TEMP)
        hi = lg.astype(BF16)
        lo = (lg - hi.astype(F32)).astype(BF16)
        b = jnp.dot(tri, hi, preferred_element_type=F32) + jnp.dot(tri, lo, preferred_element_type=F32)
        tot = jnp.concatenate(
            [jnp.broadcast_to(b[c * C + C - 1:c * C + C, :], (C, 2 * DK)) for c in range(cpr)], axis=0)
        q = q_ref[rows, :].astype(F32) * scale
        k = k_ref[rows, :].astype(F32)
        vb = v_ref[rows, :]
        dtot = jnp.exp(tot)
        for d, (qt, ke, dec, mask, out) in enumerate(((qtf, kef, decf, mask_f, of_s), (qtb, keb, decb, mask_b, ob_s))):
            sl = slice(d * DK, (d + 1) * DK)
            bd, td, ld = b[:, sl], tot[:, sl], lg[:, sl]
            if d == 0:
                e_q, e_end = bd, td - bd
            else:
                e_q, e_end = td - bd + ld, bd - ld
            qtv = (q * jnp.exp(e_q)).astype(BF16)
            ktv = (k * jnp.exp(-e_q)).astype(BF16)
            qt[rows, :] = qtv
            ke[rows, :] = (k * jnp.exp(e_end)).astype(BF16)
            s = lax.dot_general(qtv, ktv, nt, preferred_element_type=F32)
            s = jnp.where(mask, s, 0.0).astype(BF16)
            out[rows, :] = jnp.dot(s, vb, preferred_element_type=F32)
            for c in range(cpr):
                dec[pl.ds(g * cpr + c, 1), :] = dtot[c * C:c * C + 1, sl]
        return carry

    lax.fori_loop(0, L // R, prep, 0, unroll=min(GLA_PREP_UNROLL, L // R))

    def chunk(n, st, qt, ke, dec, out):
        rows = pl.ds(pl.multiple_of(n * C, C), C)
        s = st[...]
        out[rows, :] += lax.dot_general(qt[rows, :], s.astype(BF16), nt, preferred_element_type=F32)
        upd = lax.dot_general(v_ref[rows, :], ke[rows, :], tn, preferred_element_type=F32)
        st[...] = dec[pl.ds(n, 1), :] * s + upd

    def step(i, carry):
        chunk(i, stf, qtf, kef, decf, of_s)
        chunk(n_chunks - 1 - i, stb, qtb, keb, decb, ob_s)
        return carry

    stf[...] = jnp.zeros_like(stf)
    stb[...] = jnp.zeros_like(stb)
    lax.fori_loop(0, n_chunks, step, 0, unroll=min(GLA_SCAN_UNROLL, n_chunks))

    o = of_s[...] + ob_s[...]
    o = o * lax.rsqrt(jnp.mean(o * o, axis=-1, keepdims=True) + EPS) * nw_ref[...]
    r = r_ref[...].astype(F32)
    o_ref[...] = (o * (r * jax.nn.sigmoid(r))).astype(BF16)


def _gla(qkvr, alow, w2, gb, norm_w, B, L):
    T = B * L
    H = GLA_HEADS
    kq = GLA_KEY // GLA_DK
    kv = 2 * GLA_KEY // GLA_DV
    kr = (2 * GLA_KEY + GLA_VAL) // GLA_DV
    return pl.pallas_call(
        _gla_kernel,
        grid=(B, H),
        in_specs=[
            pl.BlockSpec((L, GLA_DK), lambda b, h: (b, h)),
            pl.BlockSpec((L, GLA_DK), lambda b, h: (b, kq + h)),
            pl.BlockSpec((L, GLA_DV), lambda b, h: (b, kv + h)),
            pl.BlockSpec((L, GLA_DV), lambda b, h: (b, kr + h)),
            pl.BlockSpec((L, ALOW_PAD), lambda b, h: (b, 0)),
            pl.BlockSpec((2, ALOW_PAD, GLA_DK), lambda b, h: (0, 0, h)),
            pl.BlockSpec((2, 1, GLA_DK), lambda b, h: (0, 0, h)),
            pl.BlockSpec((1, GLA_DV), lambda b, h: (0, 0)),
        ],
        out_specs=pl.BlockSpec((L, GLA_DV), lambda b, h: (b, h)),
        out_shape=jax.ShapeDtypeStruct((T, GLA_VAL), BF16),
        scratch_shapes=[pltpu.VMEM((L, GLA_DK), BF16)] * 4
        + [pltpu.VMEM((L // GLA_CHUNK, GLA_DK), F32)] * 2
        + [pltpu.VMEM((L, GLA_DV), F32)] * 2
        + [pltpu.VMEM((GLA_DV, GLA_DK), F32)] * 2,
        compiler_params=_cparams("parallel", "parallel"),
        name="gla",
    )(qkvr, qkvr, qkvr, qkvr, alow, w2, gb, norm_w.reshape(1, GLA_DV))


HY_WT = 256


def _hy_filter_kernel(z_ref, t_ref, fr_ref, w1_ref, b1_ref, w2_ref, b2_ref, w3f_ref, w3b_ref, dec_ref, o_ref, h_s):
    @pl.when(pl.program_id(0) == 0)
    def _():
        fr = fr_ref[...]
        h1 = jnp.sin(fr * (jnp.dot(z_ref[...], w1_ref[...], preferred_element_type=F32, precision=HIGHEST) + b1_ref[...]))
        h_s[...] = jnp.sin(fr * (jnp.dot(h1, w2_ref[...], preferred_element_type=F32, precision=HIGHEST) + b2_ref[...]))

    h = h_s[...]
    t = t_ref[...]
    row = lax.broadcasted_iota(jnp.int32, (z_ref.shape[0], 1), 0)
    hq = h.astype(BF16)
    hf = jnp.dot(hq, w3f_ref[...].astype(BF16), preferred_element_type=F32)
    hf = hf * jnp.exp(-t * jnp.abs(dec_ref[0:1, :]))
    hb = jnp.dot(hq, w3b_ref[...].astype(BF16), preferred_element_type=F32)
    hb = hb * jnp.exp(-t * jnp.abs(dec_ref[1:2, :]))
    hb = jnp.where(row == 0, 0.0, hb)
    ss = jnp.sum(hf * hf, axis=0, keepdims=True) + jnp.sum(hb * hb, axis=0, keepdims=True)
    inv = lax.rsqrt(ss + EPS)
    o_ref[0] = hf * inv
    o_ref[1] = hb * inv


def _hy_filter(L, hy_freq, w1, b1, w2, b2, w3, hy_decay):
    W = HY_WIDTH
    t = np.linspace(0.0, 1.0, L, dtype=np.float32)[:, None]
    omega = (np.float32(2.0 * math.pi) * np.arange(L, dtype=np.float32) / np.float32(L)).astype(np.float32)
    f = np.linspace(1e-4, HY_BANDS - 1, HY_BANDS, dtype=np.float32)
    ang = (omega[:, None] * f[None, :]).astype(np.float32)
    z = np.concatenate([t, np.cos(ang), -np.sin(ang)], axis=-1).astype(np.float32)
    nw = W // HY_WT
    return pl.pallas_call(
        _hy_filter_kernel,
        grid=(nw,),
        in_specs=[
            _const_spec((L, HY_EMB)),
            _const_spec((L, 1)),
            _const_spec((1, HY_FFN)),
            _const_spec((HY_EMB, HY_FFN)),
            _const_spec((1, HY_FFN)),
            _const_spec((HY_FFN, HY_FFN)),
            _const_spec((1, HY_FFN)),
            pl.BlockSpec((HY_FFN, HY_WT), lambda j: (0, j)),
            pl.BlockSpec((HY_FFN, HY_WT), lambda j: (0, nw + j)),
            pl.BlockSpec((2, HY_WT), lambda j: (0, j)),
        ],
        out_specs=pl.BlockSpec((2, L, HY_WT), lambda j: (0, 0, j)),
        out_shape=jax.ShapeDtypeStruct((2, L, W), F32),
        scratch_shapes=[pltpu.VMEM((L, HY_FFN), F32)],
        compiler_params=_cparams("arbitrary"),
        name="hy_filter",
    )(jnp.asarray(z), jnp.asarray(t), hy_freq.reshape(1, HY_FFN), w1, b1.reshape(1, HY_FFN), w2,
      b2.reshape(1, HY_FFN), w3, w3, hy_decay)


FFT_L = LANES
FFT_PITCH = FFT_L + 8
FFT_UNROLL = 8


def _fft_tables(L):
    N = 2 * L
    C = L // FFT_L
    n1 = N // FFT_L
    kh = min(n1, -(-(n1 // 2 + 1) // 8) * 8)
    l = np.arange(FFT_L)
    c = np.arange(C)
    k2 = np.arange(kh)
    pair_w = np.where((k2 == 0) | (k2 == n1 // 2), 1.0, np.where(k2 < n1 // 2, 2.0, 0.0))
    ang1 = 2 * np.pi * ((k2[:, None] * c[None, :]) % n1) / n1
    f1 = np.concatenate([np.cos(ang1), -np.sin(ang1)], axis=0)
    g1 = (np.stack([np.cos(ang1), -np.sin(ang1)], axis=-1) * pair_w[:, None, None]).transpose(1, 0, 2).reshape(C, 2 * kh) / N
    ang2 = 2 * np.pi * (((l[None, :, None] * l[None, None, :]) % FFT_L) / FFT_L
                        + ((k2[:, None, None] * l[None, None, :]) % N) / N)
    cs, sn = np.cos(ang2), np.sin(ang2)
    f2 = np.concatenate([np.concatenate([cs, sn], axis=2), np.concatenate([-sn, cs], axis=2)], axis=1)
    cst, snt = cs.transpose(0, 2, 1), sn.transpose(0, 2, 1)
    g2 = np.concatenate([np.concatenate([cst, -snt], axis=2), np.concatenate([snt, cst], axis=2)], axis=1)
    bf = lambda a: jnp.asarray(a.astype(np.float32)).astype(BF16)
    return dict(C=C, n1=kh, f1=bf(f1), g1=bf(g1), f2=bf(f2), g2=bf(g2))


def _fft_kernel(z_ref, f1_ref, f2_ref, g2_ref, kf_ref, g1_ref, o_ref, a_s, b_s, y_s, *, spectrum_only, epilogue=None,
                z_pitch=FFT_L):
    C = f1_ref.shape[1]
    n1 = f1_ref.shape[0] // 2
    P = FFT_PITCH
    blk = lambda j: pl.ds(pl.multiple_of(j * P, 8), FFT_L)

    def stage1(j, carry):
        x = jnp.concatenate([z_ref[pl.ds(2 * j, C, stride=z_pitch), :], z_ref[pl.ds(2 * j + 1, C, stride=z_pitch), :]], axis=1)
        r = jnp.dot(f1_ref[...], x.astype(BF16), preferred_element_type=F32)
        a_s[pl.ds(pl.multiple_of(2 * j * P, 8), 2 * n1), :] = r[:, :LANES]
        a_s[pl.ds(pl.multiple_of((2 * j + 1) * P, 8), 2 * n1), :] = r[:, LANES:]
        return carry
    lax.fori_loop(0, FFT_L // 2, stage1, 0, unroll=FFT_UNROLL)

    def stage2(k2, carry, *, accumulate=False):
        ar = a_s[pl.ds(k2, FFT_L, stride=P), :]
        ai = a_s[pl.ds(n1 + k2, FFT_L, stride=P), :]
        x = jnp.concatenate([ar, ai], axis=0).astype(BF16)
        s = jnp.dot(f2_ref[k2], x, preferred_element_type=F32)
        xr, xi = s[:FFT_L], s[FFT_L:]
        if not spectrum_only:
            kr, ki = kf_ref[k2, 0], kf_ref[k2, 1]
            y = jnp.concatenate([xr * kr - xi * ki, xr * ki + xi * kr], axis=0).astype(BF16)
            s = jnp.dot(g2_ref[k2], y, preferred_element_type=F32)
            b_s[blk(2 * k2), :] = s[:FFT_L]
            b_s[blk(2 * k2 + 1), :] = s[FFT_L:]
        elif accumulate:
            o_ref[k2, 0] += xr
            o_ref[k2, 1] -= xi
        else:
            o_ref[k2, 0] = xr
            o_ref[k2, 1] = xi
        return carry

    if spectrum_only:
        @pl.when(pl.program_id(1) == 0)
        def _():
            lax.fori_loop(0, n1, stage2, 0, unroll=FFT_UNROLL)

        @pl.when(pl.program_id(1) != 0)
        def _():
            lax.fori_loop(0, n1, functools.partial(stage2, accumulate=True), 0, unroll=FFT_UNROLL)
    else:
        lax.fori_loop(0, n1, stage2, 0, unroll=FFT_UNROLL)

    if not spectrum_only:
        def stage3(j, carry):
            x = jnp.concatenate([b_s[pl.ds(2 * j, 2 * n1, stride=P), :], b_s[pl.ds(2 * j + 1, 2 * n1, stride=P), :]], axis=1)
            r = jnp.dot(g1_ref[...], x.astype(BF16), preferred_element_type=F32)
            y_s[pl.ds(2 * j, C, stride=P), :] = r[:, :LANES]
            y_s[pl.ds(2 * j + 1, C, stride=P), :] = r[:, LANES:]
            return carry
        lax.fori_loop(0, FFT_L // 2, stage3, 0, unroll=FFT_UNROLL)
        for c in range(C):
            rows = slice(c * FFT_L, (c + 1) * FFT_L)
            o_ref[rows, :] = epilogue(c, y_s[c * P:c * P + FFT_L, :])


def _hyena_kernel(x0_ref, x1_ref, v_ref, w0_ref, w1_ref, wv_ref, b0_ref, b1_ref, bv_ref, hb_ref,
                  f1_ref, f2_ref, g2_ref, kf_ref, g1_ref, o_ref, z_s, x0_s, a_s, b_s, y_s):
    L = x0_ref.shape[0]
    row = lax.broadcasted_iota(jnp.int32, (L, 1), 0)

    def conv3(u_ref, w_ref, b_ref):
        u = u_ref[...].astype(F32)
        prev = jnp.where(row == 0, 0.0, pltpu.roll(u, 1, 0))
        nxt = jnp.where(row == L - 1, 0.0, pltpu.roll(u, L - 1, 0))
        return w_ref[0:1, :] * prev + w_ref[1:2, :] * u + w_ref[2:3, :] * nxt + b_ref[...]

    P = FFT_PITCH
    chunk = lambda c: slice(c * P, c * P + FFT_L)
    z = conv3(v_ref, wv_ref, bv_ref) * conv3(x1_ref, w1_ref, b1_ref)
    for c in range(L // FFT_L):
        z_s[chunk(c), :] = z[c * FFT_L:(c + 1) * FFT_L, :]
    x0_s[...] = conv3(x0_ref, w0_ref, b0_ref)

    def epilogue(c, y):
        return ((y + z_s[chunk(c), :] * hb_ref[...]) * x0_s[c * FFT_L:(c + 1) * FFT_L, :]).astype(BF16)

    _fft_kernel(z_s, f1_ref, f2_ref, g2_ref, kf_ref, g1_ref, o_ref, a_s, b_s, y_s, spectrum_only=False, epilogue=epilogue,
                z_pitch=P)


def _fft_specs(tabs):
    C, n1 = tabs["C"], tabs["n1"]
    specs = [_const_spec((2 * n1, C)),
             _const_spec((n1, 2 * FFT_L, 2 * FFT_L)), _const_spec((n1, 2 * FFT_L, 2 * FFT_L)),
             pl.BlockSpec((n1, 2, FFT_L, LANES), lambda w, b: (0, 0, 0, w)),
             _const_spec((C, 2 * n1))]
    scratch = [pltpu.VMEM((FFT_L * FFT_PITCH, LANES), F32), pltpu.VMEM((2 * n1 * FFT_PITCH, LANES), F32),
               pltpu.VMEM((C * FFT_PITCH, LANES), F32)]
    return specs, scratch


def _filter_spectrum(tabs, taps):
    _, L, W = taps.shape
    n1 = tabs["n1"]
    specs, scratch = _fft_specs(tabs)
    unused = jnp.zeros((n1, 2, FFT_L, W), F32)
    return pl.pallas_call(
        functools.partial(_fft_kernel, spectrum_only=True),
        grid=(W // LANES, 2),
        in_specs=[pl.BlockSpec((None, L, LANES), lambda w, b: (b, 0, w))] + specs,
        out_specs=pl.BlockSpec((n1, 2, FFT_L, LANES), lambda w, b: (0, 0, 0, w)),
        out_shape=jax.ShapeDtypeStruct((n1, 2, FFT_L, W), F32),
        scratch_shapes=scratch,
        compiler_params=_cparams("arbitrary", "arbitrary"),
        name="fft_spectrum",
    )(taps, tabs["f1"], tabs["f2"], tabs["g2"], unused, tabs["g1"])


def _hyena(hy, conv_w, conv_b, hy_bias, taps, B, L):
    W = HY_WIDTH
    nw = W // LANES
    tabs = _fft_tables(L)
    kf = _filter_spectrum(tabs, taps)
    specs, scratch = _fft_specs(tabs)
    cb = conv_b.reshape(1, 3 * W)
    blk = lambda off: pl.BlockSpec((L, LANES), lambda w, b: (b, off + w))
    wblk = lambda off: pl.BlockSpec((3, LANES), lambda w, b: (0, off + w))
    bblk = lambda off: pl.BlockSpec((1, LANES), lambda w, b: (0, off + w))
    return pl.pallas_call(
        _hyena_kernel,
        grid=(nw, B),
        in_specs=[blk(0), blk(nw), blk(2 * nw), wblk(0), wblk(nw), wblk(2 * nw), bblk(0), bblk(nw), bblk(2 * nw),
                  bblk(0)] + specs,
        out_specs=pl.BlockSpec((L, LANES), lambda w, b: (b, w)),
        out_shape=jax.ShapeDtypeStruct((B * L, W), BF16),
        scratch_shapes=[pltpu.VMEM((L // FFT_L * FFT_PITCH, LANES), F32), pltpu.VMEM((L, LANES), F32)] + scratch,
        compiler_params=_cparams("arbitrary", "arbitrary"),
        name="hyena",
    )(hy, hy, hy, conv_w, conv_w, conv_w, cb, cb, cb, hy_bias.reshape(1, W), tabs["f1"], tabs["f2"], tabs["g2"], kf,
      tabs["g1"])


def _merge_kernel(og_ref, oh_ref, gates_ref, x_ref, wbg_ref, wbh_ref, wo_ref, n2_ref, rw_ref, rb_ref,
                  h1_ref, hn3_ref, idx_ref, wts_ref):
    mg = jnp.dot(og_ref[...], wbg_ref[...], preferred_element_type=F32)
    mh = jnp.dot(oh_ref[...], wbh_ref[...], preferred_element_type=F32)
    gg = jax.nn.sigmoid(gates_ref[:, :D_MODEL].astype(F32))
    gh = jax.nn.sigmoid(gates_ref[:, D_MODEL:].astype(F32))
    merged = (gg * mg + gh * mh).astype(BF16)
    h1 = x_ref[...] + jnp.dot(merged, wo_ref[...], preferred_element_type=F32)
    h1_ref[...] = h1
    hn = h1 * lax.rsqrt(jnp.mean(h1 * h1, axis=-1, keepdims=True) + EPS) * n2_ref[...]
    _store_token_rows(hn3_ref, hn)
    hh = hn.astype(BF16)
    hl = (hn - hh.astype(F32)).astype(BF16)
    p1 = jnp.dot(hh, rw_ref[...], preferred_element_type=F32)
    p2 = jnp.dot(hl, rw_ref[:, :ROUTE_PAD], preferred_element_type=F32)
    logits = p1[:, :ROUTE_PAD] + p1[:, ROUTE_PAD:] + p2 + rb_ref[...]
    lane = lax.broadcasted_iota(jnp.int32, logits.shape, 1)
    logits = jnp.where(lane < N_EXPERTS, logits, -jnp.inf)
    idx_out = jnp.zeros(logits.shape, jnp.int32)
    val_out = jnp.zeros(logits.shape, F32)
    v0 = None
    for j in range(TOP_K):
        m = jnp.max(logits, axis=-1, keepdims=True)
        sel = jnp.min(jnp.where(logits == m, lane, ROUTE_PAD), axis=-1, keepdims=True)
        if j == 0:
            v0 = m
        idx_out = jnp.where(lane == j, sel, idx_out)
        val_out = jnp.where(lane == j, jnp.exp(m - v0), val_out)
        logits = jnp.where(lane == sel, -jnp.inf, logits)
    idx_ref[...] = idx_out[:, :TOP_K]
    wts_ref[...] = val_out / jnp.sum(val_out, axis=-1, keepdims=True)


def _merge(o_gla, o_hy, gates, x2, wbg, wbh, wo, norm2_w, rw_pad, rb_pad):
    T = x2.shape[0]
    tm = min(512, T)
    row = lambda n: pl.BlockSpec((tm, n), lambda i: (i, 0))
    return pl.pallas_call(
        _merge_kernel,
        grid=(T // tm,),
        in_specs=[row(GLA_VAL), row(HY_WIDTH), row(N_GATES), row(D_MODEL),
                  _const_spec((GLA_VAL, D_MODEL)), _const_spec((HY_WIDTH, D_MODEL)), _const_spec((D_MODEL, D_MODEL)),
                  _const_spec((1, D_MODEL)), _const_spec((D_MODEL, 2 * ROUTE_PAD)), _const_spec((1, ROUTE_PAD))],
        out_specs=[row(D_MODEL), pl.BlockSpec((tm * TOK_ROWS, LANES), lambda i: (i, 0)), row(TOP_K), row(ROUTE_PAD)],
        out_shape=[jax.ShapeDtypeStruct((T, D_MODEL), F32), jax.ShapeDtypeStruct((T * TOK_ROWS, LANES), F32),
                   jax.ShapeDtypeStruct((T, TOP_K), jnp.int32), jax.ShapeDtypeStruct((T, ROUTE_PAD), F32)],
        compiler_params=_cparams("parallel"),
        name="merge_route",
    )(o_gla, o_hy, gates, x2, wbg, wbh, wo, norm2_w.reshape(1, D_MODEL), rw_pad, rb_pad)


MOE_TM = 512
MOE_TC = 256
DMA_UNROLL = 8


DISPATCH_SLOTS = 3


def _dispatch_kernel(last_ref, pos_ref, hn_ref, xs_ref, zero_s, buf, sem, lsems, ssems):
    i = pl.program_id(0)
    n = pl.num_programs(0)
    tc = pos_ref.shape[2] // TOP_K
    rows = tc * TOK_ROWS
    zrows = zero_s.shape[0]
    slot = lax.rem(i, DISPATCH_SLOTS)
    prev_slot = lax.rem(i + DISPATCH_SLOTS - 1, DISPATCH_SLOTS)

    def load(blk, s):
        src = hn_ref.at[pl.ds(pl.multiple_of(blk * rows, rows), rows)]
        return pltpu.make_async_copy(src, buf.at[s], lsems.at[s])

    def wait_copies(s):
        for _ in range(TOP_K):
            pltpu.make_async_copy(buf.at[s], xs_ref.at[pl.ds(0, rows)], ssems.at[s]).wait()

    @pl.when(i == 0)
    def _():
        load(0, 0).start()

        @pl.when(n > 1)
        def _():
            load(1, 1).start()

        zero_s[...] = jnp.zeros_like(zero_s)
        n_fill = last_ref.shape[0]

        def zfill(e, c):
            @pl.when(last_ref[e] >= 0)
            def _():
                dst = pl.multiple_of(last_ref[e] * zrows, zrows)
                pltpu.make_async_copy(zero_s, xs_ref.at[pl.ds(dst, zrows)], sem).start()
            return c
        lax.fori_loop(0, n_fill, zfill, 0)

        def zwait(e, c):
            @pl.when(last_ref[e] >= 0)
            def _():
                pltpu.make_async_copy(zero_s, xs_ref.at[pl.ds(0, zrows)], sem).wait()
            return c
        lax.fori_loop(0, n_fill, zwait, 0)

    load(i, slot).wait()

    def issue(r, c):
        src = buf.at[slot, pl.ds(pl.multiple_of(r * TOK_ROWS, TOK_ROWS), TOK_ROWS)]
        for j in range(TOP_K):
            dst = pl.multiple_of(pos_ref[0, 0, r * TOP_K + j] * TOK_ROWS, TOK_ROWS)
            pltpu.make_async_copy(src, xs_ref.at[pl.ds(dst, TOK_ROWS)], ssems.at[slot]).start(priority=j % 2)
        return c
    lax.fori_loop(0, tc, issue, 0, unroll=DMA_UNROLL // TOP_K)

    @pl.when(i > 0)
    def _():
        wait_copies(prev_slot)

    @pl.when(i + 2 < n)
    def _():
        load(i + 2, prev_slot).start()

    @pl.when(i == n - 1)
    def _():
        wait_copies(slot)


def _dispatch(hn3, pos, last_tile, P):
    T = hn3.shape[0] // TOK_ROWS
    tc = min(MOE_TC, T)
    nt = T // tc
    return pl.pallas_call(
        _dispatch_kernel,
        grid_spec=pltpu.PrefetchScalarGridSpec(
            num_scalar_prefetch=1,
            grid=(nt,),
            in_specs=[pl.BlockSpec((1, 1, tc * TOP_K), lambda i, lt: (i, 0, 0), memory_space=pltpu.SMEM),
                      pl.BlockSpec(memory_space=pl.ANY)],
            out_specs=pl.BlockSpec(memory_space=pl.ANY),
            scratch_shapes=[pltpu.VMEM((MOE_TM * TOK_ROWS, LANES), F32),
                            pltpu.VMEM((DISPATCH_SLOTS, tc * TOK_ROWS, LANES), F32),
                            pltpu.SemaphoreType.DMA(()),
                            pltpu.SemaphoreType.DMA((DISPATCH_SLOTS,)), pltpu.SemaphoreType.DMA((DISPATCH_SLOTS,))],
        ),
        out_shape=jax.ShapeDtypeStruct((P * TOK_ROWS, LANES), F32),
        compiler_params=_cparams("arbitrary"),
        name="moe_dispatch",
    )(last_tile, pos.reshape(nt, 1, tc * TOP_K), hn3)


def _expert_kernel(te_ref, nused_ref, xs_ref, wgu_ref, bgu_ref, wd_ref, bd_ref, o_ref, wgu_s, wd_s):
    i = pl.program_id(0)
    prev = te_ref[jnp.maximum(i - 1, 0)]

    @pl.when((i == 0) | (te_ref[i] != prev))
    def _():
        wgu_s[...] = wgu_ref[0].astype(BF16)
        wd_s[...] = wd_ref[0].astype(BF16)

    @pl.when(i < nused_ref[0])
    def _():
        x = _load_token_rows(xs_ref, 0, xs_ref.shape[0] // TOK_ROWS).astype(BF16)
        gu = jnp.dot(x, wgu_s[...], preferred_element_type=F32) + bgu_ref[0]
        gate = jnp.minimum(gu[:, :D_FF], SWIGLU_LIMIT)
        up = jnp.clip(gu[:, D_FF:], -SWIGLU_LIMIT, SWIGLU_LIMIT)
        act = gate * jax.nn.sigmoid(SWIGLU_ALPHA * gate) * (up + 1.0)
        y = jnp.dot(act.astype(BF16), wd_s[...], preferred_element_type=F32) + bd_ref[0]
        _store_token_rows(o_ref, y)

    @pl.when(i >= nused_ref[0])
    def _():
        o_ref[...] = jnp.zeros_like(o_ref)


def _experts(xs3, tile_expert, n_used, w_gate_up, b_gate_up, w_down, b_down):
    P = xs3.shape[0] // TOK_ROWS
    tm = MOE_TM
    nt = P // tm
    live = lambda i, te, nu: jnp.minimum(i, nu[0] - 1)
    return pl.pallas_call(
        _expert_kernel,
        grid_spec=pltpu.PrefetchScalarGridSpec(
            num_scalar_prefetch=2,
            grid=(nt,),
            in_specs=[
                pl.BlockSpec((tm * TOK_ROWS, LANES), lambda i, te, nu: (live(i, te, nu), 0)),
                pl.BlockSpec((1, D_MODEL, 2 * D_FF), lambda i, te, nu: (te[i], 0, 0)),
                pl.BlockSpec((1, 1, 2 * D_FF), lambda i, te, nu: (te[i], 0, 0)),
                pl.BlockSpec((1, D_FF, D_MODEL), lambda i, te, nu: (te[i], 0, 0)),
                pl.BlockSpec((1, 1, D_MODEL), lambda i, te, nu: (te[i], 0, 0)),
            ],
            out_specs=pl.BlockSpec((tm * TOK_ROWS, LANES), lambda i, te, nu: (i, 0)),
            scratch_shapes=[pltpu.VMEM((D_MODEL, 2 * D_FF), BF16), pltpu.VMEM((D_FF, D_MODEL), BF16)],
        ),
        out_shape=jax.ShapeDtypeStruct((P * TOK_ROWS, LANES), F32),
        compiler_params=_cparams("arbitrary"),
        name="moe_experts",
    )(tile_expert, n_used, xs3, w_gate_up, b_gate_up.reshape(N_EXPERTS, 1, 2 * D_FF), w_down,
      b_down.reshape(N_EXPERTS, 1, D_MODEL))


COMBINE_SLOTS = 3
COMBINE_GROUP = 8


def _combine_kernel(pos_ref, pos1_ref, pos2_ref, wts_ref, h1_ref, nf_ref, ys_ref, o_ref, buf, sems):
    i = pl.program_id(0)
    n = pl.num_programs(0)
    tc = o_ref.shape[0]
    G = COMBINE_GROUP
    slot = lax.rem(i, COMBINE_SLOTS)
    slot2 = lax.rem(i + 2, COMBINE_SLOTS)

    def start_rows(p_ref, s, r):
        for j in range(TOP_K):
            src = pl.multiple_of(p_ref[0, 0, r * TOP_K + j] * TOK_ROWS, TOK_ROWS)
            dst = pl.multiple_of((j * tc + r) * TOK_ROWS, TOK_ROWS)
            pltpu.make_async_copy(ys_ref.at[pl.ds(src, TOK_ROWS)], buf.at[s, pl.ds(dst, TOK_ROWS)],
                                  sems.at[s]).start(priority=j % 2)

    def gather(p_ref, s):
        def issue(r, c):
            start_rows(p_ref, s, r)
            return c
        lax.fori_loop(0, tc, issue, 0, unroll=DMA_UNROLL // TOP_K)

    def combine_group(g):
        rows = slice(g * G, (g + 1) * G)
        w = wts_ref[rows, :]
        h = h1_ref[rows, :]
        for j in range(TOP_K):
            h = h + w[:, j:j + 1] * _load_token_rows(buf.at[slot], j * tc + g * G, G)
        o_ref[rows, :] = h * lax.rsqrt(jnp.mean(h * h, axis=-1, keepdims=True) + EPS) * nf_ref[...]

    @pl.when(i == 0)
    def _():
        gather(pos_ref, 0)

        @pl.when(n > 1)
        def _():
            gather(pos1_ref, 1)

    pltpu.make_async_copy(ys_ref.at[pl.ds(0, TOP_K * tc * TOK_ROWS)], buf.at[slot], sems.at[slot]).wait()

    @pl.when(i + 2 < n)
    def _():
        for g in range(tc // G):
            for r in range(g * G, (g + 1) * G):
                start_rows(pos2_ref, slot2, r)
            combine_group(g)

    @pl.when(i + 2 >= n)
    def _():
        for g in range(tc // G):
            combine_group(g)


def _combine(pos, wts, h1, norm_f_w, ys3):
    T = h1.shape[0]
    tc = min(MOE_TC, T)
    nt = T // tc
    pos3 = pos.reshape(nt, 1, tc * TOP_K)
    ahead = lambda k: pl.BlockSpec((1, 1, tc * TOP_K), lambda i: (jnp.minimum(i + k, nt - 1), 0, 0), memory_space=pltpu.SMEM)
    return pl.pallas_call(
        _combine_kernel,
        grid=(nt,),
        in_specs=[
            ahead(0), ahead(1), ahead(2),
            pl.BlockSpec((tc, ROUTE_PAD), lambda i: (i, 0)),
            pl.BlockSpec((tc, D_MODEL), lambda i: (i, 0)),
            _const_spec((1, D_MODEL)),
            pl.BlockSpec(memory_space=pl.ANY),
        ],
        out_specs=pl.BlockSpec((tc, D_MODEL), lambda i: (i, 0)),
        out_shape=jax.ShapeDtypeStruct((T, D_MODEL), F32),
        scratch_shapes=[pltpu.VMEM((COMBINE_SLOTS, TOP_K * tc * TOK_ROWS, LANES), F32),
                        pltpu.SemaphoreType.DMA((COMBINE_SLOTS,))],
        compiler_params=_cparams("arbitrary"),
        name="moe_combine",
    )(pos3, pos3, pos3, wts, h1, norm_f_w.reshape(1, D_MODEL), ys3)


def _route_plan(top_idx, T):
    tm = MOE_TM
    nt = (T * TOP_K) // tm + N_EXPERTS
    oh = (top_idx[:, :, None] == jnp.arange(N_EXPERTS, dtype=jnp.int32)[None, None, :]).astype(jnp.int32)
    cnt = jnp.sum(oh, axis=1)
    csum = jnp.cumsum(cnt, axis=0)
    counts = csum[-1]
    tiles = (counts + tm - 1) // tm
    tile_end = jnp.cumsum(tiles)
    offs = (tile_end - tiles) * tm
    pos = jnp.sum(oh * (offs[None, :] + csum - cnt)[:, None, :], axis=2).reshape(-1)
    tile_id = jnp.arange(nt, dtype=jnp.int32)
    tile_expert = jnp.minimum(jnp.sum((tile_end[None, :] <= tile_id[:, None]).astype(jnp.int32), axis=1),
                              N_EXPERTS - 1).astype(jnp.int32)
    tail = tile_end[-1] + jnp.arange(N_EXPERTS, dtype=jnp.int32)
    last_tile = jnp.concatenate([jnp.where(tiles > 0, tile_end - 1, -1), jnp.where(tail < nt, tail, -1)]).astype(jnp.int32)
    n_used = tile_end[-1:].astype(jnp.int32)
    return pos.astype(jnp.int32), tile_expert, last_tile, n_used, nt * tm


def kernel(x, norm1_w, w_in, gla_gate_w2, gla_gate_b, gla_norm_w, hy_conv_w, hy_conv_b, hy_freq, hy_ffn_w1, hy_ffn_b1,
           hy_ffn_w2, hy_ffn_b2, hy_ffn_w3, hy_decay, hy_bias, w_branch_gla, w_branch_hy, w_out, norm2_w, router_w,
           router_b, w_gate_up, b_gate_up, w_down, b_down, norm_f_w):
    B, L, D = x.shape
    T = B * L
    x2 = x.reshape(T, D)

    s_a = N_QKVR
    s_h = s_a + 2 * GLA_RANK
    w_cat = jnp.concatenate(
        [w_in[:, :s_a], w_in[:, s_h:], w_in[:, s_a:s_h], jnp.zeros((D, ALOW_PAD - 2 * GLA_RANK), w_in.dtype)],
        axis=1).astype(BF16)
    qkvr, hy, gates, alow = _in_proj(x2, norm1_w, w_cat)

    w2 = jnp.zeros((2, ALOW_PAD, GLA_KEY), F32)
    w2 = w2.at[0, :GLA_RANK].set(gla_gate_w2[0]).at[1, GLA_RANK:2 * GLA_RANK].set(gla_gate_w2[1])
    o_gla = _gla(qkvr, alow, w2, gla_gate_b.reshape(2, 1, GLA_KEY), gla_norm_w, B, L)

    taps = _hy_filter(L, hy_freq, hy_ffn_w1, hy_ffn_b1, hy_ffn_w2, hy_ffn_b2, hy_ffn_w3, hy_decay)
    o_hy = _hyena(hy, hy_conv_w, hy_conv_b, hy_bias, taps, B, L)

    rw_f = jnp.zeros((D, ROUTE_PAD), F32).at[:, :N_EXPERTS].set(router_w)
    rw_hi = rw_f.astype(BF16)
    rw_pad = jnp.concatenate([rw_hi, (rw_f - rw_hi.astype(F32)).astype(BF16)], axis=1)
    rb_pad = jnp.zeros((1, ROUTE_PAD), F32).at[0, :N_EXPERTS].set(router_b)
    h1, hn3, top_idx, wts_pad = _merge(o_gla, o_hy, gates, x2, w_branch_gla.astype(BF16), w_branch_hy.astype(BF16),
                                       w_out.astype(BF16), norm2_w, rw_pad, rb_pad)

    pos, tile_expert, last_tile, n_used, P = _route_plan(top_idx, T)
    xs3 = _dispatch(hn3, pos, last_tile, P)
    ys3 = _experts(xs3, tile_expert, n_used, w_gate_up, b_gate_up, w_down, b_down)
    out = _combine(pos, wts_pad, h1, norm_f_w, ys3)
    return out.reshape(B, L, D)
```

```python
import functools
import math

import numpy as np
import jax
import jax.numpy as jnp
from jax import lax
from jax.experimental import pallas as pl
from jax.experimental.pallas import tpu as pltpu

D_MODEL = 1024
EPS = 1e-6
GLA_HEADS = 4
GLA_DK = 128
GLA_DV = 256
GLA_KEY = GLA_HEADS * GLA_DK
GLA_VAL = GLA_HEADS * GLA_DV
GLA_RANK = 16
GLA_GATE_TEMP = 16.0
GLA_CHUNK = 64
HY_WIDTH = D_MODEL
HY_BANDS = 16
HY_EMB = 1 + 2 * HY_BANDS
HY_FFN = 64
N_EXPERTS = 32
TOP_K = 4
D_FF = D_MODEL
SWIGLU_LIMIT = 7.0
SWIGLU_ALPHA = 1.702

LANES = 128
ALOW_PAD = LANES
ROUTE_PAD = LANES
VMEM_LIMIT = 56 * 1024 * 1024

F32 = jnp.float32
BF16 = jnp.bfloat16
HIGHEST = lax.Precision.HIGHEST


def _cparams(*sem):
    return pltpu.CompilerParams(dimension_semantics=sem, vmem_limit_bytes=VMEM_LIMIT)


def _const_spec(shape):
    nd = len(shape)
    return pl.BlockSpec(shape, lambda *_: (0,) * nd, pipeline_mode=pl.Buffered(1))


TOK_ROWS = D_MODEL // LANES


def _store_token_rows(ref, val):
    n = val.shape[0]
    for s_ in range(TOK_ROWS):
        ref[pl.ds(s_, n, stride=TOK_ROWS), :] = val[:, s_ * LANES:(s_ + 1) * LANES]


def _load_token_rows(ref, first, n):
    return jnp.concatenate(
        [ref[pl.ds(first * TOK_ROWS + s_, n, stride=TOK_ROWS), :] for s_ in range(TOK_ROWS)], axis=1)


N_QKVR = 2 * GLA_KEY + 2 * GLA_VAL
N_HY = 3 * HY_WIDTH
N_GATES = 2 * D_MODEL
IN_COLS = N_QKVR + N_HY + N_GATES + ALOW_PAD


def _inproj_kernel(x_ref, nw_ref, w_ref, qkvr_ref, hy_ref, gates_ref, alow_ref):
    x = x_ref[...]
    ms = jnp.mean(x * x, axis=-1, keepdims=True)
    xn = (x * lax.rsqrt(ms + EPS) * nw_ref[...]).astype(BF16)
    c0, c1, c2 = N_QKVR, N_QKVR + N_HY, N_QKVR + N_HY + N_GATES
    qkvr_ref[...] = jnp.dot(xn, w_ref[:, :c0], preferred_element_type=F32).astype(BF16)
    hy_ref[...] = jnp.dot(xn, w_ref[:, c0:c1], preferred_element_type=F32).astype(BF16)
    gates_ref[...] = jnp.dot(xn, w_ref[:, c1:c2], preferred_element_type=F32).astype(BF16)
    alow_ref[...] = jnp.dot(xn, w_ref[:, c2:], preferred_element_type=F32)


def _in_proj(x2, norm1_w, w_cat):
    T = x2.shape[0]
    tm = min(512, T)
    return pl.pallas_call(
        _inproj_kernel,
        grid=(T // tm,),
        in_specs=[
            pl.BlockSpec((tm, D_MODEL), lambda i: (i, 0)),
            _const_spec((1, D_MODEL)),
            _const_spec((D_MODEL, IN_COLS)),
        ],
        out_specs=[
            pl.BlockSpec((tm, N_QKVR), lambda i: (i, 0)),
            pl.BlockSpec((tm, N_HY), lambda i: (i, 0)),
            pl.BlockSpec((tm, N_GATES), lambda i: (i, 0)),
            pl.BlockSpec((tm, ALOW_PAD), lambda i: (i, 0)),
        ],
        out_shape=[
            jax.ShapeDtypeStruct((T, N_QKVR), BF16),
            jax.ShapeDtypeStruct((T, N_HY), BF16),
            jax.ShapeDtypeStruct((T, N_GATES), BF16),
            jax.ShapeDtypeStruct((T, ALOW_PAD), F32),
        ],
        compiler_params=_cparams("parallel"),
        name="in_proj",
    )(x2, norm1_w.reshape(1, D_MODEL), w_cat)


GLA_PREP_ROWS = 256
GLA_PREP_UNROLL = 4
GLA_SCAN_UNROLL = 16


def _log_sigmoid(x):
    return jnp.minimum(x, 0.0) - jnp.log1p(jnp.exp(-jnp.abs(x)))


def _gla_kernel(q_ref, k_ref, v_ref, r_ref, alow_ref, w2_ref, gb_ref, nw_ref, o_ref,
                qtf, kef, qtb, keb, decf, decb, of_s, ob_s, stf, stb):
    L = q_ref.shape[0]
    C = GLA_CHUNK
    DK = GLA_DK
    n_chunks = L // C
    R = min(GLA_PREP_ROWS, L)
    cpr = R // C
    scale = DK ** -0.5
    nt = (((1,), (1,)), ((), ()))
    tn = (((0,), (0,)), ((), ()))

    ri = lax.broadcasted_iota(jnp.int32, (R, R), 0)
    ci = lax.broadcasted_iota(jnp.int32, (R, R), 1)
    same = (ri // C) == (ci // C)
    mask_f = same & (ri >= ci)
    mask_b = same & (ri < ci)
    tri = jnp.where(mask_f, 1.0, 0.0).astype(BF16)
    w2 = jnp.concatenate([w2_ref[0], w2_ref[1]], axis=1).astype(BF16)
    gb = jnp.concatenate([gb_ref[0], gb_ref[1]], axis=1)

    def prep(g, carry):
        rows = pl.ds(pl.multiple_of(g * R, R), R)
        logits = jnp.dot(alow_ref[rows, :].astype(BF16), w2, preferred_element_type=F32) + gb
        lg = _log_sigmoid(logits) * (1.0 / GLA_GATE_TEMP)
        hi = lg.astype(BF16)
        lo = (lg - hi.astype(F32)).astype(BF16)
        b = jnp.dot(tri, hi, preferred_element_type=F32) + jnp.dot(tri, lo, preferred_element_type=F32)
        tot = jnp.concatenate(
            [jnp.broadcast_to(b[c * C + C - 1:c * C + C, :], (C, 2 * DK)) for c in range(cpr)], axis=0)
        q = q_ref[rows, :].astype(F32) * scale
        k = k_ref[rows, :].astype(F32)
        vb = v_ref[rows, :]
        dtot = jnp.exp(tot)
        for d, (qt, ke, dec, mask, out) in enumerate(((qtf, kef, decf, mask_f, of_s), (qtb, keb, decb, mask_b, ob_s))):
            sl = slice(d * DK, (d + 1) * DK)
            bd, td, ld = b[:, sl], tot[:, sl], lg[:, sl]
            if d == 0:
                e_q, e_end = bd, td - bd
            else:
                e_q, e_end = td - bd + ld, bd - ld
            qtv = (q * jnp.exp(e_q)).astype(BF16)
            ktv = (k * jnp.exp(-e_q)).astype(BF16)
            qt[rows, :] = qtv
            ke[rows, :] = (k * jnp.exp(e_end)).astype(BF16)
            s = lax.dot_general(qtv, ktv, nt, preferred_element_type=F32)
            s = jnp.where(mask, s, 0.0).astype(BF16)
            out[rows, :] = jnp.dot(s, vb, preferred_element_type=F32)
            for c in range(cpr):
                dec[pl.ds(g * cpr + c, 1), :] = dtot[c * C:c * C + 1, sl]
        return carry

    lax.fori_loop(0, L // R, prep, 0, unroll=min(GLA_PREP_UNROLL, L // R))

    def chunk(n, st, qt, ke, dec, out):
        rows = pl.ds(pl.multiple_of(n * C, C), C)
        s = st[...]
        out[rows, :] += lax.dot_general(qt[rows, :], s.astype(BF16), nt, preferred_element_type=F32)
        upd = lax.dot_general(v_ref[rows, :], ke[rows, :], tn, preferred_element_type=F32)
        st[...] = dec[pl.ds(n, 1), :] * s + upd

    def step(i, carry):
        chunk(i, stf, qtf, kef, decf, of_s)
        chunk(n_chunks - 1 - i, stb, qtb, keb, decb, ob_s)
        return carry

    stf[...] = jnp.zeros_like(stf)
    stb[...] = jnp.zeros_like(stb)
    lax.fori_loop(0, n_chunks, step, 0, unroll=min(GLA_SCAN_UNROLL, n_chunks))

    o = of_s[...] + ob_s[...]
    o = o * lax.rsqrt(jnp.mean(o * o, axis=-1, keepdims=True) + EPS) * nw_ref[...]
    r = r_ref[...].astype(F32)
    o_ref[...] = (o * (r * jax.nn.sigmoid(r))).astype(BF16)


def _gla(qkvr, alow, w2, gb, norm_w, B, L):
    T = B * L
    H = GLA_HEADS
    kq = GLA_KEY // GLA_DK
    kv = 2 * GLA_KEY // GLA_DV
    kr = (2 * GLA_KEY + GLA_VAL) // GLA_DV
    return pl.pallas_call(
        _gla_kernel,
        grid=(B, H),
        in_specs=[
            pl.BlockSpec((L, GLA_DK), lambda b, h: (b, h)),
            pl.BlockSpec((L, GLA_DK), lambda b, h: (b, kq + h)),
            pl.BlockSpec((L, GLA_DV), lambda b, h: (b, kv + h)),
            pl.BlockSpec((L, GLA_DV), lambda b, h: (b, kr + h)),
            pl.BlockSpec((L, ALOW_PAD), lambda b, h: (b, 0)),
            pl.BlockSpec((2, ALOW_PAD, GLA_DK), lambda b, h: (0, 0, h)),
            pl.BlockSpec((2, 1, GLA_DK), lambda b, h: (0, 0, h)),
            pl.BlockSpec((1, GLA_DV), lambda b, h: (0, 0)),
        ],
        out_specs=pl.BlockSpec((L, GLA_DV), lambda b, h: (b, h)),
        out_shape=jax.ShapeDtypeStruct((T, GLA_VAL), BF16),
        scratch_shapes=[pltpu.VMEM((L, GLA_DK), BF16)] * 4
        + [pltpu.VMEM((L // GLA_CHUNK, GLA_DK), F32)] * 2
        + [pltpu.VMEM((L, GLA_DV), F32)] * 2
        + [pltpu.VMEM((GLA_DV, GLA_DK), F32)] * 2,
        compiler_params=_cparams("parallel", "parallel"),
        name="gla",
    )(qkvr, qkvr, qkvr, qkvr, alow, w2, gb, norm_w.reshape(1, GLA_DV))


HY_WT = 256


def _hy_filter_kernel(z_ref, t_ref, fr_ref, w1_ref, b1_ref, w2_ref, b2_ref, w3f_ref, w3b_ref, dec_ref, o_ref, h_s):
    @pl.when(pl.program_id(0) == 0)
    def _():
        fr = fr_ref[...]
        h1 = jnp.sin(fr * (jnp.dot(z_ref[...], w1_ref[...], preferred_element_type=F32, precision=HIGHEST) + b1_ref[...]))
        h_s[...] = jnp.sin(fr * (jnp.dot(h1, w2_ref[...], preferred_element_type=F32, precision=HIGHEST) + b2_ref[...]))

    h = h_s[...]
    t = t_ref[...]
    row = lax.broadcasted_iota(jnp.int32, (z_ref.shape[0], 1), 0)
    hq = h.astype(BF16)
    hf = jnp.dot(hq, w3f_ref[...].astype(BF16), preferred_element_type=F32)
    hf = hf * jnp.exp(-t * jnp.abs(dec_ref[0:1, :]))
    hb = jnp.dot(hq, w3b_ref[...].astype(BF16), preferred_element_type=F32)
    hb = hb * jnp.exp(-t * jnp.abs(dec_ref[1:2, :]))
    hb = jnp.where(row == 0, 0.0, hb)
    ss = jnp.sum(hf * hf, axis=0, keepdims=True) + jnp.sum(hb * hb, axis=0, keepdims=True)
    inv = lax.rsqrt(ss + EPS)
    o_ref[0] = hf * inv
    o_ref[1] = hb * inv


def _hy_filter(L, hy_freq, w1, b1, w2, b2, w3, hy_decay):
    W = HY_WIDTH
    t = np.linspace(0.0, 1.0, L, dtype=np.float32)[:, None]
    omega = (np.float32(2.0 * math.pi) * np.arange(L, dtype=np.float32) / np.float32(L)).astype(np.float32)
    f = np.linspace(1e-4, HY_BANDS - 1, HY_BANDS, dtype=np.float32)
    ang = (omega[:, None] * f[None, :]).astype(np.float32)
    z = np.concatenate([t, np.cos(ang), -np.sin(ang)], axis=-1).astype(np.float32)
    nw = W // HY_WT
    return pl.pallas_call(
        _hy_filter_kernel,
        grid=(nw,),
        in_specs=[
            _const_spec((L, HY_EMB)),
            _const_spec((L, 1)),
            _const_spec((1, HY_FFN)),
            _const_spec((HY_EMB, HY_FFN)),
            _const_spec((1, HY_FFN)),
            _const_spec((HY_FFN, HY_FFN)),
            _const_spec((1, HY_FFN)),
            pl.BlockSpec((HY_FFN, HY_WT), lambda j: (0, j)),
            pl.BlockSpec((HY_FFN, HY_WT), lambda j: (0, nw + j)),
            pl.BlockSpec((2, HY_WT), lambda j: (0, j)),
        ],
        out_specs=pl.BlockSpec((2, L, HY_WT), lambda j: (0, 0, j)),
        out_shape=jax.ShapeDtypeStruct((2, L, W), F32),
        scratch_shapes=[pltpu.VMEM((L, HY_FFN), F32)],
        compiler_params=_cparams("arbitrary"),
        name="hy_filter",
    )(jnp.asarray(z), jnp.asarray(t), hy_freq.reshape(1, HY_FFN), w1, b1.reshape(1, HY_FFN), w2,
      b2.reshape(1, HY_FFN), w3, w3, hy_decay)


FFT_L = LANES
FFT_PITCH = FFT_L + 8
FFT_UNROLL = 8


def _fft_tables(L):
    N = 2 * L
    C = L // FFT_L
    n1 = N // FFT_L
    kh = min(n1, -(-(n1 // 2 + 1) // 8) * 8)
    l = np.arange(FFT_L)
    c = np.arange(C)
    k2 = np.arange(kh)
    pair_w = np.where((k2 == 0) | (k2 == n1 // 2), 1.0, np.where(k2 < n1 // 2, 2.0, 0.0))
    ang1 = 2 * np.pi * ((k2[:, None] * c[None, :]) % n1) / n1
    f1 = np.concatenate([np.cos(ang1), -np.sin(ang1)], axis=0)
    g1 = (np.stack([np.cos(ang1), -np.sin(ang1)], axis=-1) * pair_w[:, None, None]).transpose(1, 0, 2).reshape(C, 2 * kh) / N
    ang2 = 2 * np.pi * (((l[None, :, None] * l[None, None, :]) % FFT_L) / FFT_L
                        + ((k2[:, None, None] * l[None, None, :]) % N) / N)
    cs, sn = np.cos(ang2), np.sin(ang2)
    f2 = np.concatenate([np.concatenate([cs, sn], axis=2), np.concatenate([-sn, cs], axis=2)], axis=1)
    cst, snt = cs.transpose(0, 2, 1), sn.transpose(0, 2, 1)
    g2 = np.concatenate([np.concatenate([cst, -snt], axis=2), np.concatenate([snt, cst], axis=2)], axis=1)
    bf = lambda a: jnp.asarray(a.astype(np.float32)).astype(BF16)
    return dict(C=C, n1=kh, f1=bf(f1), g1=bf(g1), f2=bf(f2), g2=bf(g2))


def _fft_kernel(z_ref, f1_ref, f2_ref, g2_ref, kf_ref, g1_ref, o_ref, a_s, b_s, y_s, *, spectrum_only, epilogue=None,
                z_pitch=FFT_L):
    C = f1_ref.shape[1]
    n1 = f1_ref.shape[0] // 2
    P = FFT_PITCH
    blk = lambda j: pl.ds(pl.multiple_of(j * P, 8), FFT_L)

    def stage1(j, carry):
        x = jnp.concatenate([z_ref[pl.ds(2 * j, C, stride=z_pitch), :], z_ref[pl.ds(2 * j + 1, C, stride=z_pitch), :]], axis=1)
        r = jnp.dot(f1_ref[...], x.astype(BF16), preferred_element_type=F32)
        a_s[pl.ds(pl.multiple_of(2 * j * P, 8), 2 * n1), :] = r[:, :LANES]
        a_s[pl.ds(pl.multiple_of((2 * j + 1) * P, 8), 2 * n1), :] = r[:, LANES:]
        return carry
    lax.fori_loop(0, FFT_L // 2, stage1, 0, unroll=FFT_UNROLL)

    def stage2(k2, carry, *, accumulate=False):
        ar = a_s[pl.ds(k2, FFT_L, stride=P), :]
        ai = a_s[pl.ds(n1 + k2, FFT_L, stride=P), :]
        x = jnp.concatenate([ar, ai], axis=0).astype(BF16)
        s = jnp.dot(f2_ref[k2], x, preferred_element_type=F32)
        xr, xi = s[:FFT_L], s[FFT_L:]
        if not spectrum_only:
            kr, ki = kf_ref[k2, 0], kf_ref[k2, 1]
            y = jnp.concatenate([xr * kr - xi * ki, xr * ki + xi * kr], axis=0).astype(BF16)
            s = jnp.dot(g2_ref[k2], y, preferred_element_type=F32)
            b_s[blk(2 * k2), :] = s[:FFT_L]
            b_s[blk(2 * k2 + 1), :] = s[FFT_L:]
        elif accumulate:
            o_ref[k2, 0] += xr
            o_ref[k2, 1] -= xi
        else:
            o_ref[k2, 0] = xr
            o_ref[k2, 1] = xi
        return carry

    if spectrum_only:
        @pl.when(pl.program_id(1) == 0)
        def _():
            lax.fori_loop(0, n1, stage2, 0, unroll=FFT_UNROLL)

        @pl.when(pl.program_id(1) != 0)
        def _():
            lax.fori_loop(0, n1, functools.partial(stage2, accumulate=True), 0, unroll=FFT_UNROLL)
    else:
        lax.fori_loop(0, n1, stage2, 0, unroll=FFT_UNROLL)

    if not spectrum_only:
        def stage3(j, carry):
            x = jnp.concatenate([b_s[pl.ds(2 * j, 2 * n1, stride=P), :], b_s[pl.ds(2 * j + 1, 2 * n1, stride=P), :]], axis=1)
            r = jnp.dot(g1_ref[...], x.astype(BF16), preferred_element_type=F32)
            y_s[pl.ds(2 * j, C, stride=P), :] = r[:, :LANES]
            y_s[pl.ds(2 * j + 1, C, stride=P), :] = r[:, LANES:]
            return carry
        lax.fori_loop(0, FFT_L // 2, stage3, 0, unroll=FFT_UNROLL)
        for c in range(C):
            rows = slice(c * FFT_L, (c + 1) * FFT_L)
            o_ref[rows, :] = epilogue(c, y_s[c * P:c * P + FFT_L, :])


def _hyena_kernel(x0_ref, x1_ref, v_ref, w0_ref, w1_ref, wv_ref, b0_ref, b1_ref, bv_ref, hb_ref,
                  f1_ref, f2_ref, g2_ref, kf_ref, g1_ref, o_ref, z_s, x0_s, a_s, b_s, y_s):
    L = x0_ref.shape[0]
    row = lax.broadcasted_iota(jnp.int32, (L, 1), 0)

    def conv3(u_ref, w_ref, b_ref):
        u = u_ref[...].astype(F32)
        prev = jnp.where(row == 0, 0.0, pltpu.roll(u, 1, 0))
        nxt = jnp.where(row == L - 1, 0.0, pltpu.roll(u, L - 1, 0))
        return w_ref[0:1, :] * prev + w_ref[1:2, :] * u + w_ref[2:3, :] * nxt + b_ref[...]

    P = FFT_PITCH
    chunk = lambda c: slice(c * P, c * P + FFT_L)
    z = conv3(v_ref, wv_ref, bv_ref) * conv3(x1_ref, w1_ref, b1_ref)
    for c in range(L // FFT_L):
        z_s[chunk(c), :] = z[c * FFT_L:(c + 1) * FFT_L, :]
    x0_s[...] = conv3(x0_ref, w0_ref, b0_ref)

    def epilogue(c, y):
        return ((y + z_s[chunk(c), :] * hb_ref[...]) * x0_s[c * FFT_L:(c + 1) * FFT_L, :]).astype(BF16)

    _fft_kernel(z_s, f1_ref, f2_ref, g2_ref, kf_ref, g1_ref, o_ref, a_s, b_s, y_s, spectrum_only=False, epilogue=epilogue,
                z_pitch=P)


def _fft_specs(tabs):
    C, n1 = tabs["C"], tabs["n1"]
    specs = [_const_spec((2 * n1, C)),
             _const_spec((n1, 2 * FFT_L, 2 * FFT_L)), _const_spec((n1, 2 * FFT_L, 2 * FFT_L)),
             pl.BlockSpec((n1, 2, FFT_L, LANES), lambda w, b: (0, 0, 0, w)),
             _const_spec((C, 2 * n1))]
    scratch = [pltpu.VMEM((FFT_L * FFT_PITCH, LANES), F32), pltpu.VMEM((2 * n1 * FFT_PITCH, LANES), F32),
               pltpu.VMEM((C * FFT_PITCH, LANES), F32)]
    return specs, scratch


def _filter_spectrum(tabs, taps):
    _, L, W = taps.shape
    n1 = tabs["n1"]
    specs, scratch = _fft_specs(tabs)

    def spectrum_kernel(z_ref, f1_ref, f2_ref, o_ref, a_s):
        _fft_kernel(z_ref, f1_ref, f2_ref, None, None, None, o_ref, a_s, None, None, spectrum_only=True)

    return pl.pallas_call(
        spectrum_kernel,
        grid=(W // LANES, 2),
        in_specs=[pl.BlockSpec((None, L, LANES), lambda w, b: (b, 0, w))] + specs[:2],
        out_specs=pl.BlockSpec((n1, 2, FFT_L, LANES), lambda w, b: (0, 0, 0, w)),
        out_shape=jax.ShapeDtypeStruct((n1, 2, FFT_L, W), F32),
        scratch_shapes=scratch[:1],
        compiler_params=_cparams("arbitrary", "arbitrary"),
        name="fft_spectrum",
    )(taps, tabs["f1"], tabs["f2"])


def _hyena(hy, conv_w, conv_b, hy_bias, taps, B, L):
    W = HY_WIDTH
    nw = W // LANES
    tabs = _fft_tables(L)
    kf = _filter_spectrum(tabs, taps)
    specs, scratch = _fft_specs(tabs)
    cb = conv_b.reshape(1, 3 * W)
    blk = lambda off: pl.BlockSpec((L, LANES), lambda w, b: (b, off + w))
    wblk = lambda off: pl.BlockSpec((3, LANES), lambda w, b: (0, off + w))
    bblk = lambda off: pl.BlockSpec((1, LANES), lambda w, b: (0, off + w))
    return pl.pallas_call(
        _hyena_kernel,
        grid=(nw, B),
        in_specs=[blk(0), blk(nw), blk(2 * nw), wblk(0), wblk(nw), wblk(2 * nw), bblk(0), bblk(nw), bblk(2 * nw),
                  bblk(0)] + specs,
        out_specs=pl.BlockSpec((L, LANES), lambda w, b: (b, w)),
        out_shape=jax.ShapeDtypeStruct((B * L, W), BF16),
        scratch_shapes=[pltpu.VMEM((L // FFT_L * FFT_PITCH, LANES), F32), pltpu.VMEM((L, LANES), F32)] + scratch,
        compiler_params=_cparams("arbitrary", "arbitrary"),
        name="hyena",
    )(hy, hy, hy, conv_w, conv_w, conv_w, cb, cb, cb, hy_bias.reshape(1, W), tabs["f1"], tabs["f2"], tabs["g2"], kf,
      tabs["g1"])


def _merge_kernel(og_ref, oh_ref, gates_ref, x_ref, wbg_ref, wbh_ref, wo_ref, n2_ref, rw_ref, rb_ref,
                  h1_ref, hn3_ref, idx_ref, wts_ref):
    mg = jnp.dot(og_ref[...], wbg_ref[...], preferred_element_type=F32)
    mh = jnp.dot(oh_ref[...], wbh_ref[...], preferred_element_type=F32)
    gg = jax.nn.sigmoid(gates_ref[:, :D_MODEL].astype(F32))
    gh = jax.nn.sigmoid(gates_ref[:, D_MODEL:].astype(F32))
    merged = (gg * mg + gh * mh).astype(BF16)
    h1 = x_ref[...] + jnp.dot(merged, wo_ref[...], preferred_element_type=F32)
    h1_ref[...] = h1
    hn = h1 * lax.rsqrt(jnp.mean(h1 * h1, axis=-1, keepdims=True) + EPS) * n2_ref[...]
    _store_token_rows(hn3_ref, hn)
    hh = hn.astype(BF16)
    hl = (hn - hh.astype(F32)).astype(BF16)
    p1 = jnp.dot(hh, rw_ref[...], preferred_element_type=F32)
    p2 = jnp.dot(hl, rw_ref[:, :ROUTE_PAD], preferred_element_type=F32)
    logits = p1[:, :ROUTE_PAD] + p1[:, ROUTE_PAD:] + p2 + rb_ref[...]
    lane = lax.broadcasted_iota(jnp.int32, logits.shape, 1)
    logits = jnp.where(lane < N_EXPERTS, logits, -jnp.inf)
    idx_out = jnp.zeros(logits.shape, jnp.int32)
    val_out = jnp.zeros(logits.shape, F32)
    v0 = None
    for j in range(TOP_K):
        m = jnp.max(logits, axis=-1, keepdims=True)
        sel = jnp.min(jnp.where(logits == m, lane, ROUTE_PAD), axis=-1, keepdims=True)
        if j == 0:
            v0 = m
        idx_out = jnp.where(lane == j, sel, idx_out)
        val_out = jnp.where(lane == j, jnp.exp(m - v0), val_out)
        logits = jnp.where(lane == sel, -jnp.inf, logits)
    idx_ref[...] = idx_out[:, :TOP_K]
    wts_ref[...] = val_out / jnp.sum(val_out, axis=-1, keepdims=True)


def _merge(o_gla, o_hy, gates, x2, wbg, wbh, wo, norm2_w, rw_pad, rb_pad):
    T = x2.shape[0]
    tm = min(512, T)
    row = lambda n: pl.BlockSpec((tm, n), lambda i: (i, 0))
    return pl.pallas_call(
        _merge_kernel,
        grid=(T // tm,),
        in_specs=[row(GLA_VAL), row(HY_WIDTH), row(N_GATES), row(D_MODEL),
                  _const_spec((GLA_VAL, D_MODEL)), _const_spec((HY_WIDTH, D_MODEL)), _const_spec((D_MODEL, D_MODEL)),
                  _const_spec((1, D_MODEL)), _const_spec((D_MODEL, 2 * ROUTE_PAD)), _const_spec((1, ROUTE_PAD))],
        out_specs=[row(D_MODEL), pl.BlockSpec((tm * TOK_ROWS, LANES), lambda i: (i, 0)), row(TOP_K), row(ROUTE_PAD)],
        out_shape=[jax.ShapeDtypeStruct((T, D_MODEL), F32), jax.ShapeDtypeStruct((T * TOK_ROWS, LANES), F32),
                   jax.ShapeDtypeStruct((T, TOP_K), jnp.int32), jax.ShapeDtypeStruct((T, ROUTE_PAD), F32)],
        compiler_params=_cparams("parallel"),
        name="merge_route",
    )(o_gla, o_hy, gates, x2, wbg, wbh, wo, norm2_w.reshape(1, D_MODEL), rw_pad, rb_pad)


MOE_TM = 512
MOE_TC = 256
DMA_UNROLL = 8


DISPATCH_SLOTS = 3


def _dispatch_kernel(last_ref, pos_ref, hn_ref, xs_ref, zero_s, buf, sem, lsems, ssems):
    i = pl.program_id(0)
    n = pl.num_programs(0)
    tc = pos_ref.shape[2] // TOP_K
    rows = tc * TOK_ROWS
    zrows = zero_s.shape[0]
    slot = lax.rem(i, DISPATCH_SLOTS)
    prev_slot = lax.rem(i + DISPATCH_SLOTS - 1, DISPATCH_SLOTS)

    def load(blk, s):
        src = hn_ref.at[pl.ds(pl.multiple_of(blk * rows, rows), rows)]
        return pltpu.make_async_copy(src, buf.at[s], lsems.at[s])

    def wait_copies(s):
        for _ in range(TOP_K):
            pltpu.make_async_copy(buf.at[s], xs_ref.at[pl.ds(0, rows)], ssems.at[s]).wait()

    @pl.when(i == 0)
    def _():
        load(0, 0).start()

        @pl.when(n > 1)
        def _():
            load(1, 1).start()

        zero_s[...] = jnp.zeros_like(zero_s)
        n_fill = last_ref.shape[0]

        def zfill(e, c):
            @pl.when(last_ref[e] >= 0)
            def _():
                dst = pl.multiple_of(last_ref[e] * zrows, zrows)
                pltpu.make_async_copy(zero_s, xs_ref.at[pl.ds(dst, zrows)], sem).start()
            return c
        lax.fori_loop(0, n_fill, zfill, 0)

        def zwait(e, c):
            @pl.when(last_ref[e] >= 0)
            def _():
                pltpu.make_async_copy(zero_s, xs_ref.at[pl.ds(0, zrows)], sem).wait()
            return c
        lax.fori_loop(0, n_fill, zwait, 0)

    load(i, slot).wait()

    def issue(r, c):
        src = buf.at[slot, pl.ds(pl.multiple_of(r * TOK_ROWS, TOK_ROWS), TOK_ROWS)]
        for j in range(TOP_K):
            dst = pl.multiple_of(pos_ref[0, 0, r * TOP_K + j] * TOK_ROWS, TOK_ROWS)
            pltpu.make_async_copy(src, xs_ref.at[pl.ds(dst, TOK_ROWS)], ssems.at[slot]).start(priority=j % 2)
        return c
    lax.fori_loop(0, tc, issue, 0, unroll=DMA_UNROLL // TOP_K)

    @pl.when(i > 0)
    def _():
        wait_copies(prev_slot)

    @pl.when(i + 2 < n)
    def _():
        load(i + 2, prev_slot).start()

    @pl.when(i == n - 1)
    def _():
        wait_copies(slot)


def _dispatch(hn3, pos, last_tile, P):
    T = hn3.shape[0] // TOK_ROWS
    tc = min(MOE_TC, T)
    nt = T // tc
    return pl.pallas_call(
        _dispatch_kernel,
        grid_spec=pltpu.PrefetchScalarGridSpec(
            num_scalar_prefetch=1,
            grid=(nt,),
            in_specs=[pl.BlockSpec((1, 1, tc * TOP_K), lambda i, lt: (i, 0, 0), memory_space=pltpu.SMEM),
                      pl.BlockSpec(memory_space=pl.ANY)],
            out_specs=pl.BlockSpec(memory_space=pl.ANY),
            scratch_shapes=[pltpu.VMEM((MOE_TM * TOK_ROWS, LANES), F32),
                            pltpu.VMEM((DISPATCH_SLOTS, tc * TOK_ROWS, LANES), F32),
                            pltpu.SemaphoreType.DMA(()),
                            pltpu.SemaphoreType.DMA((DISPATCH_SLOTS,)), pltpu.SemaphoreType.DMA((DISPATCH_SLOTS,))],
        ),
        out_shape=jax.ShapeDtypeStruct((P * TOK_ROWS, LANES), F32),
        compiler_params=_cparams("arbitrary"),
        name="moe_dispatch",
    )(last_tile, pos.reshape(nt, 1, tc * TOP_K), hn3)


def _expert_kernel(te_ref, nused_ref, xs_ref, wgu_ref, bgu_ref, wd_ref, bd_ref, o_ref, wgu_s, wd_s):
    i = pl.program_id(0)
    prev = te_ref[jnp.maximum(i - 1, 0)]

    @pl.when((i == 0) | (te_ref[i] != prev))
    def _():
        wgu_s[...] = wgu_ref[0].astype(BF16)
        wd_s[...] = wd_ref[0].astype(BF16)

    @pl.when(i < nused_ref[0])
    def _():
        x = _load_token_rows(xs_ref, 0, xs_ref.shape[0] // TOK_ROWS).astype(BF16)
        gu = jnp.dot(x, wgu_s[...], preferred_element_type=F32) + bgu_ref[0]
        gate = jnp.minimum(gu[:, :D_FF], SWIGLU_LIMIT)
        up = jnp.clip(gu[:, D_FF:], -SWIGLU_LIMIT, SWIGLU_LIMIT)
        act = gate * jax.nn.sigmoid(SWIGLU_ALPHA * gate) * (up + 1.0)
        y = jnp.dot(act.astype(BF16), wd_s[...], preferred_element_type=F32) + bd_ref[0]
        _store_token_rows(o_ref, y)

    @pl.when(i >= nused_ref[0])
    def _():
        o_ref[...] = jnp.zeros_like(o_ref)


def _experts(xs3, tile_expert, n_used, w_gate_up, b_gate_up, w_down, b_down):
    P = xs3.shape[0] // TOK_ROWS
    tm = MOE_TM
    nt = P // tm
    live = lambda i, te, nu: jnp.minimum(i, nu[0] - 1)
    return pl.pallas_call(
        _expert_kernel,
        grid_spec=pltpu.PrefetchScalarGridSpec(
            num_scalar_prefetch=2,
            grid=(nt,),
            in_specs=[
                pl.BlockSpec((tm * TOK_ROWS, LANES), lambda i, te, nu: (live(i, te, nu), 0)),
                pl.BlockSpec((1, D_MODEL, 2 * D_FF), lambda i, te, nu: (te[i], 0, 0)),
                pl.BlockSpec((1, 1, 2 * D_FF), lambda i, te, nu: (te[i], 0, 0)),
                pl.BlockSpec((1, D_FF, D_MODEL), lambda i, te, nu: (te[i], 0, 0)),
                pl.BlockSpec((1, 1, D_MODEL), lambda i, te, nu: (te[i], 0, 0)),
            ],
            out_specs=pl.BlockSpec((tm * TOK_ROWS, LANES), lambda i, te, nu: (i, 0)),
            scratch_shapes=[pltpu.VMEM((D_MODEL, 2 * D_FF), BF16), pltpu.VMEM((D_FF, D_MODEL), BF16)],
        ),
        out_shape=jax.ShapeDtypeStruct((P * TOK_ROWS, LANES), F32),
        compiler_params=_cparams("arbitrary"),
        name="moe_experts",
    )(tile_expert, n_used, xs3, w_gate_up, b_gate_up.reshape(N_EXPERTS, 1, 2 * D_FF), w_down,
      b_down.reshape(N_EXPERTS, 1, D_MODEL))


COMBINE_SLOTS = 3
COMBINE_GROUP = 8


def _combine_kernel(pos_ref, pos1_ref, pos2_ref, wts_ref, h1_ref, nf_ref, ys_ref, o_ref, buf, sems):
    i = pl.program_id(0)
    n = pl.num_programs(0)
    tc = o_ref.shape[0]
    G = COMBINE_GROUP
    slot = lax.rem(i, COMBINE_SLOTS)
    slot2 = lax.rem(i + 2, COMBINE_SLOTS)

    def start_rows(p_ref, s, r):
        for j in range(TOP_K):
            src = pl.multiple_of(p_ref[0, 0, r * TOP_K + j] * TOK_ROWS, TOK_ROWS)
            dst = pl.multiple_of((j * tc + r) * TOK_ROWS, TOK_ROWS)
            pltpu.make_async_copy(ys_ref.at[pl.ds(src, TOK_ROWS)], buf.at[s, pl.ds(dst, TOK_ROWS)],
                                  sems.at[s]).start(priority=j % 2)

    def gather(p_ref, s):
        def issue(r, c):
            start_rows(p_ref, s, r)
            return c
        lax.fori_loop(0, tc, issue, 0, unroll=DMA_UNROLL // TOP_K)

    def combine_group(g):
        rows = slice(g * G, (g + 1) * G)
        w = wts_ref[rows, :]
        h = h1_ref[rows, :]
        for j in range(TOP_K):
            h = h + w[:, j:j + 1] * _load_token_rows(buf.at[slot], j * tc + g * G, G)
        o_ref[rows, :] = h * lax.rsqrt(jnp.mean(h * h, axis=-1, keepdims=True) + EPS) * nf_ref[...]

    @pl.when(i == 0)
    def _():
        gather(pos_ref, 0)

        @pl.when(n > 1)
        def _():
            gather(pos1_ref, 1)

    pltpu.make_async_copy(ys_ref.at[pl.ds(0, TOP_K * tc * TOK_ROWS)], buf.at[slot], sems.at[slot]).wait()

    @pl.when(i + 2 < n)
    def _():
        for g in range(tc // G):
            for r in range(g * G, (g + 1) * G):
                start_rows(pos2_ref, slot2, r)
            combine_group(g)

    @pl.when(i + 2 >= n)
    def _():
        for g in range(tc // G):
            combine_group(g)


def _combine(pos, wts, h1, norm_f_w, ys3):
    T = h1.shape[0]
    tc = min(MOE_TC, T)
    nt = T // tc
    pos3 = pos.reshape(nt, 1, tc * TOP_K)
    ahead = lambda k: pl.BlockSpec((1, 1, tc * TOP_K), lambda i: (jnp.minimum(i + k, nt - 1), 0, 0), memory_space=pltpu.SMEM)
    return pl.pallas_call(
        _combine_kernel,
        grid=(nt,),
        in_specs=[
            ahead(0), ahead(1), ahead(2),
            pl.BlockSpec((tc, ROUTE_PAD), lambda i: (i, 0)),
            pl.BlockSpec((tc, D_MODEL), lambda i: (i, 0)),
            _const_spec((1, D_MODEL)),
            pl.BlockSpec(memory_space=pl.ANY),
        ],
        out_specs=pl.BlockSpec((tc, D_MODEL), lambda i: (i, 0)),
        out_shape=jax.ShapeDtypeStruct((T, D_MODEL), F32),
        scratch_shapes=[pltpu.VMEM((COMBINE_SLOTS, TOP_K * tc * TOK_ROWS, LANES), F32),
                        pltpu.SemaphoreType.DMA((COMBINE_SLOTS,))],
        compiler_params=_cparams("arbitrary"),
        name="moe_combine",
    )(pos3, pos3, pos3, wts, h1, norm_f_w.reshape(1, D_MODEL), ys3)


def _route_plan(top_idx, T):
    tm = MOE_TM
    nt = (T * TOP_K) // tm + N_EXPERTS
    oh = (top_idx[:, :, None] == jnp.arange(N_EXPERTS, dtype=jnp.int32)[None, None, :]).astype(jnp.int32)
    cnt = jnp.sum(oh, axis=1)
    csum = jnp.cumsum(cnt, axis=0)
    counts = csum[-1]
    tiles = (counts + tm - 1) // tm
    tile_end = jnp.cumsum(tiles)
    offs = (tile_end - tiles) * tm
    pos = jnp.sum(oh * (offs[None, :] + csum - cnt)[:, None, :], axis=2).reshape(-1)
    tile_id = jnp.arange(nt, dtype=jnp.int32)
    tile_expert = jnp.minimum(jnp.sum((tile_end[None, :] <= tile_id[:, None]).astype(jnp.int32), axis=1),
                              N_EXPERTS - 1).astype(jnp.int32)
    tail = tile_end[-1] + jnp.arange(N_EXPERTS, dtype=jnp.int32)
    last_tile = jnp.concatenate([jnp.where(tiles > 0, tile_end - 1, -1), jnp.where(tail < nt, tail, -1)]).astype(jnp.int32)
    n_used = tile_end[-1:].astype(jnp.int32)
    return pos.astype(jnp.int32), tile_expert, last_tile, n_used, nt * tm


def kernel(x, norm1_w, w_in, gla_gate_w2, gla_gate_b, gla_norm_w, hy_conv_w, hy_conv_b, hy_freq, hy_ffn_w1, hy_ffn_b1,
           hy_ffn_w2, hy_ffn_b2, hy_ffn_w3, hy_decay, hy_bias, w_branch_gla, w_branch_hy, w_out, norm2_w, router_w,
           router_b, w_gate_up, b_gate_up, w_down, b_down, norm_f_w):
    B, L, D = x.shape
    T = B * L
    x2 = x.reshape(T, D)

    s_a = N_QKVR
    s_h = s_a + 2 * GLA_RANK
    w_b = w_in.astype(BF16)
    w_cat = jnp.concatenate(
        [w_b[:, :s_a], w_b[:, s_h:], w_b[:, s_a:s_h], jnp.zeros((D, ALOW_PAD - 2 * GLA_RANK), BF16)], axis=1)
    qkvr, hy, gates, alow = _in_proj(x2, norm1_w, w_cat)

    w2 = jnp.zeros((2, ALOW_PAD, GLA_KEY), F32)
    w2 = w2.at[0, :GLA_RANK].set(gla_gate_w2[0]).at[1, GLA_RANK:2 * GLA_RANK].set(gla_gate_w2[1])
    o_gla = _gla(qkvr, alow, w2, gla_gate_b.reshape(2, 1, GLA_KEY), gla_norm_w, B, L)

    taps = _hy_filter(L, hy_freq, hy_ffn_w1, hy_ffn_b1, hy_ffn_w2, hy_ffn_b2, hy_ffn_w3, hy_decay)
    o_hy = _hyena(hy, hy_conv_w, hy_conv_b, hy_bias, taps, B, L)

    rw_f = jnp.zeros((D, ROUTE_PAD), F32).at[:, :N_EXPERTS].set(router_w)
    rw_hi = rw_f.astype(BF16)
    rw_pad = jnp.concatenate([rw_hi, (rw_f - rw_hi.astype(F32)).astype(BF16)], axis=1)
    rb_pad = jnp.zeros((1, ROUTE_PAD), F32).at[0, :N_EXPERTS].set(router_b)
    h1, hn3, top_idx, wts_pad = _merge(o_gla, o_hy, gates, x2, w_branch_gla.astype(BF16), w_branch_hy.astype(BF16),
                                       w_out.astype(BF16), norm2_w, rw_pad, rb_pad)

    pos, tile_expert, last_tile, n_used, P = _route_plan(top_idx, T)
    xs3 = _dispatch(hn3, pos, last_tile, P)
    ys3 = _experts(xs3, tile_expert, n_used, w_gate_up, b_gate_up, w_down, b_down)
    out = _combine(pos, wts_pad, h1, norm_f_w, ys3)
    return out.reshape(B, L, D)
```

```python
import functools
import math

import numpy as np
import jax
import jax.numpy as jnp
from jax import lax
from jax.experimental import pallas as pl
from jax.experimental.pallas import tpu as pltpu

D_MODEL = 1024
EPS = 1e-6
GLA_HEADS = 4
GLA_DK = 128
GLA_DV = 256
GLA_KEY = GLA_HEADS * GLA_DK
GLA_VAL = GLA_HEADS * GLA_DV
GLA_RANK = 16
GLA_GATE_TEMP = 16.0
GLA_CHUNK = 64
HY_WIDTH = D_MODEL
HY_BANDS = 16
HY_EMB = 1 + 2 * HY_BANDS
HY_FFN = 64
N_EXPERTS = 32
TOP_K = 4
D_FF = D_MODEL
SWIGLU_LIMIT = 7.0
SWIGLU_ALPHA = 1.702

LANES = 128
ALOW_PAD = LANES
ROUTE_PAD = LANES
VMEM_LIMIT = 56 * 1024 * 1024

F32 = jnp.float32
BF16 = jnp.bfloat16
HIGHEST = lax.Precision.HIGHEST


def _cparams(*sem):
    return pltpu.CompilerParams(dimension_semantics=sem, vmem_limit_bytes=VMEM_LIMIT)


def _const_spec(shape):
    nd = len(shape)
    return pl.BlockSpec(shape, lambda *_: (0,) * nd, pipeline_mode=pl.Buffered(1))


TOK_ROWS = D_MODEL // LANES


def _store_token_rows(ref, val):
    n = val.shape[0]
    for s_ in range(TOK_ROWS):
        ref[pl.ds(s_, n, stride=TOK_ROWS), :] = val[:, s_ * LANES:(s_ + 1) * LANES]


def _load_token_rows(ref, first, n):
    return jnp.concatenate(
        [ref[pl.ds(first * TOK_ROWS + s_, n, stride=TOK_ROWS), :] for s_ in range(TOK_ROWS)], axis=1)


N_QKVR = 2 * GLA_KEY + 2 * GLA_VAL
N_HY = 3 * HY_WIDTH
N_GATES = 2 * D_MODEL
IN_COLS = N_QKVR + N_HY + N_GATES + ALOW_PAD


def _inproj_kernel(x_ref, nw_ref, w_ref, qkvr_ref, hy_ref, gates_ref, alow_ref):
    x = x_ref[...]
    ms = jnp.mean(x * x, axis=-1, keepdims=True)
    xn = (x * lax.rsqrt(ms + EPS) * nw_ref[...]).astype(BF16)
    c0, c1, c2 = N_QKVR, N_QKVR + N_HY, N_QKVR + N_HY + N_GATES
    qkvr_ref[...] = jnp.dot(xn, w_ref[:, :c0], preferred_element_type=F32).astype(BF16)
    hy_ref[...] = jnp.dot(xn, w_ref[:, c0:c1], preferred_element_type=F32).astype(BF16)
    gates_ref[...] = jnp.dot(xn, w_ref[:, c1:c2], preferred_element_type=F32).astype(BF16)
    alow_ref[...] = jnp.dot(xn, w_ref[:, c2:], preferred_element_type=F32)


def _in_proj(x2, norm1_w, w_cat):
    T = x2.shape[0]
    tm = min(512, T)
    return pl.pallas_call(
        _inproj_kernel,
        grid=(T // tm,),
        in_specs=[
            pl.BlockSpec((tm, D_MODEL), lambda i: (i, 0)),
            _const_spec((1, D_MODEL)),
            _const_spec((D_MODEL, IN_COLS)),
        ],
        out_specs=[
            pl.BlockSpec((tm, N_QKVR), lambda i: (i, 0)),
            pl.BlockSpec((tm, N_HY), lambda i: (i, 0)),
            pl.BlockSpec((tm, N_GATES), lambda i: (i, 0)),
            pl.BlockSpec((tm, ALOW_PAD), lambda i: (i, 0)),
        ],
        out_shape=[
            jax.ShapeDtypeStruct((T, N_QKVR), BF16),
            jax.ShapeDtypeStruct((T, N_HY), BF16),
            jax.ShapeDtypeStruct((T, N_GATES), BF16),
            jax.ShapeDtypeStruct((T, ALOW_PAD), F32),
        ],
        compiler_params=_cparams("parallel"),
        name="in_proj",
    )(x2, norm1_w.reshape(1, D_MODEL), w_cat)


GLA_PREP_ROWS = 256
GLA_PREP_UNROLL = 4
GLA_SCAN_UNROLL = 64


def _log_sigmoid(x):
    return jnp.minimum(x, 0.0) - jnp.log1p(jnp.exp(-jnp.abs(x)))


def _gla_kernel(q_ref, k_ref, v_ref, r_ref, alow_ref, w2_ref, gb_ref, nw_ref, o_ref,
                qtf, kef, qtb, keb, decf, decb, of_s, ob_s, stf, stb):
    L = q_ref.shape[0]
    C = GLA_CHUNK
    DK = GLA_DK
    n_chunks = L // C
    R = min(GLA_PREP_ROWS, L)
    cpr = R // C
    scale = DK ** -0.5
    nt = (((1,), (1,)), ((), ()))
    tn = (((0,), (0,)), ((), ()))

    ri = lax.broadcasted_iota(jnp.int32, (R, R), 0)
    ci = lax.broadcasted_iota(jnp.int32, (R, R), 1)
    same = (ri // C) == (ci // C)
    mask_f = same & (ri >= ci)
    mask_b = same & (ri < ci)
    tri = jnp.where(mask_f, 1.0, 0.0).astype(BF16)
    w2 = jnp.concatenate([w2_ref[0], w2_ref[1]], axis=1).astype(BF16)
    gb = jnp.concatenate([gb_ref[0], gb_ref[1]], axis=1)

    def prep(g, carry):
        rows = pl.ds(pl.multiple_of(g * R, R), R)
        logits = jnp.dot(alow_ref[rows, :].astype(BF16), w2, preferred_element_type=F32) + gb
        lg = _log_sigmoid(logits) * (1.0 / GLA_GATE_TEMP)
        hi = lg.astype(BF16)
        lo = (lg - hi.astype(F32)).astype(BF16)
        b = jnp.dot(tri, hi, preferred_element_type=F32) + jnp.dot(tri, lo, preferred_element_type=F32)
        tot = jnp.concatenate(
            [jnp.broadcast_to(b[c * C + C - 1:c * C + C, :], (C, 2 * DK)) for c in range(cpr)], axis=0)
        q = q_ref[rows, :].astype(F32) * scale
        k = k_ref[rows, :].astype(F32)
        vb = v_ref[rows, :]
        dtot = jnp.exp(tot)
        for d, (qt, ke, dec, mask, out) in enumerate(((qtf, kef, decf, mask_f, of_s), (qtb, keb, decb, mask_b, ob_s))):
            sl = slice(d * DK, (d + 1) * DK)
            bd, td, ld = b[:, sl], tot[:, sl], lg[:, sl]
            if d == 0:
                e_q, e_end = bd, td - bd
            else:
                e_q, e_end = td - bd + ld, bd - ld
            qtv = (q * jnp.exp(e_q)).astype(BF16)
            ktv = (k * jnp.exp(-e_q)).astype(BF16)
            qt[rows, :] = qtv
            ke[rows, :] = (k * jnp.exp(e_end)).astype(BF16)
            s = lax.dot_general(qtv, ktv, nt, preferred_element_type=F32)
            s = jnp.where(mask, s, 0.0).astype(BF16)
            out[rows, :] = jnp.dot(s, vb, preferred_element_type=F32)
            for c in range(cpr):
                dec[pl.ds(g * cpr + c, 1), :] = dtot[c * C:c * C + 1, sl]
        return carry

    lax.fori_loop(0, L // R, prep, 0, unroll=min(GLA_PREP_UNROLL, L // R))

    def chunk(n, st, qt, ke, dec, out):
        rows = pl.ds(pl.multiple_of(n * C, C), C)
        s = st[...]
        out[rows, :] += lax.dot_general(qt[rows, :], s.astype(BF16), nt, preferred_element_type=F32)
        upd = lax.dot_general(v_ref[rows, :], ke[rows, :], tn, preferred_element_type=F32)
        st[...] = dec[pl.ds(n, 1), :] * s + upd

    def step(i, carry):
        chunk(i, stf, qtf, kef, decf, of_s)
        chunk(n_chunks - 1 - i, stb, qtb, keb, decb, ob_s)
        return carry

    stf[...] = jnp.zeros_like(stf)
    stb[...] = jnp.zeros_like(stb)
    lax.fori_loop(0, n_chunks, step, 0, unroll=min(GLA_SCAN_UNROLL, n_chunks))

    o = of_s[...] + ob_s[...]
    o = o * lax.rsqrt(jnp.mean(o * o, axis=-1, keepdims=True) + EPS) * nw_ref[...]
    r = r_ref[...].astype(F32)
    o_ref[...] = (o * (r * jax.nn.sigmoid(r))).astype(BF16)


def _gla(qkvr, alow, w2, gb, norm_w, B, L):
    T = B * L
    H = GLA_HEADS
    kq = GLA_KEY // GLA_DK
    kv = 2 * GLA_KEY // GLA_DV
    kr = (2 * GLA_KEY + GLA_VAL) // GLA_DV
    return pl.pallas_call(
        _gla_kernel,
        grid=(B, H),
        in_specs=[
            pl.BlockSpec((L, GLA_DK), lambda b, h: (b, h)),
            pl.BlockSpec((L, GLA_DK), lambda b, h: (b, kq + h)),
            pl.BlockSpec((L, GLA_DV), lambda b, h: (b, kv + h)),
            pl.BlockSpec((L, GLA_DV), lambda b, h: (b, kr + h)),
            pl.BlockSpec((L, ALOW_PAD), lambda b, h: (b, 0)),
            pl.BlockSpec((2, ALOW_PAD, GLA_DK), lambda b, h: (0, 0, h)),
            pl.BlockSpec((2, 1, GLA_DK), lambda b, h: (0, 0, h)),
            pl.BlockSpec((1, GLA_DV), lambda b, h: (0, 0)),
        ],
        out_specs=pl.BlockSpec((L, GLA_DV), lambda b, h: (b, h)),
        out_shape=jax.ShapeDtypeStruct((T, GLA_VAL), BF16),
        scratch_shapes=[pltpu.VMEM((L, GLA_DK), BF16)] * 4
        + [pltpu.VMEM((L // GLA_CHUNK, GLA_DK), F32)] * 2
        + [pltpu.VMEM((L, GLA_DV), F32)] * 2
        + [pltpu.VMEM((GLA_DV, GLA_DK), F32)] * 2,
        compiler_params=_cparams("parallel", "parallel"),
        name="gla",
    )(qkvr, qkvr, qkvr, qkvr, alow, w2, gb, norm_w.reshape(1, GLA_DV))


HY_WT = 256


def _hy_filter_kernel(z_ref, t_ref, fr_ref, w1_ref, b1_ref, w2_ref, b2_ref, w3f_ref, w3b_ref, dec_ref, o_ref, h_s):
    @pl.when(pl.program_id(0) == 0)
    def _():
        fr = fr_ref[...]
        h1 = jnp.sin(fr * (jnp.dot(z_ref[...], w1_ref[...], preferred_element_type=F32, precision=HIGHEST) + b1_ref[...]))
        h_s[...] = jnp.sin(fr * (jnp.dot(h1, w2_ref[...], preferred_element_type=F32, precision=HIGHEST) + b2_ref[...]))

    h = h_s[...]
    t = t_ref[...]
    row = lax.broadcasted_iota(jnp.int32, (z_ref.shape[0], 1), 0)
    hq = h.astype(BF16)
    hf = jnp.dot(hq, w3f_ref[...].astype(BF16), preferred_element_type=F32)
    hf = hf * jnp.exp(-t * jnp.abs(dec_ref[0:1, :]))
    hb = jnp.dot(hq, w3b_ref[...].astype(BF16), preferred_element_type=F32)
    hb = hb * jnp.exp(-t * jnp.abs(dec_ref[1:2, :]))
    hb = jnp.where(row == 0, 0.0, hb)
    ss = jnp.sum(hf * hf, axis=0, keepdims=True) + jnp.sum(hb * hb, axis=0, keepdims=True)
    inv = lax.rsqrt(ss + EPS)
    o_ref[0] = hf * inv
    o_ref[1] = hb * inv


def _hy_filter(L, hy_freq, w1, b1, w2, b2, w3, hy_decay):
    W = HY_WIDTH
    t = np.linspace(0.0, 1.0, L, dtype=np.float32)[:, None]
    omega = (np.float32(2.0 * math.pi) * np.arange(L, dtype=np.float32) / np.float32(L)).astype(np.float32)
    f = np.linspace(1e-4, HY_BANDS - 1, HY_BANDS, dtype=np.float32)
    ang = (omega[:, None] * f[None, :]).astype(np.float32)
    z = np.concatenate([t, np.cos(ang), -np.sin(ang)], axis=-1).astype(np.float32)
    nw = W // HY_WT
    return pl.pallas_call(
        _hy_filter_kernel,
        grid=(nw,),
        in_specs=[
            _const_spec((L, HY_EMB)),
            _const_spec((L, 1)),
            _const_spec((1, HY_FFN)),
            _const_spec((HY_EMB, HY_FFN)),
            _const_spec((1, HY_FFN)),
            _const_spec((HY_FFN, HY_FFN)),
            _const_spec((1, HY_FFN)),
            pl.BlockSpec((HY_FFN, HY_WT), lambda j: (0, j)),
            pl.BlockSpec((HY_FFN, HY_WT), lambda j: (0, nw + j)),
            pl.BlockSpec((2, HY_WT), lambda j: (0, j)),
        ],
        out_specs=pl.BlockSpec((2, L, HY_WT), lambda j: (0, 0, j)),
        out_shape=jax.ShapeDtypeStruct((2, L, W), F32),
        scratch_shapes=[pltpu.VMEM((L, HY_FFN), F32)],
        compiler_params=_cparams("arbitrary"),
        name="hy_filter",
    )(jnp.asarray(z), jnp.asarray(t), hy_freq.reshape(1, HY_FFN), w1, b1.reshape(1, HY_FFN), w2,
      b2.reshape(1, HY_FFN), w3, w3, hy_decay)


FFT_L = LANES
FFT_PITCH = FFT_L + 8
FFT_PAIR_UNROLL = 32
FFT_K2_UNROLL = 40


def _fft_tables(L):
    N = 2 * L
    C = L // FFT_L
    n1 = N // FFT_L
    kh = min(n1, -(-(n1 // 2 + 1) // 8) * 8)
    l = np.arange(FFT_L)
    c = np.arange(C)
    k2 = np.arange(kh)
    pair_w = np.where((k2 == 0) | (k2 == n1 // 2), 1.0, np.where(k2 < n1 // 2, 2.0, 0.0))
    ang1 = 2 * np.pi * ((k2[:, None] * c[None, :]) % n1) / n1
    f1 = np.concatenate([np.cos(ang1), -np.sin(ang1)], axis=0)
    g1 = (np.stack([np.cos(ang1), -np.sin(ang1)], axis=-1) * pair_w[:, None, None]).transpose(1, 0, 2).reshape(C, 2 * kh) / N
    ang2 = 2 * np.pi * (((l[None, :, None] * l[None, None, :]) % FFT_L) / FFT_L
                        + ((k2[:, None, None] * l[None, None, :]) % N) / N)
    cs, sn = np.cos(ang2), np.sin(ang2)
    f2 = np.concatenate([np.concatenate([cs, sn], axis=2), np.concatenate([-sn, cs], axis=2)], axis=1)
    cst, snt = cs.transpose(0, 2, 1), sn.transpose(0, 2, 1)
    g2 = np.concatenate([np.concatenate([cst, -snt], axis=2), np.concatenate([snt, cst], axis=2)], axis=1)
    bf = lambda a: jnp.asarray(a.astype(np.float32)).astype(BF16)
    return dict(C=C, n1=kh, f1=bf(f1), g1=bf(g1), f2=bf(f2), g2=bf(g2))


def _fft_kernel(z_ref, f1_ref, f2_ref, g2_ref, kf_ref, g1_ref, o_ref, a_s, b_s, y_s, *, spectrum_only, epilogue=None,
                z_pitch=FFT_L):
    C = f1_ref.shape[1]
    n1 = f1_ref.shape[0] // 2
    P = FFT_PITCH
    blk = lambda j: pl.ds(pl.multiple_of(j * P, 8), FFT_L)

    def stage1(j, carry):
        x = jnp.concatenate([z_ref[pl.ds(2 * j, C, stride=z_pitch), :], z_ref[pl.ds(2 * j + 1, C, stride=z_pitch), :]], axis=1)
        r = jnp.dot(f1_ref[...], x.astype(BF16), preferred_element_type=F32)
        a_s[pl.ds(pl.multiple_of(2 * j * P, 8), 2 * n1), :] = r[:, :LANES]
        a_s[pl.ds(pl.multiple_of((2 * j + 1) * P, 8), 2 * n1), :] = r[:, LANES:]
        return carry
    lax.fori_loop(0, FFT_L // 2, stage1, 0, unroll=FFT_PAIR_UNROLL)
    k2_unroll = min(FFT_K2_UNROLL, n1)

    def stage2(k2, carry, *, accumulate=False):
        ar = a_s[pl.ds(k2, FFT_L, stride=P), :]
        ai = a_s[pl.ds(n1 + k2, FFT_L, stride=P), :]
        x = jnp.concatenate([ar, ai], axis=0).astype(BF16)
        s = jnp.dot(f2_ref[k2], x, preferred_element_type=F32)
        xr, xi = s[:FFT_L], s[FFT_L:]
        if not spectrum_only:
            kr, ki = kf_ref[k2, 0], kf_ref[k2, 1]
            y = jnp.concatenate([xr * kr - xi * ki, xr * ki + xi * kr], axis=0).astype(BF16)
            s = jnp.dot(g2_ref[k2], y, preferred_element_type=F32)
            b_s[blk(2 * k2), :] = s[:FFT_L]
            b_s[blk(2 * k2 + 1), :] = s[FFT_L:]
        elif accumulate:
            o_ref[k2, 0] += xr
            o_ref[k2, 1] -= xi
        else:
            o_ref[k2, 0] = xr
            o_ref[k2, 1] = xi
        return carry

    if spectrum_only:
        @pl.when(pl.program_id(1) == 0)
        def _():
            lax.fori_loop(0, n1, stage2, 0, unroll=k2_unroll)

        @pl.when(pl.program_id(1) != 0)
        def _():
            lax.fori_loop(0, n1, functools.partial(stage2, accumulate=True), 0, unroll=k2_unroll)
    else:
        lax.fori_loop(0, n1, stage2, 0, unroll=k2_unroll)

    if not spectrum_only:
        def stage3(j, carry):
            x = jnp.concatenate([b_s[pl.ds(2 * j, 2 * n1, stride=P), :], b_s[pl.ds(2 * j + 1, 2 * n1, stride=P), :]], axis=1)
            r = jnp.dot(g1_ref[...], x.astype(BF16), preferred_element_type=F32)
            y_s[pl.ds(2 * j, C, stride=P), :] = r[:, :LANES]
            y_s[pl.ds(2 * j + 1, C, stride=P), :] = r[:, LANES:]
            return carry
        lax.fori_loop(0, FFT_L // 2, stage3, 0, unroll=FFT_PAIR_UNROLL)
        for c in range(C):
            rows = slice(c * FFT_L, (c + 1) * FFT_L)
            o_ref[rows, :] = epilogue(c, y_s[c * P:c * P + FFT_L, :])


def _hyena_kernel(x0_ref, x1_ref, v_ref, w0_ref, w1_ref, wv_ref, b0_ref, b1_ref, bv_ref, hb_ref,
                  f1_ref, f2_ref, g2_ref, kf_ref, g1_ref, o_ref, z_s, x0_s, a_s, b_s, y_s):
    L = x0_ref.shape[0]
    row = lax.broadcasted_iota(jnp.int32, (L, 1), 0)

    def conv3(u_ref, w_ref, b_ref):
        u = u_ref[...].astype(F32)
        prev = jnp.where(row == 0, 0.0, pltpu.roll(u, 1, 0))
        nxt = jnp.where(row == L - 1, 0.0, pltpu.roll(u, L - 1, 0))
        return w_ref[0:1, :] * prev + w_ref[1:2, :] * u + w_ref[2:3, :] * nxt + b_ref[...]

    P = FFT_PITCH
    chunk = lambda c: slice(c * P, c * P + FFT_L)
    z = conv3(v_ref, wv_ref, bv_ref) * conv3(x1_ref, w1_ref, b1_ref)
    for c in range(L // FFT_L):
        z_s[chunk(c), :] = z[c * FFT_L:(c + 1) * FFT_L, :]
    x0_s[...] = conv3(x0_ref, w0_ref, b0_ref)

    def epilogue(c, y):
        return ((y + z_s[chunk(c), :] * hb_ref[...]) * x0_s[c * FFT_L:(c + 1) * FFT_L, :]).astype(BF16)

    _fft_kernel(z_s, f1_ref, f2_ref, g2_ref, kf_ref, g1_ref, o_ref, a_s, b_s, y_s, spectrum_only=False, epilogue=epilogue,
                z_pitch=P)


def _fft_specs(tabs):
    C, n1 = tabs["C"], tabs["n1"]
    specs = [_const_spec((2 * n1, C)),
             _const_spec((n1, 2 * FFT_L, 2 * FFT_L)), _const_spec((n1, 2 * FFT_L, 2 * FFT_L)),
             pl.BlockSpec((n1, 2, FFT_L, LANES), lambda w, b: (0, 0, 0, w)),
             _const_spec((C, 2 * n1))]
    scratch = [pltpu.VMEM((FFT_L * FFT_PITCH, LANES), F32), pltpu.VMEM((2 * n1 * FFT_PITCH, LANES), F32),
               pltpu.VMEM((C * FFT_PITCH, LANES), F32)]
    return specs, scratch


def _filter_spectrum(tabs, taps):
    _, L, W = taps.shape
    n1 = tabs["n1"]
    specs, scratch = _fft_specs(tabs)

    def spectrum_kernel(z_ref, f1_ref, f2_ref, o_ref, a_s):
        _fft_kernel(z_ref, f1_ref, f2_ref, None, None, None, o_ref, a_s, None, None, spectrum_only=True)

    return pl.pallas_call(
        spectrum_kernel,
        grid=(W // LANES, 2),
        in_specs=[pl.BlockSpec((None, L, LANES), lambda w, b: (b, 0, w))] + specs[:2],
        out_specs=pl.BlockSpec((n1, 2, FFT_L, LANES), lambda w, b: (0, 0, 0, w)),
        out_shape=jax.ShapeDtypeStruct((n1, 2, FFT_L, W), F32),
        scratch_shapes=scratch[:1],
        compiler_params=_cparams("arbitrary", "arbitrary"),
        name="fft_spectrum",
    )(taps, tabs["f1"], tabs["f2"])


def _hyena(hy, conv_w, conv_b, hy_bias, taps, B, L):
    W = HY_WIDTH
    nw = W // LANES
    tabs = _fft_tables(L)
    kf = _filter_spectrum(tabs, taps)
    specs, scratch = _fft_specs(tabs)
    cb = conv_b.reshape(1, 3 * W)
    blk = lambda off: pl.BlockSpec((L, LANES), lambda w, b: (b, off + w))
    wblk = lambda off: pl.BlockSpec((3, LANES), lambda w, b: (0, off + w))
    bblk = lambda off: pl.BlockSpec((1, LANES), lambda w, b: (0, off + w))
    return pl.pallas_call(
        _hyena_kernel,
        grid=(nw, B),
        in_specs=[blk(0), blk(nw), blk(2 * nw), wblk(0), wblk(nw), wblk(2 * nw), bblk(0), bblk(nw), bblk(2 * nw),
                  bblk(0)] + specs,
        out_specs=pl.BlockSpec((L, LANES), lambda w, b: (b, w)),
        out_shape=jax.ShapeDtypeStruct((B * L, W), BF16),
        scratch_shapes=[pltpu.VMEM((L // FFT_L * FFT_PITCH, LANES), F32), pltpu.VMEM((L, LANES), F32)] + scratch,
        compiler_params=_cparams("arbitrary", "arbitrary"),
        name="hyena",
    )(hy, hy, hy, conv_w, conv_w, conv_w, cb, cb, cb, hy_bias.reshape(1, W), tabs["f1"], tabs["f2"], tabs["g2"], kf,
      tabs["g1"])


def _merge_kernel(og_ref, oh_ref, gates_ref, x_ref, wbg_ref, wbh_ref, wo_ref, n2_ref, rw_ref, rb_ref,
                  h1_ref, hn3_ref, idx_ref, wts_ref):
    mg = jnp.dot(og_ref[...], wbg_ref[...], preferred_element_type=F32)
    mh = jnp.dot(oh_ref[...], wbh_ref[...], preferred_element_type=F32)
    gg = jax.nn.sigmoid(gates_ref[:, :D_MODEL].astype(F32))
    gh = jax.nn.sigmoid(gates_ref[:, D_MODEL:].astype(F32))
    merged = (gg * mg + gh * mh).astype(BF16)
    h1 = x_ref[...] + jnp.dot(merged, wo_ref[...], preferred_element_type=F32)
    h1_ref[...] = h1
    hn = h1 * lax.rsqrt(jnp.mean(h1 * h1, axis=-1, keepdims=True) + EPS) * n2_ref[...]
    _store_token_rows(hn3_ref, hn)
    hh = hn.astype(BF16)
    hl = (hn - hh.astype(F32)).astype(BF16)
    p1 = jnp.dot(hh, rw_ref[...], preferred_element_type=F32)
    p2 = jnp.dot(hl, rw_ref[:, :ROUTE_PAD], preferred_element_type=F32)
    logits = p1[:, :ROUTE_PAD] + p1[:, ROUTE_PAD:] + p2 + rb_ref[...]
    lane = lax.broadcasted_iota(jnp.int32, logits.shape, 1)
    logits = jnp.where(lane < N_EXPERTS, logits, -jnp.inf)
    idx_out = jnp.zeros(logits.shape, jnp.int32)
    val_out = jnp.zeros(logits.shape, F32)
    v0 = None
    for j in range(TOP_K):
        m = jnp.max(logits, axis=-1, keepdims=True)
        sel = jnp.min(jnp.where(logits == m, lane, ROUTE_PAD), axis=-1, keepdims=True)
        if j == 0:
            v0 = m
        idx_out = jnp.where(lane == j, sel, idx_out)
        val_out = jnp.where(lane == j, jnp.exp(m - v0), val_out)
        logits = jnp.where(lane == sel, -jnp.inf, logits)
    idx_ref[...] = idx_out[:, :TOP_K]
    wts_ref[...] = val_out / jnp.sum(val_out, axis=-1, keepdims=True)


def _merge(o_gla, o_hy, gates, x2, wbg, wbh, wo, norm2_w, rw_pad, rb_pad):
    T = x2.shape[0]
    tm = min(512, T)
    row = lambda n: pl.BlockSpec((tm, n), lambda i: (i, 0))
    return pl.pallas_call(
        _merge_kernel,
        grid=(T // tm,),
        in_specs=[row(GLA_VAL), row(HY_WIDTH), row(N_GATES), row(D_MODEL),
                  _const_spec((GLA_VAL, D_MODEL)), _const_spec((HY_WIDTH, D_MODEL)), _const_spec((D_MODEL, D_MODEL)),
                  _const_spec((1, D_MODEL)), _const_spec((D_MODEL, 2 * ROUTE_PAD)), _const_spec((1, ROUTE_PAD))],
        out_specs=[row(D_MODEL), pl.BlockSpec((tm * TOK_ROWS, LANES), lambda i: (i, 0)), row(TOP_K), row(ROUTE_PAD)],
        out_shape=[jax.ShapeDtypeStruct((T, D_MODEL), F32), jax.ShapeDtypeStruct((T * TOK_ROWS, LANES), F32),
                   jax.ShapeDtypeStruct((T, TOP_K), jnp.int32), jax.ShapeDtypeStruct((T, ROUTE_PAD), F32)],
        compiler_params=_cparams("parallel"),
        name="merge_route",
    )(o_gla, o_hy, gates, x2, wbg, wbh, wo, norm2_w.reshape(1, D_MODEL), rw_pad, rb_pad)


MOE_TM = 512
MOE_TC = 256
DMA_UNROLL = 8


DISPATCH_SLOTS = 3


def _dispatch_kernel(last_ref, pos_ref, hn_ref, xs_ref, zero_s, buf, sem, lsems, ssems):
    i = pl.program_id(0)
    n = pl.num_programs(0)
    tc = pos_ref.shape[2] // TOP_K
    rows = tc * TOK_ROWS
    zrows = zero_s.shape[0]
    slot = lax.rem(i, DISPATCH_SLOTS)
    prev_slot = lax.rem(i + DISPATCH_SLOTS - 1, DISPATCH_SLOTS)

    def load(blk, s):
        src = hn_ref.at[pl.ds(pl.multiple_of(blk * rows, rows), rows)]
        return pltpu.make_async_copy(src, buf.at[s], lsems.at[s])

    def wait_copies(s):
        for _ in range(TOP_K):
            pltpu.make_async_copy(buf.at[s], xs_ref.at[pl.ds(0, rows)], ssems.at[s]).wait()

    @pl.when(i == 0)
    def _():
        load(0, 0).start()

        @pl.when(n > 1)
        def _():
            load(1, 1).start()

        zero_s[...] = jnp.zeros_like(zero_s)
        n_fill = last_ref.shape[0]

        def zfill(e, c):
            @pl.when(last_ref[e] >= 0)
            def _():
                dst = pl.multiple_of(last_ref[e] * zrows, zrows)
                pltpu.make_async_copy(zero_s, xs_ref.at[pl.ds(dst, zrows)], sem).start()
            return c
        lax.fori_loop(0, n_fill, zfill, 0)

        def zwait(e, c):
            @pl.when(last_ref[e] >= 0)
            def _():
                pltpu.make_async_copy(zero_s, xs_ref.at[pl.ds(0, zrows)], sem).wait()
            return c
        lax.fori_loop(0, n_fill, zwait, 0)

    load(i, slot).wait()

    def issue(r, c):
        src = buf.at[slot, pl.ds(pl.multiple_of(r * TOK_ROWS, TOK_ROWS), TOK_ROWS)]
        for j in range(TOP_K):
            dst = pl.multiple_of(pos_ref[0, 0, r * TOP_K + j] * TOK_ROWS, TOK_ROWS)
            pltpu.make_async_copy(src, xs_ref.at[pl.ds(dst, TOK_ROWS)], ssems.at[slot]).start(priority=j % 2)
        return c
    lax.fori_loop(0, tc, issue, 0, unroll=DMA_UNROLL // TOP_K)

    @pl.when(i > 0)
    def _():
        wait_copies(prev_slot)

    @pl.when(i + 2 < n)
    def _():
        load(i + 2, prev_slot).start()

    @pl.when(i == n - 1)
    def _():
        wait_copies(slot)


def _dispatch(hn3, pos, last_tile, P):
    T = hn3.shape[0] // TOK_ROWS
    tc = min(MOE_TC, T)
    nt = T // tc
    return pl.pallas_call(
        _dispatch_kernel,
        grid_spec=pltpu.PrefetchScalarGridSpec(
            num_scalar_prefetch=1,
            grid=(nt,),
            in_specs=[pl.BlockSpec((1, 1, tc * TOP_K), lambda i, lt: (i, 0, 0), memory_space=pltpu.SMEM),
                      pl.BlockSpec(memory_space=pl.ANY)],
            out_specs=pl.BlockSpec(memory_space=pl.ANY),
            scratch_shapes=[pltpu.VMEM((MOE_TM * TOK_ROWS, LANES), F32),
                            pltpu.VMEM((DISPATCH_SLOTS, tc * TOK_ROWS, LANES), F32),
                            pltpu.SemaphoreType.DMA(()),
                            pltpu.SemaphoreType.DMA((DISPATCH_SLOTS,)), pltpu.SemaphoreType.DMA((DISPATCH_SLOTS,))],
        ),
        out_shape=jax.ShapeDtypeStruct((P * TOK_ROWS, LANES), F32),
        compiler_params=_cparams("arbitrary"),
        name="moe_dispatch",
    )(last_tile, pos.reshape(nt, 1, tc * TOP_K), hn3)


def _expert_kernel(te_ref, nused_ref, xs_ref, wgu_ref, bgu_ref, wd_ref, bd_ref, o_ref, wgu_s, wd_s):
    i = pl.program_id(0)
    prev = te_ref[jnp.maximum(i - 1, 0)]

    @pl.when((i == 0) | (te_ref[i] != prev))
    def _():
        wgu_s[...] = wgu_ref[0].astype(BF16)
        wd_s[...] = wd_ref[0].astype(BF16)

    @pl.when(i < nused_ref[0])
    def _():
        x = _load_token_rows(xs_ref, 0, xs_ref.shape[0] // TOK_ROWS).astype(BF16)
        gu = jnp.dot(x, wgu_s[...], preferred_element_type=F32) + bgu_ref[0]
        gate = jnp.minimum(gu[:, :D_FF], SWIGLU_LIMIT)
        up = jnp.clip(gu[:, D_FF:], -SWIGLU_LIMIT, SWIGLU_LIMIT)
        act = gate * jax.nn.sigmoid(SWIGLU_ALPHA * gate) * (up + 1.0)
        y = jnp.dot(act.astype(BF16), wd_s[...], preferred_element_type=F32) + bd_ref[0]
        _store_token_rows(o_ref, y)

    @pl.when(i >= nused_ref[0])
    def _():
        o_ref[...] = jnp.zeros_like(o_ref)


def _experts(xs3, tile_expert, n_used, w_gate_up, b_gate_up, w_down, b_down):
    P = xs3.shape[0] // TOK_ROWS
    tm = MOE_TM
    nt = P // tm
    live = lambda i, te, nu: jnp.minimum(i, nu[0] - 1)
    return pl.pallas_call(
        _expert_kernel,
        grid_spec=pltpu.PrefetchScalarGridSpec(
            num_scalar_prefetch=2,
            grid=(nt,),
            in_specs=[
                pl.BlockSpec((tm * TOK_ROWS, LANES), lambda i, te, nu: (live(i, te, nu), 0)),
                pl.BlockSpec((1, D_MODEL, 2 * D_FF), lambda i, te, nu: (te[i], 0, 0)),
                pl.BlockSpec((1, 1, 2 * D_FF), lambda i, te, nu: (te[i], 0, 0)),
                pl.BlockSpec((1, D_FF, D_MODEL), lambda i, te, nu: (te[i], 0, 0)),
                pl.BlockSpec((1, 1, D_MODEL), lambda i, te, nu: (te[i], 0, 0)),
            ],
            out_specs=pl.BlockSpec((tm * TOK_ROWS, LANES), lambda i, te, nu: (i, 0)),
            scratch_shapes=[pltpu.VMEM((D_MODEL, 2 * D_FF), BF16), pltpu.VMEM((D_FF, D_MODEL), BF16)],
        ),
        out_shape=jax.ShapeDtypeStruct((P * TOK_ROWS, LANES), F32),
        compiler_params=_cparams("arbitrary"),
        name="moe_experts",
    )(tile_expert, n_used, xs3, w_gate_up, b_gate_up.reshape(N_EXPERTS, 1, 2 * D_FF), w_down,
      b_down.reshape(N_EXPERTS, 1, D_MODEL))


COMBINE_SLOTS = 3
COMBINE_GROUP = 8


def _combine_kernel(pos_ref, pos1_ref, pos2_ref, wts_ref, h1_ref, nf_ref, ys_ref, o_ref, buf, sems):
    i = pl.program_id(0)
    n = pl.num_programs(0)
    tc = o_ref.shape[0]
    G = COMBINE_GROUP
    slot = lax.rem(i, COMBINE_SLOTS)
    slot2 = lax.rem(i + 2, COMBINE_SLOTS)

    def start_rows(p_ref, s, r):
        for j in range(TOP_K):
            src = pl.multiple_of(p_ref[0, 0, r * TOP_K + j] * TOK_ROWS, TOK_ROWS)
            dst = pl.multiple_of((j * tc + r) * TOK_ROWS, TOK_ROWS)
            pltpu.make_async_copy(ys_ref.at[pl.ds(src, TOK_ROWS)], buf.at[s, pl.ds(dst, TOK_ROWS)],
                                  sems.at[s]).start(priority=j % 2)

    def gather(p_ref, s):
        def issue(r, c):
            start_rows(p_ref, s, r)
            return c
        lax.fori_loop(0, tc, issue, 0, unroll=DMA_UNROLL // TOP_K)

    def combine_group(g):
        rows = slice(g * G, (g + 1) * G)
        w = wts_ref[rows, :]
        h = h1_ref[rows, :]
        for j in range(TOP_K):
            h = h + w[:, j:j + 1] * _load_token_rows(buf.at[slot], j * tc + g * G, G)
        o_ref[rows, :] = h * lax.rsqrt(jnp.mean(h * h, axis=-1, keepdims=True) + EPS) * nf_ref[...]

    @pl.when(i == 0)
    def _():
        gather(pos_ref, 0)

        @pl.when(n > 1)
        def _():
            gather(pos1_ref, 1)

    pltpu.make_async_copy(ys_ref.at[pl.ds(0, TOP_K * tc * TOK_ROWS)], buf.at[slot], sems.at[slot]).wait()

    @pl.when(i + 2 < n)
    def _():
        for g in range(tc // G):
            for r in range(g * G, (g + 1) * G):
                start_rows(pos2_ref, slot2, r)
            combine_group(g)

    @pl.when(i + 2 >= n)
    def _():
        for g in range(tc // G):
            combine_group(g)


def _combine(pos, wts, h1, norm_f_w, ys3):
    T = h1.shape[0]
    tc = min(MOE_TC, T)
    nt = T // tc
    pos3 = pos.reshape(nt, 1, tc * TOP_K)
    ahead = lambda k: pl.BlockSpec((1, 1, tc * TOP_K), lambda i: (jnp.minimum(i + k, nt - 1), 0, 0), memory_space=pltpu.SMEM)
    return pl.pallas_call(
        _combine_kernel,
        grid=(nt,),
        in_specs=[
            ahead(0), ahead(1), ahead(2),
            pl.BlockSpec((tc, ROUTE_PAD), lambda i: (i, 0)),
            pl.BlockSpec((tc, D_MODEL), lambda i: (i, 0)),
            _const_spec((1, D_MODEL)),
            pl.BlockSpec(memory_space=pl.ANY),
        ],
        out_specs=pl.BlockSpec((tc, D_MODEL), lambda i: (i, 0)),
        out_shape=jax.ShapeDtypeStruct((T, D_MODEL), F32),
        scratch_shapes=[pltpu.VMEM((COMBINE_SLOTS, TOP_K * tc * TOK_ROWS, LANES), F32),
                        pltpu.SemaphoreType.DMA((COMBINE_SLOTS,))],
        compiler_params=_cparams("arbitrary"),
        name="moe_combine",
    )(pos3, pos3, pos3, wts, h1, norm_f_w.reshape(1, D_MODEL), ys3)


def _route_plan(top_idx, T):
    tm = MOE_TM
    nt = (T * TOP_K) // tm + N_EXPERTS
    oh = (top_idx[:, :, None] == jnp.arange(N_EXPERTS, dtype=jnp.int32)[None, None, :]).astype(jnp.int32)
    cnt = jnp.sum(oh, axis=1)
    csum = jnp.cumsum(cnt, axis=0)
    counts = csum[-1]
    tiles = (counts + tm - 1) // tm
    tile_end = jnp.cumsum(tiles)
    offs = (tile_end - tiles) * tm
    pos = jnp.sum(oh * (offs[None, :] + csum - cnt)[:, None, :], axis=2).reshape(-1)
    tile_id = jnp.arange(nt, dtype=jnp.int32)
    tile_expert = jnp.minimum(jnp.sum((tile_end[None, :] <= tile_id[:, None]).astype(jnp.int32), axis=1),
                              N_EXPERTS - 1).astype(jnp.int32)
    tail = tile_end[-1] + jnp.arange(N_EXPERTS, dtype=jnp.int32)
    last_tile = jnp.concatenate([jnp.where(tiles > 0, tile_end - 1, -1), jnp.where(tail < nt, tail, -1)]).astype(jnp.int32)
    n_used = tile_end[-1:].astype(jnp.int32)
    return pos.astype(jnp.int32), tile_expert, last_tile, n_used, nt * tm


def kernel(x, norm1_w, w_in, gla_gate_w2, gla_gate_b, gla_norm_w, hy_conv_w, hy_conv_b, hy_freq, hy_ffn_w1, hy_ffn_b1,
           hy_ffn_w2, hy_ffn_b2, hy_ffn_w3, hy_decay, hy_bias, w_branch_gla, w_branch_hy, w_out, norm2_w, router_w,
           router_b, w_gate_up, b_gate_up, w_down, b_down, norm_f_w):
    B, L, D = x.shape
    T = B * L
    x2 = x.reshape(T, D)

    s_a = N_QKVR
    s_h = s_a + 2 * GLA_RANK
    w_b = w_in.astype(BF16)
    w_cat = jnp.concatenate(
        [w_b[:, :s_a], w_b[:, s_h:], w_b[:, s_a:s_h], jnp.zeros((D, ALOW_PAD - 2 * GLA_RANK), BF16)], axis=1)
    qkvr, hy, gates, alow = _in_proj(x2, norm1_w, w_cat)

    w2 = jnp.zeros((2, ALOW_PAD, GLA_KEY), F32)
    w2 = w2.at[0, :GLA_RANK].set(gla_gate_w2[0]).at[1, GLA_RANK:2 * GLA_RANK].set(gla_gate_w2[1])
    o_gla = _gla(qkvr, alow, w2, gla_gate_b.reshape(2, 1, GLA_KEY), gla_norm_w, B, L)

    taps = _hy_filter(L, hy_freq, hy_ffn_w1, hy_ffn_b1, hy_ffn_w2, hy_ffn_b2, hy_ffn_w3, hy_decay)
    o_hy = _hyena(hy, hy_conv_w, hy_conv_b, hy_bias, taps, B, L)

    rw_f = jnp.zeros((D, ROUTE_PAD), F32).at[:, :N_EXPERTS].set(router_w)
    rw_hi = rw_f.astype(BF16)
    rw_pad = jnp.concatenate([rw_hi, (rw_f - rw_hi.astype(F32)).astype(BF16)], axis=1)
    rb_pad = jnp.zeros((1, ROUTE_PAD), F32).at[0, :N_EXPERTS].set(router_b)
    h1, hn3, top_idx, wts_pad = _merge(o_gla, o_hy, gates, x2, w_branch_gla.astype(BF16), w_branch_hy.astype(BF16),
                                       w_out.astype(BF16), norm2_w, rw_pad, rb_pad)

    pos, tile_expert, last_tile, n_used, P = _route_plan(top_idx, T)
    xs3 = _dispatch(hn3, pos, last_tile, P)
    ys3 = _experts(xs3, tile_expert, n_used, w_gate_up, b_gate_up, w_down, b_down)
    out = _combine(pos, wts_pad, h1, norm_f_w, ys3)
    return out.reshape(B, L, D)
```

```python
import functools
import math

import numpy as np
import jax
import jax.numpy as jnp
from jax import lax
from jax.experimental import pallas as pl
from jax.experimental.pallas import tpu as pltpu

D_MODEL = 1024
EPS = 1e-6
GLA_HEADS = 4
GLA_DK = 128
GLA_DV = 256
GLA_KEY = GLA_HEADS * GLA_DK
GLA_VAL = GLA_HEADS * GLA_DV
GLA_RANK = 16
GLA_GATE_TEMP = 16.0
GLA_CHUNK = 64
HY_WIDTH = D_MODEL
HY_BANDS = 16
HY_EMB = 1 + 2 * HY_BANDS
HY_FFN = 64
N_EXPERTS = 32
TOP_K = 4
D_FF = D_MODEL
SWIGLU_LIMIT = 7.0
SWIGLU_ALPHA = 1.702

LANES = 128
ALOW_PAD = LANES
ROUTE_PAD = LANES
VMEM_LIMIT = 56 * 1024 * 1024

F32 = jnp.float32
BF16 = jnp.bfloat16
HIGHEST = lax.Precision.HIGHEST


def _cparams(*sem):
    return pltpu.CompilerParams(dimension_semantics=sem, vmem_limit_bytes=VMEM_LIMIT)


def _const_spec(shape):
    nd = len(shape)
    return pl.BlockSpec(shape, lambda *_: (0,) * nd, pipeline_mode=pl.Buffered(1))


TOK_ROWS = D_MODEL // LANES


def _store_token_rows(ref, val):
    n = val.shape[0]
    for s_ in range(TOK_ROWS):
        ref[pl.ds(s_, n, stride=TOK_ROWS), :] = val[:, s_ * LANES:(s_ + 1) * LANES]


def _load_token_rows(ref, first, n):
    return jnp.concatenate(
        [ref[pl.ds(first * TOK_ROWS + s_, n, stride=TOK_ROWS), :] for s_ in range(TOK_ROWS)], axis=1)


N_QKVR = 2 * GLA_KEY + 2 * GLA_VAL
N_HY = 3 * HY_WIDTH
N_GATES = 2 * D_MODEL
IN_COLS = N_QKVR + N_HY + N_GATES + ALOW_PAD


def _inproj_kernel(x_ref, nw_ref, w_ref, qkvr_ref, hy_ref, gates_ref, alow_ref):
    x = x_ref[...]
    ms = jnp.mean(x * x, axis=-1, keepdims=True)
    xn = (x * lax.rsqrt(ms + EPS) * nw_ref[...]).astype(BF16)
    c0, c1, c2 = N_QKVR, N_QKVR + N_HY, N_QKVR + N_HY + N_GATES
    qkvr_ref[...] = jnp.dot(xn, w_ref[:, :c0], preferred_element_type=F32).astype(BF16)
    hy_ref[...] = jnp.dot(xn, w_ref[:, c0:c1], preferred_element_type=F32).astype(BF16)
    gates_ref[...] = jnp.dot(xn, w_ref[:, c1:c2], preferred_element_type=F32).astype(BF16)
    alow_ref[...] = jnp.dot(xn, w_ref[:, c2:], preferred_element_type=F32)


def _in_proj(x2, norm1_w, w_cat):
    T = x2.shape[0]
    tm = min(512, T)
    return pl.pallas_call(
        _inproj_kernel,
        grid=(T // tm,),
        in_specs=[
            pl.BlockSpec((tm, D_MODEL), lambda i: (i, 0)),
            _const_spec((1, D_MODEL)),
            _const_spec((D_MODEL, IN_COLS)),
        ],
        out_specs=[
            pl.BlockSpec((tm, N_QKVR), lambda i: (i, 0)),
            pl.BlockSpec((tm, N_HY), lambda i: (i, 0)),
            pl.BlockSpec((tm, N_GATES), lambda i: (i, 0)),
            pl.BlockSpec((tm, ALOW_PAD), lambda i: (i, 0)),
        ],
        out_shape=[
            jax.ShapeDtypeStruct((T, N_QKVR), BF16),
            jax.ShapeDtypeStruct((T, N_HY), BF16),
            jax.ShapeDtypeStruct((T, N_GATES), BF16),
            jax.ShapeDtypeStruct((T, ALOW_PAD), F32),
        ],
        compiler_params=_cparams("parallel"),
        name="in_proj",
    )(x2, norm1_w.reshape(1, D_MODEL), w_cat)


GLA_PREP_ROWS = 256
GLA_PREP_UNROLL = 4
GLA_INTRA_UNROLL = 8
GLA_SCAN_UNROLL = 64


def _log_sigmoid(x):
    return jnp.minimum(x, 0.0) - jnp.log1p(jnp.exp(-jnp.abs(x)))


def _gla_kernel(q_ref, k_ref, v_ref, r_ref, alow_ref, w2_ref, gb_ref, nw_ref, o_ref,
                qtf, ktf, kef, qtb, ktb, keb, decf, decb, of_s, ob_s, stf, stb):
    L = q_ref.shape[0]
    C = GLA_CHUNK
    DK = GLA_DK
    n_chunks = L // C
    R = min(GLA_PREP_ROWS, L)
    cpr = R // C
    scale = DK ** -0.5
    nt = (((1,), (1,)), ((), ()))
    tn = (((0,), (0,)), ((), ()))

    ri = lax.broadcasted_iota(jnp.int32, (R, R), 0)
    ci = lax.broadcasted_iota(jnp.int32, (R, R), 1)
    same = (ri // C) == (ci // C)
    mask_f = same & (ri >= ci)
    mask_b = same & (ri < ci)
    tri = jnp.where(mask_f, 1.0, 0.0).astype(BF16)
    w2 = jnp.concatenate([w2_ref[0], w2_ref[1]], axis=1).astype(BF16)
    gb = jnp.concatenate([gb_ref[0], gb_ref[1]], axis=1)

    def prep(g, carry):
        rows = pl.ds(pl.multiple_of(g * R, R), R)
        logits = jnp.dot(alow_ref[rows, :].astype(BF16), w2, preferred_element_type=F32) + gb
        lg = _log_sigmoid(logits) * (1.0 / GLA_GATE_TEMP)
        hi = lg.astype(BF16)
        lo = (lg - hi.astype(F32)).astype(BF16)
        b = jnp.dot(tri, hi, preferred_element_type=F32) + jnp.dot(tri, lo, preferred_element_type=F32)
        tot = jnp.concatenate(
            [jnp.broadcast_to(b[c * C + C - 1:c * C + C, :], (C, 2 * DK)) for c in range(cpr)], axis=0)
        q = q_ref[rows, :].astype(F32) * scale
        k = k_ref[rows, :].astype(F32)
        dtot = jnp.exp(tot)
        for d, (qt, kt, ke, dec) in enumerate(((qtf, ktf, kef, decf), (qtb, ktb, keb, decb))):
            sl = slice(d * DK, (d + 1) * DK)
            bd, td, ld = b[:, sl], tot[:, sl], lg[:, sl]
            if d == 0:
                e_q, e_end = bd, td - bd
            else:
                e_q, e_end = td - bd + ld, bd - ld
            qt[rows, :] = (q * jnp.exp(e_q)).astype(BF16)
            kt[rows, :] = (k * jnp.exp(-e_q)).astype(BF16)
            ke[rows, :] = (k * jnp.exp(e_end)).astype(BF16)
            for c in range(cpr):
                dec[pl.ds(g * cpr + c, 1), :] = dtot[c * C:c * C + 1, sl]
        return carry

    lax.fori_loop(0, L // R, prep, 0, unroll=min(GLA_PREP_UNROLL, L // R))

    def intra(g, carry):
        rows = pl.ds(pl.multiple_of(g * R, R), R)
        vb = v_ref[rows, :]
        for qt, kt, mask, out in ((qtf, ktf, mask_f, of_s), (qtb, ktb, mask_b, ob_s)):
            s = lax.dot_general(qt[rows, :], kt[rows, :], nt, preferred_element_type=F32)
            s = jnp.where(mask, s, 0.0).astype(BF16)
            out[rows, :] = jnp.dot(s, vb, preferred_element_type=F32)
        return carry

    lax.fori_loop(0, L // R, intra, 0, unroll=min(GLA_INTRA_UNROLL, L // R))

    def chunk(n, st, qt, ke, dec, out):
        rows = pl.ds(pl.multiple_of(n * C, C), C)
        s = st[...]
        out[rows, :] += lax.dot_general(qt[rows, :], s.astype(BF16), nt, preferred_element_type=F32)
        upd = lax.dot_general(v_ref[rows, :], ke[rows, :], tn, preferred_element_type=F32)
        st[...] = dec[pl.ds(n, 1), :] * s + upd

    def step(i, carry):
        chunk(i, stf, qtf, kef, decf, of_s)
        chunk(n_chunks - 1 - i, stb, qtb, keb, decb, ob_s)
        return carry

    stf[...] = jnp.zeros_like(stf)
    stb[...] = jnp.zeros_like(stb)
    lax.fori_loop(0, n_chunks, step, 0, unroll=min(GLA_SCAN_UNROLL, n_chunks))

    o = of_s[...] + ob_s[...]
    o = o * lax.rsqrt(jnp.mean(o * o, axis=-1, keepdims=True) + EPS) * nw_ref[...]
    r = r_ref[...].astype(F32)
    o_ref[...] = (o * (r * jax.nn.sigmoid(r))).astype(BF16)


def _gla(qkvr, alow, w2, gb, norm_w, B, L):
    T = B * L
    H = GLA_HEADS
    kq = GLA_KEY // GLA_DK
    kv = 2 * GLA_KEY // GLA_DV
    kr = (2 * GLA_KEY + GLA_VAL) // GLA_DV
    return pl.pallas_call(
        _gla_kernel,
        grid=(B, H),
        in_specs=[
            pl.BlockSpec((L, GLA_DK), lambda b, h: (b, h)),
            pl.BlockSpec((L, GLA_DK), lambda b, h: (b, kq + h)),
            pl.BlockSpec((L, GLA_DV), lambda b, h: (b, kv + h)),
            pl.BlockSpec((L, GLA_DV), lambda b, h: (b, kr + h)),
            pl.BlockSpec((L, ALOW_PAD), lambda b, h: (b, 0)),
            pl.BlockSpec((2, ALOW_PAD, GLA_DK), lambda b, h: (0, 0, h)),
            pl.BlockSpec((2, 1, GLA_DK), lambda b, h: (0, 0, h)),
            pl.BlockSpec((1, GLA_DV), lambda b, h: (0, 0)),
        ],
        out_specs=pl.BlockSpec((L, GLA_DV), lambda b, h: (b, h)),
        out_shape=jax.ShapeDtypeStruct((T, GLA_VAL), BF16),
        scratch_shapes=[pltpu.VMEM((L, GLA_DK), BF16)] * 6
        + [pltpu.VMEM((L // GLA_CHUNK, GLA_DK), F32)] * 2
        + [pltpu.VMEM((L, GLA_DV), F32)] * 2
        + [pltpu.VMEM((GLA_DV, GLA_DK), F32)] * 2,
        compiler_params=_cparams("parallel", "parallel"),
        name="gla",
    )(qkvr, qkvr, qkvr, qkvr, alow, w2, gb, norm_w.reshape(1, GLA_DV))


HY_WT = 256


def _hy_filter_kernel(z_ref, t_ref, fr_ref, w1_ref, b1_ref, w2_ref, b2_ref, w3f_ref, w3b_ref, dec_ref, o_ref, h_s):
    @pl.when(pl.program_id(0) == 0)
    def _():
        fr = fr_ref[...]
        h1 = jnp.sin(fr * (jnp.dot(z_ref[...], w1_ref[...], preferred_element_type=F32, precision=HIGHEST) + b1_ref[...]))
        h_s[...] = jnp.sin(fr * (jnp.dot(h1, w2_ref[...], preferred_element_type=F32, precision=HIGHEST) + b2_ref[...]))

    h = h_s[...]
    t = t_ref[...]
    row = lax.broadcasted_iota(jnp.int32, (z_ref.shape[0], 1), 0)
    hq = h.astype(BF16)
    hf = jnp.dot(hq, w3f_ref[...].astype(BF16), preferred_element_type=F32)
    hf = hf * jnp.exp(-t * jnp.abs(dec_ref[0:1, :]))
    hb = jnp.dot(hq, w3b_ref[...].astype(BF16), preferred_element_type=F32)
    hb = hb * jnp.exp(-t * jnp.abs(dec_ref[1:2, :]))
    hb = jnp.where(row == 0, 0.0, hb)
    ss = jnp.sum(hf * hf, axis=0, keepdims=True) + jnp.sum(hb * hb, axis=0, keepdims=True)
    inv = lax.rsqrt(ss + EPS)
    o_ref[0] = hf * inv
    o_ref[1] = hb * inv


def _hy_filter(L, hy_freq, w1, b1, w2, b2, w3, hy_decay):
    W = HY_WIDTH
    t = np.linspace(0.0, 1.0, L, dtype=np.float32)[:, None]
    omega = (np.float32(2.0 * math.pi) * np.arange(L, dtype=np.float32) / np.float32(L)).astype(np.float32)
    f = np.linspace(1e-4, HY_BANDS - 1, HY_BANDS, dtype=np.float32)
    ang = (omega[:, None] * f[None, :]).astype(np.float32)
    z = np.concatenate([t, np.cos(ang), -np.sin(ang)], axis=-1).astype(np.float32)
    nw = W // HY_WT
    return pl.pallas_call(
        _hy_filter_kernel,
        grid=(nw,),
        in_specs=[
            _const_spec((L, HY_EMB)),
            _const_spec((L, 1)),
            _const_spec((1, HY_FFN)),
            _const_spec((HY_EMB, HY_FFN)),
            _const_spec((1, HY_FFN)),
            _const_spec((HY_FFN, HY_FFN)),
            _const_spec((1, HY_FFN)),
            pl.BlockSpec((HY_FFN, HY_WT), lambda j: (0, j)),
            pl.BlockSpec((HY_FFN, HY_WT), lambda j: (0, nw + j)),
            pl.BlockSpec((2, HY_WT), lambda j: (0, j)),
        ],
        out_specs=pl.BlockSpec((2, L, HY_WT), lambda j: (0, 0, j)),
        out_shape=jax.ShapeDtypeStruct((2, L, W), F32),
        scratch_shapes=[pltpu.VMEM((L, HY_FFN), F32)],
        compiler_params=_cparams("arbitrary"),
        name="hy_filter",
    )(jnp.asarray(z), jnp.asarray(t), hy_freq.reshape(1, HY_FFN), w1, b1.reshape(1, HY_FFN), w2,
      b2.reshape(1, HY_FFN), w3, w3, hy_decay)


FFT_L = LANES
FFT_PITCH = FFT_L + 8
FFT_PAIR_UNROLL = 32
FFT_K2_UNROLL = 40


def _fft_tables(L):
    N = 2 * L
    C = L // FFT_L
    n1 = N // FFT_L
    kh = min(n1, -(-(n1 // 2 + 1) // 8) * 8)
    l = np.arange(FFT_L)
    c = np.arange(C)
    k2 = np.arange(kh)
    pair_w = np.where((k2 == 0) | (k2 == n1 // 2), 1.0, np.where(k2 < n1 // 2, 2.0, 0.0))
    ang1 = 2 * np.pi * ((k2[:, None] * c[None, :]) % n1) / n1
    f1 = np.concatenate([np.cos(ang1), -np.sin(ang1)], axis=0)
    g1 = (np.stack([np.cos(ang1), -np.sin(ang1)], axis=-1) * pair_w[:, None, None]).transpose(1, 0, 2).reshape(C, 2 * kh) / N
    ang2 = 2 * np.pi * (((l[None, :, None] * l[None, None, :]) % FFT_L) / FFT_L
                        + ((k2[:, None, None] * l[None, None, :]) % N) / N)
    cs, sn = np.cos(ang2), np.sin(ang2)
    f2 = np.concatenate([np.concatenate([cs, sn], axis=2), np.concatenate([-sn, cs], axis=2)], axis=1)
    cst, snt = cs.transpose(0, 2, 1), sn.transpose(0, 2, 1)
    g2 = np.concatenate([np.concatenate([cst, -snt], axis=2), np.concatenate([snt, cst], axis=2)], axis=1)
    bf = lambda a: jnp.asarray(a.astype(np.float32)).astype(BF16)
    return dict(C=C, n1=kh, f1=bf(f1), g1=bf(g1), f2=bf(f2), g2=bf(g2))


def _fft_kernel(z_ref, f1_ref, f2_ref, g2_ref, kf_ref, g1_ref, o_ref, a_s, b_s, y_s, *, spectrum_only, epilogue=None,
                z_pitch=FFT_L):
    C = f1_ref.shape[1]
    n1 = f1_ref.shape[0] // 2
    P = FFT_PITCH
    blk = lambda j: pl.ds(pl.multiple_of(j * P, 8), FFT_L)

    def stage1(j, carry):
        x = jnp.concatenate([z_ref[pl.ds(2 * j, C, stride=z_pitch), :], z_ref[pl.ds(2 * j + 1, C, stride=z_pitch), :]], axis=1)
        r = jnp.dot(f1_ref[...], x.astype(BF16), preferred_element_type=F32)
        a_s[pl.ds(pl.multiple_of(2 * j * P, 8), 2 * n1), :] = r[:, :LANES]
        a_s[pl.ds(pl.multiple_of((2 * j + 1) * P, 8), 2 * n1), :] = r[:, LANES:]
        return carry
    lax.fori_loop(0, FFT_L // 2, stage1, 0, unroll=FFT_PAIR_UNROLL)
    k2_unroll = min(FFT_K2_UNROLL, n1)

    def stage2(k2, carry, *, accumulate=False):
        ar = a_s[pl.ds(k2, FFT_L, stride=P), :]
        ai = a_s[pl.ds(n1 + k2, FFT_L, stride=P), :]
        x = jnp.concatenate([ar, ai], axis=0).astype(BF16)
        s = jnp.dot(f2_ref[k2], x, preferred_element_type=F32)
        xr, xi = s[:FFT_L], s[FFT_L:]
        if not spectrum_only:
            kr, ki = kf_ref[k2, 0], kf_ref[k2, 1]
            y = jnp.concatenate([xr * kr - xi * ki, xr * ki + xi * kr], axis=0).astype(BF16)
            s = jnp.dot(g2_ref[k2], y, preferred_element_type=F32)
            b_s[blk(2 * k2), :] = s[:FFT_L]
            b_s[blk(2 * k2 + 1), :] = s[FFT_L:]
        elif accumulate:
            o_ref[k2, 0] += xr
            o_ref[k2, 1] -= xi
        else:
            o_ref[k2, 0] = xr
            o_ref[k2, 1] = xi
        return carry

    if spectrum_only:
        @pl.when(pl.program_id(1) == 0)
        def _():
            lax.fori_loop(0, n1, stage2, 0, unroll=k2_unroll)

        @pl.when(pl.program_id(1) != 0)
        def _():
            lax.fori_loop(0, n1, functools.partial(stage2, accumulate=True), 0, unroll=k2_unroll)
    else:
        lax.fori_loop(0, n1, stage2, 0, unroll=k2_unroll)

    if not spectrum_only:
        def stage3(j, carry):
            x = jnp.concatenate([b_s[pl.ds(2 * j, 2 * n1, stride=P), :], b_s[pl.ds(2 * j + 1, 2 * n1, stride=P), :]], axis=1)
            r = jnp.dot(g1_ref[...], x.astype(BF16), preferred_element_type=F32)
            y_s[pl.ds(2 * j, C, stride=P), :] = r[:, :LANES]
            y_s[pl.ds(2 * j + 1, C, stride=P), :] = r[:, LANES:]
            return carry
        lax.fori_loop(0, FFT_L // 2, stage3, 0, unroll=FFT_PAIR_UNROLL)
        for c in range(C):
            rows = slice(c * FFT_L, (c + 1) * FFT_L)
            o_ref[rows, :] = epilogue(c, y_s[c * P:c * P + FFT_L, :])


def _hyena_kernel(x0_ref, x1_ref, v_ref, w0_ref, w1_ref, wv_ref, b0_ref, b1_ref, bv_ref, hb_ref,
                  f1_ref, f2_ref, g2_ref, kf_ref, g1_ref, o_ref, z_s, x0_s, a_s, b_s, y_s):
    L = x0_ref.shape[0]
    row = lax.broadcasted_iota(jnp.int32, (L, 1), 0)

    def conv3(u_ref, w_ref, b_ref):
        u = u_ref[...].astype(F32)
        prev = jnp.where(row == 0, 0.0, pltpu.roll(u, 1, 0))
        nxt = jnp.where(row == L - 1, 0.0, pltpu.roll(u, L - 1, 0))
        return w_ref[0:1, :] * prev + w_ref[1:2, :] * u + w_ref[2:3, :] * nxt + b_ref[...]

    P = FFT_PITCH
    chunk = lambda c: slice(c * P, c * P + FFT_L)
    z = conv3(v_ref, wv_ref, bv_ref) * conv3(x1_ref, w1_ref, b1_ref)
    for c in range(L // FFT_L):
        z_s[chunk(c), :] = z[c * FFT_L:(c + 1) * FFT_L, :]
    x0_s[...] = conv3(x0_ref, w0_ref, b0_ref)

    def epilogue(c, y):
        return ((y + z_s[chunk(c), :] * hb_ref[...]) * x0_s[c * FFT_L:(c + 1) * FFT_L, :]).astype(BF16)

    _fft_kernel(z_s, f1_ref, f2_ref, g2_ref, kf_ref, g1_ref, o_ref, a_s, b_s, y_s, spectrum_only=False, epilogue=epilogue,
                z_pitch=P)


def _fft_specs(tabs):
    C, n1 = tabs["C"], tabs["n1"]
    specs = [_const_spec((2 * n1, C)),
             _const_spec((n1, 2 * FFT_L, 2 * FFT_L)), _const_spec((n1, 2 * FFT_L, 2 * FFT_L)),
             pl.BlockSpec((n1, 2, FFT_L, LANES), lambda w, b: (0, 0, 0, w)),
             _const_spec((C, 2 * n1))]
    scratch = [pltpu.VMEM((FFT_L * FFT_PITCH, LANES), F32), pltpu.VMEM((2 * n1 * FFT_PITCH, LANES), F32),
               pltpu.VMEM((C * FFT_PITCH, LANES), F32)]
    return specs, scratch


def _filter_spectrum(tabs, taps):
    _, L, W = taps.shape
    n1 = tabs["n1"]
    specs, scratch = _fft_specs(tabs)

    def spectrum_kernel(z_ref, f1_ref, f2_ref, o_ref, a_s):
        _fft_kernel(z_ref, f1_ref, f2_ref, None, None, None, o_ref, a_s, None, None, spectrum_only=True)

    return pl.pallas_call(
        spectrum_kernel,
        grid=(W // LANES, 2),
        in_specs=[pl.BlockSpec((None, L, LANES), lambda w, b: (b, 0, w))] + specs[:2],
        out_specs=pl.BlockSpec((n1, 2, FFT_L, LANES), lambda w, b: (0, 0, 0, w)),
        out_shape=jax.ShapeDtypeStruct((n1, 2, FFT_L, W), F32),
        scratch_shapes=scratch[:1],
        compiler_params=_cparams("arbitrary", "arbitrary"),
        name="fft_spectrum",
    )(taps, tabs["f1"], tabs["f2"])


def _hyena(hy, conv_w, conv_b, hy_bias, taps, B, L):
    W = HY_WIDTH
    nw = W // LANES
    tabs = _fft_tables(L)
    kf = _filter_spectrum(tabs, taps)
    specs, scratch = _fft_specs(tabs)
    cb = conv_b.reshape(1, 3 * W)
    blk = lambda off: pl.BlockSpec((L, LANES), lambda w, b: (b, off + w))
    wblk = lambda off: pl.BlockSpec((3, LANES), lambda w, b: (0, off + w))
    bblk = lambda off: pl.BlockSpec((1, LANES), lambda w, b: (0, off + w))
    return pl.pallas_call(
        _hyena_kernel,
        grid=(nw, B),
        in_specs=[blk(0), blk(nw), blk(2 * nw), wblk(0), wblk(nw), wblk(2 * nw), bblk(0), bblk(nw), bblk(2 * nw),
                  bblk(0)] + specs,
        out_specs=pl.BlockSpec((L, LANES), lambda w, b: (b, w)),
        out_shape=jax.ShapeDtypeStruct((B * L, W), BF16),
        scratch_shapes=[pltpu.VMEM((L // FFT_L * FFT_PITCH, LANES), F32), pltpu.VMEM((L, LANES), F32)] + scratch,
        compiler_params=_cparams("arbitrary", "arbitrary"),
        name="hyena",
    )(hy, hy, hy, conv_w, conv_w, conv_w, cb, cb, cb, hy_bias.reshape(1, W), tabs["f1"], tabs["f2"], tabs["g2"], kf,
      tabs["g1"])


def _merge_kernel(og_ref, oh_ref, gates_ref, x_ref, wbg_ref, wbh_ref, wo_ref, n2_ref, rw_ref, rb_ref,
                  h1_ref, hn3_ref, idx_ref, wts_ref):
    mg = jnp.dot(og_ref[...], wbg_ref[...], preferred_element_type=F32)
    mh = jnp.dot(oh_ref[...], wbh_ref[...], preferred_element_type=F32)
    gg = jax.nn.sigmoid(gates_ref[:, :D_MODEL].astype(F32))
    gh = jax.nn.sigmoid(gates_ref[:, D_MODEL:].astype(F32))
    merged = (gg * mg + gh * mh).astype(BF16)
    h1 = x_ref[...] + jnp.dot(merged, wo_ref[...], preferred_element_type=F32)
    h1_ref[...] = h1
    hn = h1 * lax.rsqrt(jnp.mean(h1 * h1, axis=-1, keepdims=True) + EPS) * n2_ref[...]
    _store_token_rows(hn3_ref, hn)
    hh = hn.astype(BF16)
    hl = (hn - hh.astype(F32)).astype(BF16)
    p1 = jnp.dot(hh, rw_ref[...], preferred_element_type=F32)
    p2 = jnp.dot(hl, rw_ref[:, :ROUTE_PAD], preferred_element_type=F32)
    logits = p1[:, :ROUTE_PAD] + p1[:, ROUTE_PAD:] + p2 + rb_ref[...]
    lane = lax.broadcasted_iota(jnp.int32, logits.shape, 1)
    logits = jnp.where(lane < N_EXPERTS, logits, -jnp.inf)
    idx_out = jnp.zeros(logits.shape, jnp.int32)
    val_out = jnp.zeros(logits.shape, F32)
    v0 = None
    for j in range(TOP_K):
        m = jnp.max(logits, axis=-1, keepdims=True)
        sel = jnp.min(jnp.where(logits == m, lane, ROUTE_PAD), axis=-1, keepdims=True)
        if j == 0:
            v0 = m
        idx_out = jnp.where(lane == j, sel, idx_out)
        val_out = jnp.where(lane == j, jnp.exp(m - v0), val_out)
        logits = jnp.where(lane == sel, -jnp.inf, logits)
    idx_ref[...] = idx_out[:, :TOP_K]
    wts_ref[...] = val_out / jnp.sum(val_out, axis=-1, keepdims=True)


def _merge(o_gla, o_hy, gates, x2, wbg, wbh, wo, norm2_w, rw_pad, rb_pad):
    T = x2.shape[0]
    tm = min(512, T)
    row = lambda n: pl.BlockSpec((tm, n), lambda i: (i, 0))
    return pl.pallas_call(
        _merge_kernel,
        grid=(T // tm,),
        in_specs=[row(GLA_VAL), row(HY_WIDTH), row(N_GATES), row(D_MODEL),
                  _const_spec((GLA_VAL, D_MODEL)), _const_spec((HY_WIDTH, D_MODEL)), _const_spec((D_MODEL, D_MODEL)),
                  _const_spec((1, D_MODEL)), _const_spec((D_MODEL, 2 * ROUTE_PAD)), _const_spec((1, ROUTE_PAD))],
        out_specs=[row(D_MODEL), pl.BlockSpec((tm * TOK_ROWS, LANES), lambda i: (i, 0)), row(TOP_K), row(ROUTE_PAD)],
        out_shape=[jax.ShapeDtypeStruct((T, D_MODEL), F32), jax.ShapeDtypeStruct((T * TOK_ROWS, LANES), F32),
                   jax.ShapeDtypeStruct((T, TOP_K), jnp.int32), jax.ShapeDtypeStruct((T, ROUTE_PAD), F32)],
        compiler_params=_cparams("parallel"),
        name="merge_route",
    )(o_gla, o_hy, gates, x2, wbg, wbh, wo, norm2_w.reshape(1, D_MODEL), rw_pad, rb_pad)


MOE_TM = 512
MOE_TC = 256
DMA_UNROLL = 8


DISPATCH_SLOTS = 3


def _dispatch_kernel(last_ref, pos_ref, hn_ref, xs_ref, zero_s, buf, sem, lsems, ssems):
    i = pl.program_id(0)
    n = pl.num_programs(0)
    tc = pos_ref.shape[2] // TOP_K
    rows = tc * TOK_ROWS
    zrows = zero_s.shape[0]
    slot = lax.rem(i, DISPATCH_SLOTS)
    prev_slot = lax.rem(i + DISPATCH_SLOTS - 1, DISPATCH_SLOTS)

    def load(blk, s):
        src = hn_ref.at[pl.ds(pl.multiple_of(blk * rows, rows), rows)]
        return pltpu.make_async_copy(src, buf.at[s], lsems.at[s])

    def wait_copies(s):
        for _ in range(TOP_K):
            pltpu.make_async_copy(buf.at[s], xs_ref.at[pl.ds(0, rows)], ssems.at[s]).wait()

    @pl.when(i == 0)
    def _():
        load(0, 0).start()

        @pl.when(n > 1)
        def _():
            load(1, 1).start()

        zero_s[...] = jnp.zeros_like(zero_s)
        n_fill = last_ref.shape[0]

        def zfill(e, c):
            @pl.when(last_ref[e] >= 0)
            def _():
                dst = pl.multiple_of(last_ref[e] * zrows, zrows)
                pltpu.make_async_copy(zero_s, xs_ref.at[pl.ds(dst, zrows)], sem).start()
            return c
        lax.fori_loop(0, n_fill, zfill, 0)

        def zwait(e, c):
            @pl.when(last_ref[e] >= 0)
            def _():
                pltpu.make_async_copy(zero_s, xs_ref.at[pl.ds(0, zrows)], sem).wait()
            return c
        lax.fori_loop(0, n_fill, zwait, 0)

    load(i, slot).wait()

    def issue(r, c):
        src = buf.at[slot, pl.ds(pl.multiple_of(r * TOK_ROWS, TOK_ROWS), TOK_ROWS)]
        for j in range(TOP_K):
            dst = pl.multiple_of(pos_ref[0, 0, r * TOP_K + j] * TOK_ROWS, TOK_ROWS)
            pltpu.make_async_copy(src, xs_ref.at[pl.ds(dst, TOK_ROWS)], ssems.at[slot]).start(priority=j % 2)
        return c
    lax.fori_loop(0, tc, issue, 0, unroll=DMA_UNROLL // TOP_K)

    @pl.when(i > 0)
    def _():
        wait_copies(prev_slot)

    @pl.when(i + 2 < n)
    def _():
        load(i + 2, prev_slot).start()

    @pl.when(i == n - 1)
    def _():
        wait_copies(slot)


def _dispatch(hn3, pos, last_tile, P):
    T = hn3.shape[0] // TOK_ROWS
    tc = min(MOE_TC, T)
    nt = T // tc
    return pl.pallas_call(
        _dispatch_kernel,
        grid_spec=pltpu.PrefetchScalarGridSpec(
            num_scalar_prefetch=1,
            grid=(nt,),
            in_specs=[pl.BlockSpec((1, 1, tc * TOP_K), lambda i, lt: (i, 0, 0), memory_space=pltpu.SMEM),
                      pl.BlockSpec(memory_space=pl.ANY)],
            out_specs=pl.BlockSpec(memory_space=pl.ANY),
            scratch_shapes=[pltpu.VMEM((MOE_TM * TOK_ROWS, LANES), F32),
                            pltpu.VMEM((DISPATCH_SLOTS, tc * TOK_ROWS, LANES), F32),
                            pltpu.SemaphoreType.DMA(()),
                            pltpu.SemaphoreType.DMA((DISPATCH_SLOTS,)), pltpu.SemaphoreType.DMA((DISPATCH_SLOTS,))],
        ),
        out_shape=jax.ShapeDtypeStruct((P * TOK_ROWS, LANES), F32),
        compiler_params=_cparams("arbitrary"),
        name="moe_dispatch",
    )(last_tile, pos.reshape(nt, 1, tc * TOP_K), hn3)


def _expert_kernel(te_ref, nused_ref, xs_ref, wgu_ref, bgu_ref, wd_ref, bd_ref, o_ref, wgu_s, wd_s):
    i = pl.program_id(0)
    prev = te_ref[jnp.maximum(i - 1, 0)]

    @pl.when((i == 0) | (te_ref[i] != prev))
    def _():
        wgu_s[...] = wgu_ref[0].astype(BF16)
        wd_s[...] = wd_ref[0].astype(BF16)

    @pl.when(i < nused_ref[0])
    def _():
        x = _load_token_rows(xs_ref, 0, xs_ref.shape[0] // TOK_ROWS).astype(BF16)
        gu = jnp.dot(x, wgu_s[...], preferred_element_type=F32) + bgu_ref[0]
        gate = jnp.minimum(gu[:, :D_FF], SWIGLU_LIMIT)
        up = jnp.clip(gu[:, D_FF:], -SWIGLU_LIMIT, SWIGLU_LIMIT)
        act = gate * jax.nn.sigmoid(SWIGLU_ALPHA * gate) * (up + 1.0)
        y = jnp.dot(act.astype(BF16), wd_s[...], preferred_element_type=F32) + bd_ref[0]
        _store_token_rows(o_ref, y)

    @pl.when(i >= nused_ref[0])
    def _():
        o_ref[...] = jnp.zeros_like(o_ref)


def _experts(xs3, tile_expert, n_used, w_gate_up, b_gate_up, w_down, b_down):
    P = xs3.shape[0] // TOK_ROWS
    tm = MOE_TM
    nt = P // tm
    live = lambda i, te, nu: jnp.minimum(i, nu[0] - 1)
    return pl.pallas_call(
        _expert_kernel,
        grid_spec=pltpu.PrefetchScalarGridSpec(
            num_scalar_prefetch=2,
            grid=(nt,),
            in_specs=[
                pl.BlockSpec((tm * TOK_ROWS, LANES), lambda i, te, nu: (live(i, te, nu), 0)),
                pl.BlockSpec((1, D_MODEL, 2 * D_FF), lambda i, te, nu: (te[i], 0, 0)),
                pl.BlockSpec((1, 1, 2 * D_FF), lambda i, te, nu: (te[i], 0, 0)),
                pl.BlockSpec((1, D_FF, D_MODEL), lambda i, te, nu: (te[i], 0, 0)),
                pl.BlockSpec((1, 1, D_MODEL), lambda i, te, nu: (te[i], 0, 0)),
            ],
            out_specs=pl.BlockSpec((tm * TOK_ROWS, LANES), lambda i, te, nu: (i, 0)),
            scratch_shapes=[pltpu.VMEM((D_MODEL, 2 * D_FF), BF16), pltpu.VMEM((D_FF, D_MODEL), BF16)],
        ),
        out_shape=jax.ShapeDtypeStruct((P * TOK_ROWS, LANES), F32),
        compiler_params=_cparams("arbitrary"),
        name="moe_experts",
    )(tile_expert, n_used, xs3, w_gate_up, b_gate_up.reshape(N_EXPERTS, 1, 2 * D_FF), w_down,
      b_down.reshape(N_EXPERTS, 1, D_MODEL))


COMBINE_SLOTS = 3
COMBINE_GROUP = 8


def _combine_kernel(pos_ref, pos1_ref, pos2_ref, wts_ref, h1_ref, nf_ref, ys_ref, o_ref, buf, sems):
    i = pl.program_id(0)
    n = pl.num_programs(0)
    tc = o_ref.shape[0]
    G = COMBINE_GROUP
    slot = lax.rem(i, COMBINE_SLOTS)
    slot2 = lax.rem(i + 2, COMBINE_SLOTS)

    def start_rows(p_ref, s, r):
        for j in range(TOP_K):
            src = pl.multiple_of(p_ref[0, 0, r * TOP_K + j] * TOK_ROWS, TOK_ROWS)
            dst = pl.multiple_of((j * tc + r) * TOK_ROWS, TOK_ROWS)
            pltpu.make_async_copy(ys_ref.at[pl.ds(src, TOK_ROWS)], buf.at[s, pl.ds(dst, TOK_ROWS)],
                                  sems.at[s]).start(priority=j % 2)

    def gather(p_ref, s):
        def issue(r, c):
            start_rows(p_ref, s, r)
            return c
        lax.fori_loop(0, tc, issue, 0, unroll=DMA_UNROLL // TOP_K)

    def combine_group(g):
        rows = slice(g * G, (g + 1) * G)
        w = wts_ref[rows, :]
        h = h1_ref[rows, :]
        for j in range(TOP_K):
            h = h + w[:, j:j + 1] * _load_token_rows(buf.at[slot], j * tc + g * G, G)
        o_ref[rows, :] = h * lax.rsqrt(jnp.mean(h * h, axis=-1, keepdims=True) + EPS) * nf_ref[...]

    @pl.when(i == 0)
    def _():
        gather(pos_ref, 0)

        @pl.when(n > 1)
        def _():
            gather(pos1_ref, 1)

    pltpu.make_async_copy(ys_ref.at[pl.ds(0, TOP_K * tc * TOK_ROWS)], buf.at[slot], sems.at[slot]).wait()

    @pl.when(i + 2 < n)
    def _():
        for g in range(tc // G):
            for r in range(g * G, (g + 1) * G):
                start_rows(pos2_ref, slot2, r)
            combine_group(g)

    @pl.when(i + 2 >= n)
    def _():
        for g in range(tc // G):
            combine_group(g)


def _combine(pos, wts, h1, norm_f_w, ys3):
    T = h1.shape[0]
    tc = min(MOE_TC, T)
    nt = T // tc
    pos3 = pos.reshape(nt, 1, tc * TOP_K)
    ahead = lambda k: pl.BlockSpec((1, 1, tc * TOP_K), lambda i: (jnp.minimum(i + k, nt - 1), 0, 0), memory_space=pltpu.SMEM)
    return pl.pallas_call(
        _combine_kernel,
        grid=(nt,),
        in_specs=[
            ahead(0), ahead(1), ahead(2),
            pl.BlockSpec((tc, ROUTE_PAD), lambda i: (i, 0)),
            pl.BlockSpec((tc, D_MODEL), lambda i: (i, 0)),
            _const_spec((1, D_MODEL)),
            pl.BlockSpec(memory_space=pl.ANY),
        ],
        out_specs=pl.BlockSpec((tc, D_MODEL), lambda i: (i, 0)),
        out_shape=jax.ShapeDtypeStruct((T, D_MODEL), F32),
        scratch_shapes=[pltpu.VMEM((COMBINE_SLOTS, TOP_K * tc * TOK_ROWS, LANES), F32),
                        pltpu.SemaphoreType.DMA((COMBINE_SLOTS,))],
        compiler_params=_cparams("arbitrary"),
        name="moe_combine",
    )(pos3, pos3, pos3, wts, h1, norm_f_w.reshape(1, D_MODEL), ys3)


def _route_plan(top_idx, T):
    tm = MOE_TM
    nt = (T * TOP_K) // tm + N_EXPERTS
    oh = (top_idx[:, :, None] == jnp.arange(N_EXPERTS, dtype=jnp.int32)[None, None, :]).astype(jnp.int32)
    cnt = jnp.sum(oh, axis=1)
    csum = jnp.cumsum(cnt, axis=0)
    counts = csum[-1]
    tiles = (counts + tm - 1) // tm
    tile_end = jnp.cumsum(tiles)
    offs = (tile_end - tiles) * tm
    pos = jnp.sum(oh * (offs[None, :] + csum - cnt)[:, None, :], axis=2).reshape(-1)
    tile_id = jnp.arange(nt, dtype=jnp.int32)
    tile_expert = jnp.minimum(jnp.sum((tile_end[None, :] <= tile_id[:, None]).astype(jnp.int32), axis=1),
                              N_EXPERTS - 1).astype(jnp.int32)
    tail = tile_end[-1] + jnp.arange(N_EXPERTS, dtype=jnp.int32)
    last_tile = jnp.concatenate([jnp.where(tiles > 0, tile_end - 1, -1), jnp.where(tail < nt, tail, -1)]).astype(jnp.int32)
    n_used = tile_end[-1:].astype(jnp.int32)
    return pos.astype(jnp.int32), tile_expert, last_tile, n_used, nt * tm


def kernel(x, norm1_w, w_in, gla_gate_w2, gla_gate_b, gla_norm_w, hy_conv_w, hy_conv_b, hy_freq, hy_ffn_w1, hy_ffn_b1,
           hy_ffn_w2, hy_ffn_b2, hy_ffn_w3, hy_decay, hy_bias, w_branch_gla, w_branch_hy, w_out, norm2_w, router_w,
           router_b, w_gate_up, b_gate_up, w_down, b_down, norm_f_w):
    B, L, D = x.shape
    T = B * L
    x2 = x.reshape(T, D)

    s_a = N_QKVR
    s_h = s_a + 2 * GLA_RANK
    w_b = w_in.astype(BF16)
    w_cat = jnp.concatenate(
        [w_b[:, :s_a], w_b[:, s_h:], w_b[:, s_a:s_h], jnp.zeros((D, ALOW_PAD - 2 * GLA_RANK), BF16)], axis=1)
    qkvr, hy, gates, alow = _in_proj(x2, norm1_w, w_cat)

    w2 = jnp.zeros((2, ALOW_PAD, GLA_KEY), F32)
    w2 = w2.at[0, :GLA_RANK].set(gla_gate_w2[0]).at[1, GLA_RANK:2 * GLA_RANK].set(gla_gate_w2[1])
    o_gla = _gla(qkvr, alow, w2, gla_gate_b.reshape(2, 1, GLA_KEY), gla_norm_w, B, L)

    taps = _hy_filter(L, hy_freq, hy_ffn_w1, hy_ffn_b1, hy_ffn_w2, hy_ffn_b2, hy_ffn_w3, hy_decay)
    o_hy = _hyena(hy, hy_conv_w, hy_conv_b, hy_bias, taps, B, L)

    rw_f = jnp.zeros((D, ROUTE_PAD), F32).at[:, :N_EXPERTS].set(router_w)
    rw_hi = rw_f.astype(BF16)
    rw_pad = jnp.concatenate([rw_hi, (rw_f - rw_hi.astype(F32)).astype(BF16)], axis=1)
    rb_pad = jnp.zeros((1, ROUTE_PAD), F32).at[0, :N_EXPERTS].set(router_b)
    h1, hn3, top_idx, wts_pad = _merge(o_gla, o_hy, gates, x2, w_branch_gla.astype(BF16), w_branch_hy.astype(BF16),
                                       w_out.astype(BF16), norm2_w, rw_pad, rb_pad)

    pos, tile_expert, last_tile, n_used, P = _route_plan(top_idx, T)
    xs3 = _dispatch(hn3, pos, last_tile, P)
    ys3 = _experts(xs3, tile_expert, n_used, w_gate_up, b_gate_up, w_down, b_down)
    out = _combine(pos, wts_pad, h1, norm_f_w, ys3)
    return out.reshape(B, L, D)
```

```python
import functools
import math

import numpy as np
import jax
import jax.numpy as jnp
from jax import lax
from jax.experimental import pallas as pl
from jax.experimental.pallas import tpu as pltpu

D_MODEL = 1024
EPS = 1e-6
GLA_HEADS = 4
GLA_DK = 128
GLA_DV = 256
GLA_KEY = GLA_HEADS * GLA_DK
GLA_VAL = GLA_HEADS * GLA_DV
GLA_RANK = 16
GLA_GATE_TEMP = 16.0
GLA_CHUNK = 64
HY_WIDTH = D_MODEL
HY_BANDS = 16
HY_EMB = 1 + 2 * HY_BANDS
HY_FFN = 64
N_EXPERTS = 32
TOP_K = 4
D_FF = D_MODEL
SWIGLU_LIMIT = 7.0
SWIGLU_ALPHA = 1.702

LANES = 128
ALOW_PAD = LANES
ROUTE_PAD = LANES
VMEM_LIMIT = 56 * 1024 * 1024

F32 = jnp.float32
BF16 = jnp.bfloat16
HIGHEST = lax.Precision.HIGHEST


def _cparams(*sem):
    return pltpu.CompilerParams(dimension_semantics=sem, vmem_limit_bytes=VMEM_LIMIT)


def _const_spec(shape):
    nd = len(shape)
    return pl.BlockSpec(shape, lambda *_: (0,) * nd, pipeline_mode=pl.Buffered(1))


TOK_ROWS = D_MODEL // LANES


def _store_token_rows(ref, val):
    n = val.shape[0]
    for s_ in range(TOK_ROWS):
        ref[pl.ds(s_, n, stride=TOK_ROWS), :] = val[:, s_ * LANES:(s_ + 1) * LANES]


def _load_token_rows(ref, first, n):
    return jnp.concatenate(
        [ref[pl.ds(first * TOK_ROWS + s_, n, stride=TOK_ROWS), :] for s_ in range(TOK_ROWS)], axis=1)


N_QKVR = 2 * GLA_KEY + 2 * GLA_VAL
N_HY = 3 * HY_WIDTH
N_GATES = 2 * D_MODEL
IN_COLS = N_QKVR + N_HY + N_GATES + ALOW_PAD


def _inproj_kernel(x_ref, nw_ref, w_ref, qkvr_ref, hy_ref, gates_ref, alow_ref):
    x = x_ref[...]
    ms = jnp.mean(x * x, axis=-1, keepdims=True)
    xn = (x * lax.rsqrt(ms + EPS) * nw_ref[...]).astype(BF16)
    c0, c1, c2 = N_QKVR, N_QKVR + N_HY, N_QKVR + N_HY + N_GATES
    qkvr_ref[...] = jnp.dot(xn, w_ref[:, :c0], preferred_element_type=F32).astype(BF16)
    hy_ref[...] = jnp.dot(xn, w_ref[:, c0:c1], preferred_element_type=F32).astype(BF16)
    gates_ref[...] = jnp.dot(xn, w_ref[:, c1:c2], preferred_element_type=F32).astype(BF16)
    alow_ref[...] = jnp.dot(xn, w_ref[:, c2:], preferred_element_type=F32)


def _in_proj(x2, norm1_w, w_cat):
    T = x2.shape[0]
    tm = min(512, T)
    return pl.pallas_call(
        _inproj_kernel,
        grid=(T // tm,),
        in_specs=[
            pl.BlockSpec((tm, D_MODEL), lambda i: (i, 0)),
            _const_spec((1, D_MODEL)),
            _const_spec((D_MODEL, IN_COLS)),
        ],
        out_specs=[
            pl.BlockSpec((tm, N_QKVR), lambda i: (i, 0)),
            pl.BlockSpec((tm, N_HY), lambda i: (i, 0)),
            pl.BlockSpec((tm, N_GATES), lambda i: (i, 0)),
            pl.BlockSpec((tm, ALOW_PAD), lambda i: (i, 0)),
        ],
        out_shape=[
            jax.ShapeDtypeStruct((T, N_QKVR), BF16),
            jax.ShapeDtypeStruct((T, N_HY), BF16),
            jax.ShapeDtypeStruct((T, N_GATES), BF16),
            jax.ShapeDtypeStruct((T, ALOW_PAD), F32),
        ],
        compiler_params=_cparams("parallel"),
        name="in_proj",
    )(x2, norm1_w.reshape(1, D_MODEL), w_cat)


GLA_PREP_ROWS = 256
GLA_PREP_UNROLL = 4
GLA_INTRA_UNROLL = 8
GLA_SCAN_UNROLL = 64


def _log_sigmoid(x):
    return jnp.minimum(x, 0.0) - jnp.log(1.0 + jnp.exp(-jnp.abs(x)))


def _gla_kernel(q_ref, k_ref, v_ref, r_ref, alow_ref, w2_ref, gb_ref, nw_ref, o_ref,
                qtf, ktf, kef, qtb, ktb, keb, decf, decb, of_s, ob_s, stf, stb):
    L = q_ref.shape[0]
    C = GLA_CHUNK
    DK = GLA_DK
    n_chunks = L // C
    R = min(GLA_PREP_ROWS, L)
    cpr = R // C
    scale = DK ** -0.5
    nt = (((1,), (1,)), ((), ()))
    tn = (((0,), (0,)), ((), ()))

    ri = lax.broadcasted_iota(jnp.int32, (R, R), 0)
    ci = lax.broadcasted_iota(jnp.int32, (R, R), 1)
    same = (ri // C) == (ci // C)
    mask_f = same & (ri >= ci)
    mask_b = same & (ri < ci)
    tri = jnp.where(mask_f, 1.0, 0.0).astype(BF16)
    w2 = jnp.concatenate([w2_ref[0], w2_ref[1]], axis=1).astype(BF16)
    gb = jnp.concatenate([gb_ref[0], gb_ref[1]], axis=1)

    def prep(g, carry):
        rows = pl.ds(pl.multiple_of(g * R, R), R)
        logits = jnp.dot(alow_ref[rows, :].astype(BF16), w2, preferred_element_type=F32) + gb
        lg = _log_sigmoid(logits) * (1.0 / GLA_GATE_TEMP)
        hi = lg.astype(BF16)
        lo = (lg - hi.astype(F32)).astype(BF16)
        b = jnp.dot(tri, hi, preferred_element_type=F32) + jnp.dot(tri, lo, preferred_element_type=F32)
        tot = jnp.concatenate(
            [jnp.broadcast_to(b[c * C + C - 1:c * C + C, :], (C, 2 * DK)) for c in range(cpr)], axis=0)
        q = q_ref[rows, :].astype(F32) * scale
        k = k_ref[rows, :].astype(F32)
        dtot = jnp.exp(tot)
        for d, (qt, kt, ke, dec) in enumerate(((qtf, ktf, kef, decf), (qtb, ktb, keb, decb))):
            sl = slice(d * DK, (d + 1) * DK)
            bd, td, ld = b[:, sl], tot[:, sl], lg[:, sl]
            if d == 0:
                e_q, e_end = bd, td - bd
            else:
                e_q, e_end = td - bd + ld, bd - ld
            qt[rows, :] = (q * jnp.exp(e_q)).astype(BF16)
            kt[rows, :] = (k * jnp.exp(-e_q)).astype(BF16)
            ke[rows, :] = (k * jnp.exp(e_end)).astype(BF16)
            for c in range(cpr):
                dec[pl.ds(g * cpr + c, 1), :] = dtot[c * C:c * C + 1, sl]
        return carry

    lax.fori_loop(0, L // R, prep, 0, unroll=min(GLA_PREP_UNROLL, L // R))

    def intra(g, carry):
        rows = pl.ds(pl.multiple_of(g * R, R), R)
        vb = v_ref[rows, :]
        for qt, kt, mask, out in ((qtf, ktf, mask_f, of_s), (qtb, ktb, mask_b, ob_s)):
            s = lax.dot_general(qt[rows, :], kt[rows, :], nt, preferred_element_type=F32)
            s = jnp.where(mask, s, 0.0).astype(BF16)
            out[rows, :] = jnp.dot(s, vb, preferred_element_type=F32)
        return carry

    lax.fori_loop(0, L // R, intra, 0, unroll=min(GLA_INTRA_UNROLL, L // R))

    def chunk(n, st, qt, ke, dec, out):
        rows = pl.ds(pl.multiple_of(n * C, C), C)
        s = st[...]
        out[rows, :] += lax.dot_general(qt[rows, :], s.astype(BF16), nt, preferred_element_type=F32)
        upd = lax.dot_general(v_ref[rows, :], ke[rows, :], tn, preferred_element_type=F32)
        st[...] = dec[pl.ds(n, 1), :] * s + upd

    def step(i, carry):
        chunk(i, stf, qtf, kef, decf, of_s)
        chunk(n_chunks - 1 - i, stb, qtb, keb, decb, ob_s)
        return carry

    stf[...] = jnp.zeros_like(stf)
    stb[...] = jnp.zeros_like(stb)
    lax.fori_loop(0, n_chunks, step, 0, unroll=min(GLA_SCAN_UNROLL, n_chunks))

    o = of_s[...] + ob_s[...]
    o = o * lax.rsqrt(jnp.mean(o * o, axis=-1, keepdims=True) + EPS) * nw_ref[...]
    r = r_ref[...].astype(F32)
    o_ref[...] = (o * (r * jax.nn.sigmoid(r))).astype(BF16)


def _gla(qkvr, alow, w2, gb, norm_w, B, L):
    T = B * L
    H = GLA_HEADS
    kq = GLA_KEY // GLA_DK
    kv = 2 * GLA_KEY // GLA_DV
    kr = (2 * GLA_KEY + GLA_VAL) // GLA_DV
    return pl.pallas_call(
        _gla_kernel,
        grid=(B, H),
        in_specs=[
            pl.BlockSpec((L, GLA_DK), lambda b, h: (b, h)),
            pl.BlockSpec((L, GLA_DK), lambda b, h: (b, kq + h)),
            pl.BlockSpec((L, GLA_DV), lambda b, h: (b, kv + h)),
            pl.BlockSpec((L, GLA_DV), lambda b, h: (b, kr + h)),
            pl.BlockSpec((L, ALOW_PAD), lambda b, h: (b, 0)),
            pl.BlockSpec((2, ALOW_PAD, GLA_DK), lambda b, h: (0, 0, h)),
            pl.BlockSpec((2, 1, GLA_DK), lambda b, h: (0, 0, h)),
            pl.BlockSpec((1, GLA_DV), lambda b, h: (0, 0)),
        ],
        out_specs=pl.BlockSpec((L, GLA_DV), lambda b, h: (b, h)),
        out_shape=jax.ShapeDtypeStruct((T, GLA_VAL), BF16),
        scratch_shapes=[pltpu.VMEM((L, GLA_DK), BF16)] * 6
        + [pltpu.VMEM((L // GLA_CHUNK, GLA_DK), F32)] * 2
        + [pltpu.VMEM((L, GLA_DV), F32)] * 2
        + [pltpu.VMEM((GLA_DV, GLA_DK), F32)] * 2,
        compiler_params=_cparams("parallel", "parallel"),
        name="gla",
    )(qkvr, qkvr, qkvr, qkvr, alow, w2, gb, norm_w.reshape(1, GLA_DV))


HY_WT = 256


def _hy_filter_kernel(z_ref, t_ref, fr_ref, w1_ref, b1_ref, w2_ref, b2_ref, w3f_ref, w3b_ref, dec_ref, o_ref, h_s):
    @pl.when(pl.program_id(0) == 0)
    def _():
        fr = fr_ref[...]
        h1 = jnp.sin(fr * (jnp.dot(z_ref[...], w1_ref[...], preferred_element_type=F32, precision=HIGHEST) + b1_ref[...]))
        h_s[...] = jnp.sin(fr * (jnp.dot(h1, w2_ref[...], preferred_element_type=F32, precision=HIGHEST) + b2_ref[...]))

    h = h_s[...]
    t = t_ref[...]
    row = lax.broadcasted_iota(jnp.int32, (z_ref.shape[0], 1), 0)
    hq = h.astype(BF16)
    hf = jnp.dot(hq, w3f_ref[...].astype(BF16), preferred_element_type=F32)
    hf = hf * jnp.exp(-t * jnp.abs(dec_ref[0:1, :]))
    hb = jnp.dot(hq, w3b_ref[...].astype(BF16), preferred_element_type=F32)
    hb = hb * jnp.exp(-t * jnp.abs(dec_ref[1:2, :]))
    hb = jnp.where(row == 0, 0.0, hb)
    ss = jnp.sum(hf * hf, axis=0, keepdims=True) + jnp.sum(hb * hb, axis=0, keepdims=True)
    inv = lax.rsqrt(ss + EPS)
    o_ref[0] = hf * inv
    o_ref[1] = hb * inv


def _hy_filter(L, hy_freq, w1, b1, w2, b2, w3, hy_decay):
    W = HY_WIDTH
    t = np.linspace(0.0, 1.0, L, dtype=np.float32)[:, None]
    omega = (np.float32(2.0 * math.pi) * np.arange(L, dtype=np.float32) / np.float32(L)).astype(np.float32)
    f = np.linspace(1e-4, HY_BANDS - 1, HY_BANDS, dtype=np.float32)
    ang = (omega[:, None] * f[None, :]).astype(np.float32)
    z = np.concatenate([t, np.cos(ang), -np.sin(ang)], axis=-1).astype(np.float32)
    nw = W // HY_WT
    return pl.pallas_call(
        _hy_filter_kernel,
        grid=(nw,),
        in_specs=[
            _const_spec((L, HY_EMB)),
            _const_spec((L, 1)),
            _const_spec((1, HY_FFN)),
            _const_spec((HY_EMB, HY_FFN)),
            _const_spec((1, HY_FFN)),
            _const_spec((HY_FFN, HY_FFN)),
            _const_spec((1, HY_FFN)),
            pl.BlockSpec((HY_FFN, HY_WT), lambda j: (0, j)),
            pl.BlockSpec((HY_FFN, HY_WT), lambda j: (0, nw + j)),
            pl.BlockSpec((2, HY_WT), lambda j: (0, j)),
        ],
        out_specs=pl.BlockSpec((2, L, HY_WT), lambda j: (0, 0, j)),
        out_shape=jax.ShapeDtypeStruct((2, L, W), F32),
        scratch_shapes=[pltpu.VMEM((L, HY_FFN), F32)],
        compiler_params=_cparams("arbitrary"),
        name="hy_filter",
    )(jnp.asarray(z), jnp.asarray(t), hy_freq.reshape(1, HY_FFN), w1, b1.reshape(1, HY_FFN), w2,
      b2.reshape(1, HY_FFN), w3, w3, hy_decay)


FFT_L = LANES
FFT_PITCH = FFT_L + 8
FFT_PAIR_UNROLL = 32
FFT_K2_UNROLL = 40


def _fft_tables(L):
    N = 2 * L
    C = L // FFT_L
    n1 = N // FFT_L
    kh = min(n1, -(-(n1 // 2 + 1) // 8) * 8)
    l = np.arange(FFT_L)
    c = np.arange(C)
    k2 = np.arange(kh)
    pair_w = np.where((k2 == 0) | (k2 == n1 // 2), 1.0, np.where(k2 < n1 // 2, 2.0, 0.0))
    ang1 = 2 * np.pi * ((k2[:, None] * c[None, :]) % n1) / n1
    f1 = np.concatenate([np.cos(ang1), -np.sin(ang1)], axis=0)
    g1 = (np.stack([np.cos(ang1), -np.sin(ang1)], axis=-1) * pair_w[:, None, None]).transpose(1, 0, 2).reshape(C, 2 * kh) / N
    ang2 = 2 * np.pi * (((l[None, :, None] * l[None, None, :]) % FFT_L) / FFT_L
                        + ((k2[:, None, None] * l[None, None, :]) % N) / N)
    cs, sn = np.cos(ang2), np.sin(ang2)
    f2 = np.concatenate([np.concatenate([cs, sn], axis=2), np.concatenate([-sn, cs], axis=2)], axis=1)
    cst, snt = cs.transpose(0, 2, 1), sn.transpose(0, 2, 1)
    g2 = np.concatenate([np.concatenate([cst, -snt], axis=2), np.concatenate([snt, cst], axis=2)], axis=1)
    bf = lambda a: jnp.asarray(a.astype(np.float32)).astype(BF16)
    return dict(C=C, n1=kh, f1=bf(f1), g1=bf(g1), f2=bf(f2), g2=bf(g2))


def _fft_kernel(z_ref, f1_ref, f2_ref, g2_ref, kf_ref, g1_ref, o_ref, a_s, b_s, y_s, *, spectrum_only, epilogue=None,
                z_pitch=FFT_L):
    C = f1_ref.shape[1]
    n1 = f1_ref.shape[0] // 2
    P = FFT_PITCH
    blk = lambda j: pl.ds(pl.multiple_of(j * P, 8), FFT_L)

    def stage1(j, carry):
        x = jnp.concatenate([z_ref[pl.ds(2 * j, C, stride=z_pitch), :], z_ref[pl.ds(2 * j + 1, C, stride=z_pitch), :]], axis=1)
        r = jnp.dot(f1_ref[...], x.astype(BF16), preferred_element_type=F32)
        a_s[pl.ds(pl.multiple_of(2 * j * P, 8), 2 * n1), :] = r[:, :LANES]
        a_s[pl.ds(pl.multiple_of((2 * j + 1) * P, 8), 2 * n1), :] = r[:, LANES:]
        return carry
    lax.fori_loop(0, FFT_L // 2, stage1, 0, unroll=FFT_PAIR_UNROLL)
    k2_unroll = min(FFT_K2_UNROLL, n1)

    def stage2(k2, carry, *, accumulate=False):
        ar = a_s[pl.ds(k2, FFT_L, stride=P), :]
        ai = a_s[pl.ds(n1 + k2, FFT_L, stride=P), :]
        x = jnp.concatenate([ar, ai], axis=0).astype(BF16)
        s = jnp.dot(f2_ref[k2], x, preferred_element_type=F32)
        xr, xi = s[:FFT_L], s[FFT_L:]
        if not spectrum_only:
            kr, ki = kf_ref[k2, 0], kf_ref[k2, 1]
            y = jnp.concatenate([xr * kr - xi * ki, xr * ki + xi * kr], axis=0).astype(BF16)
            s = jnp.dot(g2_ref[k2], y, preferred_element_type=F32)
            b_s[blk(2 * k2), :] = s[:FFT_L]
            b_s[blk(2 * k2 + 1), :] = s[FFT_L:]
        elif accumulate:
            o_ref[k2, 0] += xr
            o_ref[k2, 1] -= xi
        else:
            o_ref[k2, 0] = xr
            o_ref[k2, 1] = xi
        return carry

    if spectrum_only:
        @pl.when(pl.program_id(1) == 0)
        def _():
            lax.fori_loop(0, n1, stage2, 0, unroll=k2_unroll)

        @pl.when(pl.program_id(1) != 0)
        def _():
            lax.fori_loop(0, n1, functools.partial(stage2, accumulate=True), 0, unroll=k2_unroll)
    else:
        lax.fori_loop(0, n1, stage2, 0, unroll=k2_unroll)

    if not spectrum_only:
        def stage3(j, carry):
            x = jnp.concatenate([b_s[pl.ds(2 * j, 2 * n1, stride=P), :], b_s[pl.ds(2 * j + 1, 2 * n1, stride=P), :]], axis=1)
            r = jnp.dot(g1_ref[...], x.astype(BF16), preferred_element_type=F32)
            y_s[pl.ds(2 * j, C, stride=P), :] = r[:, :LANES]
            y_s[pl.ds(2 * j + 1, C, stride=P), :] = r[:, LANES:]
            return carry
        lax.fori_loop(0, FFT_L // 2, stage3, 0, unroll=FFT_PAIR_UNROLL)
        for c in range(C):
            rows = slice(c * FFT_L, (c + 1) * FFT_L)
            o_ref[rows, :] = epilogue(c, y_s[c * P:c * P + FFT_L, :])


def _hyena_kernel(x0_ref, x1_ref, v_ref, w0_ref, w1_ref, wv_ref, b0_ref, b1_ref, bv_ref, hb_ref,
                  f1_ref, f2_ref, g2_ref, kf_ref, g1_ref, o_ref, z_s, x0_s, a_s, b_s, y_s):
    L = x0_ref.shape[0]
    row = lax.broadcasted_iota(jnp.int32, (L, 1), 0)

    def conv3(u_ref, w_ref, b_ref):
        u = u_ref[...].astype(F32)
        prev = jnp.where(row == 0, 0.0, pltpu.roll(u, 1, 0))
        nxt = jnp.where(row == L - 1, 0.0, pltpu.roll(u, L - 1, 0))
        return w_ref[0:1, :] * prev + w_ref[1:2, :] * u + w_ref[2:3, :] * nxt + b_ref[...]

    P = FFT_PITCH
    chunk = lambda c: slice(c * P, c * P + FFT_L)
    z = conv3(v_ref, wv_ref, bv_ref) * conv3(x1_ref, w1_ref, b1_ref)
    for c in range(L // FFT_L):
        z_s[chunk(c), :] = z[c * FFT_L:(c + 1) * FFT_L, :]
    x0_s[...] = conv3(x0_ref, w0_ref, b0_ref)

    def epilogue(c, y):
        return ((y + z_s[chunk(c), :] * hb_ref[...]) * x0_s[c * FFT_L:(c + 1) * FFT_L, :]).astype(BF16)

    _fft_kernel(z_s, f1_ref, f2_ref, g2_ref, kf_ref, g1_ref, o_ref, a_s, b_s, y_s, spectrum_only=False, epilogue=epilogue,
                z_pitch=P)


def _fft_specs(tabs):
    C, n1 = tabs["C"], tabs["n1"]
    specs = [_const_spec((2 * n1, C)),
             _const_spec((n1, 2 * FFT_L, 2 * FFT_L)), _const_spec((n1, 2 * FFT_L, 2 * FFT_L)),
             pl.BlockSpec((n1, 2, FFT_L, LANES), lambda w, b: (0, 0, 0, w)),
             _const_spec((C, 2 * n1))]
    scratch = [pltpu.VMEM((FFT_L * FFT_PITCH, LANES), F32), pltpu.VMEM((2 * n1 * FFT_PITCH, LANES), F32),
               pltpu.VMEM((C * FFT_PITCH, LANES), F32)]
    return specs, scratch


def _filter_spectrum(tabs, taps):
    _, L, W = taps.shape
    n1 = tabs["n1"]
    specs, scratch = _fft_specs(tabs)

    def spectrum_kernel(z_ref, f1_ref, f2_ref, o_ref, a_s):
        _fft_kernel(z_ref, f1_ref, f2_ref, None, None, None, o_ref, a_s, None, None, spectrum_only=True)

    return pl.pallas_call(
        spectrum_kernel,
        grid=(W // LANES, 2),
        in_specs=[pl.BlockSpec((None, L, LANES), lambda w, b: (b, 0, w))] + specs[:2],
        out_specs=pl.BlockSpec((n1, 2, FFT_L, LANES), lambda w, b: (0, 0, 0, w)),
        out_shape=jax.ShapeDtypeStruct((n1, 2, FFT_L, W), F32),
        scratch_shapes=scratch[:1],
        compiler_params=_cparams("arbitrary", "arbitrary"),
        name="fft_spectrum",
    )(taps, tabs["f1"], tabs["f2"])


def _hyena(hy, conv_w, conv_b, hy_bias, taps, B, L):
    W = HY_WIDTH
    nw = W // LANES
    tabs = _fft_tables(L)
    kf = _filter_spectrum(tabs, taps)
    specs, scratch = _fft_specs(tabs)
    cb = conv_b.reshape(1, 3 * W)
    blk = lambda off: pl.BlockSpec((L, LANES), lambda w, b: (b, off + w))
    wblk = lambda off: pl.BlockSpec((3, LANES), lambda w, b: (0, off + w))
    bblk = lambda off: pl.BlockSpec((1, LANES), lambda w, b: (0, off + w))
    return pl.pallas_call(
        _hyena_kernel,
        grid=(nw, B),
        in_specs=[blk(0), blk(nw), blk(2 * nw), wblk(0), wblk(nw), wblk(2 * nw), bblk(0), bblk(nw), bblk(2 * nw),
                  bblk(0)] + specs,
        out_specs=pl.BlockSpec((L, LANES), lambda w, b: (b, w)),
        out_shape=jax.ShapeDtypeStruct((B * L, W), BF16),
        scratch_shapes=[pltpu.VMEM((L // FFT_L * FFT_PITCH, LANES), F32), pltpu.VMEM((L, LANES), F32)] + scratch,
        compiler_params=_cparams("arbitrary", "arbitrary"),
        name="hyena",
    )(hy, hy, hy, conv_w, conv_w, conv_w, cb, cb, cb, hy_bias.reshape(1, W), tabs["f1"], tabs["f2"], tabs["g2"], kf,
      tabs["g1"])


def _merge_kernel(og_ref, oh_ref, gates_ref, x_ref, wbg_ref, wbh_ref, wo_ref, n2_ref, rw_ref, rb_ref,
                  h1_ref, hn3_ref, idx_ref, wts_ref):
    mg = jnp.dot(og_ref[...], wbg_ref[...], preferred_element_type=F32)
    mh = jnp.dot(oh_ref[...], wbh_ref[...], preferred_element_type=F32)
    gg = jax.nn.sigmoid(gates_ref[:, :D_MODEL].astype(F32))
    gh = jax.nn.sigmoid(gates_ref[:, D_MODEL:].astype(F32))
    merged = (gg * mg + gh * mh).astype(BF16)
    h1 = x_ref[...] + jnp.dot(merged, wo_ref[...], preferred_element_type=F32)
    h1_ref[...] = h1
    hn = h1 * lax.rsqrt(jnp.mean(h1 * h1, axis=-1, keepdims=True) + EPS) * n2_ref[...]
    _store_token_rows(hn3_ref, hn)
    hh = hn.astype(BF16)
    hl = (hn - hh.astype(F32)).astype(BF16)
    p1 = jnp.dot(hh, rw_ref[...], preferred_element_type=F32)
    p2 = jnp.dot(hl, rw_ref[:, :ROUTE_PAD], preferred_element_type=F32)
    logits = p1[:, :ROUTE_PAD] + p1[:, ROUTE_PAD:] + p2 + rb_ref[...]
    lane = lax.broadcasted_iota(jnp.int32, logits.shape, 1)
    logits = jnp.where(lane < N_EXPERTS, logits, -jnp.inf)
    idx_out = jnp.zeros(logits.shape, jnp.int32)
    val_out = jnp.zeros(logits.shape, F32)
    v0 = None
    for j in range(TOP_K):
        m = jnp.max(logits, axis=-1, keepdims=True)
        sel = jnp.min(jnp.where(logits == m, lane, ROUTE_PAD), axis=-1, keepdims=True)
        if j == 0:
            v0 = m
        idx_out = jnp.where(lane == j, sel, idx_out)
        val_out = jnp.where(lane == j, jnp.exp(m - v0), val_out)
        logits = jnp.where(lane == sel, -jnp.inf, logits)
    idx_ref[...] = idx_out[:, :TOP_K]
    wts_ref[...] = val_out / jnp.sum(val_out, axis=-1, keepdims=True)


def _merge(o_gla, o_hy, gates, x2, wbg, wbh, wo, norm2_w, rw_pad, rb_pad):
    T = x2.shape[0]
    tm = min(512, T)
    row = lambda n: pl.BlockSpec((tm, n), lambda i: (i, 0))
    return pl.pallas_call(
        _merge_kernel,
        grid=(T // tm,),
        in_specs=[row(GLA_VAL), row(HY_WIDTH), row(N_GATES), row(D_MODEL),
                  _const_spec((GLA_VAL, D_MODEL)), _const_spec((HY_WIDTH, D_MODEL)), _const_spec((D_MODEL, D_MODEL)),
                  _const_spec((1, D_MODEL)), _const_spec((D_MODEL, 2 * ROUTE_PAD)), _const_spec((1, ROUTE_PAD))],
        out_specs=[row(D_MODEL), pl.BlockSpec((tm * TOK_ROWS, LANES), lambda i: (i, 0)), row(TOP_K), row(ROUTE_PAD)],
        out_shape=[jax.ShapeDtypeStruct((T, D_MODEL), F32), jax.ShapeDtypeStruct((T * TOK_ROWS, LANES), F32),
                   jax.ShapeDtypeStruct((T, TOP_K), jnp.int32), jax.ShapeDtypeStruct((T, ROUTE_PAD), F32)],
        compiler_params=_cparams("parallel"),
        name="merge_route",
    )(o_gla, o_hy, gates, x2, wbg, wbh, wo, norm2_w.reshape(1, D_MODEL), rw_pad, rb_pad)


MOE_TM = 512
MOE_TC = 256
DMA_UNROLL = 8


DISPATCH_SLOTS = 3


def _dispatch_kernel(last_ref, pos_ref, hn_ref, xs_ref, zero_s, buf, sem, lsems, ssems):
    i = pl.program_id(0)
    n = pl.num_programs(0)
    tc = pos_ref.shape[2] // TOP_K
    rows = tc * TOK_ROWS
    zrows = zero_s.shape[0]
    slot = lax.rem(i, DISPATCH_SLOTS)
    prev_slot = lax.rem(i + DISPATCH_SLOTS - 1, DISPATCH_SLOTS)

    def load(blk, s):
        src = hn_ref.at[pl.ds(pl.multiple_of(blk * rows, rows), rows)]
        return pltpu.make_async_copy(src, buf.at[s], lsems.at[s])

    def wait_copies(s):
        for _ in range(TOP_K):
            pltpu.make_async_copy(buf.at[s], xs_ref.at[pl.ds(0, rows)], ssems.at[s]).wait()

    @pl.when(i == 0)
    def _():
        load(0, 0).start()

        @pl.when(n > 1)
        def _():
            load(1, 1).start()

        zero_s[...] = jnp.zeros_like(zero_s)
        n_fill = last_ref.shape[0]

        def zfill(e, c):
            @pl.when(last_ref[e] >= 0)
            def _():
                dst = pl.multiple_of(last_ref[e] * zrows, zrows)
                pltpu.make_async_copy(zero_s, xs_ref.at[pl.ds(dst, zrows)], sem).start()
            return c
        lax.fori_loop(0, n_fill, zfill, 0)

        def zwait(e, c):
            @pl.when(last_ref[e] >= 0)
            def _():
                pltpu.make_async_copy(zero_s, xs_ref.at[pl.ds(0, zrows)], sem).wait()
            return c
        lax.fori_loop(0, n_fill, zwait, 0)

    load(i, slot).wait()

    def issue(r, c):
        src = buf.at[slot, pl.ds(pl.multiple_of(r * TOK_ROWS, TOK_ROWS), TOK_ROWS)]
        for j in range(TOP_K):
            dst = pl.multiple_of(pos_ref[0, 0, r * TOP_K + j] * TOK_ROWS, TOK_ROWS)
            pltpu.make_async_copy(src, xs_ref.at[pl.ds(dst, TOK_ROWS)], ssems.at[slot]).start(priority=j % 2)
        return c
    lax.fori_loop(0, tc, issue, 0, unroll=DMA_UNROLL // TOP_K)

    @pl.when(i > 0)
    def _():
        wait_copies(prev_slot)

    @pl.when(i + 2 < n)
    def _():
        load(i + 2, prev_slot).start()

    @pl.when(i == n - 1)
    def _():
        wait_copies(slot)


def _dispatch(hn3, pos, last_tile, P):
    T = hn3.shape[0] // TOK_ROWS
    tc = min(MOE_TC, T)
    nt = T // tc
    return pl.pallas_call(
        _dispatch_kernel,
        grid_spec=pltpu.PrefetchScalarGridSpec(
            num_scalar_prefetch=1,
            grid=(nt,),
            in_specs=[pl.BlockSpec((1, 1, tc * TOP_K), lambda i, lt: (i, 0, 0), memory_space=pltpu.SMEM),
                      pl.BlockSpec(memory_space=pl.ANY)],
            out_specs=pl.BlockSpec(memory_space=pl.ANY),
            scratch_shapes=[pltpu.VMEM((MOE_TM * TOK_ROWS, LANES), F32),
                            pltpu.VMEM((DISPATCH_SLOTS, tc * TOK_ROWS, LANES), F32),
                            pltpu.SemaphoreType.DMA(()),
                            pltpu.SemaphoreType.DMA((DISPATCH_SLOTS,)), pltpu.SemaphoreType.DMA((DISPATCH_SLOTS,))],
        ),
        out_shape=jax.ShapeDtypeStruct((P * TOK_ROWS, LANES), F32),
        compiler_params=_cparams("arbitrary"),
        name="moe_dispatch",
    )(last_tile, pos.reshape(nt, 1, tc * TOP_K), hn3)


def _expert_kernel(te_ref, nused_ref, xs_ref, wgu_ref, bgu_ref, wd_ref, bd_ref, o_ref, wgu_s, wd_s):
    i = pl.program_id(0)
    prev = te_ref[jnp.maximum(i - 1, 0)]

    @pl.when((i == 0) | (te_ref[i] != prev))
    def _():
        wgu_s[...] = wgu_ref[0].astype(BF16)
        wd_s[...] = wd_ref[0].astype(BF16)

    @pl.when(i < nused_ref[0])
    def _():
        x = _load_token_rows(xs_ref, 0, xs_ref.shape[0] // TOK_ROWS).astype(BF16)
        gu = jnp.dot(x, wgu_s[...], preferred_element_type=F32) + bgu_ref[0]
        gate = jnp.minimum(gu[:, :D_FF], SWIGLU_LIMIT)
        up = jnp.clip(gu[:, D_FF:], -SWIGLU_LIMIT, SWIGLU_LIMIT)
        act = gate * jax.nn.sigmoid(SWIGLU_ALPHA * gate) * (up + 1.0)
        y = jnp.dot(act.astype(BF16), wd_s[...], preferred_element_type=F32) + bd_ref[0]
        _store_token_rows(o_ref, y)

    @pl.when(i >= nused_ref[0])
    def _():
        o_ref[...] = jnp.zeros_like(o_ref)


def _experts(xs3, tile_expert, n_used, w_gate_up, b_gate_up, w_down, b_down):
    P = xs3.shape[0] // TOK_ROWS
    tm = MOE_TM
    nt = P // tm
    live = lambda i, te, nu: jnp.minimum(i, nu[0] - 1)
    return pl.pallas_call(
        _expert_kernel,
        grid_spec=pltpu.PrefetchScalarGridSpec(
            num_scalar_prefetch=2,
            grid=(nt,),
            in_specs=[
                pl.BlockSpec((tm * TOK_ROWS, LANES), lambda i, te, nu: (live(i, te, nu), 0)),
                pl.BlockSpec((1, D_MODEL, 2 * D_FF), lambda i, te, nu: (te[i], 0, 0)),
                pl.BlockSpec((1, 1, 2 * D_FF), lambda i, te, nu: (te[i], 0, 0)),
                pl.BlockSpec((1, D_FF, D_MODEL), lambda i, te, nu: (te[i], 0, 0)),
                pl.BlockSpec((1, 1, D_MODEL), lambda i, te, nu: (te[i], 0, 0)),
            ],
            out_specs=pl.BlockSpec((tm * TOK_ROWS, LANES), lambda i, te, nu: (i, 0)),
            scratch_shapes=[pltpu.VMEM((D_MODEL, 2 * D_FF), BF16), pltpu.VMEM((D_FF, D_MODEL), BF16)],
        ),
        out_shape=jax.ShapeDtypeStruct((P * TOK_ROWS, LANES), F32),
        compiler_params=_cparams("arbitrary"),
        name="moe_experts",
    )(tile_expert, n_used, xs3, w_gate_up, b_gate_up.reshape(N_EXPERTS, 1, 2 * D_FF), w_down,
      b_down.reshape(N_EXPERTS, 1, D_MODEL))


COMBINE_SLOTS = 3
COMBINE_GROUP = 8


def _combine_kernel(pos_ref, pos1_ref, pos2_ref, wts_ref, h1_ref, nf_ref, ys_ref, o_ref, buf, sems):
    i = pl.program_id(0)
    n = pl.num_programs(0)
    tc = o_ref.shape[0]
    G = COMBINE_GROUP
    slot = lax.rem(i, COMBINE_SLOTS)
    slot2 = lax.rem(i + 2, COMBINE_SLOTS)

    def start_rows(p_ref, s, r):
        for j in range(TOP_K):
            src = pl.multiple_of(p_ref[0, 0, r * TOP_K + j] * TOK_ROWS, TOK_ROWS)
            dst = pl.multiple_of((j * tc + r) * TOK_ROWS, TOK_ROWS)
            pltpu.make_async_copy(ys_ref.at[pl.ds(src, TOK_ROWS)], buf.at[s, pl.ds(dst, TOK_ROWS)],
                                  sems.at[s]).start(priority=j % 2)

    def gather(p_ref, s):
        def issue(r, c):
            start_rows(p_ref, s, r)
            return c
        lax.fori_loop(0, tc, issue, 0, unroll=DMA_UNROLL // TOP_K)

    def combine_group(g):
        rows = slice(g * G, (g + 1) * G)
        w = wts_ref[rows, :]
        h = h1_ref[rows, :]
        for j in range(TOP_K):
            h = h + w[:, j:j + 1] * _load_token_rows(buf.at[slot], j * tc + g * G, G)
        o_ref[rows, :] = h * lax.rsqrt(jnp.mean(h * h, axis=-1, keepdims=True) + EPS) * nf_ref[...]

    @pl.when(i == 0)
    def _():
        gather(pos_ref, 0)

        @pl.when(n > 1)
        def _():
            gather(pos1_ref, 1)

    pltpu.make_async_copy(ys_ref.at[pl.ds(0, TOP_K * tc * TOK_ROWS)], buf.at[slot], sems.at[slot]).wait()

    @pl.when(i + 2 < n)
    def _():
        for g in range(tc // G):
            for r in range(g * G, (g + 1) * G):
                start_rows(pos2_ref, slot2, r)
            combine_group(g)

    @pl.when(i + 2 >= n)
    def _():
        for g in range(tc // G):
            combine_group(g)


def _combine(pos, wts, h1, norm_f_w, ys3):
    T = h1.shape[0]
    tc = min(MOE_TC, T)
    nt = T // tc
    pos3 = pos.reshape(nt, 1, tc * TOP_K)
    ahead = lambda k: pl.BlockSpec((1, 1, tc * TOP_K), lambda i: (jnp.minimum(i + k, nt - 1), 0, 0), memory_space=pltpu.SMEM)
    return pl.pallas_call(
        _combine_kernel,
        grid=(nt,),
        in_specs=[
            ahead(0), ahead(1), ahead(2),
            pl.BlockSpec((tc, ROUTE_PAD), lambda i: (i, 0)),
            pl.BlockSpec((tc, D_MODEL), lambda i: (i, 0)),
            _const_spec((1, D_MODEL)),
            pl.BlockSpec(memory_space=pl.ANY),
        ],
        out_specs=pl.BlockSpec((tc, D_MODEL), lambda i: (i, 0)),
        out_shape=jax.ShapeDtypeStruct((T, D_MODEL), F32),
        scratch_shapes=[pltpu.VMEM((COMBINE_SLOTS, TOP_K * tc * TOK_ROWS, LANES), F32),
                        pltpu.SemaphoreType.DMA((COMBINE_SLOTS,))],
        compiler_params=_cparams("arbitrary"),
        name="moe_combine",
    )(pos3, pos3, pos3, wts, h1, norm_f_w.reshape(1, D_MODEL), ys3)


def _route_plan(top_idx, T):
    tm = MOE_TM
    nt = (T * TOP_K) // tm + N_EXPERTS
    oh = (top_idx[:, :, None] == jnp.arange(N_EXPERTS, dtype=jnp.int32)[None, None, :]).astype(jnp.int32)
    cnt = jnp.sum(oh, axis=1)
    csum = jnp.cumsum(cnt, axis=0)
    counts = csum[-1]
    tiles = (counts + tm - 1) // tm
    tile_end = jnp.cumsum(tiles)
    offs = (tile_end - tiles) * tm
    pos = jnp.sum(oh * (offs[None, :] + csum - cnt)[:, None, :], axis=2).reshape(-1)
    tile_id = jnp.arange(nt, dtype=jnp.int32)
    tile_expert = jnp.minimum(jnp.sum((tile_end[None, :] <= tile_id[:, None]).astype(jnp.int32), axis=1),
                              N_EXPERTS - 1).astype(jnp.int32)
    tail = tile_end[-1] + jnp.arange(N_EXPERTS, dtype=jnp.int32)
    last_tile = jnp.concatenate([jnp.where(tiles > 0, tile_end - 1, -1), jnp.where(tail < nt, tail, -1)]).astype(jnp.int32)
    n_used = tile_end[-1:].astype(jnp.int32)
    return pos.astype(jnp.int32), tile_expert, last_tile, n_used, nt * tm


def kernel(x, norm1_w, w_in, gla_gate_w2, gla_gate_b, gla_norm_w, hy_conv_w, hy_conv_b, hy_freq, hy_ffn_w1, hy_ffn_b1,
           hy_ffn_w2, hy_ffn_b2, hy_ffn_w3, hy_decay, hy_bias, w_branch_gla, w_branch_hy, w_out, norm2_w, router_w,
           router_b, w_gate_up, b_gate_up, w_down, b_down, norm_f_w):
    B, L, D = x.shape
    T = B * L
    x2 = x.reshape(T, D)

    s_a = N_QKVR
    s_h = s_a + 2 * GLA_RANK
    w_b = w_in.astype(BF16)
    w_cat = jnp.concatenate(
        [w_b[:, :s_a], w_b[:, s_h:], w_b[:, s_a:s_h], jnp.zeros((D, ALOW_PAD - 2 * GLA_RANK), BF16)], axis=1)
    qkvr, hy, gates, alow = _in_proj(x2, norm1_w, w_cat)

    w2 = jnp.zeros((2, ALOW_PAD, GLA_KEY), F32)
    w2 = w2.at[0, :GLA_RANK].set(gla_gate_w2[0]).at[1, GLA_RANK:2 * GLA_RANK].set(gla_gate_w2[1])
    o_gla = _gla(qkvr, alow, w2, gla_gate_b.reshape(2, 1, GLA_KEY), gla_norm_w, B, L)

    taps = _hy_filter(L, hy_freq, hy_ffn_w1, hy_ffn_b1, hy_ffn_w2, hy_ffn_b2, hy_ffn_w3, hy_decay)
    o_hy = _hyena(hy, hy_conv_w, hy_conv_b, hy_bias, taps, B, L)

    rw_f = jnp.zeros((D, ROUTE_PAD), F32).at[:, :N_EXPERTS].set(router_w)
    rw_hi = rw_f.astype(BF16)
    rw_pad = jnp.concatenate([rw_hi, (rw_f - rw_hi.astype(F32)).astype(BF16)], axis=1)
    rb_pad = jnp.zeros((1, ROUTE_PAD), F32).at[0, :N_EXPERTS].set(router_b)
    h1, hn3, top_idx, wts_pad = _merge(o_gla, o_hy, gates, x2, w_branch_gla.astype(BF16), w_branch_hy.astype(BF16),
                                       w_out.astype(BF16), norm2_w, rw_pad, rb_pad)

    pos, tile_expert, last_tile, n_used, P = _route_plan(top_idx, T)
    xs3 = _dispatch(hn3, pos, last_tile, P)
    ys3 = _experts(xs3, tile_expert, n_used, w_gate_up, b_gate_up, w_down, b_down)
    out = _combine(pos, wts_pad, h1, norm_f_w, ys3)
    return out.reshape(B, L, D)
```

```python
import functools
import math

import numpy as np
import jax
import jax.numpy as jnp
from jax import lax
from jax.experimental import pallas as pl
from jax.experimental.pallas import tpu as pltpu

D_MODEL = 1024
EPS = 1e-6
GLA_HEADS = 4
GLA_DK = 128
GLA_DV = 256
GLA_KEY = GLA_HEADS * GLA_DK
GLA_VAL = GLA_HEADS * GLA_DV
GLA_RANK = 16
GLA_GATE_TEMP = 16.0
GLA_CHUNK = 64
HY_WIDTH = D_MODEL
HY_BANDS = 16
HY_EMB = 1 + 2 * HY_BANDS
HY_FFN = 64
N_EXPERTS = 32
TOP_K = 4
D_FF = D_MODEL
SWIGLU_LIMIT = 7.0
SWIGLU_ALPHA = 1.702

LANES = 128
SUBLANES = 8
ALOW_PAD = LANES
ROUTE_PAD = LANES
VMEM_LIMIT = 56 * 1024 * 1024

F32 = jnp.float32
BF16 = jnp.bfloat16
HIGHEST = lax.Precision.HIGHEST


def _cparams(*sem):
    return pltpu.CompilerParams(dimension_semantics=sem, vmem_limit_bytes=VMEM_LIMIT)


def _const_spec(shape):
    nd = len(shape)
    return pl.BlockSpec(shape, lambda *_: (0,) * nd, pipeline_mode=pl.Buffered(1))


TOK_ROWS = D_MODEL // LANES


def _store_token_rows(ref, val):
    n = val.shape[0]
    for s_ in range(TOK_ROWS):
        ref[pl.ds(s_, n, stride=TOK_ROWS), :] = val[:, s_ * LANES:(s_ + 1) * LANES]


def _load_token_rows(ref, first, n):
    return jnp.concatenate(
        [ref[pl.ds(first * TOK_ROWS + s_, n, stride=TOK_ROWS), :] for s_ in range(TOK_ROWS)], axis=1)


N_QKVR = 2 * GLA_KEY + 2 * GLA_VAL
N_HY = 3 * HY_WIDTH
N_GATES = 2 * D_MODEL
IN_COLS = N_QKVR + N_HY + N_GATES + ALOW_PAD


def _inproj_kernel(x_ref, nw_ref, w_ref, qkvr_ref, hy_ref, gates_ref, alow_ref):
    x = x_ref[...]
    ms = jnp.mean(x * x, axis=-1, keepdims=True)
    xn = (x * lax.rsqrt(ms + EPS) * nw_ref[...]).astype(BF16)
    c0, c1, c2 = N_QKVR, N_QKVR + N_HY, N_QKVR + N_HY + N_GATES
    qkvr_ref[...] = jnp.dot(xn, w_ref[:, :c0], preferred_element_type=F32).astype(BF16)
    hy_ref[...] = jnp.dot(xn, w_ref[:, c0:c1], preferred_element_type=F32).astype(BF16)
    gates_ref[...] = jnp.dot(xn, w_ref[:, c1:c2], preferred_element_type=F32).astype(BF16)
    alow_ref[...] = jnp.dot(xn, w_ref[:, c2:], preferred_element_type=F32)


def _in_proj(x2, norm1_w, w_cat):
    T = x2.shape[0]
    tm = min(512, T)
    return pl.pallas_call(
        _inproj_kernel,
        grid=(T // tm,),
        in_specs=[
            pl.BlockSpec((tm, D_MODEL), lambda i: (i, 0)),
            _const_spec((1, D_MODEL)),
            _const_spec((D_MODEL, IN_COLS)),
        ],
        out_specs=[
            pl.BlockSpec((tm, N_QKVR), lambda i: (i, 0)),
            pl.BlockSpec((tm, N_HY), lambda i: (i, 0)),
            pl.BlockSpec((tm, N_GATES), lambda i: (i, 0)),
            pl.BlockSpec((tm, ALOW_PAD), lambda i: (i, 0)),
        ],
        out_shape=[
            jax.ShapeDtypeStruct((T, N_QKVR), BF16),
            jax.ShapeDtypeStruct((T, N_HY), BF16),
            jax.ShapeDtypeStruct((T, N_GATES), BF16),
            jax.ShapeDtypeStruct((T, ALOW_PAD), F32),
        ],
        compiler_params=_cparams("parallel"),
        name="in_proj",
    )(x2, norm1_w.reshape(1, D_MODEL), w_cat)


GLA_PREP_ROWS = 256
GLA_PREP_UNROLL = 4
GLA_INTRA_UNROLL = 8
GLA_SCAN_UNROLL = 64


def _log_sigmoid(x):
    return jnp.minimum(x, 0.0) - jnp.log(1.0 + jnp.exp(-jnp.abs(x)))


def _gla_kernel(q_ref, k_ref, v_ref, r_ref, alow_ref, w2_ref, gb_ref, nw_ref, o_ref,
                qtf, ktf, kef, qtb, ktb, keb, decf, decb, of_s, ob_s, stf, stb):
    L = q_ref.shape[0]
    C = GLA_CHUNK
    DK = GLA_DK
    n_chunks = L // C
    R = min(GLA_PREP_ROWS, L)
    cpr = R // C
    scale = DK ** -0.5
    nt = (((1,), (1,)), ((), ()))
    tn = (((0,), (0,)), ((), ()))

    ri = lax.broadcasted_iota(jnp.int32, (R, R), 0)
    ci = lax.broadcasted_iota(jnp.int32, (R, R), 1)
    same = (ri // C) == (ci // C)
    mask_f = same & (ri >= ci)
    mask_b = same & (ri < ci)
    tri = jnp.where(mask_f, 1.0, 0.0).astype(BF16)
    w2 = jnp.concatenate([w2_ref[0], w2_ref[1]], axis=1).astype(BF16)
    gb = jnp.concatenate([gb_ref[0], gb_ref[1]], axis=1)

    def prep(g, carry):
        rows = pl.ds(pl.multiple_of(g * R, R), R)
        logits = jnp.dot(alow_ref[rows, :].astype(BF16), w2, preferred_element_type=F32) + gb
        lg = _log_sigmoid(logits) * (1.0 / GLA_GATE_TEMP)
        hi = lg.astype(BF16)
        lo = (lg - hi.astype(F32)).astype(BF16)
        b = jnp.dot(tri, hi, preferred_element_type=F32) + jnp.dot(tri, lo, preferred_element_type=F32)
        tot = jnp.concatenate(
            [jnp.broadcast_to(b[c * C + C - 1:c * C + C, :], (C, 2 * DK)) for c in range(cpr)], axis=0)
        q = q_ref[rows, :].astype(F32) * scale
        k = k_ref[rows, :].astype(F32)
        dtot = jnp.exp(tot)
        for d, (qt, kt, ke, dec) in enumerate(((qtf, ktf, kef, decf), (qtb, ktb, keb, decb))):
            sl = slice(d * DK, (d + 1) * DK)
            bd, td, ld = b[:, sl], tot[:, sl], lg[:, sl]
            if d == 0:
                e_q, e_end = bd, td - bd
            else:
                e_q, e_end = td - bd + ld, bd - ld
            qt[rows, :] = (q * jnp.exp(e_q)).astype(BF16)
            kt[rows, :] = (k * jnp.exp(-e_q)).astype(BF16)
            ke[rows, :] = (k * jnp.exp(e_end)).astype(BF16)
            for c in range(cpr):
                dec[pl.ds(g * cpr + c, 1), :] = dtot[c * C:c * C + 1, sl]
        return carry

    lax.fori_loop(0, L // R, prep, 0, unroll=min(GLA_PREP_UNROLL, L // R))

    def intra(g, carry):
        rows = pl.ds(pl.multiple_of(g * R, R), R)
        vb = v_ref[rows, :]
        for qt, kt, mask, out in ((qtf, ktf, mask_f, of_s), (qtb, ktb, mask_b, ob_s)):
            s = lax.dot_general(qt[rows, :], kt[rows, :], nt, preferred_element_type=F32)
            s = jnp.where(mask, s, 0.0).astype(BF16)
            out[rows, :] = jnp.dot(s, vb, preferred_element_type=F32)
        return carry

    lax.fori_loop(0, L // R, intra, 0, unroll=min(GLA_INTRA_UNROLL, L // R))

    def chunk(n, st, qt, ke, dec, out):
        rows = pl.ds(pl.multiple_of(n * C, C), C)
        s = st[...]
        out[rows, :] += lax.dot_general(qt[rows, :], s.astype(BF16), nt, preferred_element_type=F32)
        upd = lax.dot_general(v_ref[rows, :], ke[rows, :], tn, preferred_element_type=F32)
        st[...] = dec[pl.ds(n, 1), :] * s + upd

    def step(i, carry):
        chunk(i, stf, qtf, kef, decf, of_s)
        chunk(n_chunks - 1 - i, stb, qtb, keb, decb, ob_s)
        return carry

    stf[...] = jnp.zeros_like(stf)
    stb[...] = jnp.zeros_like(stb)
    lax.fori_loop(0, n_chunks, step, 0, unroll=min(GLA_SCAN_UNROLL, n_chunks))

    o = of_s[...] + ob_s[...]
    o = o * lax.rsqrt(jnp.mean(o * o, axis=-1, keepdims=True) + EPS) * nw_ref[...]
    r = r_ref[...].astype(F32)
    o_ref[...] = (o * (r * jax.nn.sigmoid(r))).astype(BF16)


def _gla(qkvr, alow, w2, gb, norm_w, B, L):
    T = B * L
    H = GLA_HEADS
    kq = GLA_KEY // GLA_DK
    kv = 2 * GLA_KEY // GLA_DV
    kr = (2 * GLA_KEY + GLA_VAL) // GLA_DV
    return pl.pallas_call(
        _gla_kernel,
        grid=(B, H),
        in_specs=[
            pl.BlockSpec((L, GLA_DK), lambda b, h: (b, h)),
            pl.BlockSpec((L, GLA_DK), lambda b, h: (b, kq + h)),
            pl.BlockSpec((L, GLA_DV), lambda b, h: (b, kv + h)),
            pl.BlockSpec((L, GLA_DV), lambda b, h: (b, kr + h)),
            pl.BlockSpec((L, ALOW_PAD), lambda b, h: (b, 0)),
            pl.BlockSpec((2, ALOW_PAD, GLA_DK), lambda b, h: (0, 0, h)),
            pl.BlockSpec((2, 1, GLA_DK), lambda b, h: (0, 0, h)),
            pl.BlockSpec((1, GLA_DV), lambda b, h: (0, 0)),
        ],
        out_specs=pl.BlockSpec((L, GLA_DV), lambda b, h: (b, h)),
        out_shape=jax.ShapeDtypeStruct((T, GLA_VAL), BF16),
        scratch_shapes=[pltpu.VMEM((L, GLA_DK), BF16)] * 6
        + [pltpu.VMEM((L // GLA_CHUNK, GLA_DK), F32)] * 2
        + [pltpu.VMEM((L, GLA_DV), F32)] * 2
        + [pltpu.VMEM((GLA_DV, GLA_DK), F32)] * 2,
        compiler_params=_cparams("parallel", "parallel"),
        name="gla",
    )(qkvr, qkvr, qkvr, qkvr, alow, w2, gb, norm_w.reshape(1, GLA_DV))


HY_WT = 256


def _hy_filter_kernel(z_ref, t_ref, fr_ref, w1_ref, b1_ref, w2_ref, b2_ref, w3f_ref, w3b_ref, dec_ref, o_ref, h_s):
    @pl.when(pl.program_id(0) == 0)
    def _():
        fr = fr_ref[...]
        h1 = jnp.sin(fr * (jnp.dot(z_ref[...], w1_ref[...], preferred_element_type=F32, precision=HIGHEST) + b1_ref[...]))
        h_s[...] = jnp.sin(fr * (jnp.dot(h1, w2_ref[...], preferred_element_type=F32, precision=HIGHEST) + b2_ref[...]))

    h = h_s[...]
    t = t_ref[...]
    row = lax.broadcasted_iota(jnp.int32, (z_ref.shape[0], 1), 0)
    hq = h.astype(BF16)
    hf = jnp.dot(hq, w3f_ref[...].astype(BF16), preferred_element_type=F32)
    hf = hf * jnp.exp(-t * jnp.abs(dec_ref[0:1, :]))
    hb = jnp.dot(hq, w3b_ref[...].astype(BF16), preferred_element_type=F32)
    hb = hb * jnp.exp(-t * jnp.abs(dec_ref[1:2, :]))
    hb = jnp.where(row == 0, 0.0, hb)
    ss = jnp.sum(hf * hf, axis=0, keepdims=True) + jnp.sum(hb * hb, axis=0, keepdims=True)
    inv = lax.rsqrt(ss + EPS)
    o_ref[0] = hf * inv
    o_ref[1] = hb * inv


def _hy_filter(L, hy_freq, w1, b1, w2, b2, w3, hy_decay):
    W = HY_WIDTH
    t = np.linspace(0.0, 1.0, L, dtype=np.float32)[:, None]
    omega = (np.float32(2.0 * math.pi) * np.arange(L, dtype=np.float32) / np.float32(L)).astype(np.float32)
    f = np.linspace(1e-4, HY_BANDS - 1, HY_BANDS, dtype=np.float32)
    ang = (omega[:, None] * f[None, :]).astype(np.float32)
    z = np.concatenate([t, np.cos(ang), -np.sin(ang)], axis=-1).astype(np.float32)
    nw = W // HY_WT
    return pl.pallas_call(
        _hy_filter_kernel,
        grid=(nw,),
        in_specs=[
            _const_spec((L, HY_EMB)),
            _const_spec((L, 1)),
            _const_spec((1, HY_FFN)),
            _const_spec((HY_EMB, HY_FFN)),
            _const_spec((1, HY_FFN)),
            _const_spec((HY_FFN, HY_FFN)),
            _const_spec((1, HY_FFN)),
            pl.BlockSpec((HY_FFN, HY_WT), lambda j: (0, j)),
            pl.BlockSpec((HY_FFN, HY_WT), lambda j: (0, nw + j)),
            pl.BlockSpec((2, HY_WT), lambda j: (0, j)),
        ],
        out_specs=pl.BlockSpec((2, L, HY_WT), lambda j: (0, 0, j)),
        out_shape=jax.ShapeDtypeStruct((2, L, W), F32),
        scratch_shapes=[pltpu.VMEM((L, HY_FFN), F32)],
        compiler_params=_cparams("arbitrary"),
        name="hy_filter",
    )(jnp.asarray(z), jnp.asarray(t), hy_freq.reshape(1, HY_FFN), w1, b1.reshape(1, HY_FFN), w2,
      b2.reshape(1, HY_FFN), w3, w3, hy_decay)


FFT_L = LANES
FFT_PITCH = FFT_L + 8
FFT_PAIR_UNROLL = 32
FFT_K2_UNROLL = 40


def _fft_tables(L):
    N = 2 * L
    C = L // FFT_L
    n1 = N // FFT_L
    kh = min(n1, -(-(n1 // 2 + 1) // 8) * 8)
    l = np.arange(FFT_L)
    c = np.arange(C)
    k2 = np.arange(kh)
    pair_w = np.where((k2 == 0) | (k2 == n1 // 2), 1.0, np.where(k2 < n1 // 2, 2.0, 0.0))
    ang1 = 2 * np.pi * ((k2[:, None] * c[None, :]) % n1) / n1
    f1 = np.concatenate([np.cos(ang1), -np.sin(ang1)], axis=0)
    g1 = (np.stack([np.cos(ang1), -np.sin(ang1)], axis=-1) * pair_w[:, None, None]).transpose(1, 0, 2).reshape(C, 2 * kh) / N
    ang2 = 2 * np.pi * (((l[None, :, None] * l[None, None, :]) % FFT_L) / FFT_L
                        + ((k2[:, None, None] * l[None, None, :]) % N) / N)
    cs, sn = np.cos(ang2), np.sin(ang2)
    f2 = np.concatenate([np.concatenate([cs, sn], axis=2), np.concatenate([-sn, cs], axis=2)], axis=1)
    cst, snt = cs.transpose(0, 2, 1), sn.transpose(0, 2, 1)
    g2 = np.concatenate([np.concatenate([cst, -snt], axis=2), np.concatenate([snt, cst], axis=2)], axis=1)
    bf = lambda a: jnp.asarray(a.astype(np.float32)).astype(BF16)
    return dict(C=C, n1=kh, f1=bf(f1), g1=bf(g1), f2=bf(f2), g2=bf(g2))


def _fft_kernel(z_ref, f1_ref, f2_ref, g2_ref, kf_ref, g1_ref, o_ref, a_s, b_s, y_s, *, spectrum_only, epilogue=None,
                z_pitch=FFT_L):
    C = f1_ref.shape[1]
    n1 = f1_ref.shape[0] // 2
    P = FFT_PITCH
    blk = lambda j: pl.ds(pl.multiple_of(j * P, 8), FFT_L)

    def stage1(j, carry):
        x = jnp.concatenate([z_ref[pl.ds(2 * j, C, stride=z_pitch), :], z_ref[pl.ds(2 * j + 1, C, stride=z_pitch), :]], axis=1)
        r = jnp.dot(f1_ref[...], x.astype(BF16), preferred_element_type=F32)
        a_s[pl.ds(pl.multiple_of(2 * j * P, 8), 2 * n1), :] = r[:, :LANES]
        a_s[pl.ds(pl.multiple_of((2 * j + 1) * P, 8), 2 * n1), :] = r[:, LANES:]
        return carry
    lax.fori_loop(0, FFT_L // 2, stage1, 0, unroll=FFT_PAIR_UNROLL)
    k2_unroll = min(FFT_K2_UNROLL, n1)

    def stage2(k2, carry, *, accumulate=False):
        ar = a_s[pl.ds(k2, FFT_L, stride=P), :]
        ai = a_s[pl.ds(n1 + k2, FFT_L, stride=P), :]
        x = jnp.concatenate([ar, ai], axis=0).astype(BF16)
        s = jnp.dot(f2_ref[k2], x, preferred_element_type=F32)
        xr, xi = s[:FFT_L], s[FFT_L:]
        if not spectrum_only:
            kr, ki = kf_ref[k2, 0], kf_ref[k2, 1]
            y = jnp.concatenate([xr * kr - xi * ki, xr * ki + xi * kr], axis=0).astype(BF16)
            s = jnp.dot(g2_ref[k2], y, preferred_element_type=F32)
            b_s[blk(2 * k2), :] = s[:FFT_L]
            b_s[blk(2 * k2 + 1), :] = s[FFT_L:]
        elif accumulate:
            o_ref[k2, 0] += xr
            o_ref[k2, 1] -= xi
        else:
            o_ref[k2, 0] = xr
            o_ref[k2, 1] = xi
        return carry

    if spectrum_only:
        @pl.when(pl.program_id(1) == 0)
        def _():
            lax.fori_loop(0, n1, stage2, 0, unroll=k2_unroll)

        @pl.when(pl.program_id(1) != 0)
        def _():
            lax.fori_loop(0, n1, functools.partial(stage2, accumulate=True), 0, unroll=k2_unroll)
    else:
        lax.fori_loop(0, n1, stage2, 0, unroll=k2_unroll)

    if not spectrum_only:
        def stage3(j, carry):
            x = jnp.concatenate([b_s[pl.ds(2 * j, 2 * n1, stride=P), :], b_s[pl.ds(2 * j + 1, 2 * n1, stride=P), :]], axis=1)
            r = jnp.dot(g1_ref[...], x.astype(BF16), preferred_element_type=F32)
            y_s[pl.ds(2 * j, C, stride=P), :] = r[:, :LANES]
            y_s[pl.ds(2 * j + 1, C, stride=P), :] = r[:, LANES:]
            return carry
        lax.fori_loop(0, FFT_L // 2, stage3, 0, unroll=FFT_PAIR_UNROLL)
        for c in range(C):
            rows = slice(c * FFT_L, (c + 1) * FFT_L)
            o_ref[rows, :] = epilogue(c, y_s[c * P:c * P + FFT_L, :])


def _hyena_kernel(x0_ref, x1_ref, v_ref, w0_ref, w1_ref, wv_ref, b0_ref, b1_ref, bv_ref, hb_ref,
                  f1_ref, f2_ref, g2_ref, kf_ref, g1_ref, o_ref, z_s, x0_s, a_s, b_s, y_s):
    L = x0_ref.shape[0]
    row = lax.broadcasted_iota(jnp.int32, (L, 1), 0)

    def conv3(u_ref, w_ref, b_ref):
        u = u_ref[...].astype(F32)
        prev = jnp.where(row == 0, 0.0, pltpu.roll(u, 1, 0))
        nxt = jnp.where(row == L - 1, 0.0, pltpu.roll(u, L - 1, 0))
        return w_ref[0:1, :] * prev + w_ref[1:2, :] * u + w_ref[2:3, :] * nxt + b_ref[...]

    P = FFT_PITCH
    chunk = lambda c: slice(c * P, c * P + FFT_L)
    z = conv3(v_ref, wv_ref, bv_ref) * conv3(x1_ref, w1_ref, b1_ref)
    for c in range(L // FFT_L):
        z_s[chunk(c), :] = z[c * FFT_L:(c + 1) * FFT_L, :]
    x0_s[...] = conv3(x0_ref, w0_ref, b0_ref)

    def epilogue(c, y):
        return ((y + z_s[chunk(c), :] * hb_ref[...]) * x0_s[c * FFT_L:(c + 1) * FFT_L, :]).astype(BF16)

    _fft_kernel(z_s, f1_ref, f2_ref, g2_ref, kf_ref, g1_ref, o_ref, a_s, b_s, y_s, spectrum_only=False, epilogue=epilogue,
                z_pitch=P)


def _fft_specs(tabs):
    C, n1 = tabs["C"], tabs["n1"]
    specs = [_const_spec((2 * n1, C)),
             _const_spec((n1, 2 * FFT_L, 2 * FFT_L)), _const_spec((n1, 2 * FFT_L, 2 * FFT_L)),
             pl.BlockSpec((n1, 2, FFT_L, LANES), lambda w, b: (0, 0, 0, w)),
             _const_spec((C, 2 * n1))]
    scratch = [pltpu.VMEM((FFT_L * FFT_PITCH, LANES), F32), pltpu.VMEM((2 * n1 * FFT_PITCH, LANES), F32),
               pltpu.VMEM((C * FFT_PITCH, LANES), F32)]
    return specs, scratch


def _filter_spectrum(tabs, taps):
    _, L, W = taps.shape
    n1 = tabs["n1"]
    specs, scratch = _fft_specs(tabs)

    def spectrum_kernel(z_ref, f1_ref, f2_ref, o_ref, a_s):
        _fft_kernel(z_ref, f1_ref, f2_ref, None, None, None, o_ref, a_s, None, None, spectrum_only=True)

    return pl.pallas_call(
        spectrum_kernel,
        grid=(W // LANES, 2),
        in_specs=[pl.BlockSpec((None, L, LANES), lambda w, b: (b, 0, w))] + specs[:2],
        out_specs=pl.BlockSpec((n1, 2, FFT_L, LANES), lambda w, b: (0, 0, 0, w)),
        out_shape=jax.ShapeDtypeStruct((n1, 2, FFT_L, W), F32),
        scratch_shapes=scratch[:1],
        compiler_params=_cparams("arbitrary", "arbitrary"),
        name="fft_spectrum",
    )(taps, tabs["f1"], tabs["f2"])


def _hyena(hy, conv_w, conv_b, hy_bias, taps, B, L):
    W = HY_WIDTH
    nw = W // LANES
    tabs = _fft_tables(L)
    kf = _filter_spectrum(tabs, taps)
    specs, scratch = _fft_specs(tabs)
    cb = conv_b.reshape(1, 3 * W)
    blk = lambda off: pl.BlockSpec((L, LANES), lambda w, b: (b, off + w))
    wblk = lambda off: pl.BlockSpec((3, LANES), lambda w, b: (0, off + w))
    bblk = lambda off: pl.BlockSpec((1, LANES), lambda w, b: (0, off + w))
    return pl.pallas_call(
        _hyena_kernel,
        grid=(nw, B),
        in_specs=[blk(0), blk(nw), blk(2 * nw), wblk(0), wblk(nw), wblk(2 * nw), bblk(0), bblk(nw), bblk(2 * nw),
                  bblk(0)] + specs,
        out_specs=pl.BlockSpec((L, LANES), lambda w, b: (b, w)),
        out_shape=jax.ShapeDtypeStruct((B * L, W), BF16),
        scratch_shapes=[pltpu.VMEM((L // FFT_L * FFT_PITCH, LANES), F32), pltpu.VMEM((L, LANES), F32)] + scratch,
        compiler_params=_cparams("arbitrary", "arbitrary"),
        name="hyena",
    )(hy, hy, hy, conv_w, conv_w, conv_w, cb, cb, cb, hy_bias.reshape(1, W), tabs["f1"], tabs["f2"], tabs["g2"], kf,
      tabs["g1"])


def _merge_kernel(og_ref, oh_ref, gates_ref, x_ref, wbg_ref, wbh_ref, wo_ref, n2_ref, rw_ref, rb_ref,
                  h1_ref, hn3_ref, idx_ref, wts_ref):
    mg = jnp.dot(og_ref[...], wbg_ref[...], preferred_element_type=F32)
    mh = jnp.dot(oh_ref[...], wbh_ref[...], preferred_element_type=F32)
    gg = jax.nn.sigmoid(gates_ref[:, :D_MODEL].astype(F32))
    gh = jax.nn.sigmoid(gates_ref[:, D_MODEL:].astype(F32))
    merged = (gg * mg + gh * mh).astype(BF16)
    h1 = x_ref[...] + jnp.dot(merged, wo_ref[...], preferred_element_type=F32)
    h1_ref[...] = h1
    hn = h1 * lax.rsqrt(jnp.mean(h1 * h1, axis=-1, keepdims=True) + EPS) * n2_ref[...]
    _store_token_rows(hn3_ref, hn)
    hh = hn.astype(BF16)
    hl = (hn - hh.astype(F32)).astype(BF16)
    p1 = jnp.dot(hh, rw_ref[...], preferred_element_type=F32)
    p2 = jnp.dot(hl, rw_ref[:, :ROUTE_PAD], preferred_element_type=F32)
    logits = p1[:, :ROUTE_PAD] + p1[:, ROUTE_PAD:] + p2 + rb_ref[...]
    tm = logits.shape[0]
    le = logits.T[:N_EXPERTS]
    erow = lax.broadcasted_iota(jnp.int32, le.shape, 0)
    sels, vals = [], []
    for j in range(TOP_K):
        m = jnp.max(le, axis=0, keepdims=True)
        sel = jnp.min(jnp.where(le == m, erow, N_EXPERTS), axis=0, keepdims=True)
        sels.append(sel)
        vals.append(m)
        le = jnp.where(erow == sel, -jnp.inf, le)
    ev = [jnp.exp(v - vals[0]) for v in vals]
    inv = 1.0 / (ev[0] + ev[1] + ev[2] + ev[3])
    srow = lax.broadcasted_iota(jnp.int32, (SUBLANES, tm), 0)
    idx8 = jnp.zeros((SUBLANES, tm), F32)
    wts8 = jnp.zeros((SUBLANES, tm), F32)
    for j in range(TOP_K):
        idx8 = jnp.where(srow == j, sels[j].astype(F32), idx8)
        wts8 = jnp.where(srow == j, ev[j] * inv, wts8)
    pad = jnp.zeros((ROUTE_PAD - SUBLANES, tm), F32)
    idx_ref[...] = jnp.concatenate([idx8, pad], axis=0).T[:, :TOP_K].astype(jnp.int32)
    wts_ref[...] = jnp.concatenate([wts8, pad], axis=0).T


def _merge(o_gla, o_hy, gates, x2, wbg, wbh, wo, norm2_w, rw_pad, rb_pad):
    T = x2.shape[0]
    tm = min(512, T)
    row = lambda n: pl.BlockSpec((tm, n), lambda i: (i, 0))
    return pl.pallas_call(
        _merge_kernel,
        grid=(T // tm,),
        in_specs=[row(GLA_VAL), row(HY_WIDTH), row(N_GATES), row(D_MODEL),
                  _const_spec((GLA_VAL, D_MODEL)), _const_spec((HY_WIDTH, D_MODEL)), _const_spec((D_MODEL, D_MODEL)),
                  _const_spec((1, D_MODEL)), _const_spec((D_MODEL, 2 * ROUTE_PAD)), _const_spec((1, ROUTE_PAD))],
        out_specs=[row(D_MODEL), pl.BlockSpec((tm * TOK_ROWS, LANES), lambda i: (i, 0)), row(TOP_K), row(ROUTE_PAD)],
        out_shape=[jax.ShapeDtypeStruct((T, D_MODEL), F32), jax.ShapeDtypeStruct((T * TOK_ROWS, LANES), F32),
                   jax.ShapeDtypeStruct((T, TOP_K), jnp.int32), jax.ShapeDtypeStruct((T, ROUTE_PAD), F32)],
        compiler_params=_cparams("parallel"),
        name="merge_route",
    )(o_gla, o_hy, gates, x2, wbg, wbh, wo, norm2_w.reshape(1, D_MODEL), rw_pad, rb_pad)


MOE_TM = 512
MOE_TC = 256
DMA_UNROLL = 8


DISPATCH_SLOTS = 3


def _dispatch_kernel(last_ref, pos_ref, hn_ref, xs_ref, zero_s, buf, sem, lsems, ssems):
    i = pl.program_id(0)
    n = pl.num_programs(0)
    tc = pos_ref.shape[2] // TOP_K
    rows = tc * TOK_ROWS
    zrows = zero_s.shape[0]
    slot = lax.rem(i, DISPATCH_SLOTS)
    prev_slot = lax.rem(i + DISPATCH_SLOTS - 1, DISPATCH_SLOTS)

    def load(blk, s):
        src = hn_ref.at[pl.ds(pl.multiple_of(blk * rows, rows), rows)]
        return pltpu.make_async_copy(src, buf.at[s], lsems.at[s])

    def wait_copies(s):
        for _ in range(TOP_K):
            pltpu.make_async_copy(buf.at[s], xs_ref.at[pl.ds(0, rows)], ssems.at[s]).wait()

    @pl.when(i == 0)
    def _():
        load(0, 0).start()

        @pl.when(n > 1)
        def _():
            load(1, 1).start()

        zero_s[...] = jnp.zeros_like(zero_s)
        n_fill = last_ref.shape[0]

        def zfill(e, c):
            @pl.when(last_ref[e] >= 0)
            def _():
                dst = pl.multiple_of(last_ref[e] * zrows, zrows)
                pltpu.make_async_copy(zero_s, xs_ref.at[pl.ds(dst, zrows)], sem).start()
            return c
        lax.fori_loop(0, n_fill, zfill, 0)

        def zwait(e, c):
            @pl.when(last_ref[e] >= 0)
            def _():
                pltpu.make_async_copy(zero_s, xs_ref.at[pl.ds(0, zrows)], sem).wait()
            return c
        lax.fori_loop(0, n_fill, zwait, 0)

    load(i, slot).wait()

    def issue(r, c):
        src = buf.at[slot, pl.ds(pl.multiple_of(r * TOK_ROWS, TOK_ROWS), TOK_ROWS)]
        for j in range(TOP_K):
            dst = pl.multiple_of(pos_ref[0, 0, r * TOP_K + j] * TOK_ROWS, TOK_ROWS)
            pltpu.make_async_copy(src, xs_ref.at[pl.ds(dst, TOK_ROWS)], ssems.at[slot]).start(priority=j % 2)
        return c
    lax.fori_loop(0, tc, issue, 0, unroll=DMA_UNROLL // TOP_K)

    @pl.when(i > 0)
    def _():
        wait_copies(prev_slot)

    @pl.when(i + 2 < n)
    def _():
        load(i + 2, prev_slot).start()

    @pl.when(i == n - 1)
    def _():
        wait_copies(slot)


def _dispatch(hn3, pos, last_tile, P):
    T = hn3.shape[0] // TOK_ROWS
    tc = min(MOE_TC, T)
    nt = T // tc
    return pl.pallas_call(
        _dispatch_kernel,
        grid_spec=pltpu.PrefetchScalarGridSpec(
            num_scalar_prefetch=1,
            grid=(nt,),
            in_specs=[pl.BlockSpec((1, 1, tc * TOP_K), lambda i, lt: (i, 0, 0), memory_space=pltpu.SMEM),
                      pl.BlockSpec(memory_space=pl.ANY)],
            out_specs=pl.BlockSpec(memory_space=pl.ANY),
            scratch_shapes=[pltpu.VMEM((MOE_TM * TOK_ROWS, LANES), F32),
                            pltpu.VMEM((DISPATCH_SLOTS, tc * TOK_ROWS, LANES), F32),
                            pltpu.SemaphoreType.DMA(()),
                            pltpu.SemaphoreType.DMA((DISPATCH_SLOTS,)), pltpu.SemaphoreType.DMA((DISPATCH_SLOTS,))],
        ),
        out_shape=jax.ShapeDtypeStruct((P * TOK_ROWS, LANES), F32),
        compiler_params=_cparams("arbitrary"),
        name="moe_dispatch",
    )(last_tile, pos.reshape(nt, 1, tc * TOP_K), hn3)


def _expert_kernel(te_ref, nused_ref, xs_ref, wgu_ref, bgu_ref, wd_ref, bd_ref, o_ref, wgu_s, wd_s):
    i = pl.program_id(0)
    prev = te_ref[jnp.maximum(i - 1, 0)]

    @pl.when((i == 0) | (te_ref[i] != prev))
    def _():
        wgu_s[...] = wgu_ref[0].astype(BF16)
        wd_s[...] = wd_ref[0].astype(BF16)

    @pl.when(i < nused_ref[0])
    def _():
        x = _load_token_rows(xs_ref, 0, xs_ref.shape[0] // TOK_ROWS).astype(BF16)
        gu = jnp.dot(x, wgu_s[...], preferred_element_type=F32) + bgu_ref[0]
        gate = jnp.minimum(gu[:, :D_FF], SWIGLU_LIMIT)
        up = jnp.clip(gu[:, D_FF:], -SWIGLU_LIMIT, SWIGLU_LIMIT)
        act = gate * jax.nn.sigmoid(SWIGLU_ALPHA * gate) * (up + 1.0)
        y = jnp.dot(act.astype(BF16), wd_s[...], preferred_element_type=F32) + bd_ref[0]
        _store_token_rows(o_ref, y)

    @pl.when(i >= nused_ref[0])
    def _():
        o_ref[...] = jnp.zeros_like(o_ref)


def _experts(xs3, tile_expert, n_used, w_gate_up, b_gate_up, w_down, b_down):
    P = xs3.shape[0] // TOK_ROWS
    tm = MOE_TM
    nt = P // tm
    live = lambda i, te, nu: jnp.minimum(i, nu[0] - 1)
    return pl.pallas_call(
        _expert_kernel,
        grid_spec=pltpu.PrefetchScalarGridSpec(
            num_scalar_prefetch=2,
            grid=(nt,),
            in_specs=[
                pl.BlockSpec((tm * TOK_ROWS, LANES), lambda i, te, nu: (live(i, te, nu), 0)),
                pl.BlockSpec((1, D_MODEL, 2 * D_FF), lambda i, te, nu: (te[i], 0, 0)),
                pl.BlockSpec((1, 1, 2 * D_FF), lambda i, te, nu: (te[i], 0, 0)),
                pl.BlockSpec((1, D_FF, D_MODEL), lambda i, te, nu: (te[i], 0, 0)),
                pl.BlockSpec((1, 1, D_MODEL), lambda i, te, nu: (te[i], 0, 0)),
            ],
            out_specs=pl.BlockSpec((tm * TOK_ROWS, LANES), lambda i, te, nu: (i, 0)),
            scratch_shapes=[pltpu.VMEM((D_MODEL, 2 * D_FF), BF16), pltpu.VMEM((D_FF, D_MODEL), BF16)],
        ),
        out_shape=jax.ShapeDtypeStruct((P * TOK_ROWS, LANES), F32),
        compiler_params=_cparams("arbitrary"),
        name="moe_experts",
    )(tile_expert, n_used, xs3, w_gate_up, b_gate_up.reshape(N_EXPERTS, 1, 2 * D_FF), w_down,
      b_down.reshape(N_EXPERTS, 1, D_MODEL))


COMBINE_SLOTS = 3
COMBINE_GROUP = 8


def _combine_kernel(pos_ref, pos1_ref, pos2_ref, wts_ref, h1_ref, nf_ref, ys_ref, o_ref, buf, sems):
    i = pl.program_id(0)
    n = pl.num_programs(0)
    tc = o_ref.shape[0]
    G = COMBINE_GROUP
    slot = lax.rem(i, COMBINE_SLOTS)
    slot2 = lax.rem(i + 2, COMBINE_SLOTS)

    def start_rows(p_ref, s, r):
        for j in range(TOP_K):
            src = pl.multiple_of(p_ref[0, 0, r * TOP_K + j] * TOK_ROWS, TOK_ROWS)
            dst = pl.multiple_of((j * tc + r) * TOK_ROWS, TOK_ROWS)
            pltpu.make_async_copy(ys_ref.at[pl.ds(src, TOK_ROWS)], buf.at[s, pl.ds(dst, TOK_ROWS)],
                                  sems.at[s]).start(priority=j % 2)

    def gather(p_ref, s):
        def issue(r, c):
            start_rows(p_ref, s, r)
            return c
        lax.fori_loop(0, tc, issue, 0, unroll=DMA_UNROLL // TOP_K)

    def combine_group(g):
        rows = slice(g * G, (g + 1) * G)
        w = wts_ref[rows, :]
        h = h1_ref[rows, :]
        for j in range(TOP_K):
            h = h + w[:, j:j + 1] * _load_token_rows(buf.at[slot], j * tc + g * G, G)
        o_ref[rows, :] = h * lax.rsqrt(jnp.mean(h * h, axis=-1, keepdims=True) + EPS) * nf_ref[...]

    @pl.when(i == 0)
    def _():
        gather(pos_ref, 0)

        @pl.when(n > 1)
        def _():
            gather(pos1_ref, 1)

    pltpu.make_async_copy(ys_ref.at[pl.ds(0, TOP_K * tc * TOK_ROWS)], buf.at[slot], sems.at[slot]).wait()

    @pl.when(i + 2 < n)
    def _():
        for g in range(tc // G):
            for r in range(g * G, (g + 1) * G):
                start_rows(pos2_ref, slot2, r)
            combine_group(g)

    @pl.when(i + 2 >= n)
    def _():
        for g in range(tc // G):
            combine_group(g)


def _combine(pos, wts, h1, norm_f_w, ys3):
    T = h1.shape[0]
    tc = min(MOE_TC, T)
    nt = T // tc
    pos3 = pos.reshape(nt, 1, tc * TOP_K)
    ahead = lambda k: pl.BlockSpec((1, 1, tc * TOP_K), lambda i: (jnp.minimum(i + k, nt - 1), 0, 0), memory_space=pltpu.SMEM)
    return pl.pallas_call(
        _combine_kernel,
        grid=(nt,),
        in_specs=[
            ahead(0), ahead(1), ahead(2),
            pl.BlockSpec((tc, ROUTE_PAD), lambda i: (i, 0)),
            pl.BlockSpec((tc, D_MODEL), lambda i: (i, 0)),
            _const_spec((1, D_MODEL)),
            pl.BlockSpec(memory_space=pl.ANY),
        ],
        out_specs=pl.BlockSpec((tc, D_MODEL), lambda i: (i, 0)),
        out_shape=jax.ShapeDtypeStruct((T, D_MODEL), F32),
        scratch_shapes=[pltpu.VMEM((COMBINE_SLOTS, TOP_K * tc * TOK_ROWS, LANES), F32),
                        pltpu.SemaphoreType.DMA((COMBINE_SLOTS,))],
        compiler_params=_cparams("arbitrary"),
        name="moe_combine",
    )(pos3, pos3, pos3, wts, h1, norm_f_w.reshape(1, D_MODEL), ys3)


def _route_plan(top_idx, T):
    tm = MOE_TM
    nt = (T * TOP_K) // tm + N_EXPERTS
    oh = (top_idx[:, :, None] == jnp.arange(N_EXPERTS, dtype=jnp.int32)[None, None, :]).astype(jnp.int32)
    cnt = jnp.sum(oh, axis=1)
    csum = jnp.cumsum(cnt, axis=0)
    counts = csum[-1]
    tiles = (counts + tm - 1) // tm
    tile_end = jnp.cumsum(tiles)
    offs = (tile_end - tiles) * tm
    pos = jnp.sum(oh * (offs[None, :] + csum - cnt)[:, None, :], axis=2).reshape(-1)
    tile_id = jnp.arange(nt, dtype=jnp.int32)
    tile_expert = jnp.minimum(jnp.sum((tile_end[None, :] <= tile_id[:, None]).astype(jnp.int32), axis=1),
                              N_EXPERTS - 1).astype(jnp.int32)
    tail = tile_end[-1] + jnp.arange(N_EXPERTS, dtype=jnp.int32)
    last_tile = jnp.concatenate([jnp.where(tiles > 0, tile_end - 1, -1), jnp.where(tail < nt, tail, -1)]).astype(jnp.int32)
    n_used = tile_end[-1:].astype(jnp.int32)
    return pos.astype(jnp.int32), tile_expert, last_tile, n_used, nt * tm


def kernel(x, norm1_w, w_in, gla_gate_w2, gla_gate_b, gla_norm_w, hy_conv_w, hy_conv_b, hy_freq, hy_ffn_w1, hy_ffn_b1,
           hy_ffn_w2, hy_ffn_b2, hy_ffn_w3, hy_decay, hy_bias, w_branch_gla, w_branch_hy, w_out, norm2_w, router_w,
           router_b, w_gate_up, b_gate_up, w_down, b_down, norm_f_w):
    B, L, D = x.shape
    T = B * L
    x2 = x.reshape(T, D)

    s_a = N_QKVR
    s_h = s_a + 2 * GLA_RANK
    w_b = w_in.astype(BF16)
    w_cat = jnp.concatenate(
        [w_b[:, :s_a], w_b[:, s_h:], w_b[:, s_a:s_h], jnp.zeros((D, ALOW_PAD - 2 * GLA_RANK), BF16)], axis=1)
    qkvr, hy, gates, alow = _in_proj(x2, norm1_w, w_cat)

    w2 = jnp.zeros((2, ALOW_PAD, GLA_KEY), F32)
    w2 = w2.at[0, :GLA_RANK].set(gla_gate_w2[0]).at[1, GLA_RANK:2 * GLA_RANK].set(gla_gate_w2[1])
    o_gla = _gla(qkvr, alow, w2, gla_gate_b.reshape(2, 1, GLA_KEY), gla_norm_w, B, L)

    taps = _hy_filter(L, hy_freq, hy_ffn_w1, hy_ffn_b1, hy_ffn_w2, hy_ffn_b2, hy_ffn_w3, hy_decay)
    o_hy = _hyena(hy, hy_conv_w, hy_conv_b, hy_bias, taps, B, L)

    rw_f = jnp.zeros((D, ROUTE_PAD), F32).at[:, :N_EXPERTS].set(router_w)
    rw_hi = rw_f.astype(BF16)
    rw_pad = jnp.concatenate([rw_hi, (rw_f - rw_hi.astype(F32)).astype(BF16)], axis=1)
    rb_pad = jnp.zeros((1, ROUTE_PAD), F32).at[0, :N_EXPERTS].set(router_b)
    h1, hn3, top_idx, wts_pad = _merge(o_gla, o_hy, gates, x2, w_branch_gla.astype(BF16), w_branch_hy.astype(BF16),
                                       w_out.astype(BF16), norm2_w, rw_pad, rb_pad)

    pos, tile_expert, last_tile, n_used, P = _route_plan(top_idx, T)
    xs3 = _dispatch(hn3, pos, last_tile, P)
    ys3 = _experts(xs3, tile_expert, n_used, w_gate_up, b_gate_up, w_down, b_down)
    out = _combine(pos, wts_pad, h1, norm_f_w, ys3)
    return out.reshape(B, L, D)
```

```python
import functools
import math

import numpy as np
import jax
import jax.numpy as jnp
from jax import lax
from jax.experimental import pallas as pl
from jax.experimental.pallas import tpu as pltpu

D_MODEL = 1024
EPS = 1e-6
GLA_HEADS = 4
GLA_DK = 128
GLA_DV = 256
GLA_KEY = GLA_HEADS * GLA_DK
GLA_VAL = GLA_HEADS * GLA_DV
GLA_RANK = 16
GLA_GATE_TEMP = 16.0
GLA_CHUNK = 64
HY_WIDTH = D_MODEL
HY_BANDS = 16
HY_EMB = 1 + 2 * HY_BANDS
HY_FFN = 64
N_EXPERTS = 32
TOP_K = 4
D_FF = D_MODEL
SWIGLU_LIMIT = 7.0
SWIGLU_ALPHA = 1.702

LANES = 128
SUBLANES = 8
ALOW_PAD = LANES
ROUTE_PAD = LANES
VMEM_LIMIT = 56 * 1024 * 1024

F32 = jnp.float32
BF16 = jnp.bfloat16
HIGHEST = lax.Precision.HIGHEST


def _cparams(*sem):
    return pltpu.CompilerParams(dimension_semantics=sem, vmem_limit_bytes=VMEM_LIMIT)


def _const_spec(shape):
    nd = len(shape)
    return pl.BlockSpec(shape, lambda *_: (0,) * nd, pipeline_mode=pl.Buffered(1))


TOK_ROWS = D_MODEL // LANES


def _store_token_rows(ref, val):
    n = val.shape[0]
    for s_ in range(TOK_ROWS):
        ref[pl.ds(s_, n, stride=TOK_ROWS), :] = val[:, s_ * LANES:(s_ + 1) * LANES]


def _load_token_rows(ref, first, n):
    return jnp.concatenate(
        [ref[pl.ds(first * TOK_ROWS + s_, n, stride=TOK_ROWS), :] for s_ in range(TOK_ROWS)], axis=1)


N_QKVR = 2 * GLA_KEY + 2 * GLA_VAL
N_HY = 3 * HY_WIDTH
N_GATES = 2 * D_MODEL
IN_COLS = N_QKVR + N_HY + N_GATES + ALOW_PAD


def _inproj_kernel(x_ref, nw_ref, w_ref, qkvr_ref, hy_ref, gates_ref, alow_ref):
    x = x_ref[...]
    ms = jnp.mean(x * x, axis=-1, keepdims=True)
    xn = (x * lax.rsqrt(ms + EPS) * nw_ref[...]).astype(BF16)
    c0, c1, c2 = N_QKVR, N_QKVR + N_HY, N_QKVR + N_HY + N_GATES
    qkvr_ref[...] = jnp.dot(xn, w_ref[:, :c0], preferred_element_type=F32).astype(BF16)
    hy_ref[...] = jnp.dot(xn, w_ref[:, c0:c1], preferred_element_type=F32).astype(BF16)
    gates_ref[...] = jnp.dot(xn, w_ref[:, c1:c2], preferred_element_type=F32).astype(BF16)
    alow_ref[...] = jnp.dot(xn, w_ref[:, c2:], preferred_element_type=F32)


def _in_proj(x2, norm1_w, w_cat):
    T = x2.shape[0]
    tm = min(512, T)
    return pl.pallas_call(
        _inproj_kernel,
        grid=(T // tm,),
        in_specs=[
            pl.BlockSpec((tm, D_MODEL), lambda i: (i, 0)),
            _const_spec((1, D_MODEL)),
            _const_spec((D_MODEL, IN_COLS)),
        ],
        out_specs=[
            pl.BlockSpec((tm, N_QKVR), lambda i: (i, 0)),
            pl.BlockSpec((tm, N_HY), lambda i: (i, 0)),
            pl.BlockSpec((tm, N_GATES), lambda i: (i, 0)),
            pl.BlockSpec((tm, ALOW_PAD), lambda i: (i, 0)),
        ],
        out_shape=[
            jax.ShapeDtypeStruct((T, N_QKVR), BF16),
            jax.ShapeDtypeStruct((T, N_HY), BF16),
            jax.ShapeDtypeStruct((T, N_GATES), BF16),
            jax.ShapeDtypeStruct((T, ALOW_PAD), F32),
        ],
        compiler_params=_cparams("parallel"),
        name="in_proj",
    )(x2, norm1_w.reshape(1, D_MODEL), w_cat)


GLA_PREP_ROWS = 256
GLA_PREP_UNROLL = 8
GLA_INTRA_UNROLL = 16
GLA_SCAN_UNROLL = 64


def _log_sigmoid(x):
    return jnp.minimum(x, 0.0) - jnp.log(1.0 + jnp.exp(-jnp.abs(x)))


def _gla_kernel(q_ref, k_ref, v_ref, r_ref, alow_ref, w2_ref, gb_ref, nw_ref, o_ref,
                qtf, ktf, kef, qtb, ktb, keb, decf, decb, of_s, ob_s, stf, stb):
    L = q_ref.shape[0]
    C = GLA_CHUNK
    DK = GLA_DK
    n_chunks = L // C
    R = min(GLA_PREP_ROWS, L)
    cpr = R // C
    scale = DK ** -0.5
    nt = (((1,), (1,)), ((), ()))
    tn = (((0,), (0,)), ((), ()))

    ri = lax.broadcasted_iota(jnp.int32, (R, R), 0)
    ci = lax.broadcasted_iota(jnp.int32, (R, R), 1)
    same = (ri // C) == (ci // C)
    mask_f = same & (ri >= ci)
    mask_b = same & (ri < ci)
    tri = jnp.where(mask_f, 1.0, 0.0).astype(BF16)
    w2 = jnp.concatenate([w2_ref[0], w2_ref[1]], axis=1).astype(BF16)
    gb = jnp.concatenate([gb_ref[0], gb_ref[1]], axis=1)

    def prep(g, carry):
        rows = pl.ds(pl.multiple_of(g * R, R), R)
        logits = jnp.dot(alow_ref[rows, :].astype(BF16), w2, preferred_element_type=F32) + gb
        lg = _log_sigmoid(logits) * (1.0 / GLA_GATE_TEMP)
        hi = lg.astype(BF16)
        lo = (lg - hi.astype(F32)).astype(BF16)
        b = jnp.dot(tri, hi, preferred_element_type=F32) + jnp.dot(tri, lo, preferred_element_type=F32)
        tot = jnp.concatenate(
            [jnp.broadcast_to(b[c * C + C - 1:c * C + C, :], (C, 2 * DK)) for c in range(cpr)], axis=0)
        q = q_ref[rows, :].astype(F32) * scale
        k = k_ref[rows, :].astype(F32)
        dtot = jnp.exp(tot)
        for d, (qt, kt, ke, dec) in enumerate(((qtf, ktf, kef, decf), (qtb, ktb, keb, decb))):
            sl = slice(d * DK, (d + 1) * DK)
            bd, td, ld = b[:, sl], tot[:, sl], lg[:, sl]
            if d == 0:
                e_q, e_end = bd, td - bd
            else:
                e_q, e_end = td - bd + ld, bd - ld
            qt[rows, :] = (q * jnp.exp(e_q)).astype(BF16)
            kt[rows, :] = (k * jnp.exp(-e_q)).astype(BF16)
            ke[rows, :] = (k * jnp.exp(e_end)).astype(BF16)
            for c in range(cpr):
                dec[pl.ds(g * cpr + c, 1), :] = dtot[c * C:c * C + 1, sl]
        return carry

    lax.fori_loop(0, L // R, prep, 0, unroll=min(GLA_PREP_UNROLL, L // R))

    def intra(g, carry):
        rows = pl.ds(pl.multiple_of(g * R, R), R)
        vb = v_ref[rows, :]
        for qt, kt, mask, out in ((qtf, ktf, mask_f, of_s), (qtb, ktb, mask_b, ob_s)):
            s = lax.dot_general(qt[rows, :], kt[rows, :], nt, preferred_element_type=F32)
            s = jnp.where(mask, s, 0.0).astype(BF16)
            out[rows, :] = jnp.dot(s, vb, preferred_element_type=F32)
        return carry

    lax.fori_loop(0, L // R, intra, 0, unroll=min(GLA_INTRA_UNROLL, L // R))

    def chunk(n, st, qt, ke, dec, out):
        rows = pl.ds(pl.multiple_of(n * C, C), C)
        s = st[...]
        out[rows, :] += lax.dot_general(qt[rows, :], s.astype(BF16), nt, preferred_element_type=F32)
        upd = lax.dot_general(v_ref[rows, :], ke[rows, :], tn, preferred_element_type=F32)
        st[...] = dec[pl.ds(n, 1), :] * s + upd

    def step(i, carry):
        chunk(i, stf, qtf, kef, decf, of_s)
        chunk(n_chunks - 1 - i, stb, qtb, keb, decb, ob_s)
        return carry

    stf[...] = jnp.zeros_like(stf)
    stb[...] = jnp.zeros_like(stb)
    lax.fori_loop(0, n_chunks, step, 0, unroll=min(GLA_SCAN_UNROLL, n_chunks))

    o = of_s[...] + ob_s[...]
    o = o * lax.rsqrt(jnp.mean(o * o, axis=-1, keepdims=True) + EPS) * nw_ref[...]
    r = r_ref[...].astype(F32)
    o_ref[...] = (o * (r * jax.nn.sigmoid(r))).astype(BF16)


def _gla(qkvr, alow, w2, gb, norm_w, B, L):
    T = B * L
    H = GLA_HEADS
    kq = GLA_KEY // GLA_DK
    kv = 2 * GLA_KEY // GLA_DV
    kr = (2 * GLA_KEY + GLA_VAL) // GLA_DV
    return pl.pallas_call(
        _gla_kernel,
        grid=(B, H),
        in_specs=[
            pl.BlockSpec((L, GLA_DK), lambda b, h: (b, h)),
            pl.BlockSpec((L, GLA_DK), lambda b, h: (b, kq + h)),
            pl.BlockSpec((L, GLA_DV), lambda b, h: (b, kv + h)),
            pl.BlockSpec((L, GLA_DV), lambda b, h: (b, kr + h)),
            pl.BlockSpec((L, ALOW_PAD), lambda b, h: (b, 0)),
            pl.BlockSpec((2, ALOW_PAD, GLA_DK), lambda b, h: (0, 0, h)),
            pl.BlockSpec((2, 1, GLA_DK), lambda b, h: (0, 0, h)),
            pl.BlockSpec((1, GLA_DV), lambda b, h: (0, 0)),
        ],
        out_specs=pl.BlockSpec((L, GLA_DV), lambda b, h: (b, h)),
        out_shape=jax.ShapeDtypeStruct((T, GLA_VAL), BF16),
        scratch_shapes=[pltpu.VMEM((L, GLA_DK), BF16)] * 6
        + [pltpu.VMEM((L // GLA_CHUNK, GLA_DK), F32)] * 2
        + [pltpu.VMEM((L, GLA_DV), F32)] * 2
        + [pltpu.VMEM((GLA_DV, GLA_DK), F32)] * 2,
        compiler_params=_cparams("parallel", "parallel"),
        name="gla",
    )(qkvr, qkvr, qkvr, qkvr, alow, w2, gb, norm_w.reshape(1, GLA_DV))


HY_WT = 256


def _hy_filter_kernel(z_ref, t_ref, fr_ref, w1_ref, b1_ref, w2_ref, b2_ref, w3f_ref, w3b_ref, dec_ref, o_ref, h_s):
    @pl.when(pl.program_id(0) == 0)
    def _():
        fr = fr_ref[...]
        h1 = jnp.sin(fr * (jnp.dot(z_ref[...], w1_ref[...], preferred_element_type=F32, precision=HIGHEST) + b1_ref[...]))
        h_s[...] = jnp.sin(fr * (jnp.dot(h1, w2_ref[...], preferred_element_type=F32, precision=HIGHEST) + b2_ref[...]))

    h = h_s[...]
    t = t_ref[...]
    row = lax.broadcasted_iota(jnp.int32, (z_ref.shape[0], 1), 0)
    hq = h.astype(BF16)
    hf = jnp.dot(hq, w3f_ref[...].astype(BF16), preferred_element_type=F32)
    hf = hf * jnp.exp(-t * jnp.abs(dec_ref[0:1, :]))
    hb = jnp.dot(hq, w3b_ref[...].astype(BF16), preferred_element_type=F32)
    hb = hb * jnp.exp(-t * jnp.abs(dec_ref[1:2, :]))
    hb = jnp.where(row == 0, 0.0, hb)
    ss = jnp.sum(hf * hf, axis=0, keepdims=True) + jnp.sum(hb * hb, axis=0, keepdims=True)
    inv = lax.rsqrt(ss + EPS)
    o_ref[0] = hf * inv
    o_ref[1] = hb * inv


def _hy_filter(L, hy_freq, w1, b1, w2, b2, w3, hy_decay):
    W = HY_WIDTH
    t = np.linspace(0.0, 1.0, L, dtype=np.float32)[:, None]
    omega = (np.float32(2.0 * math.pi) * np.arange(L, dtype=np.float32) / np.float32(L)).astype(np.float32)
    f = np.linspace(1e-4, HY_BANDS - 1, HY_BANDS, dtype=np.float32)
    ang = (omega[:, None] * f[None, :]).astype(np.float32)
    z = np.concatenate([t, np.cos(ang), -np.sin(ang)], axis=-1).astype(np.float32)
    nw = W // HY_WT
    return pl.pallas_call(
        _hy_filter_kernel,
        grid=(nw,),
        in_specs=[
            _const_spec((L, HY_EMB)),
            _const_spec((L, 1)),
            _const_spec((1, HY_FFN)),
            _const_spec((HY_EMB, HY_FFN)),
            _const_spec((1, HY_FFN)),
            _const_spec((HY_FFN, HY_FFN)),
            _const_spec((1, HY_FFN)),
            pl.BlockSpec((HY_FFN, HY_WT), lambda j: (0, j)),
            pl.BlockSpec((HY_FFN, HY_WT), lambda j: (0, nw + j)),
            pl.BlockSpec((2, HY_WT), lambda j: (0, j)),
        ],
        out_specs=pl.BlockSpec((2, L, HY_WT), lambda j: (0, 0, j)),
        out_shape=jax.ShapeDtypeStruct((2, L, W), F32),
        scratch_shapes=[pltpu.VMEM((L, HY_FFN), F32)],
        compiler_params=_cparams("arbitrary"),
        name="hy_filter",
    )(jnp.asarray(z), jnp.asarray(t), hy_freq.reshape(1, HY_FFN), w1, b1.reshape(1, HY_FFN), w2,
      b2.reshape(1, HY_FFN), w3, w3, hy_decay)


FFT_L = LANES
FFT_PITCH = FFT_L + 8
FFT_PAIR_UNROLL = 32
FFT_K2_UNROLL = 40


def _fft_tables(L):
    N = 2 * L
    C = L // FFT_L
    n1 = N // FFT_L
    kh = min(n1, -(-(n1 // 2 + 1) // 8) * 8)
    l = np.arange(FFT_L)
    c = np.arange(C)
    k2 = np.arange(kh)
    pair_w = np.where((k2 == 0) | (k2 == n1 // 2), 1.0, np.where(k2 < n1 // 2, 2.0, 0.0))
    ang1 = 2 * np.pi * ((k2[:, None] * c[None, :]) % n1) / n1
    f1 = np.concatenate([np.cos(ang1), -np.sin(ang1)], axis=0)
    g1 = (np.stack([np.cos(ang1), -np.sin(ang1)], axis=-1) * pair_w[:, None, None]).transpose(1, 0, 2).reshape(C, 2 * kh) / N
    ang2 = 2 * np.pi * (((l[None, :, None] * l[None, None, :]) % FFT_L) / FFT_L
                        + ((k2[:, None, None] * l[None, None, :]) % N) / N)
    cs, sn = np.cos(ang2), np.sin(ang2)
    f2 = np.concatenate([np.concatenate([cs, sn], axis=2), np.concatenate([-sn, cs], axis=2)], axis=1)
    cst, snt = cs.transpose(0, 2, 1), sn.transpose(0, 2, 1)
    g2 = np.concatenate([np.concatenate([cst, -snt], axis=2), np.concatenate([snt, cst], axis=2)], axis=1)
    bf = lambda a: jnp.asarray(a.astype(np.float32)).astype(BF16)
    return dict(C=C, n1=kh, f1=bf(f1), g1=bf(g1), f2=bf(f2), g2=bf(g2))


def _fft_kernel(z_ref, f1_ref, f2_ref, g2_ref, kf_ref, g1_ref, o_ref, a_s, b_s, y_s, *, spectrum_only, epilogue=None,
                z_pitch=FFT_L):
    C = f1_ref.shape[1]
    n1 = f1_ref.shape[0] // 2
    P = FFT_PITCH
    blk = lambda j: pl.ds(pl.multiple_of(j * P, 8), FFT_L)

    def stage1(j, carry):
        x = jnp.concatenate([z_ref[pl.ds(2 * j, C, stride=z_pitch), :], z_ref[pl.ds(2 * j + 1, C, stride=z_pitch), :]], axis=1)
        r = jnp.dot(f1_ref[...], x.astype(BF16), preferred_element_type=F32)
        a_s[pl.ds(pl.multiple_of(2 * j * P, 8), 2 * n1), :] = r[:, :LANES]
        a_s[pl.ds(pl.multiple_of((2 * j + 1) * P, 8), 2 * n1), :] = r[:, LANES:]
        return carry
    lax.fori_loop(0, FFT_L // 2, stage1, 0, unroll=FFT_PAIR_UNROLL)
    k2_unroll = min(FFT_K2_UNROLL, n1)

    def stage2(k2, carry, *, accumulate=False):
        ar = a_s[pl.ds(k2, FFT_L, stride=P), :]
        ai = a_s[pl.ds(n1 + k2, FFT_L, stride=P), :]
        x = jnp.concatenate([ar, ai], axis=0).astype(BF16)
        s = jnp.dot(f2_ref[k2], x, preferred_element_type=F32)
        xr, xi = s[:FFT_L], s[FFT_L:]
        if not spectrum_only:
            kr, ki = kf_ref[k2, 0], kf_ref[k2, 1]
            y = jnp.concatenate([xr * kr - xi * ki, xr * ki + xi * kr], axis=0).astype(BF16)
            s = jnp.dot(g2_ref[k2], y, preferred_element_type=F32)
            b_s[blk(2 * k2), :] = s[:FFT_L]
            b_s[blk(2 * k2 + 1), :] = s[FFT_L:]
        elif accumulate:
            o_ref[k2, 0] += xr
            o_ref[k2, 1] -= xi
        else:
            o_ref[k2, 0] = xr
            o_ref[k2, 1] = xi
        return carry

    if spectrum_only:
        @pl.when(pl.program_id(1) == 0)
        def _():
            lax.fori_loop(0, n1, stage2, 0, unroll=k2_unroll)

        @pl.when(pl.program_id(1) != 0)
        def _():
            lax.fori_loop(0, n1, functools.partial(stage2, accumulate=True), 0, unroll=k2_unroll)
    else:
        lax.fori_loop(0, n1, stage2, 0, unroll=k2_unroll)

    if not spectrum_only:
        def stage3(j, carry):
            x = jnp.concatenate([b_s[pl.ds(2 * j, 2 * n1, stride=P), :], b_s[pl.ds(2 * j + 1, 2 * n1, stride=P), :]], axis=1)
            r = jnp.dot(g1_ref[...], x.astype(BF16), preferred_element_type=F32)
            y_s[pl.ds(2 * j, C, stride=P), :] = r[:, :LANES]
            y_s[pl.ds(2 * j + 1, C, stride=P), :] = r[:, LANES:]
            return carry
        lax.fori_loop(0, FFT_L // 2, stage3, 0, unroll=FFT_PAIR_UNROLL)
        for c in range(C):
            rows = slice(c * FFT_L, (c + 1) * FFT_L)
            o_ref[rows, :] = epilogue(c, y_s[c * P:c * P + FFT_L, :])


def _hyena_kernel(x0_ref, x1_ref, v_ref, w0_ref, w1_ref, wv_ref, b0_ref, b1_ref, bv_ref, hb_ref,
                  f1_ref, f2_ref, g2_ref, kf_ref, g1_ref, o_ref, z_s, x0_s, a_s, b_s, y_s):
    L = x0_ref.shape[0]
    row = lax.broadcasted_iota(jnp.int32, (L, 1), 0)

    def conv3(u_ref, w_ref, b_ref):
        u = u_ref[...].astype(F32)
        prev = jnp.where(row == 0, 0.0, pltpu.roll(u, 1, 0))
        nxt = jnp.where(row == L - 1, 0.0, pltpu.roll(u, L - 1, 0))
        return w_ref[0:1, :] * prev + w_ref[1:2, :] * u + w_ref[2:3, :] * nxt + b_ref[...]

    P = FFT_PITCH
    chunk = lambda c: slice(c * P, c * P + FFT_L)
    z = conv3(v_ref, wv_ref, bv_ref) * conv3(x1_ref, w1_ref, b1_ref)
    for c in range(L // FFT_L):
        z_s[chunk(c), :] = z[c * FFT_L:(c + 1) * FFT_L, :]
    x0_s[...] = conv3(x0_ref, w0_ref, b0_ref)

    def epilogue(c, y):
        return ((y + z_s[chunk(c), :] * hb_ref[...]) * x0_s[c * FFT_L:(c + 1) * FFT_L, :]).astype(BF16)

    _fft_kernel(z_s, f1_ref, f2_ref, g2_ref, kf_ref, g1_ref, o_ref, a_s, b_s, y_s, spectrum_only=False, epilogue=epilogue,
                z_pitch=P)


def _fft_specs(tabs):
    C, n1 = tabs["C"], tabs["n1"]
    specs = [_const_spec((2 * n1, C)),
             _const_spec((n1, 2 * FFT_L, 2 * FFT_L)), _const_spec((n1, 2 * FFT_L, 2 * FFT_L)),
             pl.BlockSpec((n1, 2, FFT_L, LANES), lambda w, b: (0, 0, 0, w)),
             _const_spec((C, 2 * n1))]
    scratch = [pltpu.VMEM((FFT_L * FFT_PITCH, LANES), F32), pltpu.VMEM((2 * n1 * FFT_PITCH, LANES), F32),
               pltpu.VMEM((C * FFT_PITCH, LANES), F32)]
    return specs, scratch


def _filter_spectrum(tabs, taps):
    _, L, W = taps.shape
    n1 = tabs["n1"]
    specs, scratch = _fft_specs(tabs)

    def spectrum_kernel(z_ref, f1_ref, f2_ref, o_ref, a_s):
        _fft_kernel(z_ref, f1_ref, f2_ref, None, None, None, o_ref, a_s, None, None, spectrum_only=True)

    return pl.pallas_call(
        spectrum_kernel,
        grid=(W // LANES, 2),
        in_specs=[pl.BlockSpec((None, L, LANES), lambda w, b: (b, 0, w))] + specs[:2],
        out_specs=pl.BlockSpec((n1, 2, FFT_L, LANES), lambda w, b: (0, 0, 0, w)),
        out_shape=jax.ShapeDtypeStruct((n1, 2, FFT_L, W), F32),
        scratch_shapes=scratch[:1],
        compiler_params=_cparams("arbitrary", "arbitrary"),
        name="fft_spectrum",
    )(taps, tabs["f1"], tabs["f2"])


def _hyena(hy, conv_w, conv_b, hy_bias, taps, B, L):
    W = HY_WIDTH
    nw = W // LANES
    tabs = _fft_tables(L)
    kf = _filter_spectrum(tabs, taps)
    specs, scratch = _fft_specs(tabs)
    cb = conv_b.reshape(1, 3 * W)
    blk = lambda off: pl.BlockSpec((L, LANES), lambda w, b: (b, off + w))
    wblk = lambda off: pl.BlockSpec((3, LANES), lambda w, b: (0, off + w))
    bblk = lambda off: pl.BlockSpec((1, LANES), lambda w, b: (0, off + w))
    return pl.pallas_call(
        _hyena_kernel,
        grid=(nw, B),
        in_specs=[blk(0), blk(nw), blk(2 * nw), wblk(0), wblk(nw), wblk(2 * nw), bblk(0), bblk(nw), bblk(2 * nw),
                  bblk(0)] + specs,
        out_specs=pl.BlockSpec((L, LANES), lambda w, b: (b, w)),
        out_shape=jax.ShapeDtypeStruct((B * L, W), BF16),
        scratch_shapes=[pltpu.VMEM((L // FFT_L * FFT_PITCH, LANES), F32), pltpu.VMEM((L, LANES), F32)] + scratch,
        compiler_params=_cparams("arbitrary", "arbitrary"),
        name="hyena",
    )(hy, hy, hy, conv_w, conv_w, conv_w, cb, cb, cb, hy_bias.reshape(1, W), tabs["f1"], tabs["f2"], tabs["g2"], kf,
      tabs["g1"])


def _merge_kernel(og_ref, oh_ref, gates_ref, x_ref, wbg_ref, wbh_ref, wo_ref, n2_ref, rw_ref, rb_ref,
                  h1_ref, hn3_ref, idx_ref, wts_ref):
    mg = jnp.dot(og_ref[...], wbg_ref[...], preferred_element_type=F32)
    mh = jnp.dot(oh_ref[...], wbh_ref[...], preferred_element_type=F32)
    gg = jax.nn.sigmoid(gates_ref[:, :D_MODEL].astype(F32))
    gh = jax.nn.sigmoid(gates_ref[:, D_MODEL:].astype(F32))
    merged = (gg * mg + gh * mh).astype(BF16)
    h1 = x_ref[...] + jnp.dot(merged, wo_ref[...], preferred_element_type=F32)
    h1_ref[...] = h1
    hn = h1 * lax.rsqrt(jnp.mean(h1 * h1, axis=-1, keepdims=True) + EPS) * n2_ref[...]
    _store_token_rows(hn3_ref, hn)
    hh = hn.astype(BF16)
    hl = (hn - hh.astype(F32)).astype(BF16)
    p1 = jnp.dot(hh, rw_ref[...], preferred_element_type=F32)
    p2 = jnp.dot(hl, rw_ref[:, :ROUTE_PAD], preferred_element_type=F32)
    logits = p1[:, :ROUTE_PAD] + p1[:, ROUTE_PAD:] + p2 + rb_ref[...]
    tm = logits.shape[0]
    le = logits.T[:N_EXPERTS]
    erow = lax.broadcasted_iota(jnp.int32, le.shape, 0)
    sels, vals = [], []
    for j in range(TOP_K):
        m = jnp.max(le, axis=0, keepdims=True)
        sel = jnp.min(jnp.where(le == m, erow, N_EXPERTS), axis=0, keepdims=True)
        sels.append(sel)
        vals.append(m)
        le = jnp.where(erow == sel, -jnp.inf, le)
    ev = [jnp.exp(v - vals[0]) for v in vals]
    inv = 1.0 / (ev[0] + ev[1] + ev[2] + ev[3])
    srow = lax.broadcasted_iota(jnp.int32, (SUBLANES, tm), 0)
    idx8 = jnp.zeros((SUBLANES, tm), F32)
    wts8 = jnp.zeros((SUBLANES, tm), F32)
    for j in range(TOP_K):
        idx8 = jnp.where(srow == j, sels[j].astype(F32), idx8)
        wts8 = jnp.where(srow == j, ev[j] * inv, wts8)
    pad = jnp.zeros((ROUTE_PAD - SUBLANES, tm), F32)
    idx_ref[...] = jnp.concatenate([idx8, pad], axis=0).T[:, :TOP_K].astype(jnp.int32)
    wts_ref[...] = jnp.concatenate([wts8, pad], axis=0).T


def _merge(o_gla, o_hy, gates, x2, wbg, wbh, wo, norm2_w, rw_pad, rb_pad):
    T = x2.shape[0]
    tm = min(512, T)
    row = lambda n: pl.BlockSpec((tm, n), lambda i: (i, 0))
    return pl.pallas_call(
        _merge_kernel,
        grid=(T // tm,),
        in_specs=[row(GLA_VAL), row(HY_WIDTH), row(N_GATES), row(D_MODEL),
                  _const_spec((GLA_VAL, D_MODEL)), _const_spec((HY_WIDTH, D_MODEL)), _const_spec((D_MODEL, D_MODEL)),
                  _const_spec((1, D_MODEL)), _const_spec((D_MODEL, 2 * ROUTE_PAD)), _const_spec((1, ROUTE_PAD))],
        out_specs=[row(D_MODEL), pl.BlockSpec((tm * TOK_ROWS, LANES), lambda i: (i, 0)), row(TOP_K), row(ROUTE_PAD)],
        out_shape=[jax.ShapeDtypeStruct((T, D_MODEL), F32), jax.ShapeDtypeStruct((T * TOK_ROWS, LANES), F32),
                   jax.ShapeDtypeStruct((T, TOP_K), jnp.int32), jax.ShapeDtypeStruct((T, ROUTE_PAD), F32)],
        compiler_params=_cparams("parallel"),
        name="merge_route",
    )(o_gla, o_hy, gates, x2, wbg, wbh, wo, norm2_w.reshape(1, D_MODEL), rw_pad, rb_pad)


MOE_TM = 512
MOE_TC = 256
DISPATCH_TC = 512
DMA_UNROLL = 8


DISPATCH_SLOTS = 3


def _dispatch_kernel(last_ref, pos_ref, hn_ref, xs_ref, zero_s, buf, sem, lsems, ssems):
    i = pl.program_id(0)
    n = pl.num_programs(0)
    tc = pos_ref.shape[2] // TOP_K
    rows = tc * TOK_ROWS
    zrows = zero_s.shape[0]
    slot = lax.rem(i, DISPATCH_SLOTS)
    prev_slot = lax.rem(i + DISPATCH_SLOTS - 1, DISPATCH_SLOTS)

    def load(blk, s):
        src = hn_ref.at[pl.ds(pl.multiple_of(blk * rows, rows), rows)]
        return pltpu.make_async_copy(src, buf.at[s], lsems.at[s])

    def wait_copies(s):
        for _ in range(TOP_K):
            pltpu.make_async_copy(buf.at[s], xs_ref.at[pl.ds(0, rows)], ssems.at[s]).wait()

    @pl.when(i == 0)
    def _():
        load(0, 0).start()

        @pl.when(n > 1)
        def _():
            load(1, 1).start()

        zero_s[...] = jnp.zeros_like(zero_s)
        n_fill = last_ref.shape[0]

        def zfill(e, c):
            @pl.when(last_ref[e] >= 0)
            def _():
                dst = pl.multiple_of(last_ref[e] * zrows, zrows)
                pltpu.make_async_copy(zero_s, xs_ref.at[pl.ds(dst, zrows)], sem).start()
            return c
        lax.fori_loop(0, n_fill, zfill, 0)

        def zwait(e, c):
            @pl.when(last_ref[e] >= 0)
            def _():
                pltpu.make_async_copy(zero_s, xs_ref.at[pl.ds(0, zrows)], sem).wait()
            return c
        lax.fori_loop(0, n_fill, zwait, 0)

    load(i, slot).wait()

    def issue(r, c):
        src = buf.at[slot, pl.ds(pl.multiple_of(r * TOK_ROWS, TOK_ROWS), TOK_ROWS)]
        for j in range(TOP_K):
            dst = pl.multiple_of(pos_ref[0, 0, r * TOP_K + j] * TOK_ROWS, TOK_ROWS)
            pltpu.make_async_copy(src, xs_ref.at[pl.ds(dst, TOK_ROWS)], ssems.at[slot]).start(priority=j % 2)
        return c
    lax.fori_loop(0, tc, issue, 0, unroll=DMA_UNROLL // TOP_K)

    @pl.when(i > 0)
    def _():
        wait_copies(prev_slot)

    @pl.when(i + 2 < n)
    def _():
        load(i + 2, prev_slot).start()

    @pl.when(i == n - 1)
    def _():
        wait_copies(slot)


def _dispatch(hn3, pos, last_tile, P):
    T = hn3.shape[0] // TOK_ROWS
    tc = min(DISPATCH_TC, T)
    nt = T // tc
    return pl.pallas_call(
        _dispatch_kernel,
        grid_spec=pltpu.PrefetchScalarGridSpec(
            num_scalar_prefetch=1,
            grid=(nt,),
            in_specs=[pl.BlockSpec((1, 1, tc * TOP_K), lambda i, lt: (i, 0, 0), memory_space=pltpu.SMEM),
                      pl.BlockSpec(memory_space=pl.ANY)],
            out_specs=pl.BlockSpec(memory_space=pl.ANY),
            scratch_shapes=[pltpu.VMEM((MOE_TM * TOK_ROWS, LANES), F32),
                            pltpu.VMEM((DISPATCH_SLOTS, tc * TOK_ROWS, LANES), F32),
                            pltpu.SemaphoreType.DMA(()),
                            pltpu.SemaphoreType.DMA((DISPATCH_SLOTS,)), pltpu.SemaphoreType.DMA((DISPATCH_SLOTS,))],
        ),
        out_shape=jax.ShapeDtypeStruct((P * TOK_ROWS, LANES), F32),
        compiler_params=_cparams("arbitrary"),
        name="moe_dispatch",
    )(last_tile, pos.reshape(nt, 1, tc * TOP_K), hn3)


def _expert_kernel(te_ref, nused_ref, xs_ref, wgu_ref, bgu_ref, wd_ref, bd_ref, o_ref, wgu_s, wd_s):
    i = pl.program_id(0)
    prev = te_ref[jnp.maximum(i - 1, 0)]

    @pl.when((i == 0) | (te_ref[i] != prev))
    def _():
        wgu_s[...] = wgu_ref[0].astype(BF16)
        wd_s[...] = wd_ref[0].astype(BF16)

    @pl.when(i < nused_ref[0])
    def _():
        x = _load_token_rows(xs_ref, 0, xs_ref.shape[0] // TOK_ROWS).astype(BF16)
        gu = jnp.dot(x, wgu_s[...], preferred_element_type=F32) + bgu_ref[0]
        gate = jnp.minimum(gu[:, :D_FF], SWIGLU_LIMIT)
        up = jnp.clip(gu[:, D_FF:], -SWIGLU_LIMIT, SWIGLU_LIMIT)
        act = gate * jax.nn.sigmoid(SWIGLU_ALPHA * gate) * (up + 1.0)
        y = jnp.dot(act.astype(BF16), wd_s[...], preferred_element_type=F32) + bd_ref[0]
        _store_token_rows(o_ref, y)

    @pl.when(i >= nused_ref[0])
    def _():
        o_ref[...] = jnp.zeros_like(o_ref)


def _experts(xs3, tile_expert, n_used, w_gate_up, b_gate_up, w_down, b_down):
    P = xs3.shape[0] // TOK_ROWS
    tm = MOE_TM
    nt = P // tm
    live = lambda i, te, nu: jnp.minimum(i, nu[0] - 1)
    return pl.pallas_call(
        _expert_kernel,
        grid_spec=pltpu.PrefetchScalarGridSpec(
            num_scalar_prefetch=2,
            grid=(nt,),
            in_specs=[
                pl.BlockSpec((tm * TOK_ROWS, LANES), lambda i, te, nu: (live(i, te, nu), 0)),
                pl.BlockSpec((1, D_MODEL, 2 * D_FF), lambda i, te, nu: (te[i], 0, 0)),
                pl.BlockSpec((1, 1, 2 * D_FF), lambda i, te, nu: (te[i], 0, 0)),
                pl.BlockSpec((1, D_FF, D_MODEL), lambda i, te, nu: (te[i], 0, 0)),
                pl.BlockSpec((1, 1, D_MODEL), lambda i, te, nu: (te[i], 0, 0)),
            ],
            out_specs=pl.BlockSpec((tm * TOK_ROWS, LANES), lambda i, te, nu: (i, 0)),
            scratch_shapes=[pltpu.VMEM((D_MODEL, 2 * D_FF), BF16), pltpu.VMEM((D_FF, D_MODEL), BF16)],
        ),
        out_shape=jax.ShapeDtypeStruct((P * TOK_ROWS, LANES), F32),
        compiler_params=_cparams("arbitrary"),
        name="moe_experts",
    )(tile_expert, n_used, xs3, w_gate_up, b_gate_up.reshape(N_EXPERTS, 1, 2 * D_FF), w_down,
      b_down.reshape(N_EXPERTS, 1, D_MODEL))


COMBINE_SLOTS = 3
COMBINE_GROUP = 8


def _combine_kernel(pos_ref, pos1_ref, pos2_ref, wts_ref, h1_ref, nf_ref, ys_ref, o_ref, buf, sems):
    i = pl.program_id(0)
    n = pl.num_programs(0)
    tc = o_ref.shape[0]
    G = COMBINE_GROUP
    slot = lax.rem(i, COMBINE_SLOTS)
    slot2 = lax.rem(i + 2, COMBINE_SLOTS)

    def start_rows(p_ref, s, r):
        for j in range(TOP_K):
            src = pl.multiple_of(p_ref[0, 0, r * TOP_K + j] * TOK_ROWS, TOK_ROWS)
            dst = pl.multiple_of((j * tc + r) * TOK_ROWS, TOK_ROWS)
            pltpu.make_async_copy(ys_ref.at[pl.ds(src, TOK_ROWS)], buf.at[s, pl.ds(dst, TOK_ROWS)],
                                  sems.at[s]).start(priority=j % 2)

    def gather(p_ref, s):
        def issue(r, c):
            start_rows(p_ref, s, r)
            return c
        lax.fori_loop(0, tc, issue, 0, unroll=DMA_UNROLL // TOP_K)

    def combine_group(g):
        rows = slice(g * G, (g + 1) * G)
        w = wts_ref[rows, :]
        h = h1_ref[rows, :]
        for j in range(TOP_K):
            h = h + w[:, j:j + 1] * _load_token_rows(buf.at[slot], j * tc + g * G, G)
        o_ref[rows, :] = h * lax.rsqrt(jnp.mean(h * h, axis=-1, keepdims=True) + EPS) * nf_ref[...]

    @pl.when(i == 0)
    def _():
        gather(pos_ref, 0)

        @pl.when(n > 1)
        def _():
            gather(pos1_ref, 1)

    pltpu.make_async_copy(ys_ref.at[pl.ds(0, TOP_K * tc * TOK_ROWS)], buf.at[slot], sems.at[slot]).wait()

    @pl.when(i + 2 < n)
    def _():
        for g in range(tc // G):
            for r in range(g * G, (g + 1) * G):
                start_rows(pos2_ref, slot2, r)
            combine_group(g)

    @pl.when(i + 2 >= n)
    def _():
        for g in range(tc // G):
            combine_group(g)


def _combine(pos, wts, h1, norm_f_w, ys3):
    T = h1.shape[0]
    tc = min(MOE_TC, T)
    nt = T // tc
    pos3 = pos.reshape(nt, 1, tc * TOP_K)
    ahead = lambda k: pl.BlockSpec((1, 1, tc * TOP_K), lambda i: (jnp.minimum(i + k, nt - 1), 0, 0), memory_space=pltpu.SMEM)
    return pl.pallas_call(
        _combine_kernel,
        grid=(nt,),
        in_specs=[
            ahead(0), ahead(1), ahead(2),
            pl.BlockSpec((tc, ROUTE_PAD), lambda i: (i, 0)),
            pl.BlockSpec((tc, D_MODEL), lambda i: (i, 0)),
            _const_spec((1, D_MODEL)),
            pl.BlockSpec(memory_space=pl.ANY),
        ],
        out_specs=pl.BlockSpec((tc, D_MODEL), lambda i: (i, 0)),
        out_shape=jax.ShapeDtypeStruct((T, D_MODEL), F32),
        scratch_shapes=[pltpu.VMEM((COMBINE_SLOTS, TOP_K * tc * TOK_ROWS, LANES), F32),
                        pltpu.SemaphoreType.DMA((COMBINE_SLOTS,))],
        compiler_params=_cparams("arbitrary"),
        name="moe_combine",
    )(pos3, pos3, pos3, wts, h1, norm_f_w.reshape(1, D_MODEL), ys3)


def _route_plan(top_idx, T):
    tm = MOE_TM
    nt = (T * TOP_K) // tm + N_EXPERTS
    oh = (top_idx[:, :, None] == jnp.arange(N_EXPERTS, dtype=jnp.int32)[None, None, :]).astype(jnp.int32)
    cnt = jnp.sum(oh, axis=1)
    csum = jnp.cumsum(cnt, axis=0)
    counts = csum[-1]
    tiles = (counts + tm - 1) // tm
    tile_end = jnp.cumsum(tiles)
    offs = (tile_end - tiles) * tm
    pos = jnp.sum(oh * (offs[None, :] + csum - cnt)[:, None, :], axis=2).reshape(-1)
    tile_id = jnp.arange(nt, dtype=jnp.int32)
    tile_expert = jnp.minimum(jnp.sum((tile_end[None, :] <= tile_id[:, None]).astype(jnp.int32), axis=1),
                              N_EXPERTS - 1).astype(jnp.int32)
    tail = tile_end[-1] + jnp.arange(N_EXPERTS, dtype=jnp.int32)
    last_tile = jnp.concatenate([jnp.where(tiles > 0, tile_end - 1, -1), jnp.where(tail < nt, tail, -1)]).astype(jnp.int32)
    n_used = tile_end[-1:].astype(jnp.int32)
    return pos.astype(jnp.int32), tile_expert, last_tile, n_used, nt * tm


def kernel(x, norm1_w, w_in, gla_gate_w2, gla_gate_b, gla_norm_w, hy_conv_w, hy_conv_b, hy_freq, hy_ffn_w1, hy_ffn_b1,
           hy_ffn_w2, hy_ffn_b2, hy_ffn_w3, hy_decay, hy_bias, w_branch_gla, w_branch_hy, w_out, norm2_w, router_w,
           router_b, w_gate_up, b_gate_up, w_down, b_down, norm_f_w):
    B, L, D = x.shape
    T = B * L
    x2 = x.reshape(T, D)

    s_a = N_QKVR
    s_h = s_a + 2 * GLA_RANK
    w_b = w_in.astype(BF16)
    w_cat = jnp.concatenate(
        [w_b[:, :s_a], w_b[:, s_h:], w_b[:, s_a:s_h], jnp.zeros((D, ALOW_PAD - 2 * GLA_RANK), BF16)], axis=1)
    qkvr, hy, gates, alow = _in_proj(x2, norm1_w, w_cat)

    w2 = jnp.zeros((2, ALOW_PAD, GLA_KEY), F32)
    w2 = w2.at[0, :GLA_RANK].set(gla_gate_w2[0]).at[1, GLA_RANK:2 * GLA_RANK].set(gla_gate_w2[1])
    o_gla = _gla(qkvr, alow, w2, gla_gate_b.reshape(2, 1, GLA_KEY), gla_norm_w, B, L)

    taps = _hy_filter(L, hy_freq, hy_ffn_w1, hy_ffn_b1, hy_ffn_w2, hy_ffn_b2, hy_ffn_w3, hy_decay)
    o_hy = _hyena(hy, hy_conv_w, hy_conv_b, hy_bias, taps, B, L)

    rw_f = jnp.zeros((D, ROUTE_PAD), F32).at[:, :N_EXPERTS].set(router_w)
    rw_hi = rw_f.astype(BF16)
    rw_pad = jnp.concatenate([rw_hi, (rw_f - rw_hi.astype(F32)).astype(BF16)], axis=1)
    rb_pad = jnp.zeros((1, ROUTE_PAD), F32).at[0, :N_EXPERTS].set(router_b)
    h1, hn3, top_idx, wts_pad = _merge(o_gla, o_hy, gates, x2, w_branch_gla.astype(BF16), w_branch_hy.astype(BF16),
                                       w_out.astype(BF16), norm2_w, rw_pad, rb_pad)

    pos, tile_expert, last_tile, n_used, P = _route_plan(top_idx, T)
    xs3 = _dispatch(hn3, pos, last_tile, P)
    ys3 = _experts(xs3, tile_expert, n_used, w_gate_up, b_gate_up, w_down, b_down)
    out = _combine(pos, wts_pad, h1, norm_f_w, ys3)
    return out.reshape(B, L, D)
```

```python
import functools
import math

import numpy as np
import jax
import jax.numpy as jnp
from jax import lax
from jax.experimental import pallas as pl
from jax.experimental.pallas import tpu as pltpu

D_MODEL = 1024
EPS = 1e-6
GLA_HEADS = 4
GLA_DK = 128
GLA_DV = 256
GLA_KEY = GLA_HEADS * GLA_DK
GLA_VAL = GLA_HEADS * GLA_DV
GLA_RANK = 16
GLA_GATE_TEMP = 16.0
GLA_CHUNK = 64
HY_WIDTH = D_MODEL
HY_BANDS = 16
HY_EMB = 1 + 2 * HY_BANDS
HY_FFN = 64
N_EXPERTS = 32
TOP_K = 4
D_FF = D_MODEL
SWIGLU_LIMIT = 7.0
SWIGLU_ALPHA = 1.702

LANES = 128
SUBLANES = 8
ALOW_PAD = LANES
ROUTE_PAD = LANES
VMEM_LIMIT = 56 * 1024 * 1024

F32 = jnp.float32
BF16 = jnp.bfloat16
HIGHEST = lax.Precision.HIGHEST


def _cparams(*sem):
    return pltpu.CompilerParams(dimension_semantics=sem, vmem_limit_bytes=VMEM_LIMIT)


def _const_spec(shape):
    nd = len(shape)
    return pl.BlockSpec(shape, lambda *_: (0,) * nd, pipeline_mode=pl.Buffered(1))


TOK_ROWS = D_MODEL // LANES


def _store_token_rows(ref, val):
    n = val.shape[0]
    for s_ in range(TOK_ROWS):
        ref[pl.ds(s_, n, stride=TOK_ROWS), :] = val[:, s_ * LANES:(s_ + 1) * LANES]


def _load_token_rows(ref, first, n):
    return jnp.concatenate(
        [ref[pl.ds(first * TOK_ROWS + s_, n, stride=TOK_ROWS), :] for s_ in range(TOK_ROWS)], axis=1)


N_QKVR = 2 * GLA_KEY + 2 * GLA_VAL
N_HY = 3 * HY_WIDTH
N_GATES = 2 * D_MODEL
IN_COLS = N_QKVR + N_HY + N_GATES + ALOW_PAD


def _inproj_kernel(x_ref, nw_ref, w_ref, qkvr_ref, hy_ref, gates_ref, alow_ref):
    x = x_ref[...]
    ms = jnp.mean(x * x, axis=-1, keepdims=True)
    xn = (x * lax.rsqrt(ms + EPS) * nw_ref[...]).astype(BF16)
    c0, c1, c2 = N_QKVR, N_QKVR + N_HY, N_QKVR + N_HY + N_GATES
    qkvr_ref[...] = jnp.dot(xn, w_ref[:, :c0], preferred_element_type=F32).astype(BF16)
    hy_ref[...] = jnp.dot(xn, w_ref[:, c0:c1], preferred_element_type=F32).astype(BF16)
    gates_ref[...] = jnp.dot(xn, w_ref[:, c1:c2], preferred_element_type=F32).astype(BF16)
    alow_ref[...] = jnp.dot(xn, w_ref[:, c2:], preferred_element_type=F32)


def _in_proj(x2, norm1_w, w_cat):
    T = x2.shape[0]
    tm = min(512, T)
    return pl.pallas_call(
        _inproj_kernel,
        grid=(T // tm,),
        in_specs=[
            pl.BlockSpec((tm, D_MODEL), lambda i: (i, 0)),
            _const_spec((1, D_MODEL)),
            _const_spec((D_MODEL, IN_COLS)),
        ],
        out_specs=[
            pl.BlockSpec((tm, N_QKVR), lambda i: (i, 0)),
            pl.BlockSpec((tm, N_HY), lambda i: (i, 0)),
            pl.BlockSpec((tm, N_GATES), lambda i: (i, 0)),
            pl.BlockSpec((tm, ALOW_PAD), lambda i: (i, 0)),
        ],
        out_shape=[
            jax.ShapeDtypeStruct((T, N_QKVR), BF16),
            jax.ShapeDtypeStruct((T, N_HY), BF16),
            jax.ShapeDtypeStruct((T, N_GATES), BF16),
            jax.ShapeDtypeStruct((T, ALOW_PAD), F32),
        ],
        compiler_params=_cparams("parallel"),
        name="in_proj",
    )(x2, norm1_w.reshape(1, D_MODEL), w_cat)


GLA_PREP_ROWS = 256
GLA_PREP_UNROLL = 16
GLA_INTRA_UNROLL = 16
GLA_SCAN_UNROLL = 64


def _log_sigmoid(x):
    return jnp.minimum(x, 0.0) - jnp.log(1.0 + jnp.exp(-jnp.abs(x)))


def _gla_kernel(q_ref, k_ref, v_ref, r_ref, alow_ref, w2_ref, gb_ref, nw_ref, o_ref,
                qtf, ktf, kef, qtb, ktb, keb, decf, decb, of_s, ob_s, stf, stb):
    L = q_ref.shape[0]
    C = GLA_CHUNK
    DK = GLA_DK
    n_chunks = L // C
    R = min(GLA_PREP_ROWS, L)
    cpr = R // C
    scale = DK ** -0.5
    nt = (((1,), (1,)), ((), ()))
    tn = (((0,), (0,)), ((), ()))

    ri = lax.broadcasted_iota(jnp.int32, (R, R), 0)
    ci = lax.broadcasted_iota(jnp.int32, (R, R), 1)
    same = (ri // C) == (ci // C)
    mask_f = same & (ri >= ci)
    mask_b = same & (ri < ci)
    tri = jnp.where(mask_f, 1.0, 0.0).astype(BF16)
    w2 = jnp.concatenate([w2_ref[0], w2_ref[1]], axis=1).astype(BF16)
    gb = jnp.concatenate([gb_ref[0], gb_ref[1]], axis=1)

    def prep(g, carry):
        rows = pl.ds(pl.multiple_of(g * R, R), R)
        logits = jnp.dot(alow_ref[rows, :].astype(BF16), w2, preferred_element_type=F32) + gb
        lg = _log_sigmoid(logits) * (1.0 / GLA_GATE_TEMP)
        hi = lg.astype(BF16)
        lo = (lg - hi.astype(F32)).astype(BF16)
        b = jnp.dot(tri, hi, preferred_element_type=F32) + jnp.dot(tri, lo, preferred_element_type=F32)
        tot = jnp.concatenate(
            [jnp.broadcast_to(b[c * C + C - 1:c * C + C, :], (C, 2 * DK)) for c in range(cpr)], axis=0)
        q = q_ref[rows, :].astype(F32) * scale
        k = k_ref[rows, :].astype(F32)
        dtot = jnp.exp(tot)
        for d, (qt, kt, ke, dec) in enumerate(((qtf, ktf, kef, decf), (qtb, ktb, keb, decb))):
            sl = slice(d * DK, (d + 1) * DK)
            bd, td, ld = b[:, sl], tot[:, sl], lg[:, sl]
            if d == 0:
                e_q, e_end = bd, td - bd
            else:
                e_q, e_end = td - bd + ld, bd - ld
            qt[rows, :] = (q * jnp.exp(e_q)).astype(BF16)
            kt[rows, :] = (k * jnp.exp(-e_q)).astype(BF16)
            ke[rows, :] = (k * jnp.exp(e_end)).astype(BF16)
            for c in range(cpr):
                dec[pl.ds(g * cpr + c, 1), :] = dtot[c * C:c * C + 1, sl]
        return carry

    lax.fori_loop(0, L // R, prep, 0, unroll=min(GLA_PREP_UNROLL, L // R))

    def intra(g, carry):
        rows = pl.ds(pl.multiple_of(g * R, R), R)
        vb = v_ref[rows, :]
        for qt, kt, mask, out in ((qtf, ktf, mask_f, of_s), (qtb, ktb, mask_b, ob_s)):
            s = lax.dot_general(qt[rows, :], kt[rows, :], nt, preferred_element_type=F32)
            s = jnp.where(mask, s, 0.0).astype(BF16)
            out[rows, :] = jnp.dot(s, vb, preferred_element_type=F32)
        return carry

    lax.fori_loop(0, L // R, intra, 0, unroll=min(GLA_INTRA_UNROLL, L // R))

    def chunk(n, st, qt, ke, dec, out):
        rows = pl.ds(pl.multiple_of(n * C, C), C)
        s = st[...]
        out[rows, :] += lax.dot_general(qt[rows, :], s.astype(BF16), nt, preferred_element_type=F32)
        upd = lax.dot_general(v_ref[rows, :], ke[rows, :], tn, preferred_element_type=F32)
        st[...] = dec[pl.ds(n, 1), :] * s + upd

    def step(i, carry):
        chunk(i, stf, qtf, kef, decf, of_s)
        chunk(n_chunks - 1 - i, stb, qtb, keb, decb, ob_s)
        return carry

    stf[...] = jnp.zeros_like(stf)
    stb[...] = jnp.zeros_like(stb)
    lax.fori_loop(0, n_chunks, step, 0, unroll=min(GLA_SCAN_UNROLL, n_chunks))

    o = of_s[...] + ob_s[...]
    o = o * lax.rsqrt(jnp.mean(o * o, axis=-1, keepdims=True) + EPS) * nw_ref[...]
    r = r_ref[...].astype(F32)
    o_ref[...] = (o * (r * jax.nn.sigmoid(r))).astype(BF16)


def _gla(qkvr, alow, w2, gb, norm_w, B, L):
    T = B * L
    H = GLA_HEADS
    kq = GLA_KEY // GLA_DK
    kv = 2 * GLA_KEY // GLA_DV
    kr = (2 * GLA_KEY + GLA_VAL) // GLA_DV
    return pl.pallas_call(
        _gla_kernel,
        grid=(B, H),
        in_specs=[
            pl.BlockSpec((L, GLA_DK), lambda b, h: (b, h)),
            pl.BlockSpec((L, GLA_DK), lambda b, h: (b, kq + h)),
            pl.BlockSpec((L, GLA_DV), lambda b, h: (b, kv + h)),
            pl.BlockSpec((L, GLA_DV), lambda b, h: (b, kr + h)),
            pl.BlockSpec((L, ALOW_PAD), lambda b, h: (b, 0)),
            pl.BlockSpec((2, ALOW_PAD, GLA_DK), lambda b, h: (0, 0, h)),
            pl.BlockSpec((2, 1, GLA_DK), lambda b, h: (0, 0, h)),
            pl.BlockSpec((1, GLA_DV), lambda b, h: (0, 0)),
        ],
        out_specs=pl.BlockSpec((L, GLA_DV), lambda b, h: (b, h)),
        out_shape=jax.ShapeDtypeStruct((T, GLA_VAL), BF16),
        scratch_shapes=[pltpu.VMEM((L, GLA_DK), BF16)] * 6
        + [pltpu.VMEM((L // GLA_CHUNK, GLA_DK), F32)] * 2
        + [pltpu.VMEM((L, GLA_DV), F32)] * 2
        + [pltpu.VMEM((GLA_DV, GLA_DK), F32)] * 2,
        compiler_params=_cparams("parallel", "parallel"),
        name="gla",
    )(qkvr, qkvr, qkvr, qkvr, alow, w2, gb, norm_w.reshape(1, GLA_DV))


HY_WT = 256


def _hy_filter_kernel(z_ref, t_ref, fr_ref, w1_ref, b1_ref, w2_ref, b2_ref, w3f_ref, w3b_ref, dec_ref, o_ref, h_s):
    @pl.when(pl.program_id(0) == 0)
    def _():
        fr = fr_ref[...]
        h1 = jnp.sin(fr * (jnp.dot(z_ref[...], w1_ref[...], preferred_element_type=F32, precision=HIGHEST) + b1_ref[...]))
        h_s[...] = jnp.sin(fr * (jnp.dot(h1, w2_ref[...], preferred_element_type=F32, precision=HIGHEST) + b2_ref[...]))

    h = h_s[...]
    t = t_ref[...]
    row = lax.broadcasted_iota(jnp.int32, (z_ref.shape[0], 1), 0)
    hq = h.astype(BF16)
    hf = jnp.dot(hq, w3f_ref[...].astype(BF16), preferred_element_type=F32)
    hf = hf * jnp.exp(-t * jnp.abs(dec_ref[0:1, :]))
    hb = jnp.dot(hq, w3b_ref[...].astype(BF16), preferred_element_type=F32)
    hb = hb * jnp.exp(-t * jnp.abs(dec_ref[1:2, :]))
    hb = jnp.where(row == 0, 0.0, hb)
    ss = jnp.sum(hf * hf, axis=0, keepdims=True) + jnp.sum(hb * hb, axis=0, keepdims=True)
    inv = lax.rsqrt(ss + EPS)
    o_ref[0] = hf * inv
    o_ref[1] = hb * inv


def _hy_filter(L, hy_freq, w1, b1, w2, b2, w3, hy_decay):
    W = HY_WIDTH
    t = np.linspace(0.0, 1.0, L, dtype=np.float32)[:, None]
    omega = (np.float32(2.0 * math.pi) * np.arange(L, dtype=np.float32) / np.float32(L)).astype(np.float32)
    f = np.linspace(1e-4, HY_BANDS - 1, HY_BANDS, dtype=np.float32)
    ang = (omega[:, None] * f[None, :]).astype(np.float32)
    z = np.concatenate([t, np.cos(ang), -np.sin(ang)], axis=-1).astype(np.float32)
    nw = W // HY_WT
    return pl.pallas_call(
        _hy_filter_kernel,
        grid=(nw,),
        in_specs=[
            _const_spec((L, HY_EMB)),
            _const_spec((L, 1)),
            _const_spec((1, HY_FFN)),
            _const_spec((HY_EMB, HY_FFN)),
            _const_spec((1, HY_FFN)),
            _const_spec((HY_FFN, HY_FFN)),
            _const_spec((1, HY_FFN)),
            pl.BlockSpec((HY_FFN, HY_WT), lambda j: (0, j)),
            pl.BlockSpec((HY_FFN, HY_WT), lambda j: (0, nw + j)),
            pl.BlockSpec((2, HY_WT), lambda j: (0, j)),
        ],
        out_specs=pl.BlockSpec((2, L, HY_WT), lambda j: (0, 0, j)),
        out_shape=jax.ShapeDtypeStruct((2, L, W), F32),
        scratch_shapes=[pltpu.VMEM((L, HY_FFN), F32)],
        compiler_params=_cparams("arbitrary"),
        name="hy_filter",
    )(jnp.asarray(z), jnp.asarray(t), hy_freq.reshape(1, HY_FFN), w1, b1.reshape(1, HY_FFN), w2,
      b2.reshape(1, HY_FFN), w3, w3, hy_decay)


FFT_L = LANES
FFT_PITCH = FFT_L + 8
FFT_PAIR_UNROLL = 32
FFT_K2_UNROLL = 40


def _fft_tables(L):
    N = 2 * L
    C = L // FFT_L
    n1 = N // FFT_L
    kh = min(n1, -(-(n1 // 2 + 1) // 8) * 8)
    l = np.arange(FFT_L)
    c = np.arange(C)
    k2 = np.arange(kh)
    pair_w = np.where((k2 == 0) | (k2 == n1 // 2), 1.0, np.where(k2 < n1 // 2, 2.0, 0.0))
    ang1 = 2 * np.pi * ((k2[:, None] * c[None, :]) % n1) / n1
    f1 = np.concatenate([np.cos(ang1), -np.sin(ang1)], axis=0)
    g1 = (np.stack([np.cos(ang1), -np.sin(ang1)], axis=-1) * pair_w[:, None, None]).transpose(1, 0, 2).reshape(C, 2 * kh) / N
    ang2 = 2 * np.pi * (((l[None, :, None] * l[None, None, :]) % FFT_L) / FFT_L
                        + ((k2[:, None, None] * l[None, None, :]) % N) / N)
    cs, sn = np.cos(ang2), np.sin(ang2)
    f2 = np.concatenate([np.concatenate([cs, sn], axis=2), np.concatenate([-sn, cs], axis=2)], axis=1)
    cst, snt = cs.transpose(0, 2, 1), sn.transpose(0, 2, 1)
    g2 = np.concatenate([np.concatenate([cst, -snt], axis=2), np.concatenate([snt, cst], axis=2)], axis=1)
    bf = lambda a: jnp.asarray(a.astype(np.float32)).astype(BF16)
    return dict(C=C, n1=kh, f1=bf(f1), g1=bf(g1), f2=bf(f2), g2=bf(g2))


def _fft_kernel(z_ref, f1_ref, f2_ref, g2_ref, kf_ref, g1_ref, o_ref, a_s, b_s, y_s, *, spectrum_only, epilogue=None,
                z_pitch=FFT_L):
    C = f1_ref.shape[1]
    n1 = f1_ref.shape[0] // 2
    P = FFT_PITCH
    blk = lambda j: pl.ds(pl.multiple_of(j * P, 8), FFT_L)

    def stage1(j, carry):
        x = jnp.concatenate([z_ref[pl.ds(2 * j, C, stride=z_pitch), :], z_ref[pl.ds(2 * j + 1, C, stride=z_pitch), :]], axis=1)
        r = jnp.dot(f1_ref[...], x.astype(BF16), preferred_element_type=F32)
        a_s[pl.ds(pl.multiple_of(2 * j * P, 8), 2 * n1), :] = r[:, :LANES]
        a_s[pl.ds(pl.multiple_of((2 * j + 1) * P, 8), 2 * n1), :] = r[:, LANES:]
        return carry
    lax.fori_loop(0, FFT_L // 2, stage1, 0, unroll=FFT_PAIR_UNROLL)
    k2_unroll = min(FFT_K2_UNROLL, n1)

    def stage2(k2, carry, *, accumulate=False):
        ar = a_s[pl.ds(k2, FFT_L, stride=P), :]
        ai = a_s[pl.ds(n1 + k2, FFT_L, stride=P), :]
        x = jnp.concatenate([ar, ai], axis=0).astype(BF16)
        s = jnp.dot(f2_ref[k2], x, preferred_element_type=F32)
        xr, xi = s[:FFT_L], s[FFT_L:]
        if not spectrum_only:
            kr, ki = kf_ref[k2, 0], kf_ref[k2, 1]
            y = jnp.concatenate([xr * kr - xi * ki, xr * ki + xi * kr], axis=0).astype(BF16)
            s = jnp.dot(g2_ref[k2], y, preferred_element_type=F32)
            b_s[blk(2 * k2), :] = s[:FFT_L]
            b_s[blk(2 * k2 + 1), :] = s[FFT_L:]
        elif accumulate:
            o_ref[k2, 0] += xr
            o_ref[k2, 1] -= xi
        else:
            o_ref[k2, 0] = xr
            o_ref[k2, 1] = xi
        return carry

    if spectrum_only:
        @pl.when(pl.program_id(1) == 0)
        def _():
            lax.fori_loop(0, n1, stage2, 0, unroll=k2_unroll)

        @pl.when(pl.program_id(1) != 0)
        def _():
            lax.fori_loop(0, n1, functools.partial(stage2, accumulate=True), 0, unroll=k2_unroll)
    else:
        lax.fori_loop(0, n1, stage2, 0, unroll=k2_unroll)

    if not spectrum_only:
        def stage3(j, carry):
            x = jnp.concatenate([b_s[pl.ds(2 * j, 2 * n1, stride=P), :], b_s[pl.ds(2 * j + 1, 2 * n1, stride=P), :]], axis=1)
            r = jnp.dot(g1_ref[...], x.astype(BF16), preferred_element_type=F32)
            y_s[pl.ds(2 * j, C, stride=P), :] = r[:, :LANES]
            y_s[pl.ds(2 * j + 1, C, stride=P), :] = r[:, LANES:]
            return carry
        lax.fori_loop(0, FFT_L // 2, stage3, 0, unroll=FFT_PAIR_UNROLL)
        for c in range(C):
            rows = slice(c * FFT_L, (c + 1) * FFT_L)
            o_ref[rows, :] = epilogue(c, y_s[c * P:c * P + FFT_L, :])


def _hyena_kernel(x0_ref, x1_ref, v_ref, w0_ref, w1_ref, wv_ref, b0_ref, b1_ref, bv_ref, hb_ref,
                  f1_ref, f2_ref, g2_ref, kf_ref, g1_ref, o_ref, z_s, x0_s, a_s, b_s, y_s):
    L = x0_ref.shape[0]
    row = lax.broadcasted_iota(jnp.int32, (L, 1), 0)

    def conv3(u_ref, w_ref, b_ref):
        u = u_ref[...].astype(F32)
        prev = jnp.where(row == 0, 0.0, pltpu.roll(u, 1, 0))
        nxt = jnp.where(row == L - 1, 0.0, pltpu.roll(u, L - 1, 0))
        return w_ref[0:1, :] * prev + w_ref[1:2, :] * u + w_ref[2:3, :] * nxt + b_ref[...]

    P = FFT_PITCH
    chunk = lambda c: slice(c * P, c * P + FFT_L)
    z = conv3(v_ref, wv_ref, bv_ref) * conv3(x1_ref, w1_ref, b1_ref)
    for c in range(L // FFT_L):
        z_s[chunk(c), :] = z[c * FFT_L:(c + 1) * FFT_L, :]
    x0_s[...] = conv3(x0_ref, w0_ref, b0_ref)

    def epilogue(c, y):
        return ((y + z_s[chunk(c), :] * hb_ref[...]) * x0_s[c * FFT_L:(c + 1) * FFT_L, :]).astype(BF16)

    _fft_kernel(z_s, f1_ref, f2_ref, g2_ref, kf_ref, g1_ref, o_ref, a_s, b_s, y_s, spectrum_only=False, epilogue=epilogue,
                z_pitch=P)


def _fft_specs(tabs):
    C, n1 = tabs["C"], tabs["n1"]
    specs = [_const_spec((2 * n1, C)),
             _const_spec((n1, 2 * FFT_L, 2 * FFT_L)), _const_spec((n1, 2 * FFT_L, 2 * FFT_L)),
             pl.BlockSpec((n1, 2, FFT_L, LANES), lambda w, b: (0, 0, 0, w)),
             _const_spec((C, 2 * n1))]
    scratch = [pltpu.VMEM((FFT_L * FFT_PITCH, LANES), F32), pltpu.VMEM((2 * n1 * FFT_PITCH, LANES), F32),
               pltpu.VMEM((C * FFT_PITCH, LANES), F32)]
    return specs, scratch


def _filter_spectrum(tabs, taps):
    _, L, W = taps.shape
    n1 = tabs["n1"]
    specs, scratch = _fft_specs(tabs)

    def spectrum_kernel(z_ref, f1_ref, f2_ref, o_ref, a_s):
        _fft_kernel(z_ref, f1_ref, f2_ref, None, None, None, o_ref, a_s, None, None, spectrum_only=True)

    return pl.pallas_call(
        spectrum_kernel,
        grid=(W // LANES, 2),
        in_specs=[pl.BlockSpec((None, L, LANES), lambda w, b: (b, 0, w))] + specs[:2],
        out_specs=pl.BlockSpec((n1, 2, FFT_L, LANES), lambda w, b: (0, 0, 0, w)),
        out_shape=jax.ShapeDtypeStruct((n1, 2, FFT_L, W), F32),
        scratch_shapes=scratch[:1],
        compiler_params=_cparams("arbitrary", "arbitrary"),
        name="fft_spectrum",
    )(taps, tabs["f1"], tabs["f2"])


def _hyena(hy, conv_w, conv_b, hy_bias, taps, B, L):
    W = HY_WIDTH
    nw = W // LANES
    tabs = _fft_tables(L)
    kf = _filter_spectrum(tabs, taps)
    specs, scratch = _fft_specs(tabs)
    cb = conv_b.reshape(1, 3 * W)
    blk = lambda off: pl.BlockSpec((L, LANES), lambda w, b: (b, off + w))
    wblk = lambda off: pl.BlockSpec((3, LANES), lambda w, b: (0, off + w))
    bblk = lambda off: pl.BlockSpec((1, LANES), lambda w, b: (0, off + w))
    return pl.pallas_call(
        _hyena_kernel,
        grid=(nw, B),
        in_specs=[blk(0), blk(nw), blk(2 * nw), wblk(0), wblk(nw), wblk(2 * nw), bblk(0), bblk(nw), bblk(2 * nw),
                  bblk(0)] + specs,
        out_specs=pl.BlockSpec((L, LANES), lambda w, b: (b, w)),
        out_shape=jax.ShapeDtypeStruct((B * L, W), BF16),
        scratch_shapes=[pltpu.VMEM((L // FFT_L * FFT_PITCH, LANES), F32), pltpu.VMEM((L, LANES), F32)] + scratch,
        compiler_params=_cparams("arbitrary", "arbitrary"),
        name="hyena",
    )(hy, hy, hy, conv_w, conv_w, conv_w, cb, cb, cb, hy_bias.reshape(1, W), tabs["f1"], tabs["f2"], tabs["g2"], kf,
      tabs["g1"])


def _merge_kernel(og_ref, oh_ref, gates_ref, x_ref, wbg_ref, wbh_ref, wo_ref, n2_ref, rw_ref, rb_ref,
                  h1_ref, hn3_ref, idx_ref, wts_ref):
    mg = jnp.dot(og_ref[...], wbg_ref[...], preferred_element_type=F32)
    mh = jnp.dot(oh_ref[...], wbh_ref[...], preferred_element_type=F32)
    gg = jax.nn.sigmoid(gates_ref[:, :D_MODEL].astype(F32))
    gh = jax.nn.sigmoid(gates_ref[:, D_MODEL:].astype(F32))
    merged = (gg * mg + gh * mh).astype(BF16)
    h1 = x_ref[...] + jnp.dot(merged, wo_ref[...], preferred_element_type=F32)
    h1_ref[...] = h1
    hn = h1 * lax.rsqrt(jnp.mean(h1 * h1, axis=-1, keepdims=True) + EPS) * n2_ref[...]
    _store_token_rows(hn3_ref, hn)
    hh = hn.astype(BF16)
    hl = (hn - hh.astype(F32)).astype(BF16)
    p1 = jnp.dot(hh, rw_ref[...], preferred_element_type=F32)
    p2 = jnp.dot(hl, rw_ref[:, :ROUTE_PAD], preferred_element_type=F32)
    logits = p1[:, :ROUTE_PAD] + p1[:, ROUTE_PAD:] + p2 + rb_ref[...]
    tm = logits.shape[0]
    le = logits.T[:N_EXPERTS]
    erow = lax.broadcasted_iota(jnp.int32, le.shape, 0)
    sels, vals = [], []
    for j in range(TOP_K):
        m = jnp.max(le, axis=0, keepdims=True)
        sel = jnp.min(jnp.where(le == m, erow, N_EXPERTS), axis=0, keepdims=True)
        sels.append(sel)
        vals.append(m)
        le = jnp.where(erow == sel, -jnp.inf, le)
    ev = [jnp.exp(v - vals[0]) for v in vals]
    inv = 1.0 / (ev[0] + ev[1] + ev[2] + ev[3])
    srow = lax.broadcasted_iota(jnp.int32, (SUBLANES, tm), 0)
    idx8 = jnp.zeros((SUBLANES, tm), F32)
    wts8 = jnp.zeros((SUBLANES, tm), F32)
    for j in range(TOP_K):
        idx8 = jnp.where(srow == j, sels[j].astype(F32), idx8)
        wts8 = jnp.where(srow == j, ev[j] * inv, wts8)
    pad = jnp.zeros((ROUTE_PAD - SUBLANES, tm), F32)
    idx_ref[...] = jnp.concatenate([idx8, pad], axis=0).T[:, :TOP_K].astype(jnp.int32)
    wts_ref[...] = jnp.concatenate([wts8, pad], axis=0).T


def _merge(o_gla, o_hy, gates, x2, wbg, wbh, wo, norm2_w, rw_pad, rb_pad):
    T = x2.shape[0]
    tm = min(512, T)
    row = lambda n: pl.BlockSpec((tm, n), lambda i: (i, 0))
    return pl.pallas_call(
        _merge_kernel,
        grid=(T // tm,),
        in_specs=[row(GLA_VAL), row(HY_WIDTH), row(N_GATES), row(D_MODEL),
                  _const_spec((GLA_VAL, D_MODEL)), _const_spec((HY_WIDTH, D_MODEL)), _const_spec((D_MODEL, D_MODEL)),
                  _const_spec((1, D_MODEL)), _const_spec((D_MODEL, 2 * ROUTE_PAD)), _const_spec((1, ROUTE_PAD))],
        out_specs=[row(D_MODEL), pl.BlockSpec((tm * TOK_ROWS, LANES), lambda i: (i, 0)), row(TOP_K), row(ROUTE_PAD)],
        out_shape=[jax.ShapeDtypeStruct((T, D_MODEL), F32), jax.ShapeDtypeStruct((T * TOK_ROWS, LANES), F32),
                   jax.ShapeDtypeStruct((T, TOP_K), jnp.int32), jax.ShapeDtypeStruct((T, ROUTE_PAD), F32)],
        compiler_params=_cparams("parallel"),
        name="merge_route",
    )(o_gla, o_hy, gates, x2, wbg, wbh, wo, norm2_w.reshape(1, D_MODEL), rw_pad, rb_pad)


MOE_TM = 512
MOE_TC = 256
DISPATCH_TC = 512
DMA_UNROLL = 8


DISPATCH_SLOTS = 3


def _dispatch_kernel(last_ref, pos_ref, hn_ref, xs_ref, zero_s, buf, sem, lsems, ssems):
    i = pl.program_id(0)
    n = pl.num_programs(0)
    tc = pos_ref.shape[2] // TOP_K
    rows = tc * TOK_ROWS
    zrows = zero_s.shape[0]
    slot = lax.rem(i, DISPATCH_SLOTS)
    prev_slot = lax.rem(i + DISPATCH_SLOTS - 1, DISPATCH_SLOTS)

    def load(blk, s):
        src = hn_ref.at[pl.ds(pl.multiple_of(blk * rows, rows), rows)]
        return pltpu.make_async_copy(src, buf.at[s], lsems.at[s])

    def wait_copies(s):
        for _ in range(TOP_K):
            pltpu.make_async_copy(buf.at[s], xs_ref.at[pl.ds(0, rows)], ssems.at[s]).wait()

    @pl.when(i == 0)
    def _():
        load(0, 0).start()

        @pl.when(n > 1)
        def _():
            load(1, 1).start()

        zero_s[...] = jnp.zeros_like(zero_s)
        n_fill = last_ref.shape[0]

        def zfill(e, c):
            @pl.when(last_ref[e] >= 0)
            def _():
                dst = pl.multiple_of(last_ref[e] * zrows, zrows)
                pltpu.make_async_copy(zero_s, xs_ref.at[pl.ds(dst, zrows)], sem).start()
            return c
        lax.fori_loop(0, n_fill, zfill, 0)

        def zwait(e, c):
            @pl.when(last_ref[e] >= 0)
            def _():
                pltpu.make_async_copy(zero_s, xs_ref.at[pl.ds(0, zrows)], sem).wait()
            return c
        lax.fori_loop(0, n_fill, zwait, 0)

    load(i, slot).wait()

    def issue(r, c):
        src = buf.at[slot, pl.ds(pl.multiple_of(r * TOK_ROWS, TOK_ROWS), TOK_ROWS)]
        for j in range(TOP_K):
            dst = pl.multiple_of(pos_ref[0, 0, r * TOP_K + j] * TOK_ROWS, TOK_ROWS)
            pltpu.make_async_copy(src, xs_ref.at[pl.ds(dst, TOK_ROWS)], ssems.at[slot]).start(priority=j % 2)
        return c
    lax.fori_loop(0, tc, issue, 0, unroll=DMA_UNROLL // TOP_K)

    @pl.when(i > 0)
    def _():
        wait_copies(prev_slot)

    @pl.when(i + 2 < n)
    def _():
        load(i + 2, prev_slot).start()

    @pl.when(i == n - 1)
    def _():
        wait_copies(slot)


def _dispatch(hn3, pos, last_tile, P):
    T = hn3.shape[0] // TOK_ROWS
    tc = min(DISPATCH_TC, T)
    nt = T // tc
    return pl.pallas_call(
        _dispatch_kernel,
        grid_spec=pltpu.PrefetchScalarGridSpec(
            num_scalar_prefetch=1,
            grid=(nt,),
            in_specs=[pl.BlockSpec((1, 1, tc * TOP_K), lambda i, lt: (i, 0, 0), memory_space=pltpu.SMEM),
                      pl.BlockSpec(memory_space=pl.ANY)],
            out_specs=pl.BlockSpec(memory_space=pl.ANY),
            scratch_shapes=[pltpu.VMEM((MOE_TM * TOK_ROWS, LANES), F32),
                            pltpu.VMEM((DISPATCH_SLOTS, tc * TOK_ROWS, LANES), F32),
                            pltpu.SemaphoreType.DMA(()),
                            pltpu.SemaphoreType.DMA((DISPATCH_SLOTS,)), pltpu.SemaphoreType.DMA((DISPATCH_SLOTS,))],
        ),
        out_shape=jax.ShapeDtypeStruct((P * TOK_ROWS, LANES), F32),
        compiler_params=_cparams("arbitrary"),
        name="moe_dispatch",
    )(last_tile, pos.reshape(nt, 1, tc * TOP_K), hn3)


def _expert_kernel(te_ref, nused_ref, xs_ref, wgu_ref, bgu_ref, wd_ref, bd_ref, o_ref, wgu_s, wd_s):
    i = pl.program_id(0)
    prev = te_ref[jnp.maximum(i - 1, 0)]

    @pl.when((i == 0) | (te_ref[i] != prev))
    def _():
        wgu_s[...] = wgu_ref[0].astype(BF16)
        wd_s[...] = wd_ref[0].astype(BF16)

    @pl.when(i < nused_ref[0])
    def _():
        x = _load_token_rows(xs_ref, 0, xs_ref.shape[0] // TOK_ROWS).astype(BF16)
        gu = jnp.dot(x, wgu_s[...], preferred_element_type=F32) + bgu_ref[0]
        gate = jnp.minimum(gu[:, :D_FF], SWIGLU_LIMIT)
        up = jnp.clip(gu[:, D_FF:], -SWIGLU_LIMIT, SWIGLU_LIMIT)
        act = gate * jax.nn.sigmoid(SWIGLU_ALPHA * gate) * (up + 1.0)
        y = jnp.dot(act.astype(BF16), wd_s[...], preferred_element_type=F32) + bd_ref[0]
        _store_token_rows(o_ref, y)

    @pl.when(i >= nused_ref[0])
    def _():
        o_ref[...] = jnp.zeros_like(o_ref)


def _experts(xs3, tile_expert, n_used, w_gate_up, b_gate_up, w_down, b_down):
    P = xs3.shape[0] // TOK_ROWS
    tm = MOE_TM
    nt = P // tm
    live = lambda i, te, nu: jnp.minimum(i, nu[0] - 1)
    return pl.pallas_call(
        _expert_kernel,
        grid_spec=pltpu.PrefetchScalarGridSpec(
            num_scalar_prefetch=2,
            grid=(nt,),
            in_specs=[
                pl.BlockSpec((tm * TOK_ROWS, LANES), lambda i, te, nu: (live(i, te, nu), 0)),
                pl.BlockSpec((1, D_MODEL, 2 * D_FF), lambda i, te, nu: (te[i], 0, 0)),
                pl.BlockSpec((1, 1, 2 * D_FF), lambda i, te, nu: (te[i], 0, 0)),
                pl.BlockSpec((1, D_FF, D_MODEL), lambda i, te, nu: (te[i], 0, 0)),
                pl.BlockSpec((1, 1, D_MODEL), lambda i, te, nu: (te[i], 0, 0)),
            ],
            out_specs=pl.BlockSpec((tm * TOK_ROWS, LANES), lambda i, te, nu: (i, 0)),
            scratch_shapes=[pltpu.VMEM((D_MODEL, 2 * D_FF), BF16), pltpu.VMEM((D_FF, D_MODEL), BF16)],
        ),
        out_shape=jax.ShapeDtypeStruct((P * TOK_ROWS, LANES), F32),
        compiler_params=_cparams("arbitrary"),
        name="moe_experts",
    )(tile_expert, n_used, xs3, w_gate_up, b_gate_up.reshape(N_EXPERTS, 1, 2 * D_FF), w_down,
      b_down.reshape(N_EXPERTS, 1, D_MODEL))


COMBINE_SLOTS = 3
COMBINE_GROUP = 8


def _combine_kernel(pos_ref, pos1_ref, pos2_ref, wts_ref, h1_ref, nf_ref, ys_ref, o_ref, buf, sems):
    i = pl.program_id(0)
    n = pl.num_programs(0)
    tc = o_ref.shape[0]
    G = COMBINE_GROUP
    slot = lax.rem(i, COMBINE_SLOTS)
    slot2 = lax.rem(i + 2, COMBINE_SLOTS)

    def start_rows(p_ref, s, r):
        for j in range(TOP_K):
            src = pl.multiple_of(p_ref[0, 0, r * TOP_K + j] * TOK_ROWS, TOK_ROWS)
            dst = pl.multiple_of((j * tc + r) * TOK_ROWS, TOK_ROWS)
            pltpu.make_async_copy(ys_ref.at[pl.ds(src, TOK_ROWS)], buf.at[s, pl.ds(dst, TOK_ROWS)],
                                  sems.at[s]).start(priority=j % 2)

    def gather(p_ref, s):
        def issue(r, c):
            start_rows(p_ref, s, r)
            return c
        lax.fori_loop(0, tc, issue, 0, unroll=DMA_UNROLL // TOP_K)

    def combine_group(g):
        rows = slice(g * G, (g + 1) * G)
        w = wts_ref[rows, :]
        h = h1_ref[rows, :]
        for j in range(TOP_K):
            h = h + w[:, j:j + 1] * _load_token_rows(buf.at[slot], j * tc + g * G, G)
        o_ref[rows, :] = h * lax.rsqrt(jnp.mean(h * h, axis=-1, keepdims=True) + EPS) * nf_ref[...]

    @pl.when(i == 0)
    def _():
        gather(pos_ref, 0)

        @pl.when(n > 1)
        def _():
            gather(pos1_ref, 1)

    pltpu.make_async_copy(ys_ref.at[pl.ds(0, TOP_K * tc * TOK_ROWS)], buf.at[slot], sems.at[slot]).wait()

    @pl.when(i + 2 < n)
    def _():
        for g in range(tc // G):
            for r in range(g * G, (g + 1) * G):
                start_rows(pos2_ref, slot2, r)
            combine_group(g)

    @pl.when(i + 2 >= n)
    def _():
        for g in range(tc // G):
            combine_group(g)


def _combine(pos, wts, h1, norm_f_w, ys3):
    T = h1.shape[0]
    tc = min(MOE_TC, T)
    nt = T // tc
    pos3 = pos.reshape(nt, 1, tc * TOP_K)
    ahead = lambda k: pl.BlockSpec((1, 1, tc * TOP_K), lambda i: (jnp.minimum(i + k, nt - 1), 0, 0), memory_space=pltpu.SMEM)
    return pl.pallas_call(
        _combine_kernel,
        grid=(nt,),
        in_specs=[
            ahead(0), ahead(1), ahead(2),
            pl.BlockSpec((tc, ROUTE_PAD), lambda i: (i, 0)),
            pl.BlockSpec((tc, D_MODEL), lambda i: (i, 0)),
            _const_spec((1, D_MODEL)),
            pl.BlockSpec(memory_space=pl.ANY),
        ],
        out_specs=pl.BlockSpec((tc, D_MODEL), lambda i: (i, 0)),
        out_shape=jax.ShapeDtypeStruct((T, D_MODEL), F32),
        scratch_shapes=[pltpu.VMEM((COMBINE_SLOTS, TOP_K * tc * TOK_ROWS, LANES), F32),
                        pltpu.SemaphoreType.DMA((COMBINE_SLOTS,))],
        compiler_params=_cparams("arbitrary"),
        name="moe_combine",
    )(pos3, pos3, pos3, wts, h1, norm_f_w.reshape(1, D_MODEL), ys3)


def _route_plan(top_idx, T):
    tm = MOE_TM
    nt = (T * TOP_K) // tm + N_EXPERTS
    oh = (top_idx[:, :, None] == jnp.arange(N_EXPERTS, dtype=jnp.int32)[None, None, :]).astype(jnp.int32)
    cnt = jnp.sum(oh, axis=1)
    csum = jnp.cumsum(cnt, axis=0)
    counts = csum[-1]
    tiles = (counts + tm - 1) // tm
    tile_end = jnp.cumsum(tiles)
    offs = (tile_end - tiles) * tm
    pos = jnp.sum(oh * (offs[None, :] + csum - cnt)[:, None, :], axis=2).reshape(-1)
    tile_id = jnp.arange(nt, dtype=jnp.int32)
    tile_expert = jnp.minimum(jnp.sum((tile_end[None, :] <= tile_id[:, None]).astype(jnp.int32), axis=1),
                              N_EXPERTS - 1).astype(jnp.int32)
    tail = tile_end[-1] + jnp.arange(N_EXPERTS, dtype=jnp.int32)
    last_tile = jnp.concatenate([jnp.where(tiles > 0, tile_end - 1, -1), jnp.where(tail < nt, tail, -1)]).astype(jnp.int32)
    n_used = tile_end[-1:].astype(jnp.int32)
    return pos.astype(jnp.int32), tile_expert, last_tile, n_used, nt * tm


def kernel(x, norm1_w, w_in, gla_gate_w2, gla_gate_b, gla_norm_w, hy_conv_w, hy_conv_b, hy_freq, hy_ffn_w1, hy_ffn_b1,
           hy_ffn_w2, hy_ffn_b2, hy_ffn_w3, hy_decay, hy_bias, w_branch_gla, w_branch_hy, w_out, norm2_w, router_w,
           router_b, w_gate_up, b_gate_up, w_down, b_down, norm_f_w):
    B, L, D = x.shape
    T = B * L
    x2 = x.reshape(T, D)

    s_a = N_QKVR
    s_h = s_a + 2 * GLA_RANK
    w_b = w_in.astype(BF16)
    w_cat = jnp.concatenate(
        [w_b[:, :s_a], w_b[:, s_h:], w_b[:, s_a:s_h], jnp.zeros((D, ALOW_PAD - 2 * GLA_RANK), BF16)], axis=1)
    qkvr, hy, gates, alow = _in_proj(x2, norm1_w, w_cat)

    w2 = jnp.zeros((2, ALOW_PAD, GLA_KEY), F32)
    w2 = w2.at[0, :GLA_RANK].set(gla_gate_w2[0]).at[1, GLA_RANK:2 * GLA_RANK].set(gla_gate_w2[1])
    o_gla = _gla(qkvr, alow, w2, gla_gate_b.reshape(2, 1, GLA_KEY), gla_norm_w, B, L)

    taps = _hy_filter(L, hy_freq, hy_ffn_w1, hy_ffn_b1, hy_ffn_w2, hy_ffn_b2, hy_ffn_w3, hy_decay)
    o_hy = _hyena(hy, hy_conv_w, hy_conv_b, hy_bias, taps, B, L)

    rw_f = jnp.zeros((D, ROUTE_PAD), F32).at[:, :N_EXPERTS].set(router_w)
    rw_hi = rw_f.astype(BF16)
    rw_pad = jnp.concatenate([rw_hi, (rw_f - rw_hi.astype(F32)).astype(BF16)], axis=1)
    rb_pad = jnp.zeros((1, ROUTE_PAD), F32).at[0, :N_EXPERTS].set(router_b)
    h1, hn3, top_idx, wts_pad = _merge(o_gla, o_hy, gates, x2, w_branch_gla.astype(BF16), w_branch_hy.astype(BF16),
                                       w_out.astype(BF16), norm2_w, rw_pad, rb_pad)

    pos, tile_expert, last_tile, n_used, P = _route_plan(top_idx, T)
    xs3 = _dispatch(hn3, pos, last_tile, P)
    ys3 = _experts(xs3, tile_expert, n_used, w_gate_up, b_gate_up, w_down, b_down)
    out = _combine(pos, wts_pad, h1, norm_f_w, ys3)
    return out.reshape(B, L, D)
```
